```python
import math
import jax, jax.numpy as jnp
from jax import lax
import numpy as np

D_MODEL = 2048
BATCH = 1
SEQ = 16384
DEPTH = 1

GRID_W = 64
CTX_LEN = 256
MIX_W = D_MODEL
ATTN_W = MIX_W // 2
HEAD_DIM = 128
N_HEADS = ATTN_W // HEAD_DIM
N_KV_HEADS = 2
GQA_GROUP = N_HEADS // N_KV_HEADS
KV_W = N_KV_HEADS * HEAD_DIM
ATTN_SCALE = HEAD_DIM ** -0.5
ROPE_THETA = 10000.0
ROPE_AXIS_DIM = HEAD_DIM // 2
Q_BLOCK = 128
RG_W = MIX_W - ATTN_W
RG_HEADS = 8
RG_HD = RG_W // RG_HEADS
CONV_W = 4
RG_C = 8.0
PROJ_W = ATTN_W + 2 * KV_W + 2 * RG_W
N_EXPERTS = 64
N_GROUPS = 8
TOPK_GROUPS = 4
TOP_K = 8
EXPERT_FF = 512
SHARED_FF = 512
ROUTED_SCALE = 2.5
MOE_BLOCK = 128
NORM_EPS = 1e-6
DEEPNORM_ALPHA = (2.0 * DEPTH) ** 0.25
DEEPNORM_BETA = (8.0 * DEPTH) ** -0.25

kernel_name = "hybrid_attn_rglru_moe_dit_layer"


def _layer_norm(x, g, b):
    xf = x.astype(jnp.float32)
    mu = jnp.mean(xf, axis=-1, keepdims=True)
    var = jnp.mean(jnp.square(xf - mu), axis=-1, keepdims=True)
    y = (xf - mu) * lax.rsqrt(var + NORM_EPS) * g.astype(jnp.float32) + b.astype(jnp.float32)
    return y.astype(x.dtype)


def _rms_norm(x, g):
    xf = x.astype(jnp.float32)
    y = xf * lax.rsqrt(jnp.mean(jnp.square(xf), axis=-1, keepdims=True) + NORM_EPS)
    return (y * g.astype(jnp.float32)).astype(x.dtype)


def _rope_axis(x, pos):
    half = ROPE_AXIS_DIM // 2
    inv_freq = ROPE_THETA ** (-jnp.arange(half, dtype=jnp.float32) / half)
    ang = pos.astype(jnp.float32)[:, None] * inv_freq[None, :]
    cos = jnp.concatenate([jnp.cos(ang), jnp.cos(ang)], axis=-1)[None, :, None, :]
    sin = jnp.concatenate([jnp.sin(ang), jnp.sin(ang)], axis=-1)[None, :, None, :]
    xf = x.astype(jnp.float32)
    rot = jnp.concatenate([-xf[..., half:], xf[..., :half]], axis=-1)
    return (xf * cos + rot * sin).astype(x.dtype)


def _rope_2d(x, row, col):
    return jnp.concatenate([_rope_axis(x[..., :ROPE_AXIS_DIM], row),
                            _rope_axis(x[..., ROPE_AXIS_DIM:], col)], axis=-1)


def _block_attention(q, k, v):
    B, Sq = q.shape[0], q.shape[1]
    nblk = Sq // Q_BLOCK
    qb = q.reshape(B, nblk, Q_BLOCK, N_KV_HEADS, GQA_GROUP, HEAD_DIM).transpose(1, 0, 2, 3, 4, 5)

    def one_block(qblk):
        s = jnp.einsum('bqkgd,bskd->bkgqs', qblk, k).astype(jnp.float32) * ATTN_SCALE
        p = jax.nn.softmax(s, axis=-1)
        return jnp.einsum('bkgqs,bskd->bqkgd', p.astype(v.dtype), v)

    o = lax.map(one_block, qb)
    return o.transpose(1, 0, 2, 3, 4, 5).reshape(B, Sq, ATTN_W)


def _centred_dwconv(x, w, b):
    S = x.shape[1]
    left = CONV_W // 2
    right = CONV_W - 1 - left
    xp = jnp.pad(x, ((0, 0), (left, right), (0, 0)))
    out = b
    for j in range(CONV_W):
        out = out + w[j] * xp[:, j:j + S]
    return out


def _rglru_coeffs(xc, w_a, b_a, w_x, b_x, lam):
    B, S, _ = xc.shape
    xh = xc.reshape(B, S, RG_HEADS, RG_HD)
    r = jax.nn.sigmoid((jnp.einsum('bshi,hij->bshj', xh, w_a).reshape(B, S, RG_W) + b_a).astype(jnp.float32))
    i = jax.nn.sigmoid((jnp.einsum('bshi,hij->bshj', xh, w_x).reshape(B, S, RG_W) + b_x).astype(jnp.float32))
    log_a = RG_C * r * jax.nn.log_sigmoid(lam.astype(jnp.float32))
    a = jnp.exp(log_a)
    b = jnp.sqrt(-jnp.expm1(2.0 * log_a)) * (i * xc.astype(jnp.float32))
    return a, b


def _combine(left, right):
    a1, b1 = left
    a2, b2 = right
    return a1 * a2, a2 * b1 + b2


def _linear_scan(a, b, h0, reverse):
    a_cum, h = lax.associative_scan(_combine, (a, b), axis=1, reverse=reverse)
    return h + a_cum * h0[:, None, :]


def _split_proj(p):
    B, S = p.shape[0], p.shape[1]
    q, k, v, xr, yr = jnp.split(p, [ATTN_W, ATTN_W + KV_W, ATTN_W + 2 * KV_W,
                                   ATTN_W + 2 * KV_W + RG_W], axis=-1)
    return (q.reshape(B, S, N_HEADS, HEAD_DIM), k.reshape(B, S, N_KV_HEADS, HEAD_DIM),
            v.reshape(B, S, N_KV_HEADS, HEAD_DIM), xr, yr)


def _token_mixer(u_lat, u_ctx, row, col, w_in, q_norm, k_norm, conv_w, conv_b,
                 rg_wa, rg_ba, rg_wx, rg_bx, rg_lam, w_out, with_ctx_out):
    q_l, k_l, v_l, xr_l, yr_l = _split_proj(u_lat @ w_in)
    q_c, k_c, v_c, xr_c, yr_c = _split_proj(u_ctx @ w_in)
    q_l = _rope_2d(_rms_norm(q_l, q_norm), row, col)
    k_l = _rope_2d(_rms_norm(k_l, k_norm), row, col)
    k_c = _rms_norm(k_c, k_norm)
    k_all = jnp.concatenate([k_c, k_l], axis=1)
    v_all = jnp.concatenate([v_c, v_l], axis=1)
    attn_l = _block_attention(q_l, k_all, v_all)
    xc_c = _centred_dwconv(xr_c, conv_w, conv_b)
    xc_l = _centred_dwconv(xr_l, conv_w, conv_b)
    B = u_lat.shape[0]
    zero_state = jnp.zeros((B, RG_W), jnp.float32)
    rg_l = jnp.zeros(xc_l.shape, jnp.float32)
    ctx_scans = []
    for d, reverse in enumerate((False, True)):
        a_c, b_c = _rglru_coeffs(xc_c, rg_wa[d], rg_ba[d], rg_wx[d], rg_bx[d], rg_lam[d])
        h_c = _linear_scan(a_c, b_c, zero_state, reverse)
        ctx_scans.append(h_c)
        h0 = h_c[:, 0] if reverse else h_c[:, -1]
        a_l, b_l = _rglru_coeffs(xc_l, rg_wa[d], rg_ba[d], rg_wx[d], rg_bx[d], rg_lam[d])
        rg_l = rg_l + _linear_scan(a_l, b_l, h0, reverse)
    rg_l = (rg_l * jax.nn.gelu(yr_l).astype(jnp.float32)).astype(u_lat.dtype)
    out_l = jnp.concatenate([attn_l, rg_l], axis=-1) @ w_out
    if not with_ctx_out:
        return out_l, None
    attn_c = _block_attention(_rms_norm(q_c, q_norm), k_c, v_c)
    rg_c = ((ctx_scans[0] + ctx_scans[1]) * jax.nn.gelu(yr_c).astype(jnp.float32)).astype(u_ctx.dtype)
    out_c = jnp.concatenate([attn_c, rg_c], axis=-1) @ w_out
    return out_l, out_c


def _swiglu(x, w1, w3, w2):
    return (jax.nn.silu(x @ w1) * (x @ w3)) @ w2


def _moe(h, w_router, e_bias, w_e1, w_e3, w_e2, w_s1, w_s3, w_s2):
    n, d = h.shape
    scores = jax.nn.sigmoid((h @ w_router).astype(jnp.float32))
    biased = scores + e_bias.astype(jnp.float32)
    per_group = N_EXPERTS // N_GROUPS
    group_score = lax.top_k(biased.reshape(n, N_GROUPS, per_group), 2)[0].sum(-1)
    _, top_groups = lax.top_k(group_score, TOPK_GROUPS)
    group_mask = jax.nn.one_hot(top_groups, N_GROUPS, dtype=jnp.float32).sum(1) > 0
    expert_mask = jnp.repeat(group_mask, per_group, axis=1)
    _, idx = lax.top_k(jnp.where(expert_mask, biased, -jnp.inf), TOP_K)
    wts = jnp.take_along_axis(scores, idx, axis=-1)
    wts = wts / jnp.sum(wts, axis=-1, keepdims=True) * ROUTED_SCALE
    nk = n * TOP_K
    flat_e = idx.reshape(nk)
    flat_t = jnp.repeat(jnp.arange(n, dtype=jnp.int32), TOP_K)
    flat_w = wts.reshape(nk)
    order = jnp.argsort(flat_e)
    e_s, t_s, w_s = flat_e[order], flat_t[order], flat_w[order]
    counts = jnp.bincount(flat_e, length=N_EXPERTS)
    start = jnp.cumsum(counts) - counts
    padded = (counts + MOE_BLOCK - 1) // MOE_BLOCK * MOE_BLOCK
    pad_end = jnp.cumsum(padded)
    pad_start = pad_end - padded
    dest = pad_start[e_s] + jnp.arange(nk, dtype=jnp.int32) - start[e_s]
    n_blocks = -(-nk // MOE_BLOCK) + N_EXPERTS
    slot_tok = jnp.full((n_blocks * MOE_BLOCK,), n, jnp.int32).at[dest].set(t_s)
    slot_w = jnp.zeros((n_blocks * MOE_BLOCK,), jnp.float32).at[dest].set(w_s)
    blk_e = jnp.minimum(jnp.searchsorted(pad_end, jnp.arange(n_blocks, dtype=jnp.int32) * MOE_BLOCK,
                                         side='right'), N_EXPERTS - 1)
    h_pad = jnp.concatenate([h, jnp.zeros((1, d), h.dtype)], axis=0)

    def body(acc, blk):
        tok, wt, e = blk
        xb = h_pad[tok]
        yb = _swiglu(xb, w_e1[e], w_e3[e], w_e2[e]) * wt[:, None].astype(h.dtype)
        return acc.at[tok].add(yb), None

    acc, _ = lax.scan(body, jnp.zeros((n + 1, d), h.dtype),
                      (slot_tok.reshape(n_blocks, MOE_BLOCK), slot_w.reshape(n_blocks, MOE_BLOCK), blk_e))
    return acc[:n] + _swiglu(h, w_s1, w_s3, w_s2)


def setup_inputs(seed: int = 0) -> dict:
    key = jax.random.key(seed)
    ks = jax.random.split(key, 32)
    f32 = jnp.float32
    L, D = DEPTH, D_MODEL

    def nrm(k, shape, scale):
        return jax.random.normal(k, shape, f32) * scale

    u = jax.random.uniform(ks[13], (L, 2, RG_W), f32, 0.9, 0.999)
    a0 = u ** (1.0 / RG_C)
    return {
        "x": nrm(ks[0], (BATCH, SEQ, D), 1.0),
        "c": nrm(ks[1], (BATCH, D), 1.0),
        "ctx": nrm(ks[2], (BATCH, CTX_LEN, D), 1.0),
        "c_ctx": nrm(ks[3], (D,), 1.0),
        "w_mod": nrm(ks[4], (L, D, 6 * D), 0.5 * D ** -0.5),
        "b_mod": nrm(ks[5], (L, 6 * D), 0.02),
        "w_in": nrm(ks[6], (L, D, PROJ_W), D ** -0.5),
        "q_norm": 1.0 + nrm(ks[7], (L, HEAD_DIM), 0.02),
        "k_norm": 1.0 + nrm(ks[8], (L, HEAD_DIM), 0.02),
        "conv_w": nrm(ks[9], (L, CONV_W, RG_W), CONV_W ** -0.5),
        "conv_b": nrm(ks[10], (L, RG_W), 0.02),
        "rg_wa": nrm(ks[11], (L, 2, RG_HEADS, RG_HD, RG_HD), RG_HD ** -0.5),
        "rg_ba": nrm(ks[12], (L, 2, RG_W), 0.02),
        "rg_wx": nrm(ks[14], (L, 2, RG_HEADS, RG_HD, RG_HD), RG_HD ** -0.5),
        "rg_bx": nrm(ks[15], (L, 2, RG_W), 0.02),
        "rg_lam": jnp.log(a0) - jnp.log1p(-a0),
        "w_out": nrm(ks[16], (L, MIX_W, D), MIX_W ** -0.5 * DEEPNORM_BETA),
        "ln1_g": 1.0 + nrm(ks[17], (L, D), 0.02),
        "ln1_b": nrm(ks[18], (L, D), 0.02),
        "w_router": nrm(ks[19], (L, D, N_EXPERTS), D ** -0.5),
        "e_bias": nrm(ks[20], (L, N_EXPERTS), 0.01),
        "w_e1": nrm(ks[21], (L, N_EXPERTS, D, EXPERT_FF), D ** -0.5),
        "w_e3": nrm(ks[22], (L, N_EXPERTS, D, EXPERT_FF), D ** -0.5),
        "w_e2": nrm(ks[23], (L, N_EXPERTS, EXPERT_FF, D), EXPERT_FF ** -0.5 * DEEPNORM_BETA),
        "w_s1": nrm(ks[24], (L, D, SHARED_FF), D ** -0.5),
        "w_s3": nrm(ks[25], (L, D, SHARED_FF), D ** -0.5),
        "w_s2": nrm(ks[26], (L, SHARED_FF, D), SHARED_FF ** -0.5 * DEEPNORM_BETA),
        "ln2_g": 1.0 + nrm(ks[27], (L, D), 0.02),
        "ln2_b": nrm(ks[28], (L, D), 0.02),
    }


def reference(x, c, ctx, c_ctx, w_mod, b_mod, w_in, q_norm, k_norm, conv_w, conv_b,
              rg_wa, rg_ba, rg_wx, rg_bx, rg_lam, w_out, ln1_g, ln1_b, w_router, e_bias,
              w_e1, w_e3, w_e2, w_s1, w_s3, w_s2, ln2_g, ln2_b):
    B, S, D = x.shape
    n_ctx = ctx.shape[1]
    rows = S // GRID_W
    row = jnp.repeat(jnp.arange(rows, dtype=jnp.int32), GRID_W)
    col = jnp.tile(jnp.arange(GRID_W, dtype=jnp.int32), rows)
    h_lat, h_ctx = x, ctx
    for l in range(DEPTH):
        update_ctx = l + 1 < DEPTH
        mod_lat = jax.nn.silu(c) @ w_mod[l] + b_mod[l]
        mod_ctx = jax.nn.silu(c_ctx) @ w_mod[l] + b_mod[l]
        sh1, sc1, g1, sh2, sc2, g2 = jnp.split(mod_lat[:, None, :], 6, axis=-1)
        csh1, csc1, cg1, csh2, csc2, cg2 = jnp.split(mod_ctx, 6, axis=-1)
        u_lat = h_lat * (1.0 + sc1) + sh1
        u_ctx = h_ctx * (1.0 + csc1) + csh1
        mix_lat, mix_ctx = _token_mixer(u_lat, u_ctx, row, col, w_in[l], q_norm[l], k_norm[l],
                                        conv_w[l], conv_b[l], rg_wa[l], rg_ba[l], rg_wx[l],
                                        rg_bx[l], rg_lam[l], w_out[l], update_ctx)
        h_lat = _layer_norm(DEEPNORM_ALPHA * h_lat + g1 * mix_lat, ln1_g[l], ln1_b[l])
        v_lat = (h_lat * (1.0 + sc2) + sh2).reshape(B * S, D)
        if update_ctx:
            h_ctx = _layer_norm(DEEPNORM_ALPHA * h_ctx + cg1 * mix_ctx, ln1_g[l], ln1_b[l])
            v_ctx = (h_ctx * (1.0 + csc2) + csh2).reshape(B * n_ctx, D)
            ff = _moe(jnp.concatenate([v_ctx, v_lat], axis=0), w_router[l], e_bias[l],
                      w_e1[l], w_e3[l], w_e2[l], w_s1[l], w_s3[l], w_s2[l])
            ff_ctx = ff[:B * n_ctx].reshape(B, n_ctx, D)
            ff_lat = ff[B * n_ctx:].reshape(B, S, D)
            h_ctx = _layer_norm(DEEPNORM_ALPHA * h_ctx + cg2 * ff_ctx, ln2_g[l], ln2_b[l])
        else:
            ff_lat = _moe(v_lat, w_router[l], e_bias[l], w_e1[l], w_e3[l], w_e2[l],
                          w_s1[l], w_s3[l], w_s2[l]).reshape(B, S, D)
        h_lat = _layer_norm(DEEPNORM_ALPHA * h_lat + g2 * ff_lat, ln2_g[l], ln2_b[l])
    return h_lat
```

```python
import functools
import math

import jax
import jax.numpy as jnp
from jax import lax
from jax.experimental import pallas as pl
from jax.experimental.pallas import tpu as pltpu

F32 = jnp.float32
BF16 = jnp.bfloat16

GRID_W = 64
HEAD_DIM = 128
N_HEADS = 8
N_KV_HEADS = 2
GQA_GROUP = N_HEADS // N_KV_HEADS
ATTN_W = N_HEADS * HEAD_DIM
KV_W = N_KV_HEADS * HEAD_DIM
ROPE_THETA = 10000.0
RG_W = 1024
RG_HEADS = 8
RG_HD = RG_W // RG_HEADS
RG_C = 8.0
PROJ_W = ATTN_W + 2 * KV_W + 2 * RG_W
N_EXPERTS = 64
N_GROUPS = 8
GROUP_SIZE = N_EXPERTS // N_GROUPS
TOPK_GROUPS = 4
TOP_K = 8
ROUTED_SCALE = 2.5
NORM_EPS = 1e-6
DEPTH = 1
DEEPNORM_ALPHA = (2.0 * DEPTH) ** 0.25
LOG2E = 1.4426950408889634

LANES = 128
SUBLANES = 8
VMEM_LIMIT = 56 * 1024 * 1024

NEG_BIG = -1e30


def _cparams(*sem):
    return pltpu.CompilerParams(dimension_semantics=sem, vmem_limit_bytes=VMEM_LIMIT)


def _const_spec(shape):
    nd = len(shape)
    return pl.BlockSpec(shape, lambda *_: (0,) * nd)


def _mod_kernel(ct_ref, w_ref, b_ref, o_ref, sb_ref, *, tn):
    d = w_ref.shape[0]

    @pl.when(pl.program_id(0) == 0)
    def _():
        ct = ct_ref[...]
        s = ct * jax.nn.sigmoid(ct)
        sb_ref[0] = jnp.broadcast_to(s[:, 0:1], (d, LANES))
        sb_ref[1] = jnp.broadcast_to(s[:, 1:2], (d, LANES))

    for c in range(tn // LANES):
        sl = slice(c * LANES, (c + 1) * LANES)
        wc = w_ref[:, sl]
        bc = b_ref[:, sl]
        o0 = jnp.sum(wc * sb_ref[0], axis=0, keepdims=True) + bc
        o1 = jnp.sum(wc * sb_ref[1], axis=0, keepdims=True) + bc
        o_ref[:, sl] = jnp.concatenate(
            [o0, o1, jnp.zeros((SUBLANES - 2, LANES), F32)], axis=0)


def _modulation(c2t, w_mod, b_mod):
    d, n = w_mod.shape
    tn = 1024
    return pl.pallas_call(
        functools.partial(_mod_kernel, tn=tn),
        grid=(n // tn,),
        in_specs=[_const_spec((d, 2)),
                  pl.BlockSpec((d, tn), lambda j: (0, j)),
                  pl.BlockSpec((1, tn), lambda j: (0, j))],
        out_specs=pl.BlockSpec((SUBLANES, tn), lambda j: (0, j)),
        out_shape=jax.ShapeDtypeStruct((SUBLANES, n), F32),
        scratch_shapes=[pltpu.VMEM((2, d, LANES), F32)],
        compiler_params=_cparams("arbitrary"),
        name="modulation",
    )(c2t, w_mod, b_mod)


def _swap_half(y):
    lane = lax.broadcasted_iota(jnp.int32, y.shape, 1)
    return jnp.where((lane % 64) < 32,
                     pltpu.roll(y, LANES - 32, 1), pltpu.roll(y, 32, 1))


def _norm_rope(ph, g, cos, sin_signed, scale):
    ms = jnp.mean(ph * ph, axis=-1, keepdims=True)
    y = (ph * lax.rsqrt(ms + NORM_EPS)) * g
    y = y * cos + _swap_half(y) * sin_signed
    if scale != 1.0:
        y = y * scale
    return y


def _inproj_kernel(x_ref, sc_ref, sh_ref, w_ref, qg_ref, kg_ref, cos_ref, sin_ref,
                   q_ref, k_ref, v_ref, xr_ref, yr_ref, *, q_scale):
    u = (x_ref[...] * (1.0 + sc_ref[...]) + sh_ref[...]).astype(BF16)
    cos = cos_ref[...]
    sin = sin_ref[...]
    o = 0
    pq = jnp.dot(u, w_ref[:, o:o + ATTN_W], preferred_element_type=F32)
    for h in range(N_HEADS):
        sl = slice(h * HEAD_DIM, (h + 1) * HEAD_DIM)
        q_ref[:, sl] = _norm_rope(pq[:, sl], qg_ref[...], cos, sin, q_scale).astype(BF16)
    o += ATTN_W
    pk = jnp.dot(u, w_ref[:, o:o + KV_W], preferred_element_type=F32)
    for h in range(N_KV_HEADS):
        sl = slice(h * HEAD_DIM, (h + 1) * HEAD_DIM)
        k_ref[:, sl] = _norm_rope(pk[:, sl], kg_ref[...], cos, sin, 1.0).astype(BF16)
    o += KV_W
    v_ref[...] = jnp.dot(u, w_ref[:, o:o + KV_W], preferred_element_type=F32).astype(BF16)
    o += KV_W
    xr_ref[...] = jnp.dot(u, w_ref[:, o:o + RG_W], preferred_element_type=F32)
    o += RG_W
    yr_ref[...] = jnp.dot(u, w_ref[:, o:o + RG_W], preferred_element_type=F32)


def _in_projection(x, sc, sh, w_in, qg, kg, cos, sin, tm):
    s, d = x.shape
    row = lambda i: (i, 0)
    return pl.pallas_call(
        functools.partial(_inproj_kernel, q_scale=HEAD_DIM ** -0.5 * LOG2E),
        grid=(s // tm,),
        in_specs=[pl.BlockSpec((tm, d), row),
                  _const_spec((1, d)), _const_spec((1, d)),
                  pl.BlockSpec((d, PROJ_W), lambda i: (0, 0), pipeline_mode=pl.Buffered(1)),
                  _const_spec((1, HEAD_DIM)), _const_spec((1, HEAD_DIM)),
                  pl.BlockSpec((tm, HEAD_DIM), row), pl.BlockSpec((tm, HEAD_DIM), row)],
        out_specs=[pl.BlockSpec((tm, ATTN_W), row), pl.BlockSpec((tm, KV_W), row),
                   pl.BlockSpec((tm, KV_W), row), pl.BlockSpec((tm, RG_W), row),
                   pl.BlockSpec((tm, RG_W), row)],
        out_shape=[jax.ShapeDtypeStruct((s, ATTN_W), BF16),
                   jax.ShapeDtypeStruct((s, KV_W), BF16),
                   jax.ShapeDtypeStruct((s, KV_W), BF16),
                   jax.ShapeDtypeStruct((s, RG_W), F32),
                   jax.ShapeDtypeStruct((s, RG_W), F32)],
        compiler_params=_cparams("arbitrary"),
        name="in_projection",
    )(x, sc, sh, w_in, qg, kg, cos, sin)


def _attn_kernel(q_ref, kc_ref, vc_ref, kl_ref, vl_ref, o_ref, *, tq, tk, n_chunks):
    q = q_ref[...]
    qs = jnp.concatenate(
        [q[:, g * HEAD_DIM:(g + 1) * HEAD_DIM] for g in range(GQA_GROUP)], axis=0)
    rows = GQA_GROUP * tq

    def chunk(k, v, m, l, acc):
        s = lax.dot_general(qs, k, (((1,), (1,)), ((), ())), preferred_element_type=F32)
        m_new = jnp.maximum(m, jnp.max(s, axis=-1, keepdims=True))
        alpha = jnp.exp2(m - m_new)
        p = jnp.exp2(s - m_new)
        l = alpha * l + jnp.sum(p, axis=-1, keepdims=True)
        acc = alpha * acc + jnp.dot(p.astype(BF16), v, preferred_element_type=F32)
        return m_new, l, acc

    m0 = jnp.full((rows, 1), NEG_BIG, F32)
    l0 = jnp.zeros((rows, 1), F32)
    a0 = jnp.zeros((rows, HEAD_DIM), F32)
    carry = chunk(kc_ref[...], vc_ref[...], m0, l0, a0)

    def body(j, carry):
        start = pl.multiple_of(j * tk, tk)
        return chunk(kl_ref[pl.ds(start, tk), :], vl_ref[pl.ds(start, tk), :], *carry)

    _, l, acc = lax.fori_loop(0, n_chunks, body, carry)
    out = acc / l
    for g in range(GQA_GROUP):
        o_ref[:, g * HEAD_DIM:(g + 1) * HEAD_DIM] = out[g * tq:(g + 1) * tq].astype(BF16)


def _attention(q, k_c, v_c, k_l, v_l, tq, tk):
    s = q.shape[0]
    n_ctx = k_c.shape[0]
    gw = GQA_GROUP * HEAD_DIM
    return pl.pallas_call(
        functools.partial(_attn_kernel, tq=tq, tk=tk, n_chunks=s // tk),
        grid=(N_KV_HEADS, s // tq),
        in_specs=[pl.BlockSpec((tq, gw), lambda h, i: (i, h)),
                  pl.BlockSpec((n_ctx, HEAD_DIM), lambda h, i: (0, h)),
                  pl.BlockSpec((n_ctx, HEAD_DIM), lambda h, i: (0, h)),
                  pl.BlockSpec((s, HEAD_DIM), lambda h, i: (0, h)),
                  pl.BlockSpec((s, HEAD_DIM), lambda h, i: (0, h))],
        out_specs=pl.BlockSpec((tq, gw), lambda h, i: (i, h)),
        out_shape=jax.ShapeDtypeStruct((s, ATTN_W), BF16),
        compiler_params=_cparams("arbitrary", "arbitrary"),
        name="attention",
    )(q, k_c, v_c, k_l, v_l)


def _log_sigmoid(x):
    return jnp.minimum(x, 0.0) - jnp.log1p(jnp.exp(-jnp.abs(x)))


def _rg_kernel(x_ref, xp_ref, xn_ref, cw_ref, cb_ref, wg_ref, bg_ref, lam_ref, h0_ref,
               *rest, reverse, final, t, n_chunks):
    if final:
        hf_ref, yr_ref, o_ref, a_scr, b_scr, hc_scr, h_scr = rest
    else:
        o_ref, a_scr, b_scr, hc_scr = rest
        h_scr = o_ref
    i = pl.program_id(0)
    c = (n_chunks - 1 - i) if reverse else i
    w = x_ref.shape[1]

    @pl.when(i == 0)
    def _():
        hc_scr[...] = jnp.broadcast_to(h0_ref[...], (SUBLANES, w))

    x = x_ref[...]
    row = lax.broadcasted_iota(jnp.int32, (t, w), 0)
    pm = jnp.where(c == 0, 0.0, 1.0).astype(F32)
    nm = jnp.where(c == n_chunks - 1, 0.0, 1.0).astype(F32)
    p6 = xp_ref[SUBLANES - 2:SUBLANES - 1, :] * pm
    p7 = xp_ref[SUBLANES - 1:SUBLANES, :] * pm
    n0 = xn_ref[0:1, :] * nm
    x_m1 = jnp.where(row == 0, p7, pltpu.roll(x, 1, 0))
    x_m2 = jnp.where(row == 0, p6, jnp.where(row == 1, p7, pltpu.roll(x, 2, 0)))
    x_p1 = jnp.where(row == t - 1, n0, pltpu.roll(x, t - 1, 0))
    xc = cb_ref[...] + cw_ref[0:1, :] * x_m2
    xc = xc + cw_ref[1:2, :] * x_m1
    xc = xc + cw_ref[2:3, :] * x
    xc = xc + cw_ref[3:4, :] * x_p1

    xcb = xc.astype(BF16)
    clam = RG_C * _log_sigmoid(lam_ref[...])
    for h in range(RG_HEADS):
        sl = slice(h * RG_HD, (h + 1) * RG_HD)
        g = jnp.dot(xcb[:, sl], wg_ref[h], preferred_element_type=F32)
        r = jax.nn.sigmoid(g[:, :RG_HD] + bg_ref[0:1, sl])
        gi = jax.nn.sigmoid(g[:, RG_HD:] + bg_ref[1:2, sl])
        log_a = r * clam[:, sl]
        a = jnp.exp(log_a)
        a_scr[:, sl] = a
        b_scr[:, sl] = jnp.sqrt(-jnp.tanh(log_a) * (a * a + 1.0)) * (gi * xc[:, sl])

    srow = lax.broadcasted_iota(jnp.int32, (SUBLANES, w), 0)
    n_tiles = t // SUBLANES

    def tile_body(j, hprev):
        tile = (n_tiles - 1 - j) if reverse else j
        start = pl.multiple_of(tile * SUBLANES, SUBLANES)
        a = a_scr[pl.ds(start, SUBLANES), :]
        b = b_scr[pl.ds(start, SUBLANES), :]
        for k in (1, 2, 4):
            if reverse:
                keep = srow < SUBLANES - k
                shift = SUBLANES - k
            else:
                keep = srow >= k
                shift = k
            a_sh = jnp.where(keep, pltpu.roll(a, shift, 0), 1.0)
            b_sh = jnp.where(keep, pltpu.roll(b, shift, 0), 0.0)
            b = a * b_sh + b
            a = a * a_sh
        hh = a * hprev + b
        h_scr[pl.ds(start, SUBLANES), :] = hh
        last = hh[0:1, :] if reverse else hh[SUBLANES - 1:SUBLANES, :]
        return jnp.broadcast_to(last, (SUBLANES, w))

    hc_scr[...] = lax.fori_loop(0, n_tiles, tile_body, hc_scr[...])

    if final:
        gate = jax.nn.gelu(yr_ref[...], approximate=True)
        o_ref[...] = ((hf_ref[...] + h_scr[...]) * gate).astype(o_ref.dtype)


def _rg_scan(xr, conv_w, conv_b, wg, bg, lam, h0, *, reverse, t, hf=None, yr=None):
    s, w = xr.shape
    n_chunks = s // t
    final = hf is not None
    tb = t // SUBLANES
    last_blk = s // SUBLANES - 1
    if reverse:
        cidx = lambda i: n_chunks - 1 - i
    else:
        cidx = lambda i: i
    chunk_spec = pl.BlockSpec((t, w), lambda i: (cidx(i), 0))
    in_specs = [chunk_spec,
                pl.BlockSpec((SUBLANES, w), lambda i: (jnp.maximum(cidx(i) * tb - 1, 0), 0)),
                pl.BlockSpec((SUBLANES, w), lambda i: (jnp.minimum((cidx(i) + 1) * tb, last_blk), 0)),
                _const_spec((4, w)), _const_spec((1, w)),
                _const_spec((RG_HEADS, RG_HD, 2 * RG_HD)), _const_spec((2, w)),
                _const_spec((1, w)), _const_spec((1, w))]
    args = [xr, xr, xr, conv_w, conv_b, wg, bg, lam, h0]
    scratch = [pltpu.VMEM((t, w), F32), pltpu.VMEM((t, w), F32), pltpu.VMEM((SUBLANES, w), F32)]
    if final:
        in_specs += [chunk_spec, chunk_spec]
        args += [hf, yr]
        scratch.append(pltpu.VMEM((t, w), F32))
        out_dtype = BF16
    else:
        out_dtype = F32
    return pl.pallas_call(
        functools.partial(_rg_kernel, reverse=reverse, final=final, t=t, n_chunks=n_chunks),
        grid=(n_chunks,),
        in_specs=in_specs,
        out_specs=chunk_spec,
        out_shape=jax.ShapeDtypeStruct((s, w), out_dtype),
        scratch_shapes=scratch,
        compiler_params=_cparams("arbitrary"),
        name="rglru_bwd" if reverse else "rglru_fwd",
    )(*args)


def _layer_norm(y, g, b):
    mu = jnp.mean(y, axis=-1, keepdims=True)
    yc = y - mu
    var = jnp.mean(yc * yc, axis=-1, keepdims=True)
    return yc * lax.rsqrt(var + NORM_EPS) * g + b


def _outproj_kernel(attn_ref, rg_ref, x_ref, w_ref, g1_ref, lg_ref, lb_ref, sc2_ref, sh2_ref,
                    h1_ref, v_ref):
    mix = jnp.dot(attn_ref[...], w_ref[0:ATTN_W, :], preferred_element_type=F32)
    mix = mix + jnp.dot(rg_ref[...], w_ref[ATTN_W:, :], preferred_element_type=F32)
    h1 = _layer_norm(DEEPNORM_ALPHA * x_ref[...] + g1_ref[...] * mix, lg_ref[...], lb_ref[...])
    h1_ref[...] = h1
    v_ref[...] = h1 * (1.0 + sc2_ref[...]) + sh2_ref[...]


def _out_projection(attn, rg, x, w_out, g1, ln_g, ln_b, sc2, sh2, tm):
    s, d = x.shape
    row = lambda i: (i, 0)
    vec = _const_spec((1, d))
    return pl.pallas_call(
        _outproj_kernel,
        grid=(s // tm,),
        in_specs=[pl.BlockSpec((tm, ATTN_W), row), pl.BlockSpec((tm, RG_W), row),
                  pl.BlockSpec((tm, d), row),
                  pl.BlockSpec((ATTN_W + RG_W, d), lambda i: (0, 0), pipeline_mode=pl.Buffered(1)),
                  vec, vec, vec, vec, vec],
        out_specs=[pl.BlockSpec((tm, d), row), pl.BlockSpec((tm, d), row)],
        out_shape=[jax.ShapeDtypeStruct((s, d), F32), jax.ShapeDtypeStruct((s, d), F32)],
        compiler_params=_cparams("arbitrary"),
        name="out_projection",
    )(attn, rg, x, w_out, g1, ln_g, ln_b, sc2, sh2)


def _first_index_of_max(x, iota_f, axis):
    mx = jnp.max(x, axis=axis, keepdims=True)
    idx = jnp.min(jnp.where(x == mx, iota_f, float(N_EXPERTS)), axis=axis, keepdims=True)
    return mx, idx


def _router_kernel(v_ref, whi_ref, wlo_ref, eb_ref, idx_ref, wts_ref):
    v = v_ref[...]
    tm = v.shape[0]
    v_hi = v.astype(BF16)
    v_lo = (v - v_hi.astype(F32)).astype(BF16)
    logits = jnp.dot(v_hi, whi_ref[...], preferred_element_type=F32)
    logits = logits + jnp.dot(v_lo, whi_ref[...], preferred_element_type=F32)
    logits = logits + jnp.dot(v_hi, wlo_ref[...], preferred_element_type=F32)
    lt = logits.T[0:N_EXPERTS, :]
    scores = jax.nn.sigmoid(lt)
    biased = scores + eb_ref[...]
    neg_inf = float("-inf")

    ig = lax.broadcasted_iota(jnp.int32, (GROUP_SIZE, tm), 0).astype(F32)
    groups = [biased[g * GROUP_SIZE:(g + 1) * GROUP_SIZE, :] for g in range(N_GROUPS)]
    gscore = []
    for bg in groups:
        top1, i1 = _first_index_of_max(bg, ig, 0)
        top2 = jnp.max(jnp.where(ig == i1, neg_inf, bg), axis=0, keepdims=True)
        gscore.append(top1 + top2)

    masked = []
    for g in range(N_GROUPS):
        ahead = jnp.zeros((1, tm), F32)
        for o in range(N_GROUPS):
            if o == g:
                continue
            before = (gscore[o] >= gscore[g]) if o < g else (gscore[o] > gscore[g])
            ahead = ahead + jnp.where(before, 1.0, 0.0)
        keep = jnp.broadcast_to(ahead < TOPK_GROUPS, (GROUP_SIZE, tm))
        masked.append(jnp.where(keep, groups[g], neg_inf))
    masked = jnp.concatenate(masked, axis=0)

    ie = lax.broadcasted_iota(jnp.int32, masked.shape, 0).astype(F32)
    ws = []
    for k in range(TOP_K):
        _, ei = _first_index_of_max(masked, ie, 0)
        hit = ie == ei
        idx_ref[k:k + 1, :] = ei.astype(jnp.int32)
        ws.append(jnp.sum(jnp.where(hit, scores, 0.0), axis=0, keepdims=True))
        masked = jnp.where(hit, neg_inf, masked)
    total = ws[0]
    for k in range(1, TOP_K):
        total = total + ws[k]
    for k in range(TOP_K):
        wts_ref[k:k + 1, :] = ws[k] / total * ROUTED_SCALE


def _router(v, w_hi, w_lo, e_bias_col, tm):
    s, d = v.shape
    return pl.pallas_call(
        _router_kernel,
        grid=(s // tm,),
        in_specs=[pl.BlockSpec((tm, d), lambda i: (i, 0)),
                  _const_spec((d, LANES)), _const_spec((d, LANES)),
                  _const_spec((N_EXPERTS, 1))],
        out_specs=[pl.BlockSpec((TOP_K, tm), lambda i: (0, i)),
                   pl.BlockSpec((TOP_K, tm), lambda i: (0, i))],
        out_shape=[jax.ShapeDtypeStruct((TOP_K, s), jnp.int32),
                   jax.ShapeDtypeStruct((TOP_K, s), F32)],
        compiler_params=_cparams("arbitrary"),
        name="router",
    )(v, w_hi, w_lo, e_bias_col)


def _expert_kernel(blk_e_ref, blk_n_ref, tok_v, dst_v, v_hbm, w1_ref, w3_ref, w2_ref,
                   y_hbm, tok_s, dst_s, xbuf, ybuf, w1b, w3b, w2b, sem_i, sem_g, sem_s, *, bm):
    b = pl.program_id(0)
    n_valid = blk_n_ref[b]

    @pl.when(n_valid > 0)
    def _():
        cp_t = pltpu.make_async_copy(tok_v.at[0, 0], tok_s, sem_i.at[0])
        cp_d = pltpu.make_async_copy(dst_v.at[0, 0], dst_s, sem_i.at[1])
        cp_t.start()
        cp_d.start()

        prev_e = blk_e_ref[jnp.maximum(b - 1, 0)]

        @pl.when(jnp.logical_or(b == 0, blk_e_ref[b] != prev_e))
        def _():
            w1b[...] = w1_ref[...].astype(BF16)
            w3b[...] = w3_ref[...].astype(BF16)
            w2b[...] = w2_ref[...].astype(BF16)

        cp_t.wait()
        cp_d.wait()

        def gather(r):
            return pltpu.make_async_copy(v_hbm.at[pl.ds(tok_s[r], 1), :],
                                         xbuf.at[pl.ds(r, 1), :], sem_g)

        def scatter(r):
            return pltpu.make_async_copy(ybuf.at[pl.ds(r, 1), :],
                                         y_hbm.at[pl.ds(dst_s[r], 1), :], sem_s)

        def start_gather(r, carry):
            gather(r).start()
            return carry

        def wait_gather(r, carry):
            gather(r).wait()
            return carry

        lax.fori_loop(0, bm, start_gather, 0)
        lax.fori_loop(0, bm, wait_gather, 0)

        xb = xbuf[...].astype(BF16)
        h1 = jnp.dot(xb, w1b[...], preferred_element_type=F32)
        h3 = jnp.dot(xb, w3b[...], preferred_element_type=F32)
        act = (h1 * jax.nn.sigmoid(h1) * h3).astype(BF16)
        ybuf[...] = jnp.dot(act, w2b[...], preferred_element_type=F32)

        def start_scatter(r, carry):
            scatter(r).start()
            return carry

        def wait_scatter(r, carry):
            scatter(r).wait()
            return carry

        lax.fori_loop(0, n_valid, start_scatter, 0)
        lax.fori_loop(0, n_valid, wait_scatter, 0)


def _experts(blk_e, blk_n, slot_tok, slot_dst, v, w_e1, w_e3, w_e2, bm):
    s, d = v.shape
    n_blocks = slot_tok.shape[0]
    ff = w_e1.shape[-1]
    wspec1 = pl.BlockSpec((None, d, ff), lambda b, be, nu: (be[b], 0, 0))
    wspec2 = pl.BlockSpec((None, ff, d), lambda b, be, nu: (be[b], 0, 0))
    any_spec = pl.BlockSpec(memory_space=pl.ANY)
    idx_spec = pl.BlockSpec((1, 1, bm), lambda b, be, nu: (b, 0, 0))
    grid_spec = pltpu.PrefetchScalarGridSpec(
        num_scalar_prefetch=2,
        grid=(n_blocks,),
        in_specs=[idx_spec, idx_spec, any_spec, wspec1, wspec1, wspec2],
        out_specs=any_spec,
        scratch_shapes=[pltpu.SMEM((bm,), jnp.int32), pltpu.SMEM((bm,), jnp.int32),
                        pltpu.VMEM((bm, d), F32), pltpu.VMEM((bm, d), F32),
                        pltpu.VMEM((d, ff), BF16), pltpu.VMEM((d, ff), BF16),
                        pltpu.VMEM((ff, d), BF16),
                        pltpu.SemaphoreType.DMA((2,)), pltpu.SemaphoreType.DMA,
                        pltpu.SemaphoreType.DMA])
    return pl.pallas_call(
        functools.partial(_expert_kernel, bm=bm),
        grid_spec=grid_spec,
        out_shape=jax.ShapeDtypeStruct((s * TOP_K, d), F32),
        compiler_params=_cparams("arbitrary"),
        name="experts",
    )(blk_e, blk_n, slot_tok, slot_dst, v, w_e1, w_e3, w_e2)


def _dispatch_tables(idx, s, bm):
    nk = s * TOP_K
    n_blocks = nk // bm + N_EXPERTS
    flat_e = idx.reshape(nk)
    order = jnp.argsort(flat_e).astype(jnp.int32)
    counts = jnp.sum(flat_e[None, :] == jnp.arange(N_EXPERTS, dtype=jnp.int32)[:, None],
                     axis=1, dtype=jnp.int32)
    start = jnp.cumsum(counts) - counts
    padded = (counts + bm - 1) // bm * bm
    pad_end = jnp.cumsum(padded)
    pad_start = pad_end - padded
    blk_first = jnp.arange(n_blocks, dtype=jnp.int32) * bm
    blk_e = jnp.minimum(jnp.searchsorted(pad_end, blk_first, side="right"),
                        N_EXPERTS - 1).astype(jnp.int32)
    n_used = pad_end[-1] // bm
    blk_e = jnp.where(jnp.arange(n_blocks) < n_used, blk_e, blk_e[jnp.maximum(n_used - 1, 0)])
    p = jnp.arange(n_blocks * bm, dtype=jnp.int32)
    e_p = blk_e[p // bm]
    off = p - pad_start[e_p]
    valid = jnp.logical_and(off < counts[e_p], p < pad_end[-1])
    slot_f = order[jnp.clip(start[e_p] + off, 0, nk - 1)]
    slot_tok = jnp.where(valid, slot_f % s, 0).astype(jnp.int32)
    slot_dst = jnp.where(valid, slot_f, 0).astype(jnp.int32)
    blk_n = jnp.sum(valid.reshape(n_blocks, bm), axis=1, dtype=jnp.int32)
    return blk_e, blk_n, slot_tok.reshape(n_blocks, 1, bm), slot_dst.reshape(n_blocks, 1, bm)


def _combine_kernel(*refs):
    y_refs = refs[:TOP_K]
    (wts_ref, v_ref, h1_ref, g2_ref, lg_ref, lb_ref, ws1_ref, ws3_ref, ws2_ref, o_ref) = refs[TOP_K:]
    tm = v_ref.shape[0]
    wpad = jnp.concatenate([wts_ref[...], jnp.zeros((LANES - TOP_K, tm), F32)], axis=0)
    wt = wpad.T
    vb = v_ref[...].astype(BF16)
    a1 = jnp.dot(vb, ws1_ref[...], preferred_element_type=F32)
    a3 = jnp.dot(vb, ws3_ref[...], preferred_element_type=F32)
    act = (a1 * jax.nn.sigmoid(a1) * a3).astype(BF16)
    shared = jnp.dot(act, ws2_ref[...], preferred_element_type=F32)
    ff = y_refs[0][...] * wt[:, 0:1]
    for k in range(1, TOP_K):
        ff = ff + y_refs[k][...] * wt[:, k:k + 1]
    ff = ff + shared
    o_ref[...] = _layer_norm(DEEPNORM_ALPHA * h1_ref[...] + g2_ref[...] * ff,
                             lg_ref[...], lb_ref[...])


def _combine(y_slots, wts, v, h1, g2, ln_g, ln_b, w_s1, w_s3, w_s2, tm):
    s, d = v.shape
    nb = s // tm
    row = lambda i: (i, 0)
    vec = _const_spec((1, d))
    y_specs = [pl.BlockSpec((tm, d), (lambda i, k=k: (k * nb + i, 0))) for k in range(TOP_K)]
    return pl.pallas_call(
        _combine_kernel,
        grid=(nb,),
        in_specs=y_specs + [pl.BlockSpec((TOP_K, tm), lambda i: (0, i)),
                            pl.BlockSpec((tm, d), row), pl.BlockSpec((tm, d), row),
                            vec, vec, vec,
                            _const_spec(w_s1.shape), _const_spec(w_s3.shape),
                            _const_spec(w_s2.shape)],
        out_specs=pl.BlockSpec((tm, d), row),
        out_shape=jax.ShapeDtypeStruct((s, d), F32),
        compiler_params=_cparams("arbitrary"),
        name="combine",
    )(*([y_slots] * TOP_K), wts, v, h1, g2, ln_g, ln_b, w_s1, w_s3, w_s2)


def _rope_tables(s):
    half = HEAD_DIM // 4
    inv_freq = ROPE_THETA ** (-jnp.arange(half, dtype=F32) / half)
    t = jnp.arange(s, dtype=jnp.int32)
    ang_r = (t // GRID_W).astype(F32)[:, None] * inv_freq[None, :]
    ang_c = (t % GRID_W).astype(F32)[:, None] * inv_freq[None, :]
    cos = jnp.concatenate([jnp.cos(ang_r)] * 2 + [jnp.cos(ang_c)] * 2, axis=-1)
    sin = jnp.concatenate([-jnp.sin(ang_r), jnp.sin(ang_r), -jnp.sin(ang_c), jnp.sin(ang_c)], axis=-1)
    return cos, sin


def kernel(x, c, ctx, c_ctx, w_mod, b_mod, w_in, q_norm, k_norm, conv_w, conv_b, rg_wa, rg_ba,
           rg_wx, rg_bx, rg_lam, w_out, ln1_g, ln1_b, w_router, e_bias, w_e1, w_e3, w_e2,
           w_s1, w_s3, w_s2, ln2_g, ln2_b):
    assert x.shape[0] == 1 and w_mod.shape[0] == DEPTH
    _, s, d = x.shape
    n_ctx = ctx.shape[1]
    x2 = x[0]

    c2t = jnp.stack([c[0], c_ctx], axis=1)
    mod = _modulation(c2t, w_mod[0], b_mod[0].reshape(1, -1))
    sh1, sc1, g1, sh2, sc2, g2 = [mod[0:1, j * d:(j + 1) * d] for j in range(6)]
    csh1, csc1 = mod[1:2, 0:d], mod[1:2, d:2 * d]

    w_in_b = w_in[0].astype(BF16)
    qg = q_norm[0].reshape(1, HEAD_DIM)
    kg = k_norm[0].reshape(1, HEAD_DIM)
    cos, sin = _rope_tables(s)
    tm_in = min(512, s)
    q_l, k_l, v_l, xr_l, yr_l = _in_projection(x2, sc1, sh1, w_in_b, qg, kg, cos, sin, tm_in)
    _, k_c, v_c, xr_c, _ = _in_projection(
        ctx[0], csc1, csh1, w_in_b, qg, kg,
        jnp.ones((n_ctx, HEAD_DIM), F32), jnp.zeros((n_ctx, HEAD_DIM), F32), n_ctx)

    attn = _attention(q_l, k_c, v_c, k_l, v_l, tq=min(256, s), tk=min(512, s))

    wg = jnp.concatenate([rg_wa[0], rg_wx[0]], axis=-1).astype(BF16)
    cb = conv_b[0].reshape(1, RG_W)
    zero_state = jnp.zeros((1, RG_W), F32)
    rg_args = []
    for dd in range(2):
        rg_args.append((conv_w[0], cb, wg[dd],
                        jnp.stack([rg_ba[0, dd], rg_bx[0, dd]], axis=0),
                        rg_lam[0, dd].reshape(1, RG_W)))
    t_rg = min(512, s)
    hc_f = _rg_scan(xr_c, *rg_args[0], zero_state, reverse=False, t=n_ctx)
    hc_b = _rg_scan(xr_c, *rg_args[1], zero_state, reverse=True, t=n_ctx)
    h_f = _rg_scan(xr_l, *rg_args[0], hc_f[n_ctx - 1:n_ctx], reverse=False, t=t_rg)
    rg = _rg_scan(xr_l, *rg_args[1], hc_b[0:1], reverse=True, t=t_rg, hf=h_f, yr=yr_l)

    h1, v = _out_projection(attn, rg, x2, w_out[0].astype(BF16), g1,
                            ln1_g[0].reshape(1, d), ln1_b[0].reshape(1, d), sc2, sh2,
                            tm=min(256, s))

    wr = jnp.pad(w_router[0], ((0, 0), (0, LANES - N_EXPERTS)))
    wr_hi = wr.astype(BF16)
    wr_lo = (wr - wr_hi.astype(F32)).astype(BF16)
    idx, wts = _router(v, wr_hi, wr_lo, e_bias[0].reshape(N_EXPERTS, 1), tm=min(256, s))

    bm = 256
    blk_e, blk_n, slot_tok, slot_dst = _dispatch_tables(idx, s, bm)
    y_slots = _experts(blk_e, blk_n, slot_tok, slot_dst, v, w_e1[0], w_e3[0], w_e2[0], bm)

    out = _combine(y_slots, wts, v, h1, g2, ln2_g[0].reshape(1, d), ln2_b[0].reshape(1, d),
                   w_s1[0].astype(BF16), w_s3[0].astype(BF16), w_s2[0].astype(BF16),
                   tm=min(128, s))
    return out[None]
```

```python
import functools

import jax
import jax.numpy as jnp
from jax import lax
from jax.experimental import pallas as pl
from jax.experimental.pallas import tpu as pltpu

F32 = jnp.float32
BF16 = jnp.bfloat16
I32 = jnp.int32

GRID_W = 64
HEAD_DIM = 128
N_HEADS = 8
N_KV_HEADS = 2
GQA_GROUP = N_HEADS // N_KV_HEADS
ATTN_W = N_HEADS * HEAD_DIM
KV_W = N_KV_HEADS * HEAD_DIM
ROPE_THETA = 10000.0
RG_W = 1024
RG_HEADS = 8
RG_HD = RG_W // RG_HEADS
RG_C = 8.0
PROJ_W = ATTN_W + 2 * KV_W + 2 * RG_W
N_EXPERTS = 64
N_GROUPS = 8
GROUP_SIZE = N_EXPERTS // N_GROUPS
TOPK_GROUPS = 4
TOP_K = 8
ROUTED_SCALE = 2.5
NORM_EPS = 1e-6
DEPTH = 1
DEEPNORM_ALPHA = (2.0 * DEPTH) ** 0.25
LOG2E = 1.4426950408889634

LANES = 128
SUBLANES = 8
BF16_SUBLANES = 16
VMEM_LIMIT = 56 * 1024 * 1024

NEG_BIG = -1e30
KV_CHUNK = 256
VT_ROWS = HEAD_DIM + BF16_SUBLANES
EXPERT_BLOCK = 256


def _cparams(*sem):
    return pltpu.CompilerParams(dimension_semantics=sem, vmem_limit_bytes=VMEM_LIMIT)


def _const_spec(shape):
    nd = len(shape)
    return pl.BlockSpec(shape, lambda *_: (0,) * nd)


def _mod_kernel(ct_ref, w_ref, b_ref, o_ref, sb_ref, *, tn):
    d = w_ref.shape[0]

    @pl.when(pl.program_id(0) == 0)
    def _():
        ct = ct_ref[...]
        s = ct * jax.nn.sigmoid(ct)
        sb_ref[0] = jnp.broadcast_to(s[:, 0:1], (d, LANES))
        sb_ref[1] = jnp.broadcast_to(s[:, 1:2], (d, LANES))

    for c in range(tn // LANES):
        sl = slice(c * LANES, (c + 1) * LANES)
        wc = w_ref[:, sl]
        bc = b_ref[:, sl]
        o0 = jnp.sum(wc * sb_ref[0], axis=0, keepdims=True) + bc
        o1 = jnp.sum(wc * sb_ref[1], axis=0, keepdims=True) + bc
        o_ref[:, sl] = jnp.concatenate(
            [o0, o1, jnp.zeros((SUBLANES - 2, LANES), F32)], axis=0)


def _modulation(c2t, w_mod, b_mod):
    d, n = w_mod.shape
    tn = 1024
    return pl.pallas_call(
        functools.partial(_mod_kernel, tn=tn),
        grid=(n // tn,),
        in_specs=[_const_spec((d, 2)),
                  pl.BlockSpec((d, tn), lambda j: (0, j)),
                  pl.BlockSpec((1, tn), lambda j: (0, j))],
        out_specs=pl.BlockSpec((SUBLANES, tn), lambda j: (0, j)),
        out_shape=jax.ShapeDtypeStruct((SUBLANES, n), F32),
        scratch_shapes=[pltpu.VMEM((2, d, LANES), F32)],
        compiler_params=_cparams("arbitrary"),
        name="modulation",
    )(c2t, w_mod, b_mod)


def _swap_half(y):
    lane = lax.broadcasted_iota(I32, y.shape, 1)
    return jnp.where((lane % 64) < 32,
                     pltpu.roll(y, LANES - 32, 1), pltpu.roll(y, 32, 1))


def _norm_rope(ph, g, cos, sin_signed, scale):
    ms = jnp.mean(ph * ph, axis=-1, keepdims=True)
    y = (ph * lax.rsqrt(ms + NORM_EPS)) * g
    y = y * cos + _swap_half(y) * sin_signed
    if scale != 1.0:
        y = y * scale
    return y


def _inproj_kernel(x_ref, sc_ref, sh_ref, w_ref, qg_ref, kg_ref, cos_ref, sin_ref,
                   q_ref, k_ref, vt_ref, xr_ref, yr_ref, *, q_scale):
    tm = x_ref.shape[0]
    u = (x_ref[...] * (1.0 + sc_ref[...]) + sh_ref[...]).astype(BF16)
    cos = cos_ref[...]
    sin = sin_ref[...]
    o = 0
    pq = jnp.dot(u, w_ref[:, o:o + ATTN_W], preferred_element_type=F32)
    for h in range(N_HEADS):
        sl = slice(h * HEAD_DIM, (h + 1) * HEAD_DIM)
        q_ref[:, sl] = _norm_rope(pq[:, sl], qg_ref[...], cos, sin, q_scale).astype(BF16)
    o += ATTN_W
    pk = jnp.dot(u, w_ref[:, o:o + KV_W], preferred_element_type=F32)
    for h in range(N_KV_HEADS):
        sl = slice(h * HEAD_DIM, (h + 1) * HEAD_DIM)
        k_ref[:, sl] = _norm_rope(pk[:, sl], kg_ref[...], cos, sin, 1.0).astype(BF16)
    o += KV_W
    pv = jnp.dot(u, w_ref[:, o:o + KV_W], preferred_element_type=F32)
    ones_rows = jnp.where(
        lax.broadcasted_iota(I32, (VT_ROWS - HEAD_DIM, KV_CHUNK), 0) == 0, 1.0, 0.0).astype(BF16)
    for h in range(N_KV_HEADS):
        for cc in range(tm // KV_CHUNK):
            blk = pv[cc * KV_CHUNK:(cc + 1) * KV_CHUNK, h * HEAD_DIM:(h + 1) * HEAD_DIM]
            vt_ref[h, cc, 0:HEAD_DIM, :] = blk.T.astype(BF16)
            vt_ref[h, cc, HEAD_DIM:VT_ROWS, :] = ones_rows
    o += KV_W
    xr_ref[...] = jnp.dot(u, w_ref[:, o:o + RG_W], preferred_element_type=F32)
    o += RG_W
    yr_ref[...] = jnp.dot(u, w_ref[:, o:o + RG_W], preferred_element_type=F32)


def _in_projection(x, sc, sh, w_in, qg, kg, cos, sin, tm):
    s, d = x.shape
    row = lambda i: (i, 0)
    nc = tm // KV_CHUNK
    return pl.pallas_call(
        functools.partial(_inproj_kernel, q_scale=HEAD_DIM ** -0.5 * LOG2E),
        grid=(s // tm,),
        in_specs=[pl.BlockSpec((tm, d), row),
                  _const_spec((1, d)), _const_spec((1, d)),
                  pl.BlockSpec((d, PROJ_W), lambda i: (0, 0), pipeline_mode=pl.Buffered(1)),
                  _const_spec((1, HEAD_DIM)), _const_spec((1, HEAD_DIM)),
                  pl.BlockSpec((tm, HEAD_DIM), row), pl.BlockSpec((tm, HEAD_DIM), row)],
        out_specs=[pl.BlockSpec((tm, ATTN_W), row), pl.BlockSpec((tm, KV_W), row),
                   pl.BlockSpec((N_KV_HEADS, nc, VT_ROWS, KV_CHUNK), lambda i: (0, i, 0, 0)),
                   pl.BlockSpec((tm, RG_W), row), pl.BlockSpec((tm, RG_W), row)],
        out_shape=[jax.ShapeDtypeStruct((s, ATTN_W), BF16),
                   jax.ShapeDtypeStruct((s, KV_W), BF16),
                   jax.ShapeDtypeStruct((N_KV_HEADS, s // KV_CHUNK, VT_ROWS, KV_CHUNK), BF16),
                   jax.ShapeDtypeStruct((s, RG_W), F32),
                   jax.ShapeDtypeStruct((s, RG_W), F32)],
        compiler_params=_cparams("arbitrary"),
        name="in_projection",
    )(x, sc, sh, w_in, qg, kg, cos, sin)


def _attn_kernel(q_ref, k_ref, vt_ref, o_ref, *, tq, n_chunks):
    qs = [q_ref[:, g * HEAD_DIM:(g + 1) * HEAD_DIM] for g in range(GQA_GROUP)]

    def body(j, carry):
        ms, accs = carry
        start = pl.multiple_of(j * KV_CHUNK, KV_CHUNK)
        k = k_ref[pl.ds(start, KV_CHUNK), :]
        vt = vt_ref[j]
        new_ms, new_accs = [], []
        for g in range(GQA_GROUP):
            s = lax.dot_general(k, qs[g], (((1,), (1,)), ((), ())), preferred_element_type=F32)
            m_new = jnp.maximum(ms[g], jnp.max(s, axis=0, keepdims=True))
            alpha = jnp.exp2(ms[g] - m_new)
            p = jnp.exp2(s - m_new).astype(BF16)
            new_accs.append(alpha * accs[g] + jnp.dot(vt, p, preferred_element_type=F32))
            new_ms.append(m_new)
        return tuple(new_ms), tuple(new_accs)

    m0 = tuple(jnp.full((1, tq), NEG_BIG, F32) for _ in range(GQA_GROUP))
    a0 = tuple(jnp.zeros((VT_ROWS, tq), F32) for _ in range(GQA_GROUP))
    _, accs = lax.fori_loop(0, n_chunks, body, (m0, a0))
    for g in range(GQA_GROUP):
        out_t = accs[g][0:HEAD_DIM, :] / accs[g][HEAD_DIM:HEAD_DIM + 1, :]
        o_ref[:, g * HEAD_DIM:(g + 1) * HEAD_DIM] = out_t.T.astype(BF16)


def _attention(q, k_all, vt_all, tq):
    s = q.shape[0]
    s_all = k_all.shape[0]
    n_chunks = s_all // KV_CHUNK
    gw = GQA_GROUP * HEAD_DIM
    return pl.pallas_call(
        functools.partial(_attn_kernel, tq=tq, n_chunks=n_chunks),
        grid=(N_KV_HEADS, s // tq),
        in_specs=[pl.BlockSpec((tq, gw), lambda h, i: (i, h)),
                  pl.BlockSpec((s_all, HEAD_DIM), lambda h, i: (0, h)),
                  pl.BlockSpec((None, n_chunks, VT_ROWS, KV_CHUNK), lambda h, i: (h, 0, 0, 0))],
        out_specs=pl.BlockSpec((tq, gw), lambda h, i: (i, h)),
        out_shape=jax.ShapeDtypeStruct((s, ATTN_W), BF16),
        compiler_params=_cparams("arbitrary", "arbitrary"),
        name="attention",
    )(q, k_all, vt_all)


def _log_sigmoid(x):
    return jnp.minimum(x, 0.0) - jnp.log1p(jnp.exp(-jnp.abs(x)))


def _rg_kernel(x_ref, xp_ref, xn_ref, cw_ref, cb_ref, wg_ref, bg_ref, lam_ref, h0_ref,
               *rest, reverse, final, t, n_chunks):
    if final:
        hf_ref, yr_ref, o_ref, a_scr, b_scr, hc_scr, h_scr = rest
    else:
        o_ref, a_scr, b_scr, hc_scr = rest
        h_scr = o_ref
    i = pl.program_id(0)
    c = (n_chunks - 1 - i) if reverse else i
    w = x_ref.shape[1]

    @pl.when(i == 0)
    def _():
        hc_scr[...] = jnp.broadcast_to(h0_ref[...], (SUBLANES, w))

    x = x_ref[...]
    row = lax.broadcasted_iota(I32, (t, w), 0)
    pm = jnp.where(c == 0, 0.0, 1.0).astype(F32)
    nm = jnp.where(c == n_chunks - 1, 0.0, 1.0).astype(F32)
    p6 = xp_ref[SUBLANES - 2:SUBLANES - 1, :] * pm
    p7 = xp_ref[SUBLANES - 1:SUBLANES, :] * pm
    n0 = xn_ref[0:1, :] * nm
    x_m1 = jnp.where(row == 0, p7, pltpu.roll(x, 1, 0))
    x_m2 = jnp.where(row == 0, p6, jnp.where(row == 1, p7, pltpu.roll(x, 2, 0)))
    x_p1 = jnp.where(row == t - 1, n0, pltpu.roll(x, t - 1, 0))
    xc = cb_ref[...] + cw_ref[0:1, :] * x_m2
    xc = xc + cw_ref[1:2, :] * x_m1
    xc = xc + cw_ref[2:3, :] * x
    xc = xc + cw_ref[3:4, :] * x_p1

    xcb = xc.astype(BF16)
    clam = RG_C * _log_sigmoid(lam_ref[...])
    for h in range(RG_HEADS):
        sl = slice(h * RG_HD, (h + 1) * RG_HD)
        g = jnp.dot(xcb[:, sl], wg_ref[h], preferred_element_type=F32)
        r = jax.nn.sigmoid(g[:, :RG_HD] + bg_ref[0:1, sl])
        gi = jax.nn.sigmoid(g[:, RG_HD:] + bg_ref[1:2, sl])
        log_a = r * clam[:, sl]
        a = jnp.exp(log_a)
        a_scr[:, sl] = a
        b_scr[:, sl] = jnp.sqrt(-jnp.tanh(log_a) * (a * a + 1.0)) * (gi * xc[:, sl])

    srow = lax.broadcasted_iota(I32, (SUBLANES, w), 0)
    n_tiles = t // SUBLANES

    def tile_body(j, hprev):
        tile = (n_tiles - 1 - j) if reverse else j
        start = pl.multiple_of(tile * SUBLANES, SUBLANES)
        a = a_scr[pl.ds(start, SUBLANES), :]
        b = b_scr[pl.ds(start, SUBLANES), :]
        for k in (1, 2, 4):
            if reverse:
                keep = srow < SUBLANES - k
                shift = SUBLANES - k
            else:
                keep = srow >= k
                shift = k
            a_sh = jnp.where(keep, pltpu.roll(a, shift, 0), 1.0)
            b_sh = jnp.where(keep, pltpu.roll(b, shift, 0), 0.0)
            b = a * b_sh + b
            a = a * a_sh
        hh = a * hprev + b
        h_scr[pl.ds(start, SUBLANES), :] = hh
        last = hh[0:1, :] if reverse else hh[SUBLANES - 1:SUBLANES, :]
        return jnp.broadcast_to(last, (SUBLANES, w))

    hc_scr[...] = lax.fori_loop(0, n_tiles, tile_body, hc_scr[...])

    if final:
        gate = jax.nn.gelu(yr_ref[...], approximate=True)
        o_ref[...] = ((hf_ref[...] + h_scr[...]) * gate).astype(o_ref.dtype)


def _rg_scan(xr, conv_w, conv_b, wg, bg, lam, h0, *, reverse, t, hf=None, yr=None):
    s, w = xr.shape
    n_chunks = s // t
    final = hf is not None
    tb = t // SUBLANES
    last_blk = s // SUBLANES - 1
    if reverse:
        cidx = lambda i: n_chunks - 1 - i
    else:
        cidx = lambda i: i
    chunk_spec = pl.BlockSpec((t, w), lambda i: (cidx(i), 0))
    in_specs = [chunk_spec,
                pl.BlockSpec((SUBLANES, w), lambda i: (jnp.maximum(cidx(i) * tb - 1, 0), 0)),
                pl.BlockSpec((SUBLANES, w), lambda i: (jnp.minimum((cidx(i) + 1) * tb, last_blk), 0)),
                _const_spec((4, w)), _const_spec((1, w)),
                _const_spec((RG_HEADS, RG_HD, 2 * RG_HD)), _const_spec((2, w)),
                _const_spec((1, w)), _const_spec((1, w))]
    args = [xr, xr, xr, conv_w, conv_b, wg, bg, lam, h0]
    scratch = [pltpu.VMEM((t, w), F32), pltpu.VMEM((t, w), F32), pltpu.VMEM((SUBLANES, w), F32)]
    if final:
        in_specs += [chunk_spec, chunk_spec]
        args += [hf, yr]
        scratch.append(pltpu.VMEM((t, w), F32))
        out_dtype = BF16
    else:
        out_dtype = F32
    return pl.pallas_call(
        functools.partial(_rg_kernel, reverse=reverse, final=final, t=t, n_chunks=n_chunks),
        grid=(n_chunks,),
        in_specs=in_specs,
        out_specs=chunk_spec,
        out_shape=jax.ShapeDtypeStruct((s, w), out_dtype),
        scratch_shapes=scratch,
        compiler_params=_cparams("arbitrary"),
        name="rglru_bwd" if reverse else "rglru_fwd",
    )(*args)


def _layer_norm(y, g, b):
    mu = jnp.mean(y, axis=-1, keepdims=True)
    yc = y - mu
    var = jnp.mean(yc * yc, axis=-1, keepdims=True)
    return yc * lax.rsqrt(var + NORM_EPS) * g + b


def _outproj_kernel(attn_ref, rg_ref, x_ref, w_ref, g1_ref, lg_ref, lb_ref, sc2_ref, sh2_ref,
                    h1_ref, v_ref):
    mix = jnp.dot(attn_ref[...], w_ref[0:ATTN_W, :], preferred_element_type=F32)
    mix = mix + jnp.dot(rg_ref[...], w_ref[ATTN_W:, :], preferred_element_type=F32)
    h1 = _layer_norm(DEEPNORM_ALPHA * x_ref[...] + g1_ref[...] * mix, lg_ref[...], lb_ref[...])
    h1_ref[...] = h1
    v_ref[...] = h1 * (1.0 + sc2_ref[...]) + sh2_ref[...]


def _out_projection(attn, rg, x, w_out, g1, ln_g, ln_b, sc2, sh2, tm):
    s, d = x.shape
    row = lambda i: (i, 0)
    vec = _const_spec((1, d))
    return pl.pallas_call(
        _outproj_kernel,
        grid=(s // tm,),
        in_specs=[pl.BlockSpec((tm, ATTN_W), row), pl.BlockSpec((tm, RG_W), row),
                  pl.BlockSpec((tm, d), row),
                  pl.BlockSpec((ATTN_W + RG_W, d), lambda i: (0, 0), pipeline_mode=pl.Buffered(1)),
                  vec, vec, vec, vec, vec],
        out_specs=[pl.BlockSpec((tm, d), row), pl.BlockSpec((tm, d), row)],
        out_shape=[jax.ShapeDtypeStruct((s, d), F32), jax.ShapeDtypeStruct((s, d), F32)],
        compiler_params=_cparams("arbitrary"),
        name="out_projection",
    )(attn, rg, x, w_out, g1, ln_g, ln_b, sc2, sh2)


def _first_index_of_max(x, iota_f, axis):
    mx = jnp.max(x, axis=axis, keepdims=True)
    idx = jnp.min(jnp.where(x == mx, iota_f, float(N_EXPERTS)), axis=axis, keepdims=True)
    return mx, idx


def _router_kernel(v_ref, whi_ref, wlo_ref, eb_ref, tri_ref,
                   idx_ref, wts_ref, rank_ref, cnt_ref, base_scr):
    v = v_ref[...]
    tm = v.shape[0]

    @pl.when(pl.program_id(0) == 0)
    def _():
        base_scr[...] = jnp.zeros(base_scr.shape, F32)

    v_hi = v.astype(BF16)
    v_lo = (v - v_hi.astype(F32)).astype(BF16)
    logits = jnp.dot(v_hi, whi_ref[...], preferred_element_type=F32)
    logits = logits + jnp.dot(v_lo, whi_ref[...], preferred_element_type=F32)
    logits = logits + jnp.dot(v_hi, wlo_ref[...], preferred_element_type=F32)
    lt = logits.T[0:N_EXPERTS, :]
    scores = jax.nn.sigmoid(lt)
    biased = scores + eb_ref[...]
    neg_inf = float("-inf")

    ig = lax.broadcasted_iota(I32, (GROUP_SIZE, tm), 0).astype(F32)
    groups = [biased[g * GROUP_SIZE:(g + 1) * GROUP_SIZE, :] for g in range(N_GROUPS)]
    gscore = []
    for bg in groups:
        top1, i1 = _first_index_of_max(bg, ig, 0)
        top2 = jnp.max(jnp.where(ig == i1, neg_inf, bg), axis=0, keepdims=True)
        gscore.append(top1 + top2)

    masked = []
    for g in range(N_GROUPS):
        ahead = jnp.zeros((1, tm), F32)
        for o in range(N_GROUPS):
            if o == g:
                continue
            before = (gscore[o] >= gscore[g]) if o < g else (gscore[o] > gscore[g])
            ahead = ahead + jnp.where(before, 1.0, 0.0)
        keep = jnp.broadcast_to(ahead < TOPK_GROUPS, (GROUP_SIZE, tm))
        masked.append(jnp.where(keep, groups[g], neg_inf))
    masked = jnp.concatenate(masked, axis=0)

    ie = lax.broadcasted_iota(I32, masked.shape, 0).astype(F32)
    seen = base_scr[...]
    ws = []
    for k in range(TOP_K):
        _, ei = _first_index_of_max(masked, ie, 0)
        hit = ie == ei
        idx_ref[k:k + 1, :] = ei.astype(I32)
        ws.append(jnp.sum(jnp.where(hit, scores, 0.0), axis=0, keepdims=True))
        masked = jnp.where(hit, neg_inf, masked)
        onehot = jnp.where(hit, 1.0, 0.0)
        before = jnp.dot(onehot.astype(BF16), tri_ref[...], preferred_element_type=F32)
        rank = jnp.sum(jnp.where(hit, before + seen, 0.0), axis=0, keepdims=True)
        rank_ref[k:k + 1, :] = rank.astype(I32)
        seen = seen + jnp.sum(onehot, axis=1, keepdims=True)
    base_scr[...] = seen
    cnt_ref[...] = jnp.broadcast_to(seen, cnt_ref.shape)
    total = ws[0]
    for k in range(1, TOP_K):
        total = total + ws[k]
    for k in range(TOP_K):
        wts_ref[k:k + 1, :] = ws[k] / total * ROUTED_SCALE


def _router(v, w_hi, w_lo, e_bias_col, tm):
    s, d = v.shape
    tri = jnp.triu(jnp.ones((tm, tm), BF16), k=1)
    slot = pl.BlockSpec((TOP_K, tm), lambda i: (0, i))
    return pl.pallas_call(
        _router_kernel,
        grid=(s // tm,),
        in_specs=[pl.BlockSpec((tm, d), lambda i: (i, 0)),
                  _const_spec((d, LANES)), _const_spec((d, LANES)),
                  _const_spec((N_EXPERTS, 1)), _const_spec((tm, tm))],
        out_specs=[slot, slot, slot, _const_spec((N_EXPERTS, LANES))],
        out_shape=[jax.ShapeDtypeStruct((TOP_K, s), I32),
                   jax.ShapeDtypeStruct((TOP_K, s), F32),
                   jax.ShapeDtypeStruct((TOP_K, s), I32),
                   jax.ShapeDtypeStruct((N_EXPERTS, LANES), F32)],
        scratch_shapes=[pltpu.VMEM((N_EXPERTS, 1), F32)],
        compiler_params=_cparams("arbitrary"),
        name="router",
    )(v, w_hi, w_lo, e_bias_col, tri)


def _dispatch_kernel(last_ref, has_ref, v_ref, idx_ref, rank_ref, pstart_ref,
                     dest_ref, xs_hbm, dest_v, dest_s, zeros_v, sem_z, sem_i, sem_r, *, bm):
    tm = v_ref.shape[0]

    def zero_fill(e):
        row0 = pl.multiple_of(last_ref[e], bm)
        return pltpu.make_async_copy(zeros_v, xs_hbm.at[pl.ds(row0, bm), :], sem_z)

    @pl.when(pl.program_id(0) == 0)
    def _():
        zeros_v[...] = jnp.zeros(zeros_v.shape, zeros_v.dtype)

        def start(e, carry):
            @pl.when(has_ref[e] > 0)
            def _():
                zero_fill(e).start()
            return carry

        def wait(e, carry):
            @pl.when(has_ref[e] > 0)
            def _():
                zero_fill(e).wait()
            return carry

        lax.fori_loop(0, N_EXPERTS, start, 0)
        lax.fori_loop(0, N_EXPERTS, wait, 0)

    ie = lax.broadcasted_iota(I32, (N_EXPERTS, tm), 0)
    pstart = pstart_ref[...]
    for k in range(TOP_K):
        hit = ie == idx_ref[k:k + 1, :]
        seg = jnp.sum(jnp.where(hit, pstart, 0.0), axis=0, keepdims=True)
        dest_v[k:k + 1, :] = seg.astype(I32) + rank_ref[k:k + 1, :]
    dest_ref[...] = dest_v[...]
    cp = pltpu.make_async_copy(dest_v, dest_s, sem_i)
    cp.start()
    cp.wait()

    def row_copy(k, t):
        return pltpu.make_async_copy(v_ref.at[pl.ds(t, 1), :],
                                     xs_hbm.at[pl.ds(dest_s[k, t], 1), :], sem_r)

    def start_rows(t, carry):
        for k in range(TOP_K):
            row_copy(k, t).start()
        return carry

    def wait_rows(t, carry):
        for k in range(TOP_K):
            row_copy(k, t).wait()
        return carry

    lax.fori_loop(0, tm, start_rows, 0)
    lax.fori_loop(0, tm, wait_rows, 0)


def _dispatch(last_row, has_blk, v, idx, rank, pstart_col, n_rows, tm, bm):
    s, d = v.shape
    slot = pl.BlockSpec((TOP_K, tm), lambda i, *_: (0, i))
    grid_spec = pltpu.PrefetchScalarGridSpec(
        num_scalar_prefetch=2,
        grid=(s // tm,),
        in_specs=[pl.BlockSpec((tm, d), lambda i, *_: (i, 0)), slot, slot,
                  pl.BlockSpec((N_EXPERTS, 1), lambda i, *_: (0, 0))],
        out_specs=[slot, pl.BlockSpec(memory_space=pl.ANY)],
        scratch_shapes=[pltpu.VMEM((TOP_K, tm), I32), pltpu.SMEM((TOP_K, tm), I32),
                        pltpu.VMEM((bm, d), v.dtype),
                        pltpu.SemaphoreType.DMA, pltpu.SemaphoreType.DMA,
                        pltpu.SemaphoreType.DMA])
    return pl.pallas_call(
        functools.partial(_dispatch_kernel, bm=bm),
        grid_spec=grid_spec,
        out_shape=[jax.ShapeDtypeStruct((TOP_K, s), I32),
                   jax.ShapeDtypeStruct((n_rows, d), v.dtype)],
        compiler_params=_cparams("arbitrary"),
        name="dispatch",
    )(last_row, has_blk, v, idx, rank, pstart_col)


def _expert_kernel(blk_e_ref, n_used_ref, x_ref, w1_ref, w3_ref, w2_ref, y_ref, w1b, w3b, w2b):
    b = pl.program_id(0)

    @pl.when(b < n_used_ref[0])
    def _():
        prev_e = blk_e_ref[jnp.maximum(b - 1, 0)]

        @pl.when(jnp.logical_or(b == 0, blk_e_ref[b] != prev_e))
        def _():
            w1b[...] = w1_ref[...].astype(BF16)
            w3b[...] = w3_ref[...].astype(BF16)
            w2b[...] = w2_ref[...].astype(BF16)

        xb = x_ref[...].astype(BF16)
        h1 = jnp.dot(xb, w1b[...], preferred_element_type=F32)
        h3 = jnp.dot(xb, w3b[...], preferred_element_type=F32)
        act = (h1 * jax.nn.sigmoid(h1) * h3).astype(BF16)
        y_ref[...] = jnp.dot(act, w2b[...], preferred_element_type=F32)


def _experts(blk_e, n_used, xs, w_e1, w_e3, w_e2, bm):
    n_rows, d = xs.shape
    ff = w_e1.shape[-1]
    rows = lambda b, be, nu: (jnp.minimum(b, nu[0] - 1), 0)
    wspec1 = pl.BlockSpec((None, d, ff), lambda b, be, nu: (be[b], 0, 0))
    wspec2 = pl.BlockSpec((None, ff, d), lambda b, be, nu: (be[b], 0, 0))
    grid_spec = pltpu.PrefetchScalarGridSpec(
        num_scalar_prefetch=2,
        grid=(n_rows // bm,),
        in_specs=[pl.BlockSpec((bm, d), rows), wspec1, wspec1, wspec2],
        out_specs=pl.BlockSpec((bm, d), rows),
        scratch_shapes=[pltpu.VMEM((d, ff), BF16), pltpu.VMEM((d, ff), BF16),
                        pltpu.VMEM((ff, d), BF16)])
    return pl.pallas_call(
        _expert_kernel,
        grid_spec=grid_spec,
        out_shape=jax.ShapeDtypeStruct((n_rows, d), F32),
        compiler_params=_cparams("arbitrary"),
        name="experts",
    )(blk_e, n_used, xs, w_e1, w_e3, w_e2)


def _block_tables(counts, n_blocks, bm):
    padded = (counts + bm - 1) // bm * bm
    pad_end = jnp.cumsum(padded)
    pad_start = pad_end - padded
    blk_first = jnp.arange(n_blocks, dtype=I32) * bm
    blk_e = jnp.minimum(jnp.sum(pad_end[None, :] <= blk_first[:, None], axis=1), N_EXPERTS - 1)
    n_used = pad_end[-1] // bm
    blk_e = jnp.where(jnp.arange(n_blocks) < n_used, blk_e, blk_e[jnp.maximum(n_used - 1, 0)])
    last_row = jnp.maximum(pad_end - bm, 0)
    return (pad_start, last_row.astype(I32), (padded > 0).astype(I32),
            blk_e.astype(I32), n_used.astype(I32).reshape(1))


def _combine_kernel(dest_ref, wts_ref, v_ref, h1_ref, g2_ref, lg_ref, lb_ref,
                    ws1_ref, ws3_ref, ws2_ref, ys_hbm, o_ref, dest_s, ybuf, sem_i, sem_r):
    tm = v_ref.shape[0]
    cp = pltpu.make_async_copy(dest_ref, dest_s, sem_i)
    cp.start()
    cp.wait()

    def row_copy(k, t):
        return pltpu.make_async_copy(ys_hbm.at[pl.ds(dest_s[k, t], 1), :],
                                     ybuf.at[k, pl.ds(t, 1), :], sem_r)

    def start_rows(t, carry):
        for k in range(TOP_K):
            row_copy(k, t).start()
        return carry

    def wait_rows(t, carry):
        for k in range(TOP_K):
            row_copy(k, t).wait()
        return carry

    lax.fori_loop(0, tm, start_rows, 0)

    vb = v_ref[...].astype(BF16)
    a1 = jnp.dot(vb, ws1_ref[...], preferred_element_type=F32)
    a3 = jnp.dot(vb, ws3_ref[...], preferred_element_type=F32)
    act = (a1 * jax.nn.sigmoid(a1) * a3).astype(BF16)
    shared = jnp.dot(act, ws2_ref[...], preferred_element_type=F32)
    wpad = jnp.concatenate([wts_ref[...], jnp.zeros((LANES - TOP_K, tm), F32)], axis=0)
    wt = wpad.T

    lax.fori_loop(0, tm, wait_rows, 0)
    ff = ybuf[0] * wt[:, 0:1]
    for k in range(1, TOP_K):
        ff = ff + ybuf[k] * wt[:, k:k + 1]
    ff = ff + shared
    o_ref[...] = _layer_norm(DEEPNORM_ALPHA * h1_ref[...] + g2_ref[...] * ff,
                             lg_ref[...], lb_ref[...])


def _combine(dest, wts, v, h1, g2, ln_g, ln_b, w_s1, w_s3, w_s2, ys, tm):
    s, d = v.shape
    row = lambda i: (i, 0)
    vec = _const_spec((1, d))
    slot = pl.BlockSpec((TOP_K, tm), lambda i: (0, i))
    return pl.pallas_call(
        _combine_kernel,
        grid=(s // tm,),
        in_specs=[slot, slot, pl.BlockSpec((tm, d), row), pl.BlockSpec((tm, d), row),
                  vec, vec, vec,
                  _const_spec(w_s1.shape), _const_spec(w_s3.shape), _const_spec(w_s2.shape),
                  pl.BlockSpec(memory_space=pl.ANY)],
        out_specs=pl.BlockSpec((tm, d), row),
        out_shape=jax.ShapeDtypeStruct((s, d), F32),
        scratch_shapes=[pltpu.SMEM((TOP_K, tm), I32), pltpu.VMEM((TOP_K, tm, d), F32),
                        pltpu.SemaphoreType.DMA, pltpu.SemaphoreType.DMA],
        compiler_params=_cparams("arbitrary"),
        name="combine",
    )(dest, wts, v, h1, g2, ln_g, ln_b, w_s1, w_s3, w_s2, ys)


def _rope_tables(s):
    half = HEAD_DIM // 4
    inv_freq = ROPE_THETA ** (-jnp.arange(half, dtype=F32) / half)
    t = jnp.arange(s, dtype=I32)
    ang_r = (t // GRID_W).astype(F32)[:, None] * inv_freq[None, :]
    ang_c = (t % GRID_W).astype(F32)[:, None] * inv_freq[None, :]
    cos = jnp.concatenate([jnp.cos(ang_r)] * 2 + [jnp.cos(ang_c)] * 2, axis=-1)
    sin = jnp.concatenate([-jnp.sin(ang_r), jnp.sin(ang_r), -jnp.sin(ang_c), jnp.sin(ang_c)], axis=-1)
    return cos, sin


def kernel(x, c, ctx, c_ctx, w_mod, b_mod, w_in, q_norm, k_norm, conv_w, conv_b, rg_wa, rg_ba,
           rg_wx, rg_bx, rg_lam, w_out, ln1_g, ln1_b, w_router, e_bias, w_e1, w_e3, w_e2,
           w_s1, w_s3, w_s2, ln2_g, ln2_b):
    assert x.shape[0] == 1 and w_mod.shape[0] == DEPTH
    _, s, d = x.shape
    n_ctx = ctx.shape[1]
    assert n_ctx % KV_CHUNK == 0 and s % KV_CHUNK == 0
    x2 = x[0]

    c2t = jnp.stack([c[0], c_ctx], axis=1)
    mod = _modulation(c2t, w_mod[0], b_mod[0].reshape(1, -1))
    sh1, sc1, g1, sh2, sc2, g2 = [mod[0:1, j * d:(j + 1) * d] for j in range(6)]
    csh1, csc1 = mod[1:2, 0:d], mod[1:2, d:2 * d]

    w_in_b = w_in[0].astype(BF16)
    qg = q_norm[0].reshape(1, HEAD_DIM)
    kg = k_norm[0].reshape(1, HEAD_DIM)
    cos, sin = _rope_tables(s)
    q_l, k_l, vt_l, xr_l, yr_l = _in_projection(x2, sc1, sh1, w_in_b, qg, kg, cos, sin,
                                                tm=min(512, s))
    _, k_c, vt_c, xr_c, _ = _in_projection(
        ctx[0], csc1, csh1, w_in_b, qg, kg,
        jnp.ones((n_ctx, HEAD_DIM), F32), jnp.zeros((n_ctx, HEAD_DIM), F32), tm=n_ctx)

    attn = _attention(q_l, jnp.concatenate([k_c, k_l], axis=0),
                      jnp.concatenate([vt_c, vt_l], axis=1), tq=min(256, s))

    wg = jnp.concatenate([rg_wa[0], rg_wx[0]], axis=-1).astype(BF16)
    cb = conv_b[0].reshape(1, RG_W)
    zero_state = jnp.zeros((1, RG_W), F32)
    rg_args = []
    for dd in range(2):
        rg_args.append((conv_w[0], cb, wg[dd],
                        jnp.stack([rg_ba[0, dd], rg_bx[0, dd]], axis=0),
                        rg_lam[0, dd].reshape(1, RG_W)))
    t_rg = min(512, s)
    hc_f = _rg_scan(xr_c, *rg_args[0], zero_state, reverse=False, t=n_ctx)
    hc_b = _rg_scan(xr_c, *rg_args[1], zero_state, reverse=True, t=n_ctx)
    h_f = _rg_scan(xr_l, *rg_args[0], hc_f[n_ctx - 1:n_ctx], reverse=False, t=t_rg)
    rg = _rg_scan(xr_l, *rg_args[1], hc_b[0:1], reverse=True, t=t_rg, hf=h_f, yr=yr_l)

    h1, v = _out_projection(attn, rg, x2, w_out[0].astype(BF16), g1,
                            ln1_g[0].reshape(1, d), ln1_b[0].reshape(1, d), sc2, sh2,
                            tm=min(256, s))

    wr = jnp.pad(w_router[0], ((0, 0), (0, LANES - N_EXPERTS)))
    wr_hi = wr.astype(BF16)
    wr_lo = (wr - wr_hi.astype(F32)).astype(BF16)
    idx, wts, rank, cnt = _router(v, wr_hi, wr_lo, e_bias[0].reshape(N_EXPERTS, 1), tm=min(256, s))

    bm = EXPERT_BLOCK
    n_blocks = s * TOP_K // bm + N_EXPERTS
    pad_start, last_row, has_blk, blk_e, n_used = _block_tables(cnt[:, 0].astype(I32), n_blocks, bm)
    dest, xs = _dispatch(last_row, has_blk, v, idx, rank,
                         pad_start.astype(F32).reshape(N_EXPERTS, 1), n_blocks * bm,
                         tm=min(256, s), bm=bm)
    ys = _experts(blk_e, n_used, xs, w_e1[0], w_e3[0], w_e2[0], bm)

    out = _combine(dest, wts, v, h1, g2, ln2_g[0].reshape(1, d), ln2_b[0].reshape(1, d),
                   w_s1[0].astype(BF16), w_s3[0].astype(BF16), w_s2[0].astype(BF16), ys,
                   tm=min(128, s))
    return out[None]
```

```python
import functools

import jax
import jax.numpy as jnp
from jax import lax
from jax.experimental import pallas as pl
from jax.experimental.pallas import tpu as pltpu

F32 = jnp.float32
BF16 = jnp.bfloat16
I32 = jnp.int32

GRID_W = 64
HEAD_DIM = 128
N_HEADS = 8
N_KV_HEADS = 2
GQA_GROUP = N_HEADS // N_KV_HEADS
ATTN_W = N_HEADS * HEAD_DIM
KV_W = N_KV_HEADS * HEAD_DIM
ROPE_THETA = 10000.0
RG_W = 1024
RG_HEADS = 8
RG_HD = RG_W // RG_HEADS
RG_C = 8.0
PROJ_W = ATTN_W + 2 * KV_W + 2 * RG_W
N_EXPERTS = 64
N_GROUPS = 8
GROUP_SIZE = N_EXPERTS // N_GROUPS
TOPK_GROUPS = 4
TOP_K = 8
ROUTED_SCALE = 2.5
NORM_EPS = 1e-6
DEPTH = 1
DEEPNORM_ALPHA = (2.0 * DEPTH) ** 0.25
LOG2E = 1.4426950408889634

LANES = 128
SUBLANES = 8
BF16_SUBLANES = 16
VMEM_LIMIT = 56 * 1024 * 1024

NEG_BIG = -1e30
KV_CHUNK = 512
VT_ROWS = HEAD_DIM + BF16_SUBLANES
EXPERT_BLOCK = 256


def _cparams(*sem):
    return pltpu.CompilerParams(dimension_semantics=sem, vmem_limit_bytes=VMEM_LIMIT)


def _const_spec(shape):
    nd = len(shape)
    return pl.BlockSpec(shape, lambda *_: (0,) * nd)


def _mod_kernel(ct_ref, w_ref, b_ref, o_ref, sb_ref, *, tn):
    d = w_ref.shape[0]

    @pl.when(pl.program_id(0) == 0)
    def _():
        ct = ct_ref[...]
        s = ct * jax.nn.sigmoid(ct)
        sb_ref[0] = jnp.broadcast_to(s[:, 0:1], (d, LANES))
        sb_ref[1] = jnp.broadcast_to(s[:, 1:2], (d, LANES))

    for c in range(tn // LANES):
        sl = slice(c * LANES, (c + 1) * LANES)
        wc = w_ref[:, sl]
        bc = b_ref[:, sl]
        o0 = jnp.sum(wc * sb_ref[0], axis=0, keepdims=True) + bc
        o1 = jnp.sum(wc * sb_ref[1], axis=0, keepdims=True) + bc
        o_ref[:, sl] = jnp.concatenate(
            [o0, o1, jnp.zeros((SUBLANES - 2, LANES), F32)], axis=0)


def _modulation(c2t, w_mod, b_mod):
    d, n = w_mod.shape
    tn = 1024
    return pl.pallas_call(
        functools.partial(_mod_kernel, tn=tn),
        grid=(n // tn,),
        in_specs=[_const_spec((d, 2)),
                  pl.BlockSpec((d, tn), lambda j: (0, j)),
                  pl.BlockSpec((1, tn), lambda j: (0, j))],
        out_specs=pl.BlockSpec((SUBLANES, tn), lambda j: (0, j)),
        out_shape=jax.ShapeDtypeStruct((SUBLANES, n), F32),
        scratch_shapes=[pltpu.VMEM((2, d, LANES), F32)],
        compiler_params=_cparams("arbitrary"),
        name="modulation",
    )(c2t, w_mod, b_mod)


def _swap_half(y):
    lane = lax.broadcasted_iota(I32, y.shape, 1)
    return jnp.where((lane % 64) < 32,
                     pltpu.roll(y, LANES - 32, 1), pltpu.roll(y, 32, 1))


def _norm_rope(ph, g, cos, sin_signed, scale):
    ms = jnp.mean(ph * ph, axis=-1, keepdims=True)
    y = (ph * lax.rsqrt(ms + NORM_EPS)) * g
    y = y * cos + _swap_half(y) * sin_signed
    if scale != 1.0:
        y = y * scale
    return y


def _inproj_kernel(x_ref, sc_ref, sh_ref, w_ref, qg_ref, kg_ref, cos_ref, sin_ref,
                   q_ref, k_ref, vt_ref, xr_ref, yr_ref, *, q_scale):
    tm = x_ref.shape[0]
    kc = vt_ref.shape[-1]
    u = (x_ref[...] * (1.0 + sc_ref[...]) + sh_ref[...]).astype(BF16)
    cos = cos_ref[...]
    sin = sin_ref[...]
    o = 0
    pq = jnp.dot(u, w_ref[:, o:o + ATTN_W], preferred_element_type=F32)
    for h in range(N_HEADS):
        sl = slice(h * HEAD_DIM, (h + 1) * HEAD_DIM)
        q_ref[:, sl] = _norm_rope(pq[:, sl], qg_ref[...], cos, sin, q_scale).astype(BF16)
    o += ATTN_W
    pk = jnp.dot(u, w_ref[:, o:o + KV_W], preferred_element_type=F32)
    for h in range(N_KV_HEADS):
        sl = slice(h * HEAD_DIM, (h + 1) * HEAD_DIM)
        k_ref[:, sl] = _norm_rope(pk[:, sl], kg_ref[...], cos, sin, 1.0).astype(BF16)
    o += KV_W
    pv = jnp.dot(u, w_ref[:, o:o + KV_W], preferred_element_type=F32)
    ones_rows = jnp.where(
        lax.broadcasted_iota(I32, (VT_ROWS - HEAD_DIM, kc), 0) == 0, 1.0, 0.0).astype(BF16)
    for h in range(N_KV_HEADS):
        for cc in range(tm // kc):
            blk = pv[cc * kc:(cc + 1) * kc, h * HEAD_DIM:(h + 1) * HEAD_DIM]
            vt_ref[h, cc, 0:HEAD_DIM, :] = blk.T.astype(BF16)
            vt_ref[h, cc, HEAD_DIM:VT_ROWS, :] = ones_rows
    o += KV_W
    xr_ref[...] = jnp.dot(u, w_ref[:, o:o + RG_W], preferred_element_type=F32)
    o += RG_W
    yr_ref[...] = jnp.dot(u, w_ref[:, o:o + RG_W], preferred_element_type=F32)


def _in_projection(x, sc, sh, w_in, qg, kg, cos, sin, tm, kv_chunk):
    s, d = x.shape
    row = lambda i: (i, 0)
    nc = tm // kv_chunk
    return pl.pallas_call(
        functools.partial(_inproj_kernel, q_scale=HEAD_DIM ** -0.5 * LOG2E),
        grid=(s // tm,),
        in_specs=[pl.BlockSpec((tm, d), row),
                  _const_spec((1, d)), _const_spec((1, d)),
                  pl.BlockSpec((d, PROJ_W), lambda i: (0, 0), pipeline_mode=pl.Buffered(1)),
                  _const_spec((1, HEAD_DIM)), _const_spec((1, HEAD_DIM)),
                  pl.BlockSpec((tm, HEAD_DIM), row), pl.BlockSpec((tm, HEAD_DIM), row)],
        out_specs=[pl.BlockSpec((tm, ATTN_W), row), pl.BlockSpec((tm, KV_W), row),
                   pl.BlockSpec((N_KV_HEADS, nc, VT_ROWS, kv_chunk), lambda i: (0, i, 0, 0)),
                   pl.BlockSpec((tm, RG_W), row), pl.BlockSpec((tm, RG_W), row)],
        out_shape=[jax.ShapeDtypeStruct((s, ATTN_W), BF16),
                   jax.ShapeDtypeStruct((s, KV_W), BF16),
                   jax.ShapeDtypeStruct((N_KV_HEADS, s // kv_chunk, VT_ROWS, kv_chunk), BF16),
                   jax.ShapeDtypeStruct((s, RG_W), F32),
                   jax.ShapeDtypeStruct((s, RG_W), F32)],
        compiler_params=_cparams("arbitrary"),
        name="in_projection",
    )(x, sc, sh, w_in, qg, kg, cos, sin)


def _attn_kernel(q_ref, kc_ref, vtc_ref, kl_ref, vtl_ref, o_ref, sa_ref, sb_ref,
                 *, tq, tk, n_chunks):
    q = q_ref[...]
    qs = jnp.concatenate(
        [q[:, g * HEAD_DIM:(g + 1) * HEAD_DIM] for g in range(GQA_GROUP)], axis=0)
    cols = GQA_GROUP * tq

    def scores(k):
        return lax.dot_general(k, qs, (((1,), (1,)), ((), ())), preferred_element_type=F32)

    def softmax_pv(s, vt, m, acc):
        m_new = jnp.maximum(m, jnp.max(s, axis=0, keepdims=True))
        alpha = jnp.exp2(m - m_new)
        p = jnp.exp2(s - m_new).astype(BF16)
        acc = alpha * acc + jnp.dot(vt, p, preferred_element_type=F32)
        return m_new, acc

    m0 = jnp.full((1, cols), NEG_BIG, F32)
    a0 = jnp.zeros((VT_ROWS, cols), F32)
    m, acc = softmax_pv(scores(kc_ref[...]), vtc_ref[0], m0, a0)

    def latent_scores(j):
        return scores(kl_ref[pl.ds(pl.multiple_of(j * tk, tk), tk), :])

    sa_ref[...] = latent_scores(0)

    def body(i, carry):
        m, acc = carry
        j = 2 * i
        sb_ref[...] = latent_scores(j + 1)
        m, acc = softmax_pv(sa_ref[...], vtl_ref[j], m, acc)
        sa_ref[...] = latent_scores(jnp.minimum(j + 2, n_chunks - 1))
        m, acc = softmax_pv(sb_ref[...], vtl_ref[j + 1], m, acc)
        return m, acc

    _, acc = lax.fori_loop(0, n_chunks // 2, body, (m, acc))
    out_t = acc[0:HEAD_DIM, :] / acc[HEAD_DIM:HEAD_DIM + 1, :]
    for g in range(GQA_GROUP):
        o_ref[:, g * HEAD_DIM:(g + 1) * HEAD_DIM] = out_t[:, g * tq:(g + 1) * tq].T.astype(BF16)


def _attention(q, k_c, vt_c, k_l, vt_l, tq):
    s = q.shape[0]
    n_ctx = k_c.shape[0]
    n_chunks, _, tk = vt_l.shape[1:]
    assert n_chunks % 2 == 0
    gw = GQA_GROUP * HEAD_DIM
    return pl.pallas_call(
        functools.partial(_attn_kernel, tq=tq, tk=tk, n_chunks=n_chunks),
        grid=(N_KV_HEADS, s // tq),
        in_specs=[pl.BlockSpec((tq, gw), lambda h, i: (i, h)),
                  pl.BlockSpec((n_ctx, HEAD_DIM), lambda h, i: (0, h)),
                  pl.BlockSpec((None, 1, VT_ROWS, n_ctx), lambda h, i: (h, 0, 0, 0)),
                  pl.BlockSpec((s, HEAD_DIM), lambda h, i: (0, h)),
                  pl.BlockSpec((None, n_chunks, VT_ROWS, tk), lambda h, i: (h, 0, 0, 0))],
        out_specs=pl.BlockSpec((tq, gw), lambda h, i: (i, h)),
        out_shape=jax.ShapeDtypeStruct((s, ATTN_W), BF16),
        scratch_shapes=[pltpu.VMEM((tk, GQA_GROUP * tq), F32),
                        pltpu.VMEM((tk, GQA_GROUP * tq), F32)],
        compiler_params=_cparams("arbitrary", "arbitrary"),
        name="attention",
    )(q, k_c, vt_c, k_l, vt_l)


def _log_sigmoid(x):
    return jnp.minimum(x, 0.0) - jnp.log1p(jnp.exp(-jnp.abs(x)))


def _rg_kernel(x_ref, xp_ref, xn_ref, cw_ref, cb_ref, wg_ref, bg_ref, lam_ref, h0_ref,
               *rest, reverse, final, t, n_chunks):
    if final:
        hf_ref, yr_ref, o_ref, a_scr, b_scr, hc_scr, h_scr = rest
    else:
        o_ref, a_scr, b_scr, hc_scr = rest
        h_scr = o_ref
    i = pl.program_id(0)
    c = (n_chunks - 1 - i) if reverse else i
    w = x_ref.shape[1]

    @pl.when(i == 0)
    def _():
        hc_scr[...] = jnp.broadcast_to(h0_ref[...], (SUBLANES, w))

    x = x_ref[...]
    row = lax.broadcasted_iota(I32, (t, w), 0)
    pm = jnp.where(c == 0, 0.0, 1.0).astype(F32)
    nm = jnp.where(c == n_chunks - 1, 0.0, 1.0).astype(F32)
    p6 = xp_ref[SUBLANES - 2:SUBLANES - 1, :] * pm
    p7 = xp_ref[SUBLANES - 1:SUBLANES, :] * pm
    n0 = xn_ref[0:1, :] * nm
    x_m1 = jnp.where(row == 0, p7, pltpu.roll(x, 1, 0))
    x_m2 = jnp.where(row == 0, p6, jnp.where(row == 1, p7, pltpu.roll(x, 2, 0)))
    x_p1 = jnp.where(row == t - 1, n0, pltpu.roll(x, t - 1, 0))
    xc = cb_ref[...] + cw_ref[0:1, :] * x_m2
    xc = xc + cw_ref[1:2, :] * x_m1
    xc = xc + cw_ref[2:3, :] * x
    xc = xc + cw_ref[3:4, :] * x_p1

    xcb = xc.astype(BF16)
    clam = RG_C * _log_sigmoid(lam_ref[...])
    for h in range(RG_HEADS):
        sl = slice(h * RG_HD, (h + 1) * RG_HD)
        g = jnp.dot(xcb[:, sl], wg_ref[h], preferred_element_type=F32)
        r = jax.nn.sigmoid(g[:, :RG_HD] + bg_ref[0:1, sl])
        gi = jax.nn.sigmoid(g[:, RG_HD:] + bg_ref[1:2, sl])
        log_a = r * clam[:, sl]
        a = jnp.exp(log_a)
        a_scr[:, sl] = a
        b_scr[:, sl] = jnp.sqrt(-jnp.tanh(log_a) * (a * a + 1.0)) * (gi * xc[:, sl])

    srow = lax.broadcasted_iota(I32, (SUBLANES, w), 0)
    n_tiles = t // SUBLANES

    def tile_body(j, hprev):
        tile = (n_tiles - 1 - j) if reverse else j
        start = pl.multiple_of(tile * SUBLANES, SUBLANES)
        a = a_scr[pl.ds(start, SUBLANES), :]
        b = b_scr[pl.ds(start, SUBLANES), :]
        for k in (1, 2, 4):
            if reverse:
                keep = srow < SUBLANES - k
                shift = SUBLANES - k
            else:
                keep = srow >= k
                shift = k
            a_sh = jnp.where(keep, pltpu.roll(a, shift, 0), 1.0)
            b_sh = jnp.where(keep, pltpu.roll(b, shift, 0), 0.0)
            b = a * b_sh + b
            a = a * a_sh
        hh = a * hprev + b
        h_scr[pl.ds(start, SUBLANES), :] = hh
        last = hh[0:1, :] if reverse else hh[SUBLANES - 1:SUBLANES, :]
        return jnp.broadcast_to(last, (SUBLANES, w))

    hc_scr[...] = lax.fori_loop(0, n_tiles, tile_body, hc_scr[...])

    if final:
        gate = jax.nn.gelu(yr_ref[...], approximate=True)
        o_ref[...] = ((hf_ref[...] + h_scr[...]) * gate).astype(o_ref.dtype)


def _rg_scan(xr, conv_w, conv_b, wg, bg, lam, h0, *, reverse, t, hf=None, yr=None):
    s, w = xr.shape
    n_chunks = s // t
    final = hf is not None
    tb = t // SUBLANES
    last_blk = s // SUBLANES - 1
    if reverse:
        cidx = lambda i: n_chunks - 1 - i
    else:
        cidx = lambda i: i
    chunk_spec = pl.BlockSpec((t, w), lambda i: (cidx(i), 0))
    in_specs = [chunk_spec,
                pl.BlockSpec((SUBLANES, w), lambda i: (jnp.maximum(cidx(i) * tb - 1, 0), 0)),
                pl.BlockSpec((SUBLANES, w), lambda i: (jnp.minimum((cidx(i) + 1) * tb, last_blk), 0)),
                _const_spec((4, w)), _const_spec((1, w)),
                _const_spec((RG_HEADS, RG_HD, 2 * RG_HD)), _const_spec((2, w)),
                _const_spec((1, w)), _const_spec((1, w))]
    args = [xr, xr, xr, conv_w, conv_b, wg, bg, lam, h0]
    scratch = [pltpu.VMEM((t, w), F32), pltpu.VMEM((t, w), F32), pltpu.VMEM((SUBLANES, w), F32)]
    if final:
        in_specs += [chunk_spec, chunk_spec]
        args += [hf, yr]
        scratch.append(pltpu.VMEM((t, w), F32))
        out_dtype = BF16
    else:
        out_dtype = F32
    return pl.pallas_call(
        functools.partial(_rg_kernel, reverse=reverse, final=final, t=t, n_chunks=n_chunks),
        grid=(n_chunks,),
        in_specs=in_specs,
        out_specs=chunk_spec,
        out_shape=jax.ShapeDtypeStruct((s, w), out_dtype),
        scratch_shapes=scratch,
        compiler_params=_cparams("arbitrary"),
        name="rglru_bwd" if reverse else "rglru_fwd",
    )(*args)


def _layer_norm(y, g, b):
    mu = jnp.mean(y, axis=-1, keepdims=True)
    yc = y - mu
    var = jnp.mean(yc * yc, axis=-1, keepdims=True)
    return yc * lax.rsqrt(var + NORM_EPS) * g + b


def _outproj_kernel(attn_ref, rg_ref, x_ref, w_ref, g1_ref, lg_ref, lb_ref, sc2_ref, sh2_ref,
                    h1_ref, v_ref):
    mix = jnp.dot(attn_ref[...], w_ref[0:ATTN_W, :], preferred_element_type=F32)
    mix = mix + jnp.dot(rg_ref[...], w_ref[ATTN_W:, :], preferred_element_type=F32)
    h1 = _layer_norm(DEEPNORM_ALPHA * x_ref[...] + g1_ref[...] * mix, lg_ref[...], lb_ref[...])
    h1_ref[...] = h1
    v_ref[...] = h1 * (1.0 + sc2_ref[...]) + sh2_ref[...]


def _out_projection(attn, rg, x, w_out, g1, ln_g, ln_b, sc2, sh2, tm):
    s, d = x.shape
    row = lambda i: (i, 0)
    vec = _const_spec((1, d))
    return pl.pallas_call(
        _outproj_kernel,
        grid=(s // tm,),
        in_specs=[pl.BlockSpec((tm, ATTN_W), row), pl.BlockSpec((tm, RG_W), row),
                  pl.BlockSpec((tm, d), row),
                  pl.BlockSpec((ATTN_W + RG_W, d), lambda i: (0, 0), pipeline_mode=pl.Buffered(1)),
                  vec, vec, vec, vec, vec],
        out_specs=[pl.BlockSpec((tm, d), row), pl.BlockSpec((tm, d), row)],
        out_shape=[jax.ShapeDtypeStruct((s, d), F32), jax.ShapeDtypeStruct((s, d), F32)],
        compiler_params=_cparams("arbitrary"),
        name="out_projection",
    )(attn, rg, x, w_out, g1, ln_g, ln_b, sc2, sh2)


def _first_index_of_max(x, iota_f, axis):
    mx = jnp.max(x, axis=axis, keepdims=True)
    idx = jnp.min(jnp.where(x == mx, iota_f, float(N_EXPERTS)), axis=axis, keepdims=True)
    return mx, idx


def _router_kernel(v_ref, whi_ref, wlo_ref, eb_ref, tri_ref,
                   idx_ref, wts_ref, rank_ref, cnt_ref, base_scr):
    v = v_ref[...]
    tm = v.shape[0]

    @pl.when(pl.program_id(0) == 0)
    def _():
        base_scr[...] = jnp.zeros(base_scr.shape, F32)

    v_hi = v.astype(BF16)
    v_lo = (v - v_hi.astype(F32)).astype(BF16)
    logits = jnp.dot(v_hi, whi_ref[...], preferred_element_type=F32)
    logits = logits + jnp.dot(v_lo, whi_ref[...], preferred_element_type=F32)
    logits = logits + jnp.dot(v_hi, wlo_ref[...], preferred_element_type=F32)
    lt = logits.T[0:N_EXPERTS, :]
    scores = jax.nn.sigmoid(lt)
    biased = scores + eb_ref[...]
    neg_inf = float("-inf")

    ig = lax.broadcasted_iota(I32, (GROUP_SIZE, tm), 0).astype(F32)
    groups = [biased[g * GROUP_SIZE:(g + 1) * GROUP_SIZE, :] for g in range(N_GROUPS)]
    gscore = []
    for bg in groups:
        top1, i1 = _first_index_of_max(bg, ig, 0)
        top2 = jnp.max(jnp.where(ig == i1, neg_inf, bg), axis=0, keepdims=True)
        gscore.append(top1 + top2)

    masked = []
    for g in range(N_GROUPS):
        ahead = jnp.zeros((1, tm), F32)
        for o in range(N_GROUPS):
            if o == g:
                continue
            before = (gscore[o] >= gscore[g]) if o < g else (gscore[o] > gscore[g])
            ahead = ahead + jnp.where(before, 1.0, 0.0)
        keep = jnp.broadcast_to(ahead < TOPK_GROUPS, (GROUP_SIZE, tm))
        masked.append(jnp.where(keep, groups[g], neg_inf))
    masked = jnp.concatenate(masked, axis=0)

    ie = lax.broadcasted_iota(I32, masked.shape, 0).astype(F32)
    seen = base_scr[...]
    ws = []
    for k in range(TOP_K):
        _, ei = _first_index_of_max(masked, ie, 0)
        hit = ie == ei
        idx_ref[k:k + 1, :] = ei.astype(I32)
        ws.append(jnp.sum(jnp.where(hit, scores, 0.0), axis=0, keepdims=True))
        masked = jnp.where(hit, neg_inf, masked)
        onehot = jnp.where(hit, 1.0, 0.0)
        before = jnp.dot(onehot.astype(BF16), tri_ref[...], preferred_element_type=F32)
        rank = jnp.sum(jnp.where(hit, before + seen, 0.0), axis=0, keepdims=True)
        rank_ref[k:k + 1, :] = rank.astype(I32)
        seen = seen + jnp.sum(onehot, axis=1, keepdims=True)
    base_scr[...] = seen
    cnt_ref[...] = jnp.broadcast_to(seen, cnt_ref.shape)
    total = ws[0]
    for k in range(1, TOP_K):
        total = total + ws[k]
    for k in range(TOP_K):
        wts_ref[k:k + 1, :] = ws[k] / total * ROUTED_SCALE


def _router(v, w_hi, w_lo, e_bias_col, tm):
    s, d = v.shape
    tri = jnp.triu(jnp.ones((tm, tm), BF16), k=1)
    slot = pl.BlockSpec((TOP_K, tm), lambda i: (0, i))
    return pl.pallas_call(
        _router_kernel,
        grid=(s // tm,),
        in_specs=[pl.BlockSpec((tm, d), lambda i: (i, 0)),
                  _const_spec((d, LANES)), _const_spec((d, LANES)),
                  _const_spec((N_EXPERTS, 1)), _const_spec((tm, tm))],
        out_specs=[slot, slot, slot, _const_spec((N_EXPERTS, LANES))],
        out_shape=[jax.ShapeDtypeStruct((TOP_K, s), I32),
                   jax.ShapeDtypeStruct((TOP_K, s), F32),
                   jax.ShapeDtypeStruct((TOP_K, s), I32),
                   jax.ShapeDtypeStruct((N_EXPERTS, LANES), F32)],
        scratch_shapes=[pltpu.VMEM((N_EXPERTS, 1), F32)],
        compiler_params=_cparams("arbitrary"),
        name="router",
    )(v, w_hi, w_lo, e_bias_col, tri)


def _dispatch_kernel(last_ref, has_ref, v_ref, idx_ref, rank_ref, pstart_ref,
                     dest_ref, xs_hbm, dest_v, dest_s, zeros_v, sem_z, sem_i, sem_r, *, bm):
    tm = v_ref.shape[0]

    def zero_fill(e):
        row0 = pl.multiple_of(last_ref[e], bm)
        return pltpu.make_async_copy(zeros_v, xs_hbm.at[pl.ds(row0, bm), :], sem_z)

    @pl.when(pl.program_id(0) == 0)
    def _():
        zeros_v[...] = jnp.zeros(zeros_v.shape, zeros_v.dtype)

        def start(e, carry):
            @pl.when(has_ref[e] > 0)
            def _():
                zero_fill(e).start()
            return carry

        def wait(e, carry):
            @pl.when(has_ref[e] > 0)
            def _():
                zero_fill(e).wait()
            return carry

        lax.fori_loop(0, N_EXPERTS, start, 0)
        lax.fori_loop(0, N_EXPERTS, wait, 0)

    ie = lax.broadcasted_iota(I32, (N_EXPERTS, tm), 0)
    pstart = pstart_ref[...]
    for k in range(TOP_K):
        hit = ie == idx_ref[k:k + 1, :]
        seg = jnp.sum(jnp.where(hit, pstart, 0.0), axis=0, keepdims=True)
        dest_v[k:k + 1, :] = seg.astype(I32) + rank_ref[k:k + 1, :]
    dest_ref[...] = dest_v[...]
    cp = pltpu.make_async_copy(dest_v, dest_s, sem_i)
    cp.start()
    cp.wait()

    def row_copy(k, t):
        return pltpu.make_async_copy(v_ref.at[pl.ds(t, 1), :],
                                     xs_hbm.at[pl.ds(dest_s[k, t], 1), :], sem_r)

    def start_rows(t, carry):
        for k in range(TOP_K):
            row_copy(k, t).start()
        return carry

    def wait_rows(t, carry):
        for k in range(TOP_K):
            row_copy(k, t).wait()
        return carry

    lax.fori_loop(0, tm, start_rows, 0)
    lax.fori_loop(0, tm, wait_rows, 0)


def _dispatch(last_row, has_blk, v, idx, rank, pstart_col, n_rows, tm, bm):
    s, d = v.shape
    slot = pl.BlockSpec((TOP_K, tm), lambda i, *_: (0, i))
    grid_spec = pltpu.PrefetchScalarGridSpec(
        num_scalar_prefetch=2,
        grid=(s // tm,),
        in_specs=[pl.BlockSpec((tm, d), lambda i, *_: (i, 0)), slot, slot,
                  pl.BlockSpec((N_EXPERTS, 1), lambda i, *_: (0, 0))],
        out_specs=[slot, pl.BlockSpec(memory_space=pl.ANY)],
        scratch_shapes=[pltpu.VMEM((TOP_K, tm), I32), pltpu.SMEM((TOP_K, tm), I32),
                        pltpu.VMEM((bm, d), v.dtype),
                        pltpu.SemaphoreType.DMA, pltpu.SemaphoreType.DMA,
                        pltpu.SemaphoreType.DMA])
    return pl.pallas_call(
        functools.partial(_dispatch_kernel, bm=bm),
        grid_spec=grid_spec,
        out_shape=[jax.ShapeDtypeStruct((TOP_K, s), I32),
                   jax.ShapeDtypeStruct((n_rows, d), v.dtype)],
        compiler_params=_cparams("arbitrary"),
        name="dispatch",
    )(last_row, has_blk, v, idx, rank, pstart_col)


def _expert_kernel(blk_e_ref, n_used_ref, x_ref, w1_ref, w3_ref, w2_ref, y_ref, w1b, w3b, w2b):
    b = pl.program_id(0)

    @pl.when(b < n_used_ref[0])
    def _():
        prev_e = blk_e_ref[jnp.maximum(b - 1, 0)]

        @pl.when(jnp.logical_or(b == 0, blk_e_ref[b] != prev_e))
        def _():
            w1b[...] = w1_ref[...].astype(BF16)
            w3b[...] = w3_ref[...].astype(BF16)
            w2b[...] = w2_ref[...].astype(BF16)

        xb = x_ref[...].astype(BF16)
        h1 = jnp.dot(xb, w1b[...], preferred_element_type=F32)
        h3 = jnp.dot(xb, w3b[...], preferred_element_type=F32)
        act = (h1 * jax.nn.sigmoid(h1) * h3).astype(BF16)
        y_ref[...] = jnp.dot(act, w2b[...], preferred_element_type=F32)


def _experts(blk_e, n_used, xs, w_e1, w_e3, w_e2, bm):
    n_rows, d = xs.shape
    ff = w_e1.shape[-1]
    rows = lambda b, be, nu: (jnp.minimum(b, nu[0] - 1), 0)
    wspec1 = pl.BlockSpec((None, d, ff), lambda b, be, nu: (be[b], 0, 0))
    wspec2 = pl.BlockSpec((None, ff, d), lambda b, be, nu: (be[b], 0, 0))
    grid_spec = pltpu.PrefetchScalarGridSpec(
        num_scalar_prefetch=2,
        grid=(n_rows // bm,),
        in_specs=[pl.BlockSpec((bm, d), rows), wspec1, wspec1, wspec2],
        out_specs=pl.BlockSpec((bm, d), rows),
        scratch_shapes=[pltpu.VMEM((d, ff), BF16), pltpu.VMEM((d, ff), BF16),
                        pltpu.VMEM((ff, d), BF16)])
    return pl.pallas_call(
        _expert_kernel,
        grid_spec=grid_spec,
        out_shape=jax.ShapeDtypeStruct((n_rows, d), F32),
        compiler_params=_cparams("arbitrary"),
        name="experts",
    )(blk_e, n_used, xs, w_e1, w_e3, w_e2)


def _block_tables(counts, n_blocks, bm):
    padded = (counts + bm - 1) // bm * bm
    pad_end = jnp.cumsum(padded)
    pad_start = pad_end - padded
    blk_first = jnp.arange(n_blocks, dtype=I32) * bm
    blk_e = jnp.minimum(jnp.sum(pad_end[None, :] <= blk_first[:, None], axis=1), N_EXPERTS - 1)
    n_used = pad_end[-1] // bm
    blk_e = jnp.where(jnp.arange(n_blocks) < n_used, blk_e, blk_e[jnp.maximum(n_used - 1, 0)])
    last_row = jnp.maximum(pad_end - bm, 0)
    return (pad_start, last_row.astype(I32), (padded > 0).astype(I32),
            blk_e.astype(I32), n_used.astype(I32).reshape(1))


def _combine_kernel(dest_ref, wts_ref, v_ref, h1_ref, g2_ref, lg_ref, lb_ref,
                    ws1_ref, ws3_ref, ws2_ref, ys_hbm, o_ref, dest_s, ybuf, sem_i, sem_r):
    tm = v_ref.shape[0]
    cp = pltpu.make_async_copy(dest_ref, dest_s, sem_i)
    cp.start()
    cp.wait()

    def row_copy(k, t):
        return pltpu.make_async_copy(ys_hbm.at[pl.ds(dest_s[k, t], 1), :],
                                     ybuf.at[k, pl.ds(t, 1), :], sem_r)

    def start_rows(t, carry):
        for k in range(TOP_K):
            row_copy(k, t).start()
        return carry

    def wait_rows(t, carry):
        for k in range(TOP_K):
            row_copy(k, t).wait()
        return carry

    lax.fori_loop(0, tm, start_rows, 0)

    vb = v_ref[...].astype(BF16)
    a1 = jnp.dot(vb, ws1_ref[...], preferred_element_type=F32)
    a3 = jnp.dot(vb, ws3_ref[...], preferred_element_type=F32)
    act = (a1 * jax.nn.sigmoid(a1) * a3).astype(BF16)
    shared = jnp.dot(act, ws2_ref[...], preferred_element_type=F32)
    wpad = jnp.concatenate([wts_ref[...], jnp.zeros((LANES - TOP_K, tm), F32)], axis=0)
    wt = wpad.T

    lax.fori_loop(0, tm, wait_rows, 0)
    ff = ybuf[0] * wt[:, 0:1]
    for k in range(1, TOP_K):
        ff = ff + ybuf[k] * wt[:, k:k + 1]
    ff = ff + shared
    o_ref[...] = _layer_norm(DEEPNORM_ALPHA * h1_ref[...] + g2_ref[...] * ff,
                             lg_ref[...], lb_ref[...])


def _combine(dest, wts, v, h1, g2, ln_g, ln_b, w_s1, w_s3, w_s2, ys, tm):
    s, d = v.shape
    row = lambda i: (i, 0)
    vec = _const_spec((1, d))
    slot = pl.BlockSpec((TOP_K, tm), lambda i: (0, i))
    return pl.pallas_call(
        _combine_kernel,
        grid=(s // tm,),
        in_specs=[slot, slot, pl.BlockSpec((tm, d), row), pl.BlockSpec((tm, d), row),
                  vec, vec, vec,
                  _const_spec(w_s1.shape), _const_spec(w_s3.shape), _const_spec(w_s2.shape),
                  pl.BlockSpec(memory_space=pl.ANY)],
        out_specs=pl.BlockSpec((tm, d), row),
        out_shape=jax.ShapeDtypeStruct((s, d), F32),
        scratch_shapes=[pltpu.SMEM((TOP_K, tm), I32), pltpu.VMEM((TOP_K, tm, d), F32),
                        pltpu.SemaphoreType.DMA, pltpu.SemaphoreType.DMA],
        compiler_params=_cparams("arbitrary"),
        name="combine",
    )(dest, wts, v, h1, g2, ln_g, ln_b, w_s1, w_s3, w_s2, ys)


def _rope_tables(s):
    half = HEAD_DIM // 4
    inv_freq = ROPE_THETA ** (-jnp.arange(half, dtype=F32) / half)
    t = jnp.arange(s, dtype=I32)
    ang_r = (t // GRID_W).astype(F32)[:, None] * inv_freq[None, :]
    ang_c = (t % GRID_W).astype(F32)[:, None] * inv_freq[None, :]
    cos = jnp.concatenate([jnp.cos(ang_r)] * 2 + [jnp.cos(ang_c)] * 2, axis=-1)
    sin = jnp.concatenate([-jnp.sin(ang_r), jnp.sin(ang_r), -jnp.sin(ang_c), jnp.sin(ang_c)], axis=-1)
    return cos, sin


def kernel(x, c, ctx, c_ctx, w_mod, b_mod, w_in, q_norm, k_norm, conv_w, conv_b, rg_wa, rg_ba,
           rg_wx, rg_bx, rg_lam, w_out, ln1_g, ln1_b, w_router, e_bias, w_e1, w_e3, w_e2,
           w_s1, w_s3, w_s2, ln2_g, ln2_b):
    assert x.shape[0] == 1 and w_mod.shape[0] == DEPTH
    _, s, d = x.shape
    n_ctx = ctx.shape[1]
    x2 = x[0]

    c2t = jnp.stack([c[0], c_ctx], axis=1)
    mod = _modulation(c2t, w_mod[0], b_mod[0].reshape(1, -1))
    sh1, sc1, g1, sh2, sc2, g2 = [mod[0:1, j * d:(j + 1) * d] for j in range(6)]
    csh1, csc1 = mod[1:2, 0:d], mod[1:2, d:2 * d]

    w_in_b = w_in[0].astype(BF16)
    qg = q_norm[0].reshape(1, HEAD_DIM)
    kg = k_norm[0].reshape(1, HEAD_DIM)
    cos, sin = _rope_tables(s)
    q_l, k_l, vt_l, xr_l, yr_l = _in_projection(x2, sc1, sh1, w_in_b, qg, kg, cos, sin,
                                                tm=min(512, s), kv_chunk=min(KV_CHUNK, s))
    _, k_c, vt_c, xr_c, _ = _in_projection(
        ctx[0], csc1, csh1, w_in_b, qg, kg,
        jnp.ones((n_ctx, HEAD_DIM), F32), jnp.zeros((n_ctx, HEAD_DIM), F32),
        tm=n_ctx, kv_chunk=n_ctx)

    attn = _attention(q_l, k_c, vt_c, k_l, vt_l, tq=min(256, s))

    wg = jnp.concatenate([rg_wa[0], rg_wx[0]], axis=-1).astype(BF16)
    cb = conv_b[0].reshape(1, RG_W)
    zero_state = jnp.zeros((1, RG_W), F32)
    rg_args = []
    for dd in range(2):
        rg_args.append((conv_w[0], cb, wg[dd],
                        jnp.stack([rg_ba[0, dd], rg_bx[0, dd]], axis=0),
                        rg_lam[0, dd].reshape(1, RG_W)))
    t_rg = min(512, s)
    hc_f = _rg_scan(xr_c, *rg_args[0], zero_state, reverse=False, t=n_ctx)
    hc_b = _rg_scan(xr_c, *rg_args[1], zero_state, reverse=True, t=n_ctx)
    h_f = _rg_scan(xr_l, *rg_args[0], hc_f[n_ctx - 1:n_ctx], reverse=False, t=t_rg)
    rg = _rg_scan(xr_l, *rg_args[1], hc_b[0:1], reverse=True, t=t_rg, hf=h_f, yr=yr_l)

    h1, v = _out_projection(attn, rg, x2, w_out[0].astype(BF16), g1,
                            ln1_g[0].reshape(1, d), ln1_b[0].reshape(1, d), sc2, sh2,
                            tm=min(256, s))

    wr = jnp.pad(w_router[0], ((0, 0), (0, LANES - N_EXPERTS)))
    wr_hi = wr.astype(BF16)
    wr_lo = (wr - wr_hi.astype(F32)).astype(BF16)
    idx, wts, rank, cnt = _router(v, wr_hi, wr_lo, e_bias[0].reshape(N_EXPERTS, 1), tm=min(256, s))

    bm = EXPERT_BLOCK
    n_blocks = s * TOP_K // bm + N_EXPERTS
    pad_start, last_row, has_blk, blk_e, n_used = _block_tables(cnt[:, 0].astype(I32), n_blocks, bm)
    dest, xs = _dispatch(last_row, has_blk, v, idx, rank,
                         pad_start.astype(F32).reshape(N_EXPERTS, 1), n_blocks * bm,
                         tm=min(256, s), bm=bm)
    ys = _experts(blk_e, n_used, xs, w_e1[0], w_e3[0], w_e2[0], bm)

    out = _combine(dest, wts, v, h1, g2, ln2_g[0].reshape(1, d), ln2_b[0].reshape(1, d),
                   w_s1[0].astype(BF16), w_s3[0].astype(BF16), w_s2[0].astype(BF16), ys,
                   tm=min(128, s))
    return out[None]
```

```python
import functools

import jax
import jax.numpy as jnp
from jax import lax
from jax.experimental import pallas as pl
from jax.experimental.pallas import tpu as pltpu

F32 = jnp.float32
BF16 = jnp.bfloat16
I32 = jnp.int32
U32 = jnp.uint32

GRID_W = 64
HEAD_DIM = 128
N_HEADS = 8
N_KV_HEADS = 2
GQA_GROUP = N_HEADS // N_KV_HEADS
ATTN_W = N_HEADS * HEAD_DIM
KV_W = N_KV_HEADS * HEAD_DIM
ROPE_THETA = 10000.0
RG_W = 1024
RG_HEADS = 8
RG_HD = RG_W // RG_HEADS
RG_C = 8.0
PROJ_W = ATTN_W + 2 * KV_W + 2 * RG_W
N_EXPERTS = 64
N_GROUPS = 8
GROUP_SIZE = N_EXPERTS // N_GROUPS
TOPK_GROUPS = 4
TOP_K = 8
ROUTED_SCALE = 2.5
NORM_EPS = 1e-6
DEPTH = 1
DEEPNORM_ALPHA = (2.0 * DEPTH) ** 0.25
LOG2E = 1.4426950408889634

LANES = 128
SUBLANES = 8
BF16_SUBLANES = 16
VMEM_LIMIT = 56 * 1024 * 1024

NEG_BIG = -1e30
KV_CHUNK = 512
VT_ROWS = HEAD_DIM + BF16_SUBLANES
EXPERT_BLOCK = 256


def _cparams(*sem):
    return pltpu.CompilerParams(dimension_semantics=sem, vmem_limit_bytes=VMEM_LIMIT)


def _const_spec(shape):
    nd = len(shape)
    return pl.BlockSpec(shape, lambda *_: (0,) * nd)


def _mod_kernel(ct_ref, w_ref, b_ref, o_ref, sb_ref, *, tn):
    d = w_ref.shape[0]

    @pl.when(pl.program_id(0) == 0)
    def _():
        ct = ct_ref[...]
        s = ct * jax.nn.sigmoid(ct)
        sb_ref[0] = jnp.broadcast_to(s[:, 0:1], (d, LANES))
        sb_ref[1] = jnp.broadcast_to(s[:, 1:2], (d, LANES))

    for c in range(tn // LANES):
        sl = slice(c * LANES, (c + 1) * LANES)
        wc = w_ref[:, sl]
        bc = b_ref[:, sl]
        o0 = jnp.sum(wc * sb_ref[0], axis=0, keepdims=True) + bc
        o1 = jnp.sum(wc * sb_ref[1], axis=0, keepdims=True) + bc
        o_ref[:, sl] = jnp.concatenate(
            [o0, o1, jnp.zeros((SUBLANES - 2, LANES), F32)], axis=0)


def _modulation(c2t, w_mod, b_mod):
    d, n = w_mod.shape
    tn = 1024
    return pl.pallas_call(
        functools.partial(_mod_kernel, tn=tn),
        grid=(n // tn,),
        in_specs=[_const_spec((d, 2)),
                  pl.BlockSpec((d, tn), lambda j: (0, j)),
                  pl.BlockSpec((1, tn), lambda j: (0, j))],
        out_specs=pl.BlockSpec((SUBLANES, tn), lambda j: (0, j)),
        out_shape=jax.ShapeDtypeStruct((SUBLANES, n), F32),
        scratch_shapes=[pltpu.VMEM((2, d, LANES), F32)],
        compiler_params=_cparams("arbitrary"),
        name="modulation",
    )(c2t, w_mod, b_mod)


def _swap_half(y):
    lane = lax.broadcasted_iota(I32, y.shape, 1)
    return jnp.where((lane % 64) < 32,
                     pltpu.roll(y, LANES - 32, 1), pltpu.roll(y, 32, 1))


def _norm_rope(ph, g, cos, sin_signed, scale):
    ms = jnp.mean(ph * ph, axis=-1, keepdims=True)
    y = (ph * lax.rsqrt(ms + NORM_EPS)) * g
    y = y * cos + _swap_half(y) * sin_signed
    if scale != 1.0:
        y = y * scale
    return y


def _inproj_kernel(x_ref, sc_ref, sh_ref, w_ref, qg_ref, kg_ref, cos_ref, sin_ref,
                   q_ref, k_ref, vt_ref, xr_ref, yr_ref, *, q_scale):
    tm = x_ref.shape[0]
    kc = vt_ref.shape[-1]
    u = (x_ref[...] * (1.0 + sc_ref[...]) + sh_ref[...]).astype(BF16)
    cos = cos_ref[...]
    sin = sin_ref[...]
    o = 0
    pq = jnp.dot(u, w_ref[:, o:o + ATTN_W], preferred_element_type=F32)
    for h in range(N_HEADS):
        sl = slice(h * HEAD_DIM, (h + 1) * HEAD_DIM)
        q_ref[:, sl] = _norm_rope(pq[:, sl], qg_ref[...], cos, sin, q_scale).astype(BF16)
    o += ATTN_W
    pk = jnp.dot(u, w_ref[:, o:o + KV_W], preferred_element_type=F32)
    for h in range(N_KV_HEADS):
        sl = slice(h * HEAD_DIM, (h + 1) * HEAD_DIM)
        k_ref[:, sl] = _norm_rope(pk[:, sl], kg_ref[...], cos, sin, 1.0).astype(BF16)
    o += KV_W
    pv = jnp.dot(u, w_ref[:, o:o + KV_W], preferred_element_type=F32)
    ones_rows = jnp.where(
        lax.broadcasted_iota(I32, (VT_ROWS - HEAD_DIM, kc), 0) == 0, 1.0, 0.0).astype(BF16)
    for h in range(N_KV_HEADS):
        for cc in range(tm // kc):
            blk = pv[cc * kc:(cc + 1) * kc, h * HEAD_DIM:(h + 1) * HEAD_DIM]
            vt_ref[h, cc, 0:HEAD_DIM, :] = blk.T.astype(BF16)
            vt_ref[h, cc, HEAD_DIM:VT_ROWS, :] = ones_rows
    o += KV_W
    xr_ref[...] = jnp.dot(u, w_ref[:, o:o + RG_W], preferred_element_type=F32)
    o += RG_W
    yr_ref[...] = jnp.dot(u, w_ref[:, o:o + RG_W], preferred_element_type=F32)


def _in_projection(x, sc, sh, w_in, qg, kg, cos, sin, tm, kv_chunk):
    s, d = x.shape
    row = lambda i: (i, 0)
    nc = tm // kv_chunk
    return pl.pallas_call(
        functools.partial(_inproj_kernel, q_scale=HEAD_DIM ** -0.5 * LOG2E),
        grid=(s // tm,),
        in_specs=[pl.BlockSpec((tm, d), row),
                  _const_spec((1, d)), _const_spec((1, d)),
                  pl.BlockSpec((d, PROJ_W), lambda i: (0, 0), pipeline_mode=pl.Buffered(1)),
                  _const_spec((1, HEAD_DIM)), _const_spec((1, HEAD_DIM)),
                  pl.BlockSpec((tm, HEAD_DIM), row), pl.BlockSpec((tm, HEAD_DIM), row)],
        out_specs=[pl.BlockSpec((tm, ATTN_W), row), pl.BlockSpec((tm, KV_W), row),
                   pl.BlockSpec((N_KV_HEADS, nc, VT_ROWS, kv_chunk), lambda i: (0, i, 0, 0)),
                   pl.BlockSpec((tm, RG_W), row), pl.BlockSpec((tm, RG_W), row)],
        out_shape=[jax.ShapeDtypeStruct((s, ATTN_W), BF16),
                   jax.ShapeDtypeStruct((s, KV_W), BF16),
                   jax.ShapeDtypeStruct((N_KV_HEADS, s // kv_chunk, VT_ROWS, kv_chunk), BF16),
                   jax.ShapeDtypeStruct((s, RG_W), F32),
                   jax.ShapeDtypeStruct((s, RG_W), F32)],
        compiler_params=_cparams("arbitrary"),
        name="in_projection",
    )(x, sc, sh, w_in, qg, kg, cos, sin)


def _attn_kernel(q_ref, kc_ref, vtc_ref, kl_ref, vtl_ref, o_ref, sa_ref, sb_ref, pa_ref, pb_ref,
                 *, tq, tk, n_chunks):
    q = q_ref[...]
    qs = jnp.concatenate(
        [q[:, g * HEAD_DIM:(g + 1) * HEAD_DIM] for g in range(GQA_GROUP)], axis=0)
    cols = GQA_GROUP * tq

    def scores(k):
        return lax.dot_general(k, qs, (((1,), (1,)), ((), ())), preferred_element_type=F32)

    def softmax(s, m):
        m_new = jnp.maximum(m, jnp.max(s, axis=0, keepdims=True))
        return m_new, jnp.exp2(m - m_new), jnp.exp2(s - m_new).astype(BF16)

    def weighted_values(vt, p, alpha, acc):
        return alpha * acc + jnp.dot(vt, p, preferred_element_type=F32)

    m0 = jnp.full((1, cols), NEG_BIG, F32)
    a0 = jnp.zeros((VT_ROWS, cols), F32)
    m, alpha, p = softmax(scores(kc_ref[...]), m0)
    acc = weighted_values(vtc_ref[0], p, alpha, a0)

    def latent_scores(j):
        j = jnp.minimum(j, n_chunks - 1)
        return scores(kl_ref[pl.ds(pl.multiple_of(j * tk, tk), tk), :])

    sa_ref[...] = latent_scores(0)
    sb_ref[...] = latent_scores(1)
    m, alpha_a, pa_ref[...] = softmax(sa_ref[...], m)

    def body(i, carry):
        m, acc, alpha_a = carry
        j = 2 * i
        sa_ref[...] = latent_scores(j + 2)
        m, alpha_b, pb_ref[...] = softmax(sb_ref[...], m)
        acc = weighted_values(vtl_ref[j], pa_ref[...], alpha_a, acc)
        sb_ref[...] = latent_scores(j + 3)
        m, alpha_a, pa_ref[...] = softmax(sa_ref[...], m)
        acc = weighted_values(vtl_ref[j + 1], pb_ref[...], alpha_b, acc)
        return m, acc, alpha_a

    _, acc, _ = lax.fori_loop(0, n_chunks // 2, body, (m, acc, alpha_a))
    out_t = acc[0:HEAD_DIM, :] / acc[HEAD_DIM:HEAD_DIM + 1, :]
    for g in range(GQA_GROUP):
        o_ref[:, g * HEAD_DIM:(g + 1) * HEAD_DIM] = out_t[:, g * tq:(g + 1) * tq].T.astype(BF16)


def _attention(q, k_c, vt_c, k_l, vt_l, tq):
    s = q.shape[0]
    n_ctx = k_c.shape[0]
    n_chunks, _, tk = vt_l.shape[1:]
    assert n_chunks % 2 == 0
    gw = GQA_GROUP * HEAD_DIM
    return pl.pallas_call(
        functools.partial(_attn_kernel, tq=tq, tk=tk, n_chunks=n_chunks),
        grid=(N_KV_HEADS, s // tq),
        in_specs=[pl.BlockSpec((tq, gw), lambda h, i: (i, h)),
                  pl.BlockSpec((n_ctx, HEAD_DIM), lambda h, i: (0, h)),
                  pl.BlockSpec((None, 1, VT_ROWS, n_ctx), lambda h, i: (h, 0, 0, 0)),
                  pl.BlockSpec((s, HEAD_DIM), lambda h, i: (0, h)),
                  pl.BlockSpec((None, n_chunks, VT_ROWS, tk), lambda h, i: (h, 0, 0, 0))],
        out_specs=pl.BlockSpec((tq, gw), lambda h, i: (i, h)),
        out_shape=jax.ShapeDtypeStruct((s, ATTN_W), BF16),
        scratch_shapes=[pltpu.VMEM((tk, GQA_GROUP * tq), F32),
                        pltpu.VMEM((tk, GQA_GROUP * tq), F32),
                        pltpu.VMEM((tk, GQA_GROUP * tq), BF16),
                        pltpu.VMEM((tk, GQA_GROUP * tq), BF16)],
        compiler_params=_cparams("arbitrary", "arbitrary"),
        name="attention",
    )(q, k_c, vt_c, k_l, vt_l)


def _log_sigmoid(x):
    return jnp.minimum(x, 0.0) - jnp.log1p(jnp.exp(-jnp.abs(x)))


def _rg_kernel(x_ref, xp_ref, xn_ref, cw_ref, cb_ref, wg_ref, bg_ref, lam_ref, h0_ref,
               *rest, reverse, final, t, n_chunks):
    if final:
        hf_ref, yr_ref, o_ref, a_scr, b_scr, hc_scr, h_scr = rest
    else:
        o_ref, a_scr, b_scr, hc_scr = rest
        h_scr = o_ref
    i = pl.program_id(0)
    c = (n_chunks - 1 - i) if reverse else i
    w = x_ref.shape[1]

    @pl.when(i == 0)
    def _():
        hc_scr[...] = jnp.broadcast_to(h0_ref[...], (SUBLANES, w))

    x = x_ref[...]
    row = lax.broadcasted_iota(I32, (t, w), 0)
    pm = jnp.where(c == 0, 0.0, 1.0).astype(F32)
    nm = jnp.where(c == n_chunks - 1, 0.0, 1.0).astype(F32)
    p6 = xp_ref[SUBLANES - 2:SUBLANES - 1, :] * pm
    p7 = xp_ref[SUBLANES - 1:SUBLANES, :] * pm
    n0 = xn_ref[0:1, :] * nm
    x_m1 = jnp.where(row == 0, p7, pltpu.roll(x, 1, 0))
    x_m2 = jnp.where(row == 0, p6, jnp.where(row == 1, p7, pltpu.roll(x, 2, 0)))
    x_p1 = jnp.where(row == t - 1, n0, pltpu.roll(x, t - 1, 0))
    xc = cb_ref[...] + cw_ref[0:1, :] * x_m2
    xc = xc + cw_ref[1:2, :] * x_m1
    xc = xc + cw_ref[2:3, :] * x
    xc = xc + cw_ref[3:4, :] * x_p1

    xcb = xc.astype(BF16)
    clam = RG_C * _log_sigmoid(lam_ref[...])
    for h in range(RG_HEADS):
        sl = slice(h * RG_HD, (h + 1) * RG_HD)
        g = jnp.dot(xcb[:, sl], wg_ref[h], preferred_element_type=F32)
        r = jax.nn.sigmoid(g[:, :RG_HD] + bg_ref[0:1, sl])
        gi = jax.nn.sigmoid(g[:, RG_HD:] + bg_ref[1:2, sl])
        log_a = r * clam[:, sl]
        a = jnp.exp(log_a)
        a_scr[:, sl] = a
        b_scr[:, sl] = jnp.sqrt(-jnp.tanh(log_a) * (a * a + 1.0)) * (gi * xc[:, sl])

    srow = lax.broadcasted_iota(I32, (SUBLANES, w), 0)
    n_tiles = t // SUBLANES

    def tile_body(j, hprev):
        tile = (n_tiles - 1 - j) if reverse else j
        start = pl.multiple_of(tile * SUBLANES, SUBLANES)
        a = a_scr[pl.ds(start, SUBLANES), :]
        b = b_scr[pl.ds(start, SUBLANES), :]
        for k in (1, 2, 4):
            if reverse:
                keep = srow < SUBLANES - k
                shift = SUBLANES - k
            else:
                keep = srow >= k
                shift = k
            a_sh = jnp.where(keep, pltpu.roll(a, shift, 0), 1.0)
            b_sh = jnp.where(keep, pltpu.roll(b, shift, 0), 0.0)
            b = a * b_sh + b
            a = a * a_sh
        hh = a * hprev + b
        h_scr[pl.ds(start, SUBLANES), :] = hh
        last = hh[0:1, :] if reverse else hh[SUBLANES - 1:SUBLANES, :]
        return jnp.broadcast_to(last, (SUBLANES, w))

    hc_scr[...] = lax.fori_loop(0, n_tiles, tile_body, hc_scr[...])

    if final:
        gate = jax.nn.gelu(yr_ref[...], approximate=True)
        o_ref[...] = ((hf_ref[...] + h_scr[...]) * gate).astype(o_ref.dtype)


def _rg_scan(xr, conv_w, conv_b, wg, bg, lam, h0, *, reverse, t, hf=None, yr=None):
    s, w = xr.shape
    n_chunks = s // t
    final = hf is not None
    tb = t // SUBLANES
    last_blk = s // SUBLANES - 1
    if reverse:
        cidx = lambda i: n_chunks - 1 - i
    else:
        cidx = lambda i: i
    chunk_spec = pl.BlockSpec((t, w), lambda i: (cidx(i), 0))
    in_specs = [chunk_spec,
                pl.BlockSpec((SUBLANES, w), lambda i: (jnp.maximum(cidx(i) * tb - 1, 0), 0)),
                pl.BlockSpec((SUBLANES, w), lambda i: (jnp.minimum((cidx(i) + 1) * tb, last_blk), 0)),
                _const_spec((4, w)), _const_spec((1, w)),
                _const_spec((RG_HEADS, RG_HD, 2 * RG_HD)), _const_spec((2, w)),
                _const_spec((1, w)), _const_spec((1, w))]
    args = [xr, xr, xr, conv_w, conv_b, wg, bg, lam, h0]
    scratch = [pltpu.VMEM((t, w), F32), pltpu.VMEM((t, w), F32), pltpu.VMEM((SUBLANES, w), F32)]
    if final:
        in_specs += [chunk_spec, chunk_spec]
        args += [hf, yr]
        scratch.append(pltpu.VMEM((t, w), F32))
        out_dtype = BF16
    else:
        out_dtype = F32
    return pl.pallas_call(
        functools.partial(_rg_kernel, reverse=reverse, final=final, t=t, n_chunks=n_chunks),
        grid=(n_chunks,),
        in_specs=in_specs,
        out_specs=chunk_spec,
        out_shape=jax.ShapeDtypeStruct((s, w), out_dtype),
        scratch_shapes=scratch,
        compiler_params=_cparams("arbitrary"),
        name="rglru_bwd" if reverse else "rglru_fwd",
    )(*args)


def _layer_norm(y, g, b):
    mu = jnp.mean(y, axis=-1, keepdims=True)
    yc = y - mu
    var = jnp.mean(yc * yc, axis=-1, keepdims=True)
    return yc * lax.rsqrt(var + NORM_EPS) * g + b


def _outproj_kernel(attn_ref, rg_ref, x_ref, w_ref, g1_ref, lg_ref, lb_ref, sc2_ref, sh2_ref,
                    h1_ref, v_ref):
    mix = jnp.dot(attn_ref[...], w_ref[0:ATTN_W, :], preferred_element_type=F32)
    mix = mix + jnp.dot(rg_ref[...], w_ref[ATTN_W:, :], preferred_element_type=F32)
    h1 = _layer_norm(DEEPNORM_ALPHA * x_ref[...] + g1_ref[...] * mix, lg_ref[...], lb_ref[...])
    h1_ref[...] = h1
    v_ref[...] = h1 * (1.0 + sc2_ref[...]) + sh2_ref[...]


def _out_projection(attn, rg, x, w_out, g1, ln_g, ln_b, sc2, sh2, tm):
    s, d = x.shape
    row = lambda i: (i, 0)
    vec = _const_spec((1, d))
    return pl.pallas_call(
        _outproj_kernel,
        grid=(s // tm,),
        in_specs=[pl.BlockSpec((tm, ATTN_W), row), pl.BlockSpec((tm, RG_W), row),
                  pl.BlockSpec((tm, d), row),
                  pl.BlockSpec((ATTN_W + RG_W, d), lambda i: (0, 0), pipeline_mode=pl.Buffered(1)),
                  vec, vec, vec, vec, vec],
        out_specs=[pl.BlockSpec((tm, d), row), pl.BlockSpec((tm, d), row)],
        out_shape=[jax.ShapeDtypeStruct((s, d), F32), jax.ShapeDtypeStruct((s, d), F32)],
        compiler_params=_cparams("arbitrary"),
        name="out_projection",
    )(attn, rg, x, w_out, g1, ln_g, ln_b, sc2, sh2)


def _first_index_of_max(x, iota_f, axis):
    mx = jnp.max(x, axis=axis, keepdims=True)
    idx = jnp.min(jnp.where(x == mx, iota_f, float(N_EXPERTS)), axis=axis, keepdims=True)
    return mx, idx


def _router_kernel(v_ref, whi_ref, wlo_ref, eb_ref, tri_ref,
                   idx_ref, wts_ref, rank_ref, cnt_ref, base_scr):
    v = v_ref[...]
    tm = v.shape[0]

    @pl.when(pl.program_id(0) == 0)
    def _():
        base_scr[...] = jnp.zeros(base_scr.shape, F32)

    v_hi = v.astype(BF16)
    v_lo = (v - v_hi.astype(F32)).astype(BF16)
    logits = jnp.dot(v_hi, whi_ref[...], preferred_element_type=F32)
    logits = logits + jnp.dot(v_lo, whi_ref[...], preferred_element_type=F32)
    logits = logits + jnp.dot(v_hi, wlo_ref[...], preferred_element_type=F32)
    lt = logits.T[0:N_EXPERTS, :]
    scores = jax.nn.sigmoid(lt)
    biased = scores + eb_ref[...]
    neg_inf = float("-inf")

    ig = lax.broadcasted_iota(I32, (GROUP_SIZE, tm), 0).astype(F32)
    groups = [biased[g * GROUP_SIZE:(g + 1) * GROUP_SIZE, :] for g in range(N_GROUPS)]
    gscore = []
    for bg in groups:
        top1, i1 = _first_index_of_max(bg, ig, 0)
        top2 = jnp.max(jnp.where(ig == i1, neg_inf, bg), axis=0, keepdims=True)
        gscore.append(top1 + top2)

    masked = []
    for g in range(N_GROUPS):
        ahead = jnp.zeros((1, tm), F32)
        for o in range(N_GROUPS):
            if o == g:
                continue
            before = (gscore[o] >= gscore[g]) if o < g else (gscore[o] > gscore[g])
            ahead = ahead + jnp.where(before, 1.0, 0.0)
        keep = jnp.broadcast_to(ahead < TOPK_GROUPS, (GROUP_SIZE, tm))
        masked.append(jnp.where(keep, groups[g], neg_inf))
    masked = jnp.concatenate(masked, axis=0)

    ie = lax.broadcasted_iota(I32, masked.shape, 0).astype(F32)
    seen = base_scr[...]
    ws = []
    for k in range(TOP_K):
        _, ei = _first_index_of_max(masked, ie, 0)
        hit = ie == ei
        idx_ref[k:k + 1, :] = ei.astype(I32)
        ws.append(jnp.sum(jnp.where(hit, scores, 0.0), axis=0, keepdims=True))
        masked = jnp.where(hit, neg_inf, masked)
        onehot = jnp.where(hit, 1.0, 0.0)
        before = jnp.dot(onehot.astype(BF16), tri_ref[...], preferred_element_type=F32)
        rank = jnp.sum(jnp.where(hit, before + seen, 0.0), axis=0, keepdims=True)
        rank_ref[k:k + 1, :] = rank.astype(I32)
        seen = seen + jnp.sum(onehot, axis=1, keepdims=True)
    base_scr[...] = seen
    cnt_ref[...] = jnp.broadcast_to(seen, cnt_ref.shape)
    total = ws[0]
    for k in range(1, TOP_K):
        total = total + ws[k]
    for k in range(TOP_K):
        wts_ref[k:k + 1, :] = ws[k] / total * ROUTED_SCALE


def _router(v, w_hi, w_lo, e_bias_col, tm):
    s, d = v.shape
    tri = jnp.triu(jnp.ones((tm, tm), BF16), k=1)
    slot = pl.BlockSpec((TOP_K, tm), lambda i: (0, i))
    return pl.pallas_call(
        _router_kernel,
        grid=(s // tm,),
        in_specs=[pl.BlockSpec((tm, d), lambda i: (i, 0)),
                  _const_spec((d, LANES)), _const_spec((d, LANES)),
                  _const_spec((N_EXPERTS, 1)), _const_spec((tm, tm))],
        out_specs=[slot, slot, slot, _const_spec((N_EXPERTS, LANES))],
        out_shape=[jax.ShapeDtypeStruct((TOP_K, s), I32),
                   jax.ShapeDtypeStruct((TOP_K, s), F32),
                   jax.ShapeDtypeStruct((TOP_K, s), I32),
                   jax.ShapeDtypeStruct((N_EXPERTS, LANES), F32)],
        scratch_shapes=[pltpu.VMEM((N_EXPERTS, 1), F32)],
        compiler_params=_cparams("arbitrary"),
        name="router",
    )(v, w_hi, w_lo, e_bias_col, tri)


def _pack_pairs(x):
    h = x.shape[1] // 2
    lo = lax.bitcast_convert_type(x[:, :h].astype(BF16).astype(F32), U32)
    hi = lax.bitcast_convert_type(x[:, h:].astype(BF16).astype(F32), U32)
    return (lo >> 16) | (hi & jnp.uint32(0xFFFF0000))


def _unpack_pairs(w):
    lo = lax.bitcast_convert_type(w << 16, F32)
    hi = lax.bitcast_convert_type(w & jnp.uint32(0xFFFF0000), F32)
    return jnp.concatenate([lo, hi], axis=1)


def _dispatch_kernel(last_ref, has_ref, v_ref, idx_ref, rank_ref, pstart_ref,
                     dest_ref, xs_hbm, dest_v, dest_s, zeros_v, pk_v, sem_z, sem_i, sem_r, *, bm):
    tm = v_ref.shape[0]

    def zero_fill(e):
        row0 = pl.multiple_of(last_ref[e], bm)
        return pltpu.make_async_copy(zeros_v, xs_hbm.at[pl.ds(row0, bm), :], sem_z)

    @pl.when(pl.program_id(0) == 0)
    def _():
        zeros_v[...] = jnp.zeros(zeros_v.shape, zeros_v.dtype)

        def start(e, carry):
            @pl.when(has_ref[e] > 0)
            def _():
                zero_fill(e).start()
            return carry

        def wait(e, carry):
            @pl.when(has_ref[e] > 0)
            def _():
                zero_fill(e).wait()
            return carry

        lax.fori_loop(0, N_EXPERTS, start, 0)
        lax.fori_loop(0, N_EXPERTS, wait, 0)

    ie = lax.broadcasted_iota(I32, (N_EXPERTS, tm), 0)
    pstart = pstart_ref[...]
    for k in range(TOP_K):
        hit = ie == idx_ref[k:k + 1, :]
        seg = jnp.sum(jnp.where(hit, pstart, 0.0), axis=0, keepdims=True)
        dest_v[k:k + 1, :] = seg.astype(I32) + rank_ref[k:k + 1, :]
    dest_ref[...] = dest_v[...]
    cp = pltpu.make_async_copy(dest_v, dest_s, sem_i)
    cp.start()
    pk_v[...] = _pack_pairs(v_ref[...])
    cp.wait()

    def start_rows(t, carry):
        for k in range(TOP_K):
            pltpu.make_async_copy(pk_v.at[pl.ds(t, 1), :],
                                  xs_hbm.at[pl.ds(dest_s[k, t], 1), :], sem_r).start()
        return carry

    lax.fori_loop(0, tm, start_rows, 0, unroll=4)
    for k in range(TOP_K):
        pltpu.make_async_copy(pk_v, xs_hbm.at[pl.ds(0, tm), :], sem_r).wait()


def _dispatch(last_row, has_blk, v, idx, rank, pstart_col, n_rows, tm, bm):
    s, d = v.shape
    slot = pl.BlockSpec((TOP_K, tm), lambda i, *_: (0, i))
    grid_spec = pltpu.PrefetchScalarGridSpec(
        num_scalar_prefetch=2,
        grid=(s // tm,),
        in_specs=[pl.BlockSpec((tm, d), lambda i, *_: (i, 0)), slot, slot,
                  pl.BlockSpec((N_EXPERTS, 1), lambda i, *_: (0, 0))],
        out_specs=[slot, pl.BlockSpec(memory_space=pl.ANY)],
        scratch_shapes=[pltpu.VMEM((TOP_K, tm), I32), pltpu.SMEM((TOP_K, tm), I32),
                        pltpu.VMEM((bm, d // 2), U32), pltpu.VMEM((tm, d // 2), U32),
                        pltpu.SemaphoreType.DMA, pltpu.SemaphoreType.DMA,
                        pltpu.SemaphoreType.DMA])
    return pl.pallas_call(
        functools.partial(_dispatch_kernel, bm=bm),
        grid_spec=grid_spec,
        out_shape=[jax.ShapeDtypeStruct((TOP_K, s), I32),
                   jax.ShapeDtypeStruct((n_rows, d // 2), U32)],
        compiler_params=_cparams("arbitrary"),
        name="dispatch",
    )(last_row, has_blk, v, idx, rank, pstart_col)


def _expert_kernel(blk_e_ref, n_used_ref, x_ref, w1_ref, w3_ref, w2_ref, y_ref, w1b, w3b, w2b):
    b = pl.program_id(0)

    @pl.when(b < n_used_ref[0])
    def _():
        prev_e = blk_e_ref[jnp.maximum(b - 1, 0)]

        @pl.when(jnp.logical_or(b == 0, blk_e_ref[b] != prev_e))
        def _():
            w1b[...] = w1_ref[...].astype(BF16)
            w3b[...] = w3_ref[...].astype(BF16)
            w2b[...] = w2_ref[...].astype(BF16)

        xb = _unpack_pairs(x_ref[...]).astype(BF16)
        h1 = jnp.dot(xb, w1b[...], preferred_element_type=F32)
        h3 = jnp.dot(xb, w3b[...], preferred_element_type=F32)
        act = (h1 * jax.nn.sigmoid(h1) * h3).astype(BF16)
        y_ref[...] = _pack_pairs(jnp.dot(act, w2b[...], preferred_element_type=F32))


def _experts(blk_e, n_used, xs, w_e1, w_e3, w_e2, bm):
    n_rows = xs.shape[0]
    d, ff = w_e1.shape[-2:]
    rows = lambda b, be, nu: (jnp.minimum(b, nu[0] - 1), 0)
    wspec1 = pl.BlockSpec((None, d, ff), lambda b, be, nu: (be[b], 0, 0))
    wspec2 = pl.BlockSpec((None, ff, d), lambda b, be, nu: (be[b], 0, 0))
    grid_spec = pltpu.PrefetchScalarGridSpec(
        num_scalar_prefetch=2,
        grid=(n_rows // bm,),
        in_specs=[pl.BlockSpec((bm, d // 2), rows), wspec1, wspec1, wspec2],
        out_specs=pl.BlockSpec((bm, d // 2), rows),
        scratch_shapes=[pltpu.VMEM((d, ff), BF16), pltpu.VMEM((d, ff), BF16),
                        pltpu.VMEM((ff, d), BF16)])
    return pl.pallas_call(
        _expert_kernel,
        grid_spec=grid_spec,
        out_shape=jax.ShapeDtypeStruct((n_rows, d // 2), U32),
        compiler_params=_cparams("arbitrary"),
        name="experts",
    )(blk_e, n_used, xs, w_e1, w_e3, w_e2)


def _block_tables(counts, n_blocks, bm):
    padded = (counts + bm - 1) // bm * bm
    pad_end = jnp.cumsum(padded)
    pad_start = pad_end - padded
    blk_first = jnp.arange(n_blocks, dtype=I32) * bm
    blk_e = jnp.minimum(jnp.sum(pad_end[None, :] <= blk_first[:, None], axis=1), N_EXPERTS - 1)
    n_used = pad_end[-1] // bm
    blk_e = jnp.where(jnp.arange(n_blocks) < n_used, blk_e, blk_e[jnp.maximum(n_used - 1, 0)])
    last_row = jnp.maximum(pad_end - bm, 0)
    return (pad_start, last_row.astype(I32), (padded > 0).astype(I32),
            blk_e.astype(I32), n_used.astype(I32).reshape(1))


def _combine_kernel(dest_ref, wts_ref, v_ref, h1_ref, g2_ref, lg_ref, lb_ref,
                    ws1_ref, ws3_ref, ws2_ref, ys_hbm, o_ref, dest_s, ybuf, sem_i, sem_r):
    tm = v_ref.shape[0]
    cp = pltpu.make_async_copy(dest_ref, dest_s, sem_i)
    cp.start()
    cp.wait()

    def start_rows(t, carry):
        for k in range(TOP_K):
            pltpu.make_async_copy(ys_hbm.at[pl.ds(dest_s[k, t], 1), :],
                                  ybuf.at[k, pl.ds(t, 1), :], sem_r).start()
        return carry

    lax.fori_loop(0, tm, start_rows, 0, unroll=4)

    vb = v_ref[...].astype(BF16)
    a1 = jnp.dot(vb, ws1_ref[...], preferred_element_type=F32)
    a3 = jnp.dot(vb, ws3_ref[...], preferred_element_type=F32)
    act = (a1 * jax.nn.sigmoid(a1) * a3).astype(BF16)
    shared = jnp.dot(act, ws2_ref[...], preferred_element_type=F32)
    wpad = jnp.concatenate([wts_ref[...], jnp.zeros((LANES - TOP_K, tm), F32)], axis=0)
    wt = wpad.T

    for k in range(TOP_K):
        pltpu.make_async_copy(ys_hbm.at[pl.ds(0, tm), :], ybuf.at[k], sem_r).wait()
    ff = _unpack_pairs(ybuf[0]) * wt[:, 0:1]
    for k in range(1, TOP_K):
        ff = ff + _unpack_pairs(ybuf[k]) * wt[:, k:k + 1]
    ff = ff + shared
    o_ref[...] = _layer_norm(DEEPNORM_ALPHA * h1_ref[...] + g2_ref[...] * ff,
                             lg_ref[...], lb_ref[...])


def _combine(dest, wts, v, h1, g2, ln_g, ln_b, w_s1, w_s3, w_s2, ys, tm):
    s, d = v.shape
    row = lambda i: (i, 0)
    vec = _const_spec((1, d))
    slot = pl.BlockSpec((TOP_K, tm), lambda i: (0, i))
    return pl.pallas_call(
        _combine_kernel,
        grid=(s // tm,),
        in_specs=[slot, slot, pl.BlockSpec((tm, d), row), pl.BlockSpec((tm, d), row),
                  vec, vec, vec,
                  _const_spec(w_s1.shape), _const_spec(w_s3.shape), _const_spec(w_s2.shape),
                  pl.BlockSpec(memory_space=pl.ANY)],
        out_specs=pl.BlockSpec((tm, d), row),
        out_shape=jax.ShapeDtypeStruct((s, d), F32),
        scratch_shapes=[pltpu.SMEM((TOP_K, tm), I32), pltpu.VMEM((TOP_K, tm, d // 2), U32),
                        pltpu.SemaphoreType.DMA, pltpu.SemaphoreType.DMA],
        compiler_params=_cparams("arbitrary"),
        name="combine",
    )(dest, wts, v, h1, g2, ln_g, ln_b, w_s1, w_s3, w_s2, ys)


def _rope_tables(s):
    half = HEAD_DIM // 4
    inv_freq = ROPE_THETA ** (-jnp.arange(half, dtype=F32) / half)
    t = jnp.arange(s, dtype=I32)
    ang_r = (t // GRID_W).astype(F32)[:, None] * inv_freq[None, :]
    ang_c = (t % GRID_W).astype(F32)[:, None] * inv_freq[None, :]
    cos = jnp.concatenate([jnp.cos(ang_r)] * 2 + [jnp.cos(ang_c)] * 2, axis=-1)
    sin = jnp.concatenate([-jnp.sin(ang_r), jnp.sin(ang_r), -jnp.sin(ang_c), jnp.sin(ang_c)], axis=-1)
    return cos, sin


def kernel(x, c, ctx, c_ctx, w_mod, b_mod, w_in, q_norm, k_norm, conv_w, conv_b, rg_wa, rg_ba,
           rg_wx, rg_bx, rg_lam, w_out, ln1_g, ln1_b, w_router, e_bias, w_e1, w_e3, w_e2,
           w_s1, w_s3, w_s2, ln2_g, ln2_b):
    assert x.shape[0] == 1 and w_mod.shape[0] == DEPTH
    _, s, d = x.shape
    n_ctx = ctx.shape[1]
    x2 = x[0]

    c2t = jnp.stack([c[0], c_ctx], axis=1)
    mod = _modulation(c2t, w_mod[0], b_mod[0].reshape(1, -1))
    sh1, sc1, g1, sh2, sc2, g2 = [mod[0:1, j * d:(j + 1) * d] for j in range(6)]
    csh1, csc1 = mod[1:2, 0:d], mod[1:2, d:2 * d]

    w_in_b = w_in[0].astype(BF16)
    qg = q_norm[0].reshape(1, HEAD_DIM)
    kg = k_norm[0].reshape(1, HEAD_DIM)
    cos, sin = _rope_tables(s)
    q_l, k_l, vt_l, xr_l, yr_l = _in_projection(x2, sc1, sh1, w_in_b, qg, kg, cos, sin,
                                                tm=min(512, s), kv_chunk=min(KV_CHUNK, s))
    _, k_c, vt_c, xr_c, _ = _in_projection(
        ctx[0], csc1, csh1, w_in_b, qg, kg,
        jnp.ones((n_ctx, HEAD_DIM), F32), jnp.zeros((n_ctx, HEAD_DIM), F32),
        tm=n_ctx, kv_chunk=n_ctx)

    attn = _attention(q_l, k_c, vt_c, k_l, vt_l, tq=min(256, s))

    wg = jnp.concatenate([rg_wa[0], rg_wx[0]], axis=-1).astype(BF16)
    cb = conv_b[0].reshape(1, RG_W)
    zero_state = jnp.zeros((1, RG_W), F32)
    rg_args = []
    for dd in range(2):
        rg_args.append((conv_w[0], cb, wg[dd],
                        jnp.stack([rg_ba[0, dd], rg_bx[0, dd]], axis=0),
                        rg_lam[0, dd].reshape(1, RG_W)))
    t_rg = min(512, s)
    hc_f = _rg_scan(xr_c, *rg_args[0], zero_state, reverse=False, t=n_ctx)
    hc_b = _rg_scan(xr_c, *rg_args[1], zero_state, reverse=True, t=n_ctx)
    h_f = _rg_scan(xr_l, *rg_args[0], hc_f[n_ctx - 1:n_ctx], reverse=False, t=t_rg)
    rg = _rg_scan(xr_l, *rg_args[1], hc_b[0:1], reverse=True, t=t_rg, hf=h_f, yr=yr_l)

    h1, v = _out_projection(attn, rg, x2, w_out[0].astype(BF16), g1,
                            ln1_g[0].reshape(1, d), ln1_b[0].reshape(1, d), sc2, sh2,
                            tm=min(256, s))

    wr = jnp.pad(w_router[0], ((0, 0), (0, LANES - N_EXPERTS)))
    wr_hi = wr.astype(BF16)
    wr_lo = (wr - wr_hi.astype(F32)).astype(BF16)
    idx, wts, rank, cnt = _router(v, wr_hi, wr_lo, e_bias[0].reshape(N_EXPERTS, 1), tm=min(256, s))

    bm = EXPERT_BLOCK
    n_blocks = s * TOP_K // bm + N_EXPERTS
    pad_start, last_row, has_blk, blk_e, n_used = _block_tables(cnt[:, 0].astype(I32), n_blocks, bm)
    dest, xs = _dispatch(last_row, has_blk, v, idx, rank,
                         pad_start.astype(F32).reshape(N_EXPERTS, 1), n_blocks * bm,
                         tm=min(256, s), bm=bm)
    ys = _experts(blk_e, n_used, xs, w_e1[0], w_e3[0], w_e2[0], bm)

    out = _combine(dest, wts, v, h1, g2, ln2_g[0].reshape(1, d), ln2_b[0].reshape(1, d),
                   w_s1[0].astype(BF16), w_s3[0].astype(BF16), w_s2[0].astype(BF16), ys,
                   tm=min(256, s))
    return out[None]
```

```python
import functools

import jax
import jax.numpy as jnp
import numpy as np
from jax import lax
from jax.experimental import pallas as pl
from jax.experimental.pallas import tpu as pltpu

F32 = jnp.float32
BF16 = jnp.bfloat16
I32 = jnp.int32
U32 = jnp.uint32

GRID_W = 64
HEAD_DIM = 128
N_HEADS = 8
N_KV_HEADS = 2
GQA_GROUP = N_HEADS // N_KV_HEADS
ATTN_W = N_HEADS * HEAD_DIM
KV_W = N_KV_HEADS * HEAD_DIM
ROPE_THETA = 10000.0
RG_W = 1024
RG_HEADS = 8
RG_HD = RG_W // RG_HEADS
RG_C = 8.0
PROJ_W = ATTN_W + 2 * KV_W + 2 * RG_W
N_EXPERTS = 64
N_GROUPS = 8
GROUP_SIZE = N_EXPERTS // N_GROUPS
TOPK_GROUPS = 4
TOP_K = 8
ROUTED_SCALE = 2.5
NORM_EPS = 1e-6
DEPTH = 1
DEEPNORM_ALPHA = (2.0 * DEPTH) ** 0.25
LOG2E = 1.4426950408889634

LANES = 128
SUBLANES = 8
BF16_SUBLANES = 16
VMEM_LIMIT = 56 * 1024 * 1024

NEG_BIG = -1e30
KV_CHUNK = 512
ATTN_UNROLL = 8
VT_ROWS = HEAD_DIM + BF16_SUBLANES
EXPERT_BLOCK = 512


def _cparams(*sem):
    return pltpu.CompilerParams(dimension_semantics=sem, vmem_limit_bytes=VMEM_LIMIT)


def _const_spec(shape):
    nd = len(shape)
    return pl.BlockSpec(shape, lambda *_: (0,) * nd)


def _mod_kernel(ct_ref, w_ref, b_ref, o_ref, sb_ref, *, tn):
    d = w_ref.shape[0]

    @pl.when(pl.program_id(0) == 0)
    def _():
        ct = ct_ref[...]
        s = ct * jax.nn.sigmoid(ct)
        sb_ref[0] = jnp.broadcast_to(s[:, 0:1], (d, LANES))
        sb_ref[1] = jnp.broadcast_to(s[:, 1:2], (d, LANES))

    for c in range(tn // LANES):
        sl = slice(c * LANES, (c + 1) * LANES)
        wc = w_ref[:, sl]
        bc = b_ref[:, sl]
        o0 = jnp.sum(wc * sb_ref[0], axis=0, keepdims=True) + bc
        o1 = jnp.sum(wc * sb_ref[1], axis=0, keepdims=True) + bc
        o_ref[:, sl] = jnp.concatenate(
            [o0, o1, jnp.zeros((SUBLANES - 2, LANES), F32)], axis=0)


def _modulation(c2t, w_mod, b_mod):
    d, n = w_mod.shape
    tn = 1024
    return pl.pallas_call(
        functools.partial(_mod_kernel, tn=tn),
        grid=(n // tn,),
        in_specs=[_const_spec((d, 2)),
                  pl.BlockSpec((d, tn), lambda j: (0, j)),
                  pl.BlockSpec((1, tn), lambda j: (0, j))],
        out_specs=pl.BlockSpec((SUBLANES, tn), lambda j: (0, j)),
        out_shape=jax.ShapeDtypeStruct((SUBLANES, n), F32),
        scratch_shapes=[pltpu.VMEM((2, d, LANES), F32)],
        compiler_params=_cparams("arbitrary"),
        name="modulation",
    )(c2t, w_mod, b_mod)


def _swap_half(y):
    lane = lax.broadcasted_iota(I32, y.shape, 1)
    return jnp.where((lane % 64) < 32,
                     pltpu.roll(y, LANES - 32, 1), pltpu.roll(y, 32, 1))


def _norm_rope(ph, g, cos, sin_signed, scale):
    ms = jnp.mean(ph * ph, axis=-1, keepdims=True)
    y = (ph * lax.rsqrt(ms + NORM_EPS)) * g
    y = y * cos + _swap_half(y) * sin_signed
    if scale != 1.0:
        y = y * scale
    return y


def _inproj_kernel(x_ref, sc_ref, sh_ref, w_ref, qg_ref, kg_ref, cos_ref, sin_ref,
                   q_ref, k_ref, vt_ref, xr_ref, yr_ref, *, q_scale):
    tm = x_ref.shape[0]
    kc = vt_ref.shape[-1]
    u = (x_ref[...] * (1.0 + sc_ref[...]) + sh_ref[...]).astype(BF16)
    cos = cos_ref[...]
    sin = sin_ref[...]
    o = 0
    pq = jnp.dot(u, w_ref[:, o:o + ATTN_W], preferred_element_type=F32)
    for h in range(N_HEADS):
        sl = slice(h * HEAD_DIM, (h + 1) * HEAD_DIM)
        q_ref[:, sl] = _norm_rope(pq[:, sl], qg_ref[...], cos, sin, q_scale).astype(BF16)
    o += ATTN_W
    pk = jnp.dot(u, w_ref[:, o:o + KV_W], preferred_element_type=F32)
    for h in range(N_KV_HEADS):
        sl = slice(h * HEAD_DIM, (h + 1) * HEAD_DIM)
        k_ref[:, sl] = _norm_rope(pk[:, sl], kg_ref[...], cos, sin, 1.0).astype(BF16)
    o += KV_W
    pv = jnp.dot(u, w_ref[:, o:o + KV_W], preferred_element_type=F32)
    ones_rows = jnp.where(
        lax.broadcasted_iota(I32, (VT_ROWS - HEAD_DIM, kc), 0) == 0, 1.0, 0.0).astype(BF16)
    for h in range(N_KV_HEADS):
        for cc in range(tm // kc):
            blk = pv[cc * kc:(cc + 1) * kc, h * HEAD_DIM:(h + 1) * HEAD_DIM]
            vt_ref[h, cc, 0:HEAD_DIM, :] = blk.T.astype(BF16)
            vt_ref[h, cc, HEAD_DIM:VT_ROWS, :] = ones_rows
    o += KV_W
    xr_ref[...] = jnp.dot(u, w_ref[:, o:o + RG_W], preferred_element_type=F32)
    o += RG_W
    yr_ref[...] = jnp.dot(u, w_ref[:, o:o + RG_W], preferred_element_type=F32)


def _in_projection(x, sc, sh, w_in, qg, kg, cos, sin, tm, kv_chunk):
    s, d = x.shape
    row = lambda i: (i, 0)
    if kv_chunk >= tm:
        per = kv_chunk // tm
        vt_spec = pl.BlockSpec((N_KV_HEADS, 1, VT_ROWS, tm), lambda i: (0, i // per, 0, i % per))
    else:
        vt_spec = pl.BlockSpec((N_KV_HEADS, tm // kv_chunk, VT_ROWS, kv_chunk),
                               lambda i: (0, i, 0, 0))
    return pl.pallas_call(
        functools.partial(_inproj_kernel, q_scale=HEAD_DIM ** -0.5 * LOG2E),
        grid=(s // tm,),
        in_specs=[pl.BlockSpec((tm, d), row),
                  _const_spec((1, d)), _const_spec((1, d)),
                  pl.BlockSpec((d, PROJ_W), lambda i: (0, 0), pipeline_mode=pl.Buffered(1)),
                  _const_spec((1, HEAD_DIM)), _const_spec((1, HEAD_DIM)),
                  pl.BlockSpec((tm, HEAD_DIM), row), pl.BlockSpec((tm, HEAD_DIM), row)],
        out_specs=[pl.BlockSpec((tm, ATTN_W), row), pl.BlockSpec((tm, KV_W), row),
                   vt_spec,
                   pl.BlockSpec((tm, RG_W), row), pl.BlockSpec((tm, RG_W), row)],
        out_shape=[jax.ShapeDtypeStruct((s, ATTN_W), BF16),
                   jax.ShapeDtypeStruct((s, KV_W), BF16),
                   jax.ShapeDtypeStruct((N_KV_HEADS, s // kv_chunk, VT_ROWS, kv_chunk), BF16),
                   jax.ShapeDtypeStruct((s, RG_W), F32),
                   jax.ShapeDtypeStruct((s, RG_W), F32)],
        compiler_params=_cparams("arbitrary"),
        name="in_projection",
    )(x, sc, sh, w_in, qg, kg, cos, sin)


def _attn_kernel(q_ref, kc_ref, vtc_ref, kl_ref, vtl_ref, o_ref, sa_ref, sb_ref, pa_ref, pb_ref, sc_ref,
                 *, tq, tk, n_chunks):
    q = q_ref[...]
    qs = jnp.concatenate(
        [q[:, g * HEAD_DIM:(g + 1) * HEAD_DIM] for g in range(GQA_GROUP)], axis=0)
    cols = GQA_GROUP * tq

    def scores(k):
        return lax.dot_general(k, qs, (((1,), (1,)), ((), ())), preferred_element_type=F32)

    def softmax(s_ref, m):
        m_new = jnp.maximum(m, jnp.max(s_ref[...], axis=0, keepdims=True))
        return m_new, jnp.exp2(m - m_new), jnp.exp2(s_ref[...] - m_new).astype(BF16)

    def weighted_values(vt, p, alpha, acc):
        return alpha * acc + jnp.dot(vt, p, preferred_element_type=F32)

    m0 = jnp.full((1, cols), NEG_BIG, F32)
    a0 = jnp.zeros((VT_ROWS, cols), F32)
    sc_ref[...] = scores(kc_ref[...])
    m, alpha, p = softmax(sc_ref, m0)
    acc = weighted_values(vtc_ref[0], p, alpha, a0)

    def latent_scores(j):
        j = jnp.minimum(j, n_chunks - 1)
        return scores(kl_ref[pl.ds(pl.multiple_of(j * tk, tk), tk), :])

    s_refs = (sa_ref, sb_ref)
    p_refs = (pa_ref, pb_ref)
    sa_ref[...] = latent_scores(0)
    sb_ref[...] = latent_scores(1)
    m, alpha, pa_ref[...] = softmax(sa_ref, m)

    def body(i, carry):
        m, acc, alpha = carry
        for u in range(ATTN_UNROLL):
            c = ATTN_UNROLL * i + u
            cur, nxt = u % 2, (u + 1) % 2
            s_refs[cur][...] = latent_scores(c + 2)
            m, alpha_next, p_refs[nxt][...] = softmax(s_refs[nxt], m)
            acc = weighted_values(vtl_ref[c], p_refs[cur][...], alpha, acc)
            alpha = alpha_next
        return m, acc, alpha

    _, acc, _ = lax.fori_loop(0, n_chunks // ATTN_UNROLL, body, (m, acc, alpha))
    out_t = acc[0:HEAD_DIM, :] / acc[HEAD_DIM:HEAD_DIM + 1, :]
    for g in range(GQA_GROUP):
        o_ref[:, g * HEAD_DIM:(g + 1) * HEAD_DIM] = out_t[:, g * tq:(g + 1) * tq].T.astype(BF16)


def _attention(q, k_c, vt_c, k_l, vt_l, tq):
    s = q.shape[0]
    n_ctx = k_c.shape[0]
    n_chunks, _, tk = vt_l.shape[1:]
    assert n_chunks % ATTN_UNROLL == 0 and ATTN_UNROLL % 2 == 0
    gw = GQA_GROUP * HEAD_DIM
    return pl.pallas_call(
        functools.partial(_attn_kernel, tq=tq, tk=tk, n_chunks=n_chunks),
        grid=(N_KV_HEADS, s // tq),
        in_specs=[pl.BlockSpec((tq, gw), lambda h, i: (i, h)),
                  pl.BlockSpec((n_ctx, HEAD_DIM), lambda h, i: (0, h)),
                  pl.BlockSpec((None, 1, VT_ROWS, n_ctx), lambda h, i: (h, 0, 0, 0)),
                  pl.BlockSpec((s, HEAD_DIM), lambda h, i: (0, h)),
                  pl.BlockSpec((None, n_chunks, VT_ROWS, tk), lambda h, i: (h, 0, 0, 0))],
        out_specs=pl.BlockSpec((tq, gw), lambda h, i: (i, h)),
        out_shape=jax.ShapeDtypeStruct((s, ATTN_W), BF16),
        scratch_shapes=[pltpu.VMEM((tk, GQA_GROUP * tq), F32),
                        pltpu.VMEM((tk, GQA_GROUP * tq), F32),
                        pltpu.VMEM((tk, GQA_GROUP * tq), BF16),
                        pltpu.VMEM((tk, GQA_GROUP * tq), BF16),
                        pltpu.VMEM((n_ctx, GQA_GROUP * tq), F32)],
        compiler_params=_cparams("arbitrary", "arbitrary"),
        name="attention",
    )(q, k_c, vt_c, k_l, vt_l)


def _log_sigmoid(x):
    return jnp.minimum(x, 0.0) - jnp.log1p(jnp.exp(-jnp.abs(x)))


def _rg_kernel(x_ref, xp_ref, xn_ref, cw_ref, cb_ref, wg_ref, bg_ref, lam_ref, h0_ref,
               *rest, reverse, final, t, n_chunks):
    if final:
        hf_ref, yr_ref, o_ref, a_scr, b_scr, hc_scr, h_scr = rest
    else:
        o_ref, a_scr, b_scr, hc_scr = rest
        h_scr = o_ref
    i = pl.program_id(0)
    c = (n_chunks - 1 - i) if reverse else i
    w = x_ref.shape[1]

    @pl.when(i == 0)
    def _():
        hc_scr[...] = jnp.broadcast_to(h0_ref[...], (SUBLANES, w))

    x = x_ref[...]
    row = lax.broadcasted_iota(I32, (t, w), 0)
    pm = jnp.where(c == 0, 0.0, 1.0).astype(F32)
    nm = jnp.where(c == n_chunks - 1, 0.0, 1.0).astype(F32)
    p6 = xp_ref[SUBLANES - 2:SUBLANES - 1, :] * pm
    p7 = xp_ref[SUBLANES - 1:SUBLANES, :] * pm
    n0 = xn_ref[0:1, :] * nm
    x_m1 = jnp.where(row == 0, p7, pltpu.roll(x, 1, 0))
    x_m2 = jnp.where(row == 0, p6, jnp.where(row == 1, p7, pltpu.roll(x, 2, 0)))
    x_p1 = jnp.where(row == t - 1, n0, pltpu.roll(x, t - 1, 0))
    xc = cb_ref[...] + cw_ref[0:1, :] * x_m2
    xc = xc + cw_ref[1:2, :] * x_m1
    xc = xc + cw_ref[2:3, :] * x
    xc = xc + cw_ref[3:4, :] * x_p1

    xcb = xc.astype(BF16)
    clam = RG_C * _log_sigmoid(lam_ref[...])
    for h in range(RG_HEADS):
        sl = slice(h * RG_HD, (h + 1) * RG_HD)
        g = jnp.dot(xcb[:, sl], wg_ref[h], preferred_element_type=F32)
        r = jax.nn.sigmoid(g[:, :RG_HD] + bg_ref[0:1, sl])
        gi = jax.nn.sigmoid(g[:, RG_HD:] + bg_ref[1:2, sl])
        log_a = r * clam[:, sl]
        a = jnp.exp(log_a)
        a_scr[:, sl] = a
        b_scr[:, sl] = jnp.sqrt(-jnp.tanh(log_a) * (a * a + 1.0)) * (gi * xc[:, sl])

    srow = lax.broadcasted_iota(I32, (SUBLANES, w), 0)
    n_tiles = t // SUBLANES

    def tile_body(j, hprev):
        tile = (n_tiles - 1 - j) if reverse else j
        start = pl.multiple_of(tile * SUBLANES, SUBLANES)
        a = a_scr[pl.ds(start, SUBLANES), :]
        b = b_scr[pl.ds(start, SUBLANES), :]
        for k in (1, 2, 4):
            if reverse:
                keep = srow < SUBLANES - k
                shift = SUBLANES - k
            else:
                keep = srow >= k
                shift = k
            a_sh = jnp.where(keep, pltpu.roll(a, shift, 0), 1.0)
            b_sh = jnp.where(keep, pltpu.roll(b, shift, 0), 0.0)
            b = a * b_sh + b
            a = a * a_sh
        hh = a * hprev + b
        h_scr[pl.ds(start, SUBLANES), :] = hh
        last = hh[0:1, :] if reverse else hh[SUBLANES - 1:SUBLANES, :]
        return jnp.broadcast_to(last, (SUBLANES, w))

    hc_scr[...] = lax.fori_loop(0, n_tiles, tile_body, hc_scr[...])

    if final:
        gate = jax.nn.gelu(yr_ref[...], approximate=True)
        o_ref[...] = ((hf_ref[...] + h_scr[...]) * gate).astype(o_ref.dtype)


def _rg_scan(xr, conv_w, conv_b, wg, bg, lam, h0, *, reverse, t, hf=None, yr=None):
    s, w = xr.shape
    n_chunks = s // t
    final = hf is not None
    tb = t // SUBLANES
    last_blk = s // SUBLANES - 1
    if reverse:
        cidx = lambda i: n_chunks - 1 - i
    else:
        cidx = lambda i: i
    chunk_spec = pl.BlockSpec((t, w), lambda i: (cidx(i), 0))
    in_specs = [chunk_spec,
                pl.BlockSpec((SUBLANES, w), lambda i: (jnp.maximum(cidx(i) * tb - 1, 0), 0)),
                pl.BlockSpec((SUBLANES, w), lambda i: (jnp.minimum((cidx(i) + 1) * tb, last_blk), 0)),
                _const_spec((4, w)), _const_spec((1, w)),
                _const_spec((RG_HEADS, RG_HD, 2 * RG_HD)), _const_spec((2, w)),
                _const_spec((1, w)), _const_spec((1, w))]
    args = [xr, xr, xr, conv_w, conv_b, wg, bg, lam, h0]
    scratch = [pltpu.VMEM((t, w), F32), pltpu.VMEM((t, w), F32), pltpu.VMEM((SUBLANES, w), F32)]
    if final:
        in_specs += [chunk_spec, chunk_spec]
        args += [hf, yr]
        scratch.append(pltpu.VMEM((t, w), F32))
        out_dtype = BF16
    else:
        out_dtype = F32
    return pl.pallas_call(
        functools.partial(_rg_kernel, reverse=reverse, final=final, t=t, n_chunks=n_chunks),
        grid=(n_chunks,),
        in_specs=in_specs,
        out_specs=chunk_spec,
        out_shape=jax.ShapeDtypeStruct((s, w), out_dtype),
        scratch_shapes=scratch,
        compiler_params=_cparams("arbitrary"),
        name="rglru_bwd" if reverse else "rglru_fwd",
    )(*args)


def _layer_norm(y, g, b):
    mu = jnp.mean(y, axis=-1, keepdims=True)
    yc = y - mu
    var = jnp.mean(yc * yc, axis=-1, keepdims=True)
    return yc * lax.rsqrt(var + NORM_EPS) * g + b


def _outproj_kernel(attn_ref, rg_ref, x_ref, w_ref, g1_ref, lg_ref, lb_ref, sc2_ref, sh2_ref,
                    h1_ref, v_ref):
    mix = jnp.dot(attn_ref[...], w_ref[0:ATTN_W, :], preferred_element_type=F32)
    mix = mix + jnp.dot(rg_ref[...], w_ref[ATTN_W:, :], preferred_element_type=F32)
    h1 = _layer_norm(DEEPNORM_ALPHA * x_ref[...] + g1_ref[...] * mix, lg_ref[...], lb_ref[...])
    h1_ref[...] = h1
    v_ref[...] = h1 * (1.0 + sc2_ref[...]) + sh2_ref[...]


def _out_projection(attn, rg, x, w_out, g1, ln_g, ln_b, sc2, sh2, tm):
    s, d = x.shape
    row = lambda i: (i, 0)
    vec = _const_spec((1, d))
    return pl.pallas_call(
        _outproj_kernel,
        grid=(s // tm,),
        in_specs=[pl.BlockSpec((tm, ATTN_W), row), pl.BlockSpec((tm, RG_W), row),
                  pl.BlockSpec((tm, d), row),
                  pl.BlockSpec((ATTN_W + RG_W, d), lambda i: (0, 0), pipeline_mode=pl.Buffered(1)),
                  vec, vec, vec, vec, vec],
        out_specs=[pl.BlockSpec((tm, d), row), pl.BlockSpec((tm, d), row)],
        out_shape=[jax.ShapeDtypeStruct((s, d), F32), jax.ShapeDtypeStruct((s, d), F32)],
        compiler_params=_cparams("arbitrary"),
        name="out_projection",
    )(attn, rg, x, w_out, g1, ln_g, ln_b, sc2, sh2)


def _first_index_of_max(x, iota_f, axis):
    mx = jnp.max(x, axis=axis, keepdims=True)
    idx = jnp.min(jnp.where(x == mx, iota_f, float(N_EXPERTS)), axis=axis, keepdims=True)
    return mx, idx


def _router_kernel(v_ref, whi_ref, wlo_ref, eb_ref, tri_ref,
                   idx_ref, wts_ref, rank_ref, cnt_ref, base_scr):
    v = v_ref[...]
    tm = v.shape[0]

    @pl.when(pl.program_id(0) == 0)
    def _():
        base_scr[...] = jnp.zeros(base_scr.shape, F32)

    v_hi = v.astype(BF16)
    v_lo = (v - v_hi.astype(F32)).astype(BF16)
    logits = jnp.dot(v_hi, whi_ref[...], preferred_element_type=F32)
    logits = logits + jnp.dot(v_lo, whi_ref[...], preferred_element_type=F32)
    logits = logits + jnp.dot(v_hi, wlo_ref[...], preferred_element_type=F32)
    lt = logits.T[0:N_EXPERTS, :]
    scores = jax.nn.sigmoid(lt)
    biased = scores + eb_ref[...]
    neg_inf = float("-inf")

    ig = lax.broadcasted_iota(I32, (GROUP_SIZE, tm), 0).astype(F32)
    groups = [biased[g * GROUP_SIZE:(g + 1) * GROUP_SIZE, :] for g in range(N_GROUPS)]
    gscore = []
    for bg in groups:
        top1, i1 = _first_index_of_max(bg, ig, 0)
        top2 = jnp.max(jnp.where(ig == i1, neg_inf, bg), axis=0, keepdims=True)
        gscore.append(top1 + top2)

    masked = []
    for g in range(N_GROUPS):
        ahead = jnp.zeros((1, tm), F32)
        for o in range(N_GROUPS):
            if o == g:
                continue
            before = (gscore[o] >= gscore[g]) if o < g else (gscore[o] > gscore[g])
            ahead = ahead + jnp.where(before, 1.0, 0.0)
        keep = jnp.broadcast_to(ahead < TOPK_GROUPS, (GROUP_SIZE, tm))
        masked.append(jnp.where(keep, groups[g], neg_inf))
    masked = jnp.concatenate(masked, axis=0)

    ie = lax.broadcasted_iota(I32, masked.shape, 0).astype(F32)
    seen = base_scr[...]
    ws = []
    for k in range(TOP_K):
        _, ei = _first_index_of_max(masked, ie, 0)
        hit = ie == ei
        idx_ref[k:k + 1, :] = ei.astype(I32)
        ws.append(jnp.sum(jnp.where(hit, scores, 0.0), axis=0, keepdims=True))
        masked = jnp.where(hit, neg_inf, masked)
        onehot = jnp.where(hit, 1.0, 0.0)
        before = jnp.dot(onehot.astype(BF16), tri_ref[...], preferred_element_type=F32)
        rank = jnp.sum(jnp.where(hit, before + seen, 0.0), axis=0, keepdims=True)
        rank_ref[k:k + 1, :] = rank.astype(I32)
        seen = seen + jnp.sum(onehot, axis=1, keepdims=True)
    base_scr[...] = seen
    cnt_ref[...] = jnp.broadcast_to(seen, cnt_ref.shape)
    total = ws[0]
    for k in range(1, TOP_K):
        total = total + ws[k]
    for k in range(TOP_K):
        wts_ref[k:k + 1, :] = ws[k] / total * ROUTED_SCALE


def _router(v, w_hi, w_lo, e_bias_col, tm):
    s, d = v.shape
    tri = jnp.triu(jnp.ones((tm, tm), BF16), k=1)
    slot = pl.BlockSpec((TOP_K, tm), lambda i: (0, i))
    return pl.pallas_call(
        _router_kernel,
        grid=(s // tm,),
        in_specs=[pl.BlockSpec((tm, d), lambda i: (i, 0)),
                  _const_spec((d, LANES)), _const_spec((d, LANES)),
                  _const_spec((N_EXPERTS, 1)), _const_spec((tm, tm))],
        out_specs=[slot, slot, slot, _const_spec((N_EXPERTS, LANES))],
        out_shape=[jax.ShapeDtypeStruct((TOP_K, s), I32),
                   jax.ShapeDtypeStruct((TOP_K, s), F32),
                   jax.ShapeDtypeStruct((TOP_K, s), I32),
                   jax.ShapeDtypeStruct((N_EXPERTS, LANES), F32)],
        scratch_shapes=[pltpu.VMEM((N_EXPERTS, 1), F32)],
        compiler_params=_cparams("arbitrary"),
        name="router",
    )(v, w_hi, w_lo, e_bias_col, tri)


def _pack_pairs(x):
    h = x.shape[1] // 2
    lo = lax.bitcast_convert_type(x[:, :h].astype(BF16).astype(F32), U32)
    hi = lax.bitcast_convert_type(x[:, h:].astype(BF16).astype(F32), U32)
    return (lo >> 16) | (hi & jnp.uint32(0xFFFF0000))


def _unpack_pairs(w):
    lo = lax.bitcast_convert_type(w << 16, F32)
    hi = lax.bitcast_convert_type(w & jnp.uint32(0xFFFF0000), F32)
    return jnp.concatenate([lo, hi], axis=1)


def _dispatch_kernel(last_ref, has_ref, v_ref, idx_ref, rank_ref, pstart_ref,
                     dest_ref, xs_hbm, dest_v, dest_s, zeros_v, pk_v, sem_z, sem_i, sem_r, *, bm):
    tm = v_ref.shape[0]

    def zero_fill(e):
        row0 = pl.multiple_of(last_ref[e], bm)
        return pltpu.make_async_copy(zeros_v, xs_hbm.at[pl.ds(row0, bm), :], sem_z)

    @pl.when(pl.program_id(0) == 0)
    def _():
        zeros_v[...] = jnp.zeros(zeros_v.shape, zeros_v.dtype)

        def start(e, carry):
            @pl.when(has_ref[e] > 0)
            def _():
                zero_fill(e).start()
            return carry

        def wait(e, carry):
            @pl.when(has_ref[e] > 0)
            def _():
                zero_fill(e).wait()
            return carry

        lax.fori_loop(0, N_EXPERTS, start, 0)
        lax.fori_loop(0, N_EXPERTS, wait, 0)

    ie = lax.broadcasted_iota(I32, (N_EXPERTS, tm), 0)
    pstart = pstart_ref[...]
    for k in range(TOP_K):
        hit = ie == idx_ref[k:k + 1, :]
        seg = jnp.sum(jnp.where(hit, pstart, 0.0), axis=0, keepdims=True)
        dest_v[k:k + 1, :] = seg.astype(I32) + rank_ref[k:k + 1, :]
    dest_ref[...] = dest_v[...]
    cp = pltpu.make_async_copy(dest_v, dest_s, sem_i)
    cp.start()
    pk_v[...] = _pack_pairs(v_ref[...])
    cp.wait()

    def start_rows(t, carry):
        for k in range(TOP_K):
            pltpu.make_async_copy(pk_v.at[pl.ds(t, 1), :],
                                  xs_hbm.at[pl.ds(dest_s[k, t], 1), :], sem_r).start()
        return carry

    lax.fori_loop(0, tm, start_rows, 0, unroll=4)
    for k in range(TOP_K):
        pltpu.make_async_copy(pk_v, xs_hbm.at[pl.ds(0, tm), :], sem_r).wait()


def _dispatch(last_row, has_blk, v, idx, rank, pstart_col, n_rows, tm, bm):
    s, d = v.shape
    slot = pl.BlockSpec((TOP_K, tm), lambda i, *_: (0, i))
    grid_spec = pltpu.PrefetchScalarGridSpec(
        num_scalar_prefetch=2,
        grid=(s // tm,),
        in_specs=[pl.BlockSpec((tm, d), lambda i, *_: (i, 0)), slot, slot,
                  pl.BlockSpec((N_EXPERTS, 1), lambda i, *_: (0, 0))],
        out_specs=[slot, pl.BlockSpec(memory_space=pl.ANY)],
        scratch_shapes=[pltpu.VMEM((TOP_K, tm), I32), pltpu.SMEM((TOP_K, tm), I32),
                        pltpu.VMEM((bm, d // 2), U32), pltpu.VMEM((tm, d // 2), U32),
                        pltpu.SemaphoreType.DMA, pltpu.SemaphoreType.DMA,
                        pltpu.SemaphoreType.DMA])
    return pl.pallas_call(
        functools.partial(_dispatch_kernel, bm=bm),
        grid_spec=grid_spec,
        out_shape=[jax.ShapeDtypeStruct((TOP_K, s), I32),
                   jax.ShapeDtypeStruct((n_rows, d // 2), U32)],
        compiler_params=_cparams("arbitrary"),
        name="dispatch",
    )(last_row, has_blk, v, idx, rank, pstart_col)


def _expert_kernel(blk_e_ref, n_used_ref, x_ref, w1_ref, w3_ref, w2_ref, y_ref, w1b, w3b, w2b):
    b = pl.program_id(0)

    @pl.when(b < n_used_ref[0])
    def _():
        prev_e = blk_e_ref[jnp.maximum(b - 1, 0)]

        @pl.when(jnp.logical_or(b == 0, blk_e_ref[b] != prev_e))
        def _():
            w1b[...] = w1_ref[...].astype(BF16)
            w3b[...] = w3_ref[...].astype(BF16)
            w2b[...] = w2_ref[...].astype(BF16)

        xb = _unpack_pairs(x_ref[...]).astype(BF16)
        h1 = jnp.dot(xb, w1b[...], preferred_element_type=F32)
        h3 = jnp.dot(xb, w3b[...], preferred_element_type=F32)
        act = (h1 * jax.nn.sigmoid(h1) * h3).astype(BF16)
        y_ref[...] = _pack_pairs(jnp.dot(act, w2b[...], preferred_element_type=F32))


def _experts(blk_e, n_used, xs, w_e1, w_e3, w_e2, bm):
    n_rows = xs.shape[0]
    d, ff = w_e1.shape[-2:]
    rows = lambda b, be, nu: (jnp.minimum(b, nu[0] - 1), 0)
    wspec1 = pl.BlockSpec((None, d, ff), lambda b, be, nu: (be[b], 0, 0))
    wspec2 = pl.BlockSpec((None, ff, d), lambda b, be, nu: (be[b], 0, 0))
    grid_spec = pltpu.PrefetchScalarGridSpec(
        num_scalar_prefetch=2,
        grid=(n_rows // bm,),
        in_specs=[pl.BlockSpec((bm, d // 2), rows), wspec1, wspec1, wspec2],
        out_specs=pl.BlockSpec((bm, d // 2), rows),
        scratch_shapes=[pltpu.VMEM((d, ff), BF16), pltpu.VMEM((d, ff), BF16),
                        pltpu.VMEM((ff, d), BF16)])
    return pl.pallas_call(
        _expert_kernel,
        grid_spec=grid_spec,
        out_shape=jax.ShapeDtypeStruct((n_rows, d // 2), U32),
        compiler_params=_cparams("arbitrary"),
        name="experts",
    )(blk_e, n_used, xs, w_e1, w_e3, w_e2)


def _block_tables(counts, n_blocks, bm):
    padded = (counts + bm - 1) // bm * bm
    pad_end = jnp.cumsum(padded)
    pad_start = pad_end - padded
    blk_first = jnp.arange(n_blocks, dtype=I32) * bm
    blk_e = jnp.minimum(jnp.sum(pad_end[None, :] <= blk_first[:, None], axis=1), N_EXPERTS - 1)
    n_used = pad_end[-1] // bm
    blk_e = jnp.where(jnp.arange(n_blocks) < n_used, blk_e, blk_e[jnp.maximum(n_used - 1, 0)])
    last_row = jnp.maximum(pad_end - bm, 0)
    return (pad_start, last_row.astype(I32), (padded > 0).astype(I32),
            blk_e.astype(I32), n_used.astype(I32).reshape(1))


def _combine_kernel(dest_ref, wts_ref, v_ref, h1_ref, g2_ref, lg_ref, lb_ref,
                    ws1_ref, ws3_ref, ws2_ref, ys_hbm, o_ref, dest_s, ybuf, sem_i, sem_r):
    tm = v_ref.shape[0]
    cp = pltpu.make_async_copy(dest_ref, dest_s, sem_i)
    cp.start()
    cp.wait()

    def start_rows(t, carry):
        for k in range(TOP_K):
            pltpu.make_async_copy(ys_hbm.at[pl.ds(dest_s[k, t], 1), :],
                                  ybuf.at[k, pl.ds(t, 1), :], sem_r).start()
        return carry

    lax.fori_loop(0, tm, start_rows, 0, unroll=4)

    vb = v_ref[...].astype(BF16)
    a1 = jnp.dot(vb, ws1_ref[...], preferred_element_type=F32)
    a3 = jnp.dot(vb, ws3_ref[...], preferred_element_type=F32)
    act = (a1 * jax.nn.sigmoid(a1) * a3).astype(BF16)
    shared = jnp.dot(act, ws2_ref[...], preferred_element_type=F32)
    wpad = jnp.concatenate([wts_ref[...], jnp.zeros((LANES - TOP_K, tm), F32)], axis=0)
    wt = wpad.T

    for k in range(TOP_K):
        pltpu.make_async_copy(ys_hbm.at[pl.ds(0, tm), :], ybuf.at[k], sem_r).wait()
    ff = _unpack_pairs(ybuf[0]) * wt[:, 0:1]
    for k in range(1, TOP_K):
        ff = ff + _unpack_pairs(ybuf[k]) * wt[:, k:k + 1]
    ff = ff + shared
    o_ref[...] = _layer_norm(DEEPNORM_ALPHA * h1_ref[...] + g2_ref[...] * ff,
                             lg_ref[...], lb_ref[...])


def _combine(dest, wts, v, h1, g2, ln_g, ln_b, w_s1, w_s3, w_s2, ys, tm):
    s, d = v.shape
    row = lambda i: (i, 0)
    vec = _const_spec((1, d))
    slot = pl.BlockSpec((TOP_K, tm), lambda i: (0, i))
    return pl.pallas_call(
        _combine_kernel,
        grid=(s // tm,),
        in_specs=[slot, slot, pl.BlockSpec((tm, d), row), pl.BlockSpec((tm, d), row),
                  vec, vec, vec,
                  _const_spec(w_s1.shape), _const_spec(w_s3.shape), _const_spec(w_s2.shape),
                  pl.BlockSpec(memory_space=pl.ANY)],
        out_specs=pl.BlockSpec((tm, d), row),
        out_shape=jax.ShapeDtypeStruct((s, d), F32),
        scratch_shapes=[pltpu.SMEM((TOP_K, tm), I32), pltpu.VMEM((TOP_K, tm, d // 2), U32),
                        pltpu.SemaphoreType.DMA, pltpu.SemaphoreType.DMA],
        compiler_params=_cparams("arbitrary"),
        name="combine",
    )(dest, wts, v, h1, g2, ln_g, ln_b, w_s1, w_s3, w_s2, ys)


def _rope_tables(s):
    half = HEAD_DIM // 4
    inv_freq = ROPE_THETA ** (-np.arange(half, dtype=np.float32) / half)
    t = np.arange(s)
    ang_r = (t // GRID_W).astype(np.float32)[:, None] * inv_freq[None, :]
    ang_c = (t % GRID_W).astype(np.float32)[:, None] * inv_freq[None, :]
    cos = np.concatenate([np.cos(ang_r)] * 2 + [np.cos(ang_c)] * 2, axis=-1)
    sin = np.concatenate([-np.sin(ang_r), np.sin(ang_r), -np.sin(ang_c), np.sin(ang_c)], axis=-1)
    return jnp.asarray(cos, F32), jnp.asarray(sin, F32)


def kernel(x, c, ctx, c_ctx, w_mod, b_mod, w_in, q_norm, k_norm, conv_w, conv_b, rg_wa, rg_ba,
           rg_wx, rg_bx, rg_lam, w_out, ln1_g, ln1_b, w_router, e_bias, w_e1, w_e3, w_e2,
           w_s1, w_s3, w_s2, ln2_g, ln2_b):
    assert x.shape[0] == 1 and w_mod.shape[0] == DEPTH
    _, s, d = x.shape
    n_ctx = ctx.shape[1]
    x2 = x[0]

    c2t = jnp.stack([c[0], c_ctx], axis=1)
    mod = _modulation(c2t, w_mod[0], b_mod[0].reshape(1, -1))
    sh1, sc1, g1, sh2, sc2, g2 = [mod[0:1, j * d:(j + 1) * d] for j in range(6)]
    csh1, csc1 = mod[1:2, 0:d], mod[1:2, d:2 * d]

    w_in_b = w_in[0].astype(BF16)
    qg = q_norm[0].reshape(1, HEAD_DIM)
    kg = k_norm[0].reshape(1, HEAD_DIM)
    cos, sin = _rope_tables(s)
    q_l, k_l, vt_l, xr_l, yr_l = _in_projection(x2, sc1, sh1, w_in_b, qg, kg, cos, sin,
                                                tm=min(512, s), kv_chunk=min(KV_CHUNK, s))
    _, k_c, vt_c, xr_c, _ = _in_projection(
        ctx[0], csc1, csh1, w_in_b, qg, kg,
        jnp.ones((n_ctx, HEAD_DIM), F32), jnp.zeros((n_ctx, HEAD_DIM), F32),
        tm=n_ctx, kv_chunk=n_ctx)

    attn = _attention(q_l, k_c, vt_c, k_l, vt_l, tq=min(256, s))

    wg = jnp.concatenate([rg_wa[0], rg_wx[0]], axis=-1).astype(BF16)
    cb = conv_b[0].reshape(1, RG_W)
    zero_state = jnp.zeros((1, RG_W), F32)
    rg_args = []
    for dd in range(2):
        rg_args.append((conv_w[0], cb, wg[dd],
                        jnp.stack([rg_ba[0, dd], rg_bx[0, dd]], axis=0),
                        rg_lam[0, dd].reshape(1, RG_W)))
    t_rg = min(512, s)
    hc_f = _rg_scan(xr_c, *rg_args[0], zero_state, reverse=False, t=n_ctx)
    hc_b = _rg_scan(xr_c, *rg_args[1], zero_state, reverse=True, t=n_ctx)
    h_f = _rg_scan(xr_l, *rg_args[0], hc_f[n_ctx - 1:n_ctx], reverse=False, t=t_rg)
    rg = _rg_scan(xr_l, *rg_args[1], hc_b[0:1], reverse=True, t=t_rg, hf=h_f, yr=yr_l)

    h1, v = _out_projection(attn, rg, x2, w_out[0].astype(BF16), g1,
                            ln1_g[0].reshape(1, d), ln1_b[0].reshape(1, d), sc2, sh2,
                            tm=min(256, s))

    wr = jnp.pad(w_router[0], ((0, 0), (0, LANES - N_EXPERTS)))
    wr_hi = wr.astype(BF16)
    wr_lo = (wr - wr_hi.astype(F32)).astype(BF16)
    idx, wts, rank, cnt = _router(v, wr_hi, wr_lo, e_bias[0].reshape(N_EXPERTS, 1), tm=min(256, s))

    bm = EXPERT_BLOCK
    n_blocks = s * TOP_K // bm + N_EXPERTS
    pad_start, last_row, has_blk, blk_e, n_used = _block_tables(cnt[:, 0].astype(I32), n_blocks, bm)
    dest, xs = _dispatch(last_row, has_blk, v, idx, rank,
                         pad_start.astype(F32).reshape(N_EXPERTS, 1), n_blocks * bm,
                         tm=min(256, s), bm=bm)
    ys = _experts(blk_e, n_used, xs, w_e1[0], w_e3[0], w_e2[0], bm)

    out = _combine(dest, wts, v, h1, g2, ln2_g[0].reshape(1, d), ln2_b[0].reshape(1, d),
                   w_s1[0].astype(BF16), w_s3[0].astype(BF16), w_s2[0].astype(BF16), ys,
                   tm=min(256, s))
    return out[None]
```

```python
import functools

import jax
import jax.numpy as jnp
import numpy as np
from jax import lax
from jax.experimental import pallas as pl
from jax.experimental.pallas import tpu as pltpu

F32 = jnp.float32
BF16 = jnp.bfloat16
I32 = jnp.int32
U32 = jnp.uint32

GRID_W = 64
HEAD_DIM = 128
N_HEADS = 8
N_KV_HEADS = 2
GQA_GROUP = N_HEADS // N_KV_HEADS
ATTN_W = N_HEADS * HEAD_DIM
KV_W = N_KV_HEADS * HEAD_DIM
ROPE_THETA = 10000.0
RG_W = 1024
RG_HEADS = 8
RG_HD = RG_W // RG_HEADS
RG_C = 8.0
PROJ_W = ATTN_W + 2 * KV_W + 2 * RG_W
N_EXPERTS = 64
N_GROUPS = 8
GROUP_SIZE = N_EXPERTS // N_GROUPS
TOPK_GROUPS = 4
TOP_K = 8
ROUTED_SCALE = 2.5
NORM_EPS = 1e-6
DEPTH = 1
DEEPNORM_ALPHA = (2.0 * DEPTH) ** 0.25
LOG2E = 1.4426950408889634

LANES = 128
SUBLANES = 8
BF16_SUBLANES = 16
VMEM_LIMIT = 56 * 1024 * 1024

NEG_BIG = -1e30
KV_CHUNK = 512
ATTN_UNROLL = 8
VT_ROWS = HEAD_DIM + BF16_SUBLANES
EXPERT_BLOCK = 512


def _cparams(*sem):
    return pltpu.CompilerParams(dimension_semantics=sem, vmem_limit_bytes=VMEM_LIMIT)


def _const_spec(shape):
    nd = len(shape)
    return pl.BlockSpec(shape, lambda *_: (0,) * nd)


def _mod_kernel(ct_ref, w_ref, b_ref, o_ref, sb_ref, *, tn):
    d = w_ref.shape[0]

    @pl.when(pl.program_id(0) == 0)
    def _():
        ct = ct_ref[...]
        s = ct * jax.nn.sigmoid(ct)
        sb_ref[0] = jnp.broadcast_to(s[:, 0:1], (d, LANES))
        sb_ref[1] = jnp.broadcast_to(s[:, 1:2], (d, LANES))

    for c in range(tn // LANES):
        sl = slice(c * LANES, (c + 1) * LANES)
        wc = w_ref[:, sl]
        bc = b_ref[:, sl]
        o0 = jnp.sum(wc * sb_ref[0], axis=0, keepdims=True) + bc
        o1 = jnp.sum(wc * sb_ref[1], axis=0, keepdims=True) + bc
        o_ref[:, sl] = jnp.concatenate(
            [o0, o1, jnp.zeros((SUBLANES - 2, LANES), F32)], axis=0)


def _modulation(c2t, w_mod, b_mod):
    d, n = w_mod.shape
    tn = 1024
    return pl.pallas_call(
        functools.partial(_mod_kernel, tn=tn),
        grid=(n // tn,),
        in_specs=[_const_spec((d, 2)),
                  pl.BlockSpec((d, tn), lambda j: (0, j)),
                  pl.BlockSpec((1, tn), lambda j: (0, j))],
        out_specs=pl.BlockSpec((SUBLANES, tn), lambda j: (0, j)),
        out_shape=jax.ShapeDtypeStruct((SUBLANES, n), F32),
        scratch_shapes=[pltpu.VMEM((2, d, LANES), F32)],
        compiler_params=_cparams("arbitrary"),
        name="modulation",
    )(c2t, w_mod, b_mod)


def _swap_half(y):
    lane = lax.broadcasted_iota(I32, y.shape, 1)
    return jnp.where((lane % 64) < 32,
                     pltpu.roll(y, LANES - 32, 1), pltpu.roll(y, 32, 1))


def _norm_rope(ph, g, cos, sin_signed, scale):
    ms = jnp.mean(ph * ph, axis=-1, keepdims=True)
    y = (ph * lax.rsqrt(ms + NORM_EPS)) * g
    y = y * cos + _swap_half(y) * sin_signed
    if scale != 1.0:
        y = y * scale
    return y


def _inproj_kernel(x_ref, sc_ref, sh_ref, w_ref, qg_ref, kg_ref, cos_ref, sin_ref,
                   q_ref, k_ref, vt_ref, xr_ref, yr_ref, *, q_scale):
    tm = x_ref.shape[0]
    kc = vt_ref.shape[-1]
    u = (x_ref[...] * (1.0 + sc_ref[...]) + sh_ref[...]).astype(BF16)
    cos = cos_ref[...]
    sin = sin_ref[...]
    o = 0
    pq = jnp.dot(u, w_ref[:, o:o + ATTN_W], preferred_element_type=F32)
    for h in range(N_HEADS):
        sl = slice(h * HEAD_DIM, (h + 1) * HEAD_DIM)
        q_ref[:, sl] = _norm_rope(pq[:, sl], qg_ref[...], cos, sin, q_scale).astype(BF16)
    o += ATTN_W
    pk = jnp.dot(u, w_ref[:, o:o + KV_W], preferred_element_type=F32)
    for h in range(N_KV_HEADS):
        sl = slice(h * HEAD_DIM, (h + 1) * HEAD_DIM)
        k_ref[:, sl] = _norm_rope(pk[:, sl], kg_ref[...], cos, sin, 1.0).astype(BF16)
    o += KV_W
    pv = jnp.dot(u, w_ref[:, o:o + KV_W], preferred_element_type=F32)
    ones_rows = jnp.where(
        lax.broadcasted_iota(I32, (VT_ROWS - HEAD_DIM, kc), 0) == 0, 1.0, 0.0).astype(BF16)
    for h in range(N_KV_HEADS):
        for cc in range(tm // kc):
            blk = pv[cc * kc:(cc + 1) * kc, h * HEAD_DIM:(h + 1) * HEAD_DIM]
            vt_ref[h, cc, 0:HEAD_DIM, :] = blk.T.astype(BF16)
            vt_ref[h, cc, HEAD_DIM:VT_ROWS, :] = ones_rows
    o += KV_W
    xr_ref[...] = jnp.dot(u, w_ref[:, o:o + RG_W], preferred_element_type=F32)
    o += RG_W
    yr_ref[...] = jnp.dot(u, w_ref[:, o:o + RG_W], preferred_element_type=F32)


def _in_projection(x, sc, sh, w_in, qg, kg, cos, sin, tm, kv_chunk):
    s, d = x.shape
    row = lambda i: (i, 0)
    if kv_chunk >= tm:
        per = kv_chunk // tm
        vt_spec = pl.BlockSpec((N_KV_HEADS, 1, VT_ROWS, tm), lambda i: (0, i // per, 0, i % per))
    else:
        vt_spec = pl.BlockSpec((N_KV_HEADS, tm // kv_chunk, VT_ROWS, kv_chunk),
                               lambda i: (0, i, 0, 0))
    return pl.pallas_call(
        functools.partial(_inproj_kernel, q_scale=HEAD_DIM ** -0.5 * LOG2E),
        grid=(s // tm,),
        in_specs=[pl.BlockSpec((tm, d), row),
                  _const_spec((1, d)), _const_spec((1, d)),
                  pl.BlockSpec((d, PROJ_W), lambda i: (0, 0), pipeline_mode=pl.Buffered(1)),
                  _const_spec((1, HEAD_DIM)), _const_spec((1, HEAD_DIM)),
                  pl.BlockSpec((tm, HEAD_DIM), row), pl.BlockSpec((tm, HEAD_DIM), row)],
        out_specs=[pl.BlockSpec((tm, ATTN_W), row), pl.BlockSpec((tm, KV_W), row),
                   vt_spec,
                   pl.BlockSpec((tm, RG_W), row), pl.BlockSpec((tm, RG_W), row)],
        out_shape=[jax.ShapeDtypeStruct((s, ATTN_W), BF16),
                   jax.ShapeDtypeStruct((s, KV_W), BF16),
                   jax.ShapeDtypeStruct((N_KV_HEADS, s // kv_chunk, VT_ROWS, kv_chunk), BF16),
                   jax.ShapeDtypeStruct((s, RG_W), F32),
                   jax.ShapeDtypeStruct((s, RG_W), F32)],
        compiler_params=_cparams("arbitrary"),
        name="in_projection",
    )(x, sc, sh, w_in, qg, kg, cos, sin)


def _attn_kernel(q_ref, kc_ref, vtc_ref, kl_ref, vtl_ref, o_ref, sa_ref, sb_ref, pa_ref, pb_ref, sc_ref,
                 *, tq, tk, n_chunks, unroll):
    q = q_ref[...]
    qs = jnp.concatenate(
        [q[:, g * HEAD_DIM:(g + 1) * HEAD_DIM] for g in range(GQA_GROUP)], axis=0)
    cols = GQA_GROUP * tq

    def scores(k):
        return lax.dot_general(k, qs, (((1,), (1,)), ((), ())), preferred_element_type=F32)

    def softmax(s_ref, m):
        m_new = jnp.maximum(m, jnp.max(s_ref[...], axis=0, keepdims=True))
        return m_new, jnp.exp2(m - m_new), jnp.exp2(s_ref[...] - m_new).astype(BF16)

    def weighted_values(vt, p, alpha, acc):
        return alpha * acc + jnp.dot(vt, p, preferred_element_type=F32)

    m0 = jnp.full((1, cols), NEG_BIG, F32)
    a0 = jnp.zeros((VT_ROWS, cols), F32)
    sc_ref[...] = scores(kc_ref[...])
    m, alpha, p = softmax(sc_ref, m0)
    acc = weighted_values(vtc_ref[0], p, alpha, a0)

    def latent_scores(j):
        j = jnp.minimum(j, n_chunks - 1)
        return scores(kl_ref[pl.ds(pl.multiple_of(j * tk, tk), tk), :])

    s_refs = (sa_ref, sb_ref)
    p_refs = (pa_ref, pb_ref)
    sa_ref[...] = latent_scores(0)
    sb_ref[...] = latent_scores(1)
    m, alpha, pa_ref[...] = softmax(sa_ref, m)

    def body(i, carry):
        m, acc, alpha = carry
        for u in range(unroll):
            c = unroll * i + u
            cur, nxt = u % 2, (u + 1) % 2
            s_refs[cur][...] = latent_scores(c + 2)
            m, alpha_next, p_refs[nxt][...] = softmax(s_refs[nxt], m)
            acc = weighted_values(vtl_ref[c], p_refs[cur][...], alpha, acc)
            alpha = alpha_next
        return m, acc, alpha

    _, acc, _ = lax.fori_loop(0, n_chunks // unroll, body, (m, acc, alpha))
    out_t = acc[0:HEAD_DIM, :] / acc[HEAD_DIM:HEAD_DIM + 1, :]
    for g in range(GQA_GROUP):
        o_ref[:, g * HEAD_DIM:(g + 1) * HEAD_DIM] = out_t[:, g * tq:(g + 1) * tq].T.astype(BF16)


def _attention(q, k_c, vt_c, k_l, vt_l, tq):
    s = q.shape[0]
    n_ctx = k_c.shape[0]
    n_chunks, _, tk = vt_l.shape[1:]
    unroll = min(ATTN_UNROLL, n_chunks)
    assert n_chunks % unroll == 0 and unroll % 2 == 0
    gw = GQA_GROUP * HEAD_DIM
    return pl.pallas_call(
        functools.partial(_attn_kernel, tq=tq, tk=tk, n_chunks=n_chunks, unroll=unroll),
        grid=(N_KV_HEADS, s // tq),
        in_specs=[pl.BlockSpec((tq, gw), lambda h, i: (i, h)),
                  pl.BlockSpec((n_ctx, HEAD_DIM), lambda h, i: (0, h)),
                  pl.BlockSpec((None, 1, VT_ROWS, n_ctx), lambda h, i: (h, 0, 0, 0)),
                  pl.BlockSpec((s, HEAD_DIM), lambda h, i: (0, h)),
                  pl.BlockSpec((None, n_chunks, VT_ROWS, tk), lambda h, i: (h, 0, 0, 0))],
        out_specs=pl.BlockSpec((tq, gw), lambda h, i: (i, h)),
        out_shape=jax.ShapeDtypeStruct((s, ATTN_W), BF16),
        scratch_shapes=[pltpu.VMEM((tk, GQA_GROUP * tq), F32),
                        pltpu.VMEM((tk, GQA_GROUP * tq), F32),
                        pltpu.VMEM((tk, GQA_GROUP * tq), BF16),
                        pltpu.VMEM((tk, GQA_GROUP * tq), BF16),
                        pltpu.VMEM((n_ctx, GQA_GROUP * tq), F32)],
        compiler_params=_cparams("arbitrary", "arbitrary"),
        name="attention",
    )(q, k_c, vt_c, k_l, vt_l)


def _log_sigmoid(x):
    return jnp.minimum(x, 0.0) - jnp.log1p(jnp.exp(-jnp.abs(x)))


def _rg_kernel(x_ref, xp_ref, xn_ref, cw_ref, cb_ref, wg_ref, bg_ref, lam_ref, h0_ref,
               *rest, reverse, final, t, n_chunks):
    if final:
        hf_ref, yr_ref, o_ref, a_scr, b_scr, hc_scr, h_scr = rest
    else:
        o_ref, a_scr, b_scr, hc_scr = rest
        h_scr = o_ref
    i = pl.program_id(0)
    c = (n_chunks - 1 - i) if reverse else i
    w = x_ref.shape[1]

    @pl.when(i == 0)
    def _():
        hc_scr[...] = jnp.broadcast_to(h0_ref[...], (SUBLANES, w))

    x = x_ref[...]
    row = lax.broadcasted_iota(I32, (t, w), 0)
    pm = jnp.where(c == 0, 0.0, 1.0).astype(F32)
    nm = jnp.where(c == n_chunks - 1, 0.0, 1.0).astype(F32)
    p6 = xp_ref[SUBLANES - 2:SUBLANES - 1, :] * pm
    p7 = xp_ref[SUBLANES - 1:SUBLANES, :] * pm
    n0 = xn_ref[0:1, :] * nm
    x_m1 = jnp.where(row == 0, p7, pltpu.roll(x, 1, 0))
    x_m2 = jnp.where(row == 0, p6, jnp.where(row == 1, p7, pltpu.roll(x, 2, 0)))
    x_p1 = jnp.where(row == t - 1, n0, pltpu.roll(x, t - 1, 0))
    xc = cb_ref[...] + cw_ref[0:1, :] * x_m2
    xc = xc + cw_ref[1:2, :] * x_m1
    xc = xc + cw_ref[2:3, :] * x
    xc = xc + cw_ref[3:4, :] * x_p1

    xcb = xc.astype(BF16)
    clam = RG_C * _log_sigmoid(lam_ref[...])
    for h in range(RG_HEADS):
        sl = slice(h * RG_HD, (h + 1) * RG_HD)
        g = jnp.dot(xcb[:, sl], wg_ref[h], preferred_element_type=F32)
        r = jax.nn.sigmoid(g[:, :RG_HD] + bg_ref[0:1, sl])
        gi = jax.nn.sigmoid(g[:, RG_HD:] + bg_ref[1:2, sl])
        log_a = r * clam[:, sl]
        a = jnp.exp(log_a)
        a_scr[:, sl] = a
        b_scr[:, sl] = jnp.sqrt(-jnp.tanh(log_a) * (a * a + 1.0)) * (gi * xc[:, sl])

    srow = lax.broadcasted_iota(I32, (SUBLANES, w), 0)
    n_tiles = t // SUBLANES

    def tile_body(j, hprev):
        tile = (n_tiles - 1 - j) if reverse else j
        start = pl.multiple_of(tile * SUBLANES, SUBLANES)
        a = a_scr[pl.ds(start, SUBLANES), :]
        b = b_scr[pl.ds(start, SUBLANES), :]
        for k in (1, 2, 4):
            if reverse:
                keep = srow < SUBLANES - k
                shift = SUBLANES - k
            else:
                keep = srow >= k
                shift = k
            a_sh = jnp.where(keep, pltpu.roll(a, shift, 0), 1.0)
            b_sh = jnp.where(keep, pltpu.roll(b, shift, 0), 0.0)
            b = a * b_sh + b
            a = a * a_sh
        hh = a * hprev + b
        h_scr[pl.ds(start, SUBLANES), :] = hh
        last = hh[0:1, :] if reverse else hh[SUBLANES - 1:SUBLANES, :]
        return jnp.broadcast_to(last, (SUBLANES, w))

    hc_scr[...] = lax.fori_loop(0, n_tiles, tile_body, hc_scr[...])

    if final:
        gate = jax.nn.gelu(yr_ref[...], approximate=True)
        o_ref[...] = ((hf_ref[...] + h_scr[...]) * gate).astype(o_ref.dtype)


def _rg_scan(xr, conv_w, conv_b, wg, bg, lam, h0, *, reverse, t, hf=None, yr=None):
    s, w = xr.shape
    n_chunks = s // t
    final = hf is not None
    tb = t // SUBLANES
    last_blk = s // SUBLANES - 1
    if reverse:
        cidx = lambda i: n_chunks - 1 - i
    else:
        cidx = lambda i: i
    chunk_spec = pl.BlockSpec((t, w), lambda i: (cidx(i), 0))
    in_specs = [chunk_spec,
                pl.BlockSpec((SUBLANES, w), lambda i: (jnp.maximum(cidx(i) * tb - 1, 0), 0)),
                pl.BlockSpec((SUBLANES, w), lambda i: (jnp.minimum((cidx(i) + 1) * tb, last_blk), 0)),
                _const_spec((4, w)), _const_spec((1, w)),
                _const_spec((RG_HEADS, RG_HD, 2 * RG_HD)), _const_spec((2, w)),
                _const_spec((1, w)), _const_spec((1, w))]
    args = [xr, xr, xr, conv_w, conv_b, wg, bg, lam, h0]
    scratch = [pltpu.VMEM((t, w), F32), pltpu.VMEM((t, w), F32), pltpu.VMEM((SUBLANES, w), F32)]
    if final:
        in_specs += [chunk_spec, chunk_spec]
        args += [hf, yr]
        scratch.append(pltpu.VMEM((t, w), F32))
        out_dtype = BF16
    else:
        out_dtype = F32
    return pl.pallas_call(
        functools.partial(_rg_kernel, reverse=reverse, final=final, t=t, n_chunks=n_chunks),
        grid=(n_chunks,),
        in_specs=in_specs,
        out_specs=chunk_spec,
        out_shape=jax.ShapeDtypeStruct((s, w), out_dtype),
        scratch_shapes=scratch,
        compiler_params=_cparams("arbitrary"),
        name="rglru_bwd" if reverse else "rglru_fwd",
    )(*args)


def _layer_norm(y, g, b):
    mu = jnp.mean(y, axis=-1, keepdims=True)
    yc = y - mu
    var = jnp.mean(yc * yc, axis=-1, keepdims=True)
    return yc * lax.rsqrt(var + NORM_EPS) * g + b


def _outproj_kernel(attn_ref, rg_ref, x_ref, w_ref, g1_ref, lg_ref, lb_ref, sc2_ref, sh2_ref,
                    h1_ref, v_ref, vp_ref):
    mix = jnp.dot(attn_ref[...], w_ref[0:ATTN_W, :], preferred_element_type=F32)
    mix = mix + jnp.dot(rg_ref[...], w_ref[ATTN_W:, :], preferred_element_type=F32)
    h1 = _layer_norm(DEEPNORM_ALPHA * x_ref[...] + g1_ref[...] * mix, lg_ref[...], lb_ref[...])
    h1_ref[...] = h1
    v = h1 * (1.0 + sc2_ref[...]) + sh2_ref[...]
    v_ref[...] = v
    vp_ref[...] = _pack_pairs(v)


def _out_projection(attn, rg, x, w_out, g1, ln_g, ln_b, sc2, sh2, tm):
    s, d = x.shape
    row = lambda i: (i, 0)
    vec = _const_spec((1, d))
    return pl.pallas_call(
        _outproj_kernel,
        grid=(s // tm,),
        in_specs=[pl.BlockSpec((tm, ATTN_W), row), pl.BlockSpec((tm, RG_W), row),
                  pl.BlockSpec((tm, d), row),
                  pl.BlockSpec((ATTN_W + RG_W, d), lambda i: (0, 0), pipeline_mode=pl.Buffered(1)),
                  vec, vec, vec, vec, vec],
        out_specs=[pl.BlockSpec((tm, d), row), pl.BlockSpec((tm, d), row),
                   pl.BlockSpec((tm, d // 2), row)],
        out_shape=[jax.ShapeDtypeStruct((s, d), F32), jax.ShapeDtypeStruct((s, d), F32),
                   jax.ShapeDtypeStruct((s, d // 2), U32)],
        compiler_params=_cparams("arbitrary"),
        name="out_projection",
    )(attn, rg, x, w_out, g1, ln_g, ln_b, sc2, sh2)


def _first_index_of_max(x, iota_f, axis):
    mx = jnp.max(x, axis=axis, keepdims=True)
    idx = jnp.min(jnp.where(x == mx, iota_f, float(N_EXPERTS)), axis=axis, keepdims=True)
    return mx, idx


def _router_kernel(v_ref, whi_ref, wlo_ref, eb_ref, tri_ref,
                   idx_ref, wts_ref, rank_ref, cnt_ref, base_scr):
    v = v_ref[...]
    tm = v.shape[0]

    @pl.when(pl.program_id(0) == 0)
    def _():
        base_scr[...] = jnp.zeros(base_scr.shape, F32)

    v_hi = v.astype(BF16)
    v_lo = (v - v_hi.astype(F32)).astype(BF16)
    logits = jnp.dot(v_hi, whi_ref[...], preferred_element_type=F32)
    logits = logits + jnp.dot(v_lo, whi_ref[...], preferred_element_type=F32)
    logits = logits + jnp.dot(v_hi, wlo_ref[...], preferred_element_type=F32)
    lt = logits.T[0:N_EXPERTS, :]
    scores = jax.nn.sigmoid(lt)
    biased = scores + eb_ref[...]
    neg_inf = float("-inf")

    ig = lax.broadcasted_iota(I32, (GROUP_SIZE, tm), 0).astype(F32)
    groups = [biased[g * GROUP_SIZE:(g + 1) * GROUP_SIZE, :] for g in range(N_GROUPS)]
    gscore = []
    for bg in groups:
        top1, i1 = _first_index_of_max(bg, ig, 0)
        top2 = jnp.max(jnp.where(ig == i1, neg_inf, bg), axis=0, keepdims=True)
        gscore.append(top1 + top2)

    masked = []
    for g in range(N_GROUPS):
        ahead = jnp.zeros((1, tm), F32)
        for o in range(N_GROUPS):
            if o == g:
                continue
            before = (gscore[o] >= gscore[g]) if o < g else (gscore[o] > gscore[g])
            ahead = ahead + jnp.where(before, 1.0, 0.0)
        keep = jnp.broadcast_to(ahead < TOPK_GROUPS, (GROUP_SIZE, tm))
        masked.append(jnp.where(keep, groups[g], neg_inf))
    masked = jnp.concatenate(masked, axis=0)

    ie = lax.broadcasted_iota(I32, masked.shape, 0).astype(F32)
    seen = base_scr[...]
    ws = []
    for k in range(TOP_K):
        _, ei = _first_index_of_max(masked, ie, 0)
        hit = ie == ei
        idx_ref[k:k + 1, :] = ei.astype(I32)
        ws.append(jnp.sum(jnp.where(hit, scores, 0.0), axis=0, keepdims=True))
        masked = jnp.where(hit, neg_inf, masked)
        onehot = jnp.where(hit, 1.0, 0.0)
        before = jnp.dot(onehot.astype(BF16), tri_ref[...], preferred_element_type=F32)
        rank = jnp.sum(jnp.where(hit, before + seen, 0.0), axis=0, keepdims=True)
        rank_ref[k:k + 1, :] = rank.astype(I32)
        seen = seen + jnp.sum(onehot, axis=1, keepdims=True)
    base_scr[...] = seen
    cnt_ref[...] = jnp.broadcast_to(seen, cnt_ref.shape)
    total = ws[0]
    for k in range(1, TOP_K):
        total = total + ws[k]
    for k in range(TOP_K):
        wts_ref[k:k + 1, :] = ws[k] / total * ROUTED_SCALE


def _router(v, w_hi, w_lo, e_bias_col, tm):
    s, d = v.shape
    tri = jnp.triu(jnp.ones((tm, tm), BF16), k=1)
    slot = pl.BlockSpec((TOP_K, tm), lambda i: (0, i))
    return pl.pallas_call(
        _router_kernel,
        grid=(s // tm,),
        in_specs=[pl.BlockSpec((tm, d), lambda i: (i, 0)),
                  _const_spec((d, LANES)), _const_spec((d, LANES)),
                  _const_spec((N_EXPERTS, 1)), _const_spec((tm, tm))],
        out_specs=[slot, slot, slot, _const_spec((N_EXPERTS, LANES))],
        out_shape=[jax.ShapeDtypeStruct((TOP_K, s), I32),
                   jax.ShapeDtypeStruct((TOP_K, s), F32),
                   jax.ShapeDtypeStruct((TOP_K, s), I32),
                   jax.ShapeDtypeStruct((N_EXPERTS, LANES), F32)],
        scratch_shapes=[pltpu.VMEM((N_EXPERTS, 1), F32)],
        compiler_params=_cparams("arbitrary"),
        name="router",
    )(v, w_hi, w_lo, e_bias_col, tri)


def _pack_pairs(x):
    h = x.shape[1] // 2
    lo = lax.bitcast_convert_type(x[:, :h].astype(BF16).astype(F32), U32)
    hi = lax.bitcast_convert_type(x[:, h:].astype(BF16).astype(F32), U32)
    return (lo >> 16) | (hi & jnp.uint32(0xFFFF0000))


def _unpack_pairs(w):
    lo = lax.bitcast_convert_type(w << 16, F32)
    hi = lax.bitcast_convert_type(w & jnp.uint32(0xFFFF0000), F32)
    return jnp.concatenate([lo, hi], axis=1)


def _dest_kernel(idx_ref, rank_ref, pstart_ref, dest_ref):
    tm = idx_ref.shape[1]
    ie = lax.broadcasted_iota(I32, (N_EXPERTS, tm), 0)
    pstart = pstart_ref[...]
    for k in range(TOP_K):
        hit = ie == idx_ref[k:k + 1, :]
        seg = jnp.sum(jnp.where(hit, pstart, 0.0), axis=0, keepdims=True)
        dest_ref[k:k + 1, :] = seg.astype(I32) + rank_ref[k:k + 1, :]


def _slot_positions(idx, rank, pstart_col, tm):
    s = idx.shape[1]
    slot = pl.BlockSpec((TOP_K, tm), lambda i: (0, i))
    return pl.pallas_call(
        _dest_kernel,
        grid=(s // tm,),
        in_specs=[slot, slot, _const_spec((N_EXPERTS, 1))],
        out_specs=slot,
        out_shape=jax.ShapeDtypeStruct((TOP_K, s), I32),
        compiler_params=_cparams("arbitrary"),
        name="slot_positions",
    )(idx, rank, pstart_col)


def _expert_kernel(blk_e_ref, n_used_ref, tok01_v, tok_v, dst_v, vp_hbm, w1_ref, w3_ref, w2_ref,
                   y_hbm, tok_s, dst_s, xbuf, ybuf, w1b, w3b, w2b, sem_i, sem_g, sem_s,
                   *, bm, spare_row0):
    b = pl.program_id(0)
    n_used = n_used_ref[0]
    cur = b % 2
    nxt = 1 - cur

    def gather_start(ids_slot, buf_slot, r):
        pltpu.make_async_copy(vp_hbm.at[pl.ds(tok_s[ids_slot, r], 1), :],
                              xbuf.at[buf_slot, pl.ds(r, 1), :], sem_g.at[buf_slot]).start()

    def scatter_start(slot, r):
        pltpu.make_async_copy(ybuf.at[slot, pl.ds(r, 1), :],
                              y_hbm.at[pl.ds(dst_s[slot, r], 1), :], sem_s.at[slot]).start()

    def gather_wait(slot):
        pltpu.make_async_copy(vp_hbm.at[pl.ds(0, bm), :], xbuf.at[slot], sem_g.at[slot]).wait()

    def scatter_wait(slot):
        pltpu.make_async_copy(ybuf.at[slot], y_hbm.at[pl.ds(0, bm), :], sem_s.at[slot]).wait()

    @pl.when(b == 0)
    def _():
        c0 = pltpu.make_async_copy(tok01_v.at[0, 0], tok_s.at[0], sem_i.at[0])
        c1 = pltpu.make_async_copy(tok01_v.at[1, 0], tok_s.at[1], sem_i.at[1])
        c0.start()
        c1.start()
        ybuf[1] = jnp.zeros(ybuf.shape[1:], ybuf.dtype)

        def spare(r, carry):
            dst_s[1, r] = spare_row0 + bm + r
            return carry

        lax.fori_loop(0, bm, spare, 0)
        c0.wait()
        c1.wait()

        def first(r, carry):
            gather_start(0, 0, r)
            return carry

        lax.fori_loop(0, bm, first, 0)

    @pl.when(b < n_used)
    def _():
        gather_wait(cur)

        @pl.when(b >= 1)
        def _():
            scatter_wait(cur)

        next_ids = (pltpu.make_async_copy(tok_v.at[0, 0], tok_s.at[cur], sem_i.at[0]),
                    pltpu.make_async_copy(dst_v.at[0, 0], dst_s.at[cur], sem_i.at[1]))
        for cp in next_ids:
            cp.start()

        prev_e = blk_e_ref[jnp.maximum(b - 1, 0)]

        @pl.when(jnp.logical_or(b == 0, blk_e_ref[b] != prev_e))
        def _():
            w1b[...] = w1_ref[...].astype(BF16)
            w3b[...] = w3_ref[...].astype(BF16)
            w2b[...] = w2_ref[...].astype(BF16)

        xb = _unpack_pairs(xbuf[cur]).astype(BF16)
        third = bm // 3
        for r in range(0, third):
            gather_start(nxt, nxt, r)
            scatter_start(nxt, r)
        h1 = jnp.dot(xb, w1b[...], preferred_element_type=F32)
        for r in range(third, 2 * third):
            gather_start(nxt, nxt, r)
            scatter_start(nxt, r)
        h3 = jnp.dot(xb, w3b[...], preferred_element_type=F32)
        for r in range(2 * third, bm):
            gather_start(nxt, nxt, r)
            scatter_start(nxt, r)
        act = (h1 * jax.nn.sigmoid(h1) * h3).astype(BF16)
        ybuf[cur] = _pack_pairs(jnp.dot(act, w2b[...], preferred_element_type=F32))

        for cp in next_ids:
            cp.wait()

    @pl.when(b == n_used)
    def _():
        gather_wait(cur)
        scatter_wait(cur)

        def last(r, carry):
            scatter_start(nxt, r)
            return carry

        lax.fori_loop(0, bm, last, 0)
        scatter_wait(nxt)


def _experts(blk_e, n_used, slot_tok, slot_dst, vp, w_e1, w_e3, w_e2, n_out_rows, bm):
    n_blocks = slot_tok.shape[0]
    hw = vp.shape[1]
    d, ff = w_e1.shape[-2:]
    wspec1 = pl.BlockSpec((None, d, ff), lambda b, be, nu: (be[b], 0, 0))
    wspec2 = pl.BlockSpec((None, ff, d), lambda b, be, nu: (be[b], 0, 0))
    any_spec = pl.BlockSpec(memory_space=pl.ANY)
    last = n_blocks - 1
    grid_spec = pltpu.PrefetchScalarGridSpec(
        num_scalar_prefetch=2,
        grid=(n_blocks + 1,),
        in_specs=[pl.BlockSpec((2, 1, bm), lambda b, be, nu: (0, 0, 0)),
                  pl.BlockSpec((1, 1, bm), lambda b, be, nu: (jnp.minimum(b + 2, last), 0, 0)),
                  pl.BlockSpec((1, 1, bm), lambda b, be, nu: (jnp.minimum(b, last), 0, 0)),
                  any_spec, wspec1, wspec1, wspec2],
        out_specs=any_spec,
        scratch_shapes=[pltpu.SMEM((2, bm), I32), pltpu.SMEM((2, bm), I32),
                        pltpu.VMEM((2, bm, hw), U32), pltpu.VMEM((2, bm, hw), U32),
                        pltpu.VMEM((d, ff), BF16), pltpu.VMEM((d, ff), BF16),
                        pltpu.VMEM((ff, d), BF16),
                        pltpu.SemaphoreType.DMA((2,)), pltpu.SemaphoreType.DMA((2,)),
                        pltpu.SemaphoreType.DMA((2,))])
    blk_e_pad = jnp.concatenate([blk_e, blk_e[-1:]])
    return pl.pallas_call(
        functools.partial(_expert_kernel, bm=bm, spare_row0=n_out_rows - 2 * bm),
        grid_spec=grid_spec,
        out_shape=jax.ShapeDtypeStruct((n_out_rows, hw), U32),
        compiler_params=_cparams("arbitrary"),
        name="experts",
    )(blk_e_pad, n_used, slot_tok, slot_tok, slot_dst, vp, w_e1, w_e3, w_e2)


def _block_tables(counts, n_blocks, bm):
    padded = (counts + bm - 1) // bm * bm
    pad_end = jnp.cumsum(padded)
    pad_start = pad_end - padded
    blk_first = jnp.arange(n_blocks, dtype=I32) * bm
    blk_e = jnp.minimum(jnp.sum(pad_end[None, :] <= blk_first[:, None], axis=1), N_EXPERTS - 1)
    n_used = pad_end[-1] // bm
    blk_e = jnp.where(jnp.arange(n_blocks) < n_used, blk_e, blk_e[jnp.maximum(n_used - 1, 0)])
    return pad_start, blk_e.astype(I32), n_used.astype(I32).reshape(1)


def _slot_tables(dest, n_blocks, bm):
    nk = dest.size
    s = dest.shape[1]
    n_slots = n_blocks * bm
    slot_f = jnp.full((n_slots,), -1, I32).at[dest.reshape(nk)].set(
        jnp.arange(nk, dtype=I32), unique_indices=True)
    p = jnp.arange(n_slots, dtype=I32)
    spare = nk + ((p // bm) % 2) * bm + p % bm
    valid = slot_f >= 0
    slot_tok = jnp.where(valid, slot_f % s, 0)
    slot_dst = jnp.where(valid, slot_f, spare)
    return slot_tok.reshape(n_blocks, 1, bm), slot_dst.reshape(n_blocks, 1, bm)


def _combine_kernel(*refs):
    y_refs = refs[:TOP_K]
    (wts_ref, v_ref, h1_ref, g2_ref, lg_ref, lb_ref, ws1_ref, ws3_ref, ws2_ref, o_ref) = refs[TOP_K:]
    tm = v_ref.shape[0]
    vb = v_ref[...].astype(BF16)
    a1 = jnp.dot(vb, ws1_ref[...], preferred_element_type=F32)
    a3 = jnp.dot(vb, ws3_ref[...], preferred_element_type=F32)
    act = (a1 * jax.nn.sigmoid(a1) * a3).astype(BF16)
    ff = jnp.dot(act, ws2_ref[...], preferred_element_type=F32)
    wpad = jnp.concatenate([wts_ref[...], jnp.zeros((LANES - TOP_K, tm), F32)], axis=0)
    wt = wpad.T
    for k in range(TOP_K):
        ff = ff + _unpack_pairs(y_refs[k][...]) * wt[:, k:k + 1]
    o_ref[...] = _layer_norm(DEEPNORM_ALPHA * h1_ref[...] + g2_ref[...] * ff,
                             lg_ref[...], lb_ref[...])


def _combine(y_slots, wts, v, h1, g2, ln_g, ln_b, w_s1, w_s3, w_s2, tm):
    s, d = v.shape
    nb = s // tm
    hw = y_slots.shape[1]
    row = lambda i: (i, 0)
    vec = _const_spec((1, d))
    y_specs = [pl.BlockSpec((tm, hw), (lambda i, k=k: (k * nb + i, 0))) for k in range(TOP_K)]
    return pl.pallas_call(
        _combine_kernel,
        grid=(nb,),
        in_specs=y_specs + [pl.BlockSpec((TOP_K, tm), lambda i: (0, i)),
                            pl.BlockSpec((tm, d), row), pl.BlockSpec((tm, d), row),
                            vec, vec, vec,
                            _const_spec(w_s1.shape), _const_spec(w_s3.shape),
                            _const_spec(w_s2.shape)],
        out_specs=pl.BlockSpec((tm, d), row),
        out_shape=jax.ShapeDtypeStruct((s, d), F32),
        compiler_params=_cparams("arbitrary"),
        name="combine",
    )(*([y_slots] * TOP_K), wts, v, h1, g2, ln_g, ln_b, w_s1, w_s3, w_s2)


def _rope_tables(s):
    half = HEAD_DIM // 4
    inv_freq = ROPE_THETA ** (-np.arange(half, dtype=np.float32) / half)
    t = np.arange(s)
    ang_r = (t // GRID_W).astype(np.float32)[:, None] * inv_freq[None, :]
    ang_c = (t % GRID_W).astype(np.float32)[:, None] * inv_freq[None, :]
    cos = np.concatenate([np.cos(ang_r)] * 2 + [np.cos(ang_c)] * 2, axis=-1)
    sin = np.concatenate([-np.sin(ang_r), np.sin(ang_r), -np.sin(ang_c), np.sin(ang_c)], axis=-1)
    return jnp.asarray(cos, F32), jnp.asarray(sin, F32)


def kernel(x, c, ctx, c_ctx, w_mod, b_mod, w_in, q_norm, k_norm, conv_w, conv_b, rg_wa, rg_ba,
           rg_wx, rg_bx, rg_lam, w_out, ln1_g, ln1_b, w_router, e_bias, w_e1, w_e3, w_e2,
           w_s1, w_s3, w_s2, ln2_g, ln2_b):
    assert x.shape[0] == 1 and w_mod.shape[0] == DEPTH
    _, s, d = x.shape
    n_ctx = ctx.shape[1]
    x2 = x[0]

    c2t = jnp.stack([c[0], c_ctx], axis=1)
    mod = _modulation(c2t, w_mod[0], b_mod[0].reshape(1, -1))
    sh1, sc1, g1, sh2, sc2, g2 = [mod[0:1, j * d:(j + 1) * d] for j in range(6)]
    csh1, csc1 = mod[1:2, 0:d], mod[1:2, d:2 * d]

    w_in_b = w_in[0].astype(BF16)
    qg = q_norm[0].reshape(1, HEAD_DIM)
    kg = k_norm[0].reshape(1, HEAD_DIM)
    cos, sin = _rope_tables(s)
    q_l, k_l, vt_l, xr_l, yr_l = _in_projection(x2, sc1, sh1, w_in_b, qg, kg, cos, sin,
                                                tm=min(512, s), kv_chunk=min(KV_CHUNK, s))
    _, k_c, vt_c, xr_c, _ = _in_projection(
        ctx[0], csc1, csh1, w_in_b, qg, kg,
        jnp.ones((n_ctx, HEAD_DIM), F32), jnp.zeros((n_ctx, HEAD_DIM), F32),
        tm=n_ctx, kv_chunk=n_ctx)

    attn = _attention(q_l, k_c, vt_c, k_l, vt_l, tq=min(256, s))

    wg = jnp.concatenate([rg_wa[0], rg_wx[0]], axis=-1).astype(BF16)
    cb = conv_b[0].reshape(1, RG_W)
    zero_state = jnp.zeros((1, RG_W), F32)
    rg_args = []
    for dd in range(2):
        rg_args.append((conv_w[0], cb, wg[dd],
                        jnp.stack([rg_ba[0, dd], rg_bx[0, dd]], axis=0),
                        rg_lam[0, dd].reshape(1, RG_W)))
    t_rg = min(512, s)
    hc_f = _rg_scan(xr_c, *rg_args[0], zero_state, reverse=False, t=n_ctx)
    hc_b = _rg_scan(xr_c, *rg_args[1], zero_state, reverse=True, t=n_ctx)
    h_f = _rg_scan(xr_l, *rg_args[0], hc_f[n_ctx - 1:n_ctx], reverse=False, t=t_rg)
    rg = _rg_scan(xr_l, *rg_args[1], hc_b[0:1], reverse=True, t=t_rg, hf=h_f, yr=yr_l)

    h1, v, vp = _out_projection(attn, rg, x2, w_out[0].astype(BF16), g1,
                                ln1_g[0].reshape(1, d), ln1_b[0].reshape(1, d), sc2, sh2,
                                tm=min(256, s))

    wr = jnp.pad(w_router[0], ((0, 0), (0, LANES - N_EXPERTS)))
    wr_hi = wr.astype(BF16)
    wr_lo = (wr - wr_hi.astype(F32)).astype(BF16)
    idx, wts, rank, cnt = _router(v, wr_hi, wr_lo, e_bias[0].reshape(N_EXPERTS, 1), tm=min(256, s))

    bm = min(EXPERT_BLOCK, s)
    n_blocks = s * TOP_K // bm + N_EXPERTS
    pad_start, blk_e, n_used = _block_tables(cnt[:, 0].astype(I32), n_blocks, bm)
    dest = _slot_positions(idx, rank, pad_start.astype(F32).reshape(N_EXPERTS, 1), tm=min(256, s))
    slot_tok, slot_dst = _slot_tables(dest, n_blocks, bm)
    y_slots = _experts(blk_e, n_used, slot_tok, slot_dst, vp, w_e1[0], w_e3[0], w_e2[0],
                       n_out_rows=s * TOP_K + 2 * bm, bm=bm)

    out = _combine(y_slots, wts, v, h1, g2, ln2_g[0].reshape(1, d), ln2_b[0].reshape(1, d),
                   w_s1[0].astype(BF16), w_s3[0].astype(BF16), w_s2[0].astype(BF16),
                   tm=min(256, s))
    return out[None]
```

```python
import functools

import jax
import jax.numpy as jnp
import numpy as np
from jax import lax
from jax.experimental import pallas as pl
from jax.experimental.pallas import tpu as pltpu

F32 = jnp.float32
BF16 = jnp.bfloat16
I32 = jnp.int32
U32 = jnp.uint32

GRID_W = 64
HEAD_DIM = 128
N_HEADS = 8
N_KV_HEADS = 2
GQA_GROUP = N_HEADS // N_KV_HEADS
ATTN_W = N_HEADS * HEAD_DIM
KV_W = N_KV_HEADS * HEAD_DIM
ROPE_THETA = 10000.0
RG_W = 1024
RG_HEADS = 8
RG_HD = RG_W // RG_HEADS
RG_C = 8.0
PROJ_W = ATTN_W + 2 * KV_W + 2 * RG_W
N_EXPERTS = 64
N_GROUPS = 8
GROUP_SIZE = N_EXPERTS // N_GROUPS
TOPK_GROUPS = 4
TOP_K = 8
ROUTED_SCALE = 2.5
NORM_EPS = 1e-6
DEPTH = 1
DEEPNORM_ALPHA = (2.0 * DEPTH) ** 0.25
LOG2E = 1.4426950408889634

LANES = 128
SUBLANES = 8
BF16_SUBLANES = 16
VMEM_LIMIT = 56 * 1024 * 1024

NEG_BIG = -1e30
KV_CHUNK = 512
ATTN_UNROLL = 16
VT_ROWS = HEAD_DIM + BF16_SUBLANES
EXPERT_BLOCK = 512


def _cparams(*sem):
    return pltpu.CompilerParams(dimension_semantics=sem, vmem_limit_bytes=VMEM_LIMIT)


def _const_spec(shape):
    nd = len(shape)
    return pl.BlockSpec(shape, lambda *_: (0,) * nd)


def _mod_kernel(ct_ref, w_ref, b_ref, o_ref, sb_ref, *, tn):
    d = w_ref.shape[0]

    @pl.when(pl.program_id(0) == 0)
    def _():
        ct = ct_ref[...]
        s = ct * jax.nn.sigmoid(ct)
        sb_ref[0] = jnp.broadcast_to(s[:, 0:1], (d, LANES))
        sb_ref[1] = jnp.broadcast_to(s[:, 1:2], (d, LANES))

    for c in range(tn // LANES):
        sl = slice(c * LANES, (c + 1) * LANES)
        wc = w_ref[:, sl]
        bc = b_ref[:, sl]
        o0 = jnp.sum(wc * sb_ref[0], axis=0, keepdims=True) + bc
        o1 = jnp.sum(wc * sb_ref[1], axis=0, keepdims=True) + bc
        o_ref[:, sl] = jnp.concatenate(
            [o0, o1, jnp.zeros((SUBLANES - 2, LANES), F32)], axis=0)


def _modulation(c2t, w_mod, b_mod):
    d, n = w_mod.shape
    tn = 1024
    return pl.pallas_call(
        functools.partial(_mod_kernel, tn=tn),
        grid=(n // tn,),
        in_specs=[_const_spec((d, 2)),
                  pl.BlockSpec((d, tn), lambda j: (0, j)),
                  pl.BlockSpec((1, tn), lambda j: (0, j))],
        out_specs=pl.BlockSpec((SUBLANES, tn), lambda j: (0, j)),
        out_shape=jax.ShapeDtypeStruct((SUBLANES, n), F32),
        scratch_shapes=[pltpu.VMEM((2, d, LANES), F32)],
        compiler_params=_cparams("arbitrary"),
        name="modulation",
    )(c2t, w_mod, b_mod)


def _swap_half(y):
    lane = lax.broadcasted_iota(I32, y.shape, 1)
    return jnp.where((lane % 64) < 32,
                     pltpu.roll(y, LANES - 32, 1), pltpu.roll(y, 32, 1))


def _norm_rope(ph, g, cos, sin_signed, scale):
    ms = jnp.mean(ph * ph, axis=-1, keepdims=True)
    y = (ph * lax.rsqrt(ms + NORM_EPS)) * g
    y = y * cos + _swap_half(y) * sin_signed
    if scale != 1.0:
        y = y * scale
    return y


def _inproj_kernel(x_ref, sc_ref, sh_ref, w_ref, qg_ref, kg_ref, cos_ref, sin_ref,
                   q_ref, k_ref, vt_ref, xr_ref, yr_ref, *, q_scale):
    tm = x_ref.shape[0]
    kc = vt_ref.shape[-1]
    u = (x_ref[...] * (1.0 + sc_ref[...]) + sh_ref[...]).astype(BF16)
    cos = cos_ref[...]
    sin = sin_ref[...]
    o = 0
    pq = jnp.dot(u, w_ref[:, o:o + ATTN_W], preferred_element_type=F32)
    for h in range(N_HEADS):
        sl = slice(h * HEAD_DIM, (h + 1) * HEAD_DIM)
        q_ref[:, sl] = _norm_rope(pq[:, sl], qg_ref[...], cos, sin, q_scale).astype(BF16)
    o += ATTN_W
    pk = jnp.dot(u, w_ref[:, o:o + KV_W], preferred_element_type=F32)
    for h in range(N_KV_HEADS):
        sl = slice(h * HEAD_DIM, (h + 1) * HEAD_DIM)
        k_ref[:, sl] = _norm_rope(pk[:, sl], kg_ref[...], cos, sin, 1.0).astype(BF16)
    o += KV_W
    pv = jnp.dot(u, w_ref[:, o:o + KV_W], preferred_element_type=F32)
    ones_rows = jnp.where(
        lax.broadcasted_iota(I32, (VT_ROWS - HEAD_DIM, kc), 0) == 0, 1.0, 0.0).astype(BF16)
    for h in range(N_KV_HEADS):
        for cc in range(tm // kc):
            blk = pv[cc * kc:(cc + 1) * kc, h * HEAD_DIM:(h + 1) * HEAD_DIM]
            vt_ref[h, cc, 0:HEAD_DIM, :] = blk.T.astype(BF16)
            vt_ref[h, cc, HEAD_DIM:VT_ROWS, :] = ones_rows
    o += KV_W
    xr_ref[...] = jnp.dot(u, w_ref[:, o:o + RG_W], preferred_element_type=F32)
    o += RG_W
    yr_ref[...] = jnp.dot(u, w_ref[:, o:o + RG_W], preferred_element_type=F32)


def _in_projection(x, sc, sh, w_in, qg, kg, cos, sin, tm, kv_chunk):
    s, d = x.shape
    row = lambda i: (i, 0)
    if kv_chunk >= tm:
        per = kv_chunk // tm
        vt_spec = pl.BlockSpec((N_KV_HEADS, 1, VT_ROWS, tm), lambda i: (0, i // per, 0, i % per))
    else:
        vt_spec = pl.BlockSpec((N_KV_HEADS, tm // kv_chunk, VT_ROWS, kv_chunk),
                               lambda i: (0, i, 0, 0))
    return pl.pallas_call(
        functools.partial(_inproj_kernel, q_scale=HEAD_DIM ** -0.5 * LOG2E),
        grid=(s // tm,),
        in_specs=[pl.BlockSpec((tm, d), row),
                  _const_spec((1, d)), _const_spec((1, d)),
                  pl.BlockSpec((d, PROJ_W), lambda i: (0, 0), pipeline_mode=pl.Buffered(1)),
                  _const_spec((1, HEAD_DIM)), _const_spec((1, HEAD_DIM)),
                  pl.BlockSpec((tm, HEAD_DIM), row), pl.BlockSpec((tm, HEAD_DIM), row)],
        out_specs=[pl.BlockSpec((tm, ATTN_W), row), pl.BlockSpec((tm, KV_W), row),
                   vt_spec,
                   pl.BlockSpec((tm, RG_W), row), pl.BlockSpec((tm, RG_W), row)],
        out_shape=[jax.ShapeDtypeStruct((s, ATTN_W), BF16),
                   jax.ShapeDtypeStruct((s, KV_W), BF16),
                   jax.ShapeDtypeStruct((N_KV_HEADS, s // kv_chunk, VT_ROWS, kv_chunk), BF16),
                   jax.ShapeDtypeStruct((s, RG_W), F32),
                   jax.ShapeDtypeStruct((s, RG_W), F32)],
        compiler_params=_cparams("arbitrary"),
        name="in_projection",
    )(x, sc, sh, w_in, qg, kg, cos, sin)


def _attn_kernel(q_ref, kc_ref, vtc_ref, kl_ref, vtl_ref, o_ref, sa_ref, sb_ref, pa_ref, pb_ref, sc_ref,
                 *, tq, tk, n_chunks, unroll):
    q = q_ref[...]
    qs = jnp.concatenate(
        [q[:, g * HEAD_DIM:(g + 1) * HEAD_DIM] for g in range(GQA_GROUP)], axis=0)
    cols = GQA_GROUP * tq

    def scores(k):
        return lax.dot_general(k, qs, (((1,), (1,)), ((), ())), preferred_element_type=F32)

    def softmax(s_ref, m):
        m_new = jnp.maximum(m, jnp.max(s_ref[...], axis=0, keepdims=True))
        return m_new, jnp.exp2(m - m_new), jnp.exp2(s_ref[...] - m_new).astype(BF16)

    def weighted_values(vt, p, alpha, acc):
        return alpha * acc + jnp.dot(vt, p, preferred_element_type=F32)

    m0 = jnp.full((1, cols), NEG_BIG, F32)
    a0 = jnp.zeros((VT_ROWS, cols), F32)
    sc_ref[...] = scores(kc_ref[...])
    m, alpha, p = softmax(sc_ref, m0)
    acc = weighted_values(vtc_ref[0], p, alpha, a0)

    def latent_scores(j):
        j = jnp.minimum(j, n_chunks - 1)
        return scores(kl_ref[pl.ds(pl.multiple_of(j * tk, tk), tk), :])

    s_refs = (sa_ref, sb_ref)
    p_refs = (pa_ref, pb_ref)
    sa_ref[...] = latent_scores(0)
    sb_ref[...] = latent_scores(1)
    m, alpha, pa_ref[...] = softmax(sa_ref, m)

    def body(i, carry):
        m, acc, alpha = carry
        for u in range(unroll):
            c = unroll * i + u
            cur, nxt = u % 2, (u + 1) % 2
            s_refs[cur][...] = latent_scores(c + 2)
            m, alpha_next, p_refs[nxt][...] = softmax(s_refs[nxt], m)
            acc = weighted_values(vtl_ref[c], p_refs[cur][...], alpha, acc)
            alpha = alpha_next
        return m, acc, alpha

    _, acc, _ = lax.fori_loop(0, n_chunks // unroll, body, (m, acc, alpha))
    out_t = acc[0:HEAD_DIM, :] / acc[HEAD_DIM:HEAD_DIM + 1, :]
    for g in range(GQA_GROUP):
        o_ref[:, g * HEAD_DIM:(g + 1) * HEAD_DIM] = out_t[:, g * tq:(g + 1) * tq].T.astype(BF16)


def _attention(q, k_c, vt_c, k_l, vt_l, tq):
    s = q.shape[0]
    n_ctx = k_c.shape[0]
    n_chunks, _, tk = vt_l.shape[1:]
    unroll = min(ATTN_UNROLL, n_chunks)
    assert n_chunks % unroll == 0 and unroll % 2 == 0
    gw = GQA_GROUP * HEAD_DIM
    return pl.pallas_call(
        functools.partial(_attn_kernel, tq=tq, tk=tk, n_chunks=n_chunks, unroll=unroll),
        grid=(N_KV_HEADS, s // tq),
        in_specs=[pl.BlockSpec((tq, gw), lambda h, i: (i, h)),
                  pl.BlockSpec((n_ctx, HEAD_DIM), lambda h, i: (0, h)),
                  pl.BlockSpec((None, 1, VT_ROWS, n_ctx), lambda h, i: (h, 0, 0, 0)),
                  pl.BlockSpec((s, HEAD_DIM), lambda h, i: (0, h)),
                  pl.BlockSpec((None, n_chunks, VT_ROWS, tk), lambda h, i: (h, 0, 0, 0))],
        out_specs=pl.BlockSpec((tq, gw), lambda h, i: (i, h)),
        out_shape=jax.ShapeDtypeStruct((s, ATTN_W), BF16),
        scratch_shapes=[pltpu.VMEM((tk, GQA_GROUP * tq), F32),
                        pltpu.VMEM((tk, GQA_GROUP * tq), F32),
                        pltpu.VMEM((tk, GQA_GROUP * tq), BF16),
                        pltpu.VMEM((tk, GQA_GROUP * tq), BF16),
                        pltpu.VMEM((n_ctx, GQA_GROUP * tq), F32)],
        compiler_params=_cparams("arbitrary", "arbitrary"),
        name="attention",
    )(q, k_c, vt_c, k_l, vt_l)


def _log_sigmoid(x):
    return jnp.minimum(x, 0.0) - jnp.log1p(jnp.exp(-jnp.abs(x)))


def _rg_kernel(x_ref, xp_ref, xn_ref, cw_ref, cb_ref, wg_ref, bg_ref, lam_ref, h0_ref,
               *rest, reverse, final, t, n_chunks):
    if final:
        hf_ref, yr_ref, o_ref, a_scr, b_scr, hc_scr, h_scr = rest
    else:
        o_ref, a_scr, b_scr, hc_scr = rest
        h_scr = o_ref
    i = pl.program_id(0)
    c = (n_chunks - 1 - i) if reverse else i
    w = x_ref.shape[1]

    @pl.when(i == 0)
    def _():
        hc_scr[...] = jnp.broadcast_to(h0_ref[...], (SUBLANES, w))

    x = x_ref[...]
    row = lax.broadcasted_iota(I32, (t, w), 0)
    pm = jnp.where(c == 0, 0.0, 1.0).astype(F32)
    nm = jnp.where(c == n_chunks - 1, 0.0, 1.0).astype(F32)
    p6 = xp_ref[SUBLANES - 2:SUBLANES - 1, :] * pm
    p7 = xp_ref[SUBLANES - 1:SUBLANES, :] * pm
    n0 = xn_ref[0:1, :] * nm
    x_m1 = jnp.where(row == 0, p7, pltpu.roll(x, 1, 0))
    x_m2 = jnp.where(row == 0, p6, jnp.where(row == 1, p7, pltpu.roll(x, 2, 0)))
    x_p1 = jnp.where(row == t - 1, n0, pltpu.roll(x, t - 1, 0))
    xc = cb_ref[...] + cw_ref[0:1, :] * x_m2
    xc = xc + cw_ref[1:2, :] * x_m1
    xc = xc + cw_ref[2:3, :] * x
    xc = xc + cw_ref[3:4, :] * x_p1

    xcb = xc.astype(BF16)
    clam = RG_C * _log_sigmoid(lam_ref[...])
    for h in range(RG_HEADS):
        sl = slice(h * RG_HD, (h + 1) * RG_HD)
        g = jnp.dot(xcb[:, sl], wg_ref[h], preferred_element_type=F32)
        r = jax.nn.sigmoid(g[:, :RG_HD] + bg_ref[0:1, sl])
        gi = jax.nn.sigmoid(g[:, RG_HD:] + bg_ref[1:2, sl])
        log_a = r * clam[:, sl]
        a = jnp.exp(log_a)
        a_scr[:, sl] = a
        b_scr[:, sl] = jnp.sqrt(-jnp.tanh(log_a) * (a * a + 1.0)) * (gi * xc[:, sl])

    srow = lax.broadcasted_iota(I32, (SUBLANES, w), 0)
    n_tiles = t // SUBLANES

    def tile_body(j, hprev):
        tile = (n_tiles - 1 - j) if reverse else j
        start = pl.multiple_of(tile * SUBLANES, SUBLANES)
        a = a_scr[pl.ds(start, SUBLANES), :]
        b = b_scr[pl.ds(start, SUBLANES), :]
        for k in (1, 2, 4):
            if reverse:
                keep = srow < SUBLANES - k
                shift = SUBLANES - k
            else:
                keep = srow >= k
                shift = k
            a_sh = jnp.where(keep, pltpu.roll(a, shift, 0), 1.0)
            b_sh = jnp.where(keep, pltpu.roll(b, shift, 0), 0.0)
            b = a * b_sh + b
            a = a * a_sh
        hh = a * hprev + b
        h_scr[pl.ds(start, SUBLANES), :] = hh
        last = hh[0:1, :] if reverse else hh[SUBLANES - 1:SUBLANES, :]
        return jnp.broadcast_to(last, (SUBLANES, w))

    hc_scr[...] = lax.fori_loop(0, n_tiles, tile_body, hc_scr[...])

    if final:
        gate = jax.nn.gelu(yr_ref[...], approximate=True)
        o_ref[...] = ((hf_ref[...] + h_scr[...]) * gate).astype(o_ref.dtype)


def _rg_scan(xr, conv_w, conv_b, wg, bg, lam, h0, *, reverse, t, hf=None, yr=None):
    s, w = xr.shape
    n_chunks = s // t
    final = hf is not None
    tb = t // SUBLANES
    last_blk = s // SUBLANES - 1
    if reverse:
        cidx = lambda i: n_chunks - 1 - i
    else:
        cidx = lambda i: i
    chunk_spec = pl.BlockSpec((t, w), lambda i: (cidx(i), 0))
    in_specs = [chunk_spec,
                pl.BlockSpec((SUBLANES, w), lambda i: (jnp.maximum(cidx(i) * tb - 1, 0), 0)),
                pl.BlockSpec((SUBLANES, w), lambda i: (jnp.minimum((cidx(i) + 1) * tb, last_blk), 0)),
                _const_spec((4, w)), _const_spec((1, w)),
                _const_spec((RG_HEADS, RG_HD, 2 * RG_HD)), _const_spec((2, w)),
                _const_spec((1, w)), _const_spec((1, w))]
    args = [xr, xr, xr, conv_w, conv_b, wg, bg, lam, h0]
    scratch = [pltpu.VMEM((t, w), F32), pltpu.VMEM((t, w), F32), pltpu.VMEM((SUBLANES, w), F32)]
    if final:
        in_specs += [chunk_spec, chunk_spec]
        args += [hf, yr]
        scratch.append(pltpu.VMEM((t, w), F32))
        out_dtype = BF16
    else:
        out_dtype = F32
    return pl.pallas_call(
        functools.partial(_rg_kernel, reverse=reverse, final=final, t=t, n_chunks=n_chunks),
        grid=(n_chunks,),
        in_specs=in_specs,
        out_specs=chunk_spec,
        out_shape=jax.ShapeDtypeStruct((s, w), out_dtype),
        scratch_shapes=scratch,
        compiler_params=_cparams("arbitrary"),
        name="rglru_bwd" if reverse else "rglru_fwd",
    )(*args)


def _layer_norm(y, g, b):
    mu = jnp.mean(y, axis=-1, keepdims=True)
    yc = y - mu
    var = jnp.mean(yc * yc, axis=-1, keepdims=True)
    return yc * lax.rsqrt(var + NORM_EPS) * g + b


def _outproj_kernel(attn_ref, rg_ref, x_ref, w_ref, g1_ref, lg_ref, lb_ref, sc2_ref, sh2_ref,
                    whi_ref, wlo_ref, eb_ref, tri_ref,
                    h1_ref, v_ref, idx_ref, wts_ref, rank_ref, cnt_ref, base_scr):
    mix = jnp.dot(attn_ref[...], w_ref[0:ATTN_W, :], preferred_element_type=F32)
    mix = mix + jnp.dot(rg_ref[...], w_ref[ATTN_W:, :], preferred_element_type=F32)
    h1 = _layer_norm(DEEPNORM_ALPHA * x_ref[...] + g1_ref[...] * mix, lg_ref[...], lb_ref[...])
    h1_ref[...] = h1
    v = h1 * (1.0 + sc2_ref[...]) + sh2_ref[...]
    v_ref[...] = v
    _route(v, whi_ref, wlo_ref, eb_ref, tri_ref, idx_ref, wts_ref, rank_ref, cnt_ref, base_scr)


def _out_projection(attn, rg, x, w_out, g1, ln_g, ln_b, sc2, sh2, w_hi, w_lo, e_bias_col, tm):
    s, d = x.shape
    row = lambda i: (i, 0)
    vec = _const_spec((1, d))
    tri = jnp.triu(jnp.ones((tm, tm), BF16), k=1)
    slot = pl.BlockSpec((TOP_K, tm), lambda i: (0, i))
    return pl.pallas_call(
        _outproj_kernel,
        grid=(s // tm,),
        in_specs=[pl.BlockSpec((tm, ATTN_W), row), pl.BlockSpec((tm, RG_W), row),
                  pl.BlockSpec((tm, d), row),
                  pl.BlockSpec((ATTN_W + RG_W, d), lambda i: (0, 0), pipeline_mode=pl.Buffered(1)),
                  vec, vec, vec, vec, vec,
                  _const_spec((d, LANES)), _const_spec((d, LANES)),
                  _const_spec((N_EXPERTS, 1)), _const_spec((tm, tm))],
        out_specs=[pl.BlockSpec((tm, d), row), pl.BlockSpec((tm, d), row),
                   slot, slot, slot, _const_spec((N_EXPERTS, LANES))],
        out_shape=[jax.ShapeDtypeStruct((s, d), F32), jax.ShapeDtypeStruct((s, d), F32),
                   jax.ShapeDtypeStruct((TOP_K, s), I32),
                   jax.ShapeDtypeStruct((TOP_K, s), F32),
                   jax.ShapeDtypeStruct((TOP_K, s), I32),
                   jax.ShapeDtypeStruct((N_EXPERTS, LANES), F32)],
        scratch_shapes=[pltpu.VMEM((N_EXPERTS, 1), F32)],
        compiler_params=_cparams("arbitrary"),
        name="out_projection",
    )(attn, rg, x, w_out, g1, ln_g, ln_b, sc2, sh2, w_hi, w_lo, e_bias_col, tri)


def _first_index_of_max(x, iota_f, axis):
    mx = jnp.max(x, axis=axis, keepdims=True)
    idx = jnp.min(jnp.where(x == mx, iota_f, float(N_EXPERTS)), axis=axis, keepdims=True)
    return mx, idx


def _route(v, whi_ref, wlo_ref, eb_ref, tri_ref, idx_ref, wts_ref, rank_ref, cnt_ref, base_scr):
    tm = v.shape[0]

    @pl.when(pl.program_id(0) == 0)
    def _():
        base_scr[...] = jnp.zeros(base_scr.shape, F32)

    v_hi = v.astype(BF16)
    v_lo = (v - v_hi.astype(F32)).astype(BF16)
    logits = jnp.dot(v_hi, whi_ref[...], preferred_element_type=F32)
    logits = logits + jnp.dot(v_lo, whi_ref[...], preferred_element_type=F32)
    logits = logits + jnp.dot(v_hi, wlo_ref[...], preferred_element_type=F32)
    lt = logits.T[0:N_EXPERTS, :]
    scores = jax.nn.sigmoid(lt)
    biased = scores + eb_ref[...]
    neg_inf = float("-inf")

    ig = lax.broadcasted_iota(I32, (GROUP_SIZE, tm), 0).astype(F32)
    groups = [biased[g * GROUP_SIZE:(g + 1) * GROUP_SIZE, :] for g in range(N_GROUPS)]
    gscore = []
    for bg in groups:
        top1, i1 = _first_index_of_max(bg, ig, 0)
        top2 = jnp.max(jnp.where(ig == i1, neg_inf, bg), axis=0, keepdims=True)
        gscore.append(top1 + top2)

    masked = []
    for g in range(N_GROUPS):
        ahead = jnp.zeros((1, tm), F32)
        for o in range(N_GROUPS):
            if o == g:
                continue
            before = (gscore[o] >= gscore[g]) if o < g else (gscore[o] > gscore[g])
            ahead = ahead + jnp.where(before, 1.0, 0.0)
        keep = jnp.broadcast_to(ahead < TOPK_GROUPS, (GROUP_SIZE, tm))
        masked.append(jnp.where(keep, groups[g], neg_inf))
    masked = jnp.concatenate(masked, axis=0)

    ie = lax.broadcasted_iota(I32, masked.shape, 0).astype(F32)
    seen = base_scr[...]
    ws = []
    for k in range(TOP_K):
        _, ei = _first_index_of_max(masked, ie, 0)
        hit = ie == ei
        idx_ref[k:k + 1, :] = ei.astype(I32)
        ws.append(jnp.sum(jnp.where(hit, scores, 0.0), axis=0, keepdims=True))
        masked = jnp.where(hit, neg_inf, masked)
        onehot = jnp.where(hit, 1.0, 0.0)
        before = jnp.dot(onehot.astype(BF16), tri_ref[...], preferred_element_type=F32)
        rank = jnp.sum(jnp.where(hit, before + seen, 0.0), axis=0, keepdims=True)
        rank_ref[k:k + 1, :] = rank.astype(I32)
        seen = seen + jnp.sum(onehot, axis=1, keepdims=True)
    base_scr[...] = seen
    cnt_ref[...] = jnp.broadcast_to(seen, cnt_ref.shape)
    total = ws[0]
    for k in range(1, TOP_K):
        total = total + ws[k]
    for k in range(TOP_K):
        wts_ref[k:k + 1, :] = ws[k] / total * ROUTED_SCALE


def _pack_pairs(x):
    h = x.shape[1] // 2
    lo = lax.bitcast_convert_type(x[:, :h].astype(BF16).astype(F32), U32)
    hi = lax.bitcast_convert_type(x[:, h:].astype(BF16).astype(F32), U32)
    return (lo >> 16) | (hi & jnp.uint32(0xFFFF0000))


def _unpack_pairs(w):
    lo = lax.bitcast_convert_type(w << 16, F32)
    hi = lax.bitcast_convert_type(w & jnp.uint32(0xFFFF0000), F32)
    return jnp.concatenate([lo, hi], axis=1)


def _dispatch_kernel(last_ref, has_ref, v_ref, idx_ref, rank_ref, pstart_ref,
                     dest_ref, xs_hbm, dest_v, dest_s, zeros_v, pk_v, sem_z, sem_i, sem_r, *, bm):
    tm = v_ref.shape[0]

    def zero_fill(e):
        row0 = pl.multiple_of(last_ref[e], bm)
        return pltpu.make_async_copy(zeros_v, xs_hbm.at[pl.ds(row0, bm), :], sem_z)

    @pl.when(pl.program_id(0) == 0)
    def _():
        zeros_v[...] = jnp.zeros(zeros_v.shape, zeros_v.dtype)

        def start(e, carry):
            @pl.when(has_ref[e] > 0)
            def _():
                zero_fill(e).start()
            return carry

        def wait(e, carry):
            @pl.when(has_ref[e] > 0)
            def _():
                zero_fill(e).wait()
            return carry

        lax.fori_loop(0, N_EXPERTS, start, 0)
        lax.fori_loop(0, N_EXPERTS, wait, 0)

    ie = lax.broadcasted_iota(I32, (N_EXPERTS, tm), 0)
    pstart = pstart_ref[...]
    for k in range(TOP_K):
        hit = ie == idx_ref[k:k + 1, :]
        seg = jnp.sum(jnp.where(hit, pstart, 0.0), axis=0, keepdims=True)
        dest_v[k:k + 1, :] = seg.astype(I32) + rank_ref[k:k + 1, :]
    dest_ref[...] = dest_v[...]
    cp = pltpu.make_async_copy(dest_v, dest_s, sem_i)
    cp.start()
    pk_v[...] = _pack_pairs(v_ref[...])
    cp.wait()

    def start_rows(t, carry):
        for k in range(TOP_K):
            pltpu.make_async_copy(pk_v.at[pl.ds(t, 1), :],
                                  xs_hbm.at[pl.ds(dest_s[k, t], 1), :], sem_r).start()
        return carry

    lax.fori_loop(0, tm, start_rows, 0, unroll=4)
    for k in range(TOP_K):
        pltpu.make_async_copy(pk_v, xs_hbm.at[pl.ds(0, tm), :], sem_r).wait()


def _dispatch(last_row, has_blk, v, idx, rank, pstart_col, n_rows, tm, bm):
    s, d = v.shape
    slot = pl.BlockSpec((TOP_K, tm), lambda i, *_: (0, i))
    grid_spec = pltpu.PrefetchScalarGridSpec(
        num_scalar_prefetch=2,
        grid=(s // tm,),
        in_specs=[pl.BlockSpec((tm, d), lambda i, *_: (i, 0)), slot, slot,
                  pl.BlockSpec((N_EXPERTS, 1), lambda i, *_: (0, 0))],
        out_specs=[slot, pl.BlockSpec(memory_space=pl.ANY)],
        scratch_shapes=[pltpu.VMEM((TOP_K, tm), I32), pltpu.SMEM((TOP_K, tm), I32),
                        pltpu.VMEM((bm, d // 2), U32), pltpu.VMEM((tm, d // 2), U32),
                        pltpu.SemaphoreType.DMA, pltpu.SemaphoreType.DMA,
                        pltpu.SemaphoreType.DMA])
    return pl.pallas_call(
        functools.partial(_dispatch_kernel, bm=bm),
        grid_spec=grid_spec,
        out_shape=[jax.ShapeDtypeStruct((TOP_K, s), I32),
                   jax.ShapeDtypeStruct((n_rows, d // 2), U32)],
        compiler_params=_cparams("arbitrary"),
        name="dispatch",
    )(last_row, has_blk, v, idx, rank, pstart_col)


def _expert_kernel(blk_e_ref, n_used_ref, x_ref, w1_ref, w3_ref, w2_ref, y_ref, w1b, w3b, w2b):
    b = pl.program_id(0)

    @pl.when(b < n_used_ref[0])
    def _():
        prev_e = blk_e_ref[jnp.maximum(b - 1, 0)]

        @pl.when(jnp.logical_or(b == 0, blk_e_ref[b] != prev_e))
        def _():
            w1b[...] = w1_ref[...].astype(BF16)
            w3b[...] = w3_ref[...].astype(BF16)
            w2b[...] = w2_ref[...].astype(BF16)

        xb = _unpack_pairs(x_ref[...]).astype(BF16)
        h1 = jnp.dot(xb, w1b[...], preferred_element_type=F32)
        h3 = jnp.dot(xb, w3b[...], preferred_element_type=F32)
        act = (h1 * jax.nn.sigmoid(h1) * h3).astype(BF16)
        y_ref[...] = _pack_pairs(jnp.dot(act, w2b[...], preferred_element_type=F32))


def _experts(blk_e, n_used, xs, w_e1, w_e3, w_e2, bm):
    n_rows = xs.shape[0]
    d, ff = w_e1.shape[-2:]
    rows = lambda b, be, nu: (jnp.minimum(b, nu[0] - 1), 0)
    wspec1 = pl.BlockSpec((None, d, ff), lambda b, be, nu: (be[b], 0, 0))
    wspec2 = pl.BlockSpec((None, ff, d), lambda b, be, nu: (be[b], 0, 0))
    grid_spec = pltpu.PrefetchScalarGridSpec(
        num_scalar_prefetch=2,
        grid=(n_rows // bm,),
        in_specs=[pl.BlockSpec((bm, d // 2), rows), wspec1, wspec1, wspec2],
        out_specs=pl.BlockSpec((bm, d // 2), rows),
        scratch_shapes=[pltpu.VMEM((d, ff), BF16), pltpu.VMEM((d, ff), BF16),
                        pltpu.VMEM((ff, d), BF16)])
    return pl.pallas_call(
        _expert_kernel,
        grid_spec=grid_spec,
        out_shape=jax.ShapeDtypeStruct((n_rows, d // 2), U32),
        compiler_params=_cparams("arbitrary"),
        name="experts",
    )(blk_e, n_used, xs, w_e1, w_e3, w_e2)


def _block_tables(counts, n_blocks, bm):
    padded = (counts + bm - 1) // bm * bm
    pad_end = jnp.cumsum(padded)
    pad_start = pad_end - padded
    blk_first = jnp.arange(n_blocks, dtype=I32) * bm
    blk_e = jnp.minimum(jnp.sum(pad_end[None, :] <= blk_first[:, None], axis=1), N_EXPERTS - 1)
    n_used = pad_end[-1] // bm
    blk_e = jnp.where(jnp.arange(n_blocks) < n_used, blk_e, blk_e[jnp.maximum(n_used - 1, 0)])
    last_row = jnp.maximum(pad_end - bm, 0)
    return (pad_start, last_row.astype(I32), (padded > 0).astype(I32),
            blk_e.astype(I32), n_used.astype(I32).reshape(1))


def _combine_kernel(dest_ref, wts_ref, v_ref, h1_ref, g2_ref, lg_ref, lb_ref,
                    ws1_ref, ws3_ref, ws2_ref, ys_hbm, o_ref, dest_s, ybuf, sem_i, sem_r):
    tm = v_ref.shape[0]
    cp = pltpu.make_async_copy(dest_ref, dest_s, sem_i)
    cp.start()
    cp.wait()

    def start_rows(t, carry):
        for k in range(TOP_K):
            pltpu.make_async_copy(ys_hbm.at[pl.ds(dest_s[k, t], 1), :],
                                  ybuf.at[k, pl.ds(t, 1), :], sem_r).start()
        return carry

    lax.fori_loop(0, tm, start_rows, 0, unroll=4)

    vb = v_ref[...].astype(BF16)
    a1 = jnp.dot(vb, ws1_ref[...], preferred_element_type=F32)
    a3 = jnp.dot(vb, ws3_ref[...], preferred_element_type=F32)
    act = (a1 * jax.nn.sigmoid(a1) * a3).astype(BF16)
    shared = jnp.dot(act, ws2_ref[...], preferred_element_type=F32)
    wpad = jnp.concatenate([wts_ref[...], jnp.zeros((LANES - TOP_K, tm), F32)], axis=0)
    wt = wpad.T

    for k in range(TOP_K):
        pltpu.make_async_copy(ys_hbm.at[pl.ds(0, tm), :], ybuf.at[k], sem_r).wait()
    ff = _unpack_pairs(ybuf[0]) * wt[:, 0:1]
    for k in range(1, TOP_K):
        ff = ff + _unpack_pairs(ybuf[k]) * wt[:, k:k + 1]
    ff = ff + shared
    o_ref[...] = _layer_norm(DEEPNORM_ALPHA * h1_ref[...] + g2_ref[...] * ff,
                             lg_ref[...], lb_ref[...])


def _combine(dest, wts, v, h1, g2, ln_g, ln_b, w_s1, w_s3, w_s2, ys, tm):
    s, d = v.shape
    row = lambda i: (i, 0)
    vec = _const_spec((1, d))
    slot = pl.BlockSpec((TOP_K, tm), lambda i: (0, i))
    return pl.pallas_call(
        _combine_kernel,
        grid=(s // tm,),
        in_specs=[slot, slot, pl.BlockSpec((tm, d), row), pl.BlockSpec((tm, d), row),
                  vec, vec, vec,
                  _const_spec(w_s1.shape), _const_spec(w_s3.shape), _const_spec(w_s2.shape),
                  pl.BlockSpec(memory_space=pl.ANY)],
        out_specs=pl.BlockSpec((tm, d), row),
        out_shape=jax.ShapeDtypeStruct((s, d), F32),
        scratch_shapes=[pltpu.SMEM((TOP_K, tm), I32), pltpu.VMEM((TOP_K, tm, d // 2), U32),
                        pltpu.SemaphoreType.DMA, pltpu.SemaphoreType.DMA],
        compiler_params=_cparams("arbitrary"),
        name="combine",
    )(dest, wts, v, h1, g2, ln_g, ln_b, w_s1, w_s3, w_s2, ys)


def _rope_tables(s):
    half = HEAD_DIM // 4
    inv_freq = ROPE_THETA ** (-np.arange(half, dtype=np.float32) / half)
    t = np.arange(s)
    ang_r = (t // GRID_W).astype(np.float32)[:, None] * inv_freq[None, :]
    ang_c = (t % GRID_W).astype(np.float32)[:, None] * inv_freq[None, :]
    cos = np.concatenate([np.cos(ang_r)] * 2 + [np.cos(ang_c)] * 2, axis=-1)
    sin = np.concatenate([-np.sin(ang_r), np.sin(ang_r), -np.sin(ang_c), np.sin(ang_c)], axis=-1)
    return jnp.asarray(cos, F32), jnp.asarray(sin, F32)


def kernel(x, c, ctx, c_ctx, w_mod, b_mod, w_in, q_norm, k_norm, conv_w, conv_b, rg_wa, rg_ba,
           rg_wx, rg_bx, rg_lam, w_out, ln1_g, ln1_b, w_router, e_bias, w_e1, w_e3, w_e2,
           w_s1, w_s3, w_s2, ln2_g, ln2_b):
    assert x.shape[0] == 1 and w_mod.shape[0] == DEPTH
    _, s, d = x.shape
    n_ctx = ctx.shape[1]
    x2 = x[0]

    c2t = jnp.stack([c[0], c_ctx], axis=1)
    mod = _modulation(c2t, w_mod[0], b_mod[0].reshape(1, -1))
    sh1, sc1, g1, sh2, sc2, g2 = [mod[0:1, j * d:(j + 1) * d] for j in range(6)]
    csh1, csc1 = mod[1:2, 0:d], mod[1:2, d:2 * d]

    w_in_b = w_in[0].astype(BF16)
    qg = q_norm[0].reshape(1, HEAD_DIM)
    kg = k_norm[0].reshape(1, HEAD_DIM)
    cos, sin = _rope_tables(s)
    q_l, k_l, vt_l, xr_l, yr_l = _in_projection(x2, sc1, sh1, w_in_b, qg, kg, cos, sin,
                                                tm=min(512, s), kv_chunk=min(KV_CHUNK, s))
    _, k_c, vt_c, xr_c, _ = _in_projection(
        ctx[0], csc1, csh1, w_in_b, qg, kg,
        jnp.ones((n_ctx, HEAD_DIM), F32), jnp.zeros((n_ctx, HEAD_DIM), F32),
        tm=n_ctx, kv_chunk=n_ctx)

    attn = _attention(q_l, k_c, vt_c, k_l, vt_l, tq=min(256, s))

    wg = jnp.concatenate([rg_wa[0], rg_wx[0]], axis=-1).astype(BF16)
    cb = conv_b[0].reshape(1, RG_W)
    zero_state = jnp.zeros((1, RG_W), F32)
    rg_args = []
    for dd in range(2):
        rg_args.append((conv_w[0], cb, wg[dd],
                        jnp.stack([rg_ba[0, dd], rg_bx[0, dd]], axis=0),
                        rg_lam[0, dd].reshape(1, RG_W)))
    t_rg = min(512, s)
    hc_f = _rg_scan(xr_c, *rg_args[0], zero_state, reverse=False, t=n_ctx)
    hc_b = _rg_scan(xr_c, *rg_args[1], zero_state, reverse=True, t=n_ctx)
    h_f = _rg_scan(xr_l, *rg_args[0], hc_f[n_ctx - 1:n_ctx], reverse=False, t=t_rg)
    rg = _rg_scan(xr_l, *rg_args[1], hc_b[0:1], reverse=True, t=t_rg, hf=h_f, yr=yr_l)

    wr = jnp.pad(w_router[0], ((0, 0), (0, LANES - N_EXPERTS)))
    wr_hi = wr.astype(BF16)
    wr_lo = (wr - wr_hi.astype(F32)).astype(BF16)
    h1, v, idx, wts, rank, cnt = _out_projection(
        attn, rg, x2, w_out[0].astype(BF16), g1, ln1_g[0].reshape(1, d), ln1_b[0].reshape(1, d),
        sc2, sh2, wr_hi, wr_lo, e_bias[0].reshape(N_EXPERTS, 1), tm=min(512, s))

    bm = min(EXPERT_BLOCK, s)
    n_blocks = s * TOP_K // bm + N_EXPERTS
    pad_start, last_row, has_blk, blk_e, n_used = _block_tables(cnt[:, 0].astype(I32), n_blocks, bm)
    dest, xs = _dispatch(last_row, has_blk, v, idx, rank,
                         pad_start.astype(F32).reshape(N_EXPERTS, 1), n_blocks * bm,
                         tm=min(256, s), bm=bm)
    ys = _experts(blk_e, n_used, xs, w_e1[0], w_e3[0], w_e2[0], bm)

    out = _combine(dest, wts, v, h1, g2, ln2_g[0].reshape(1, d), ln2_b[0].reshape(1, d),
                   w_s1[0].astype(BF16), w_s3[0].astype(BF16), w_s2[0].astype(BF16), ys,
                   tm=min(256, s))
    return out[None]
```

```python
import functools

import jax
import jax.numpy as jnp
import numpy as np
from jax import lax
from jax.experimental import pallas as pl
from jax.experimental.pallas import tpu as pltpu

F32 = jnp.float32
BF16 = jnp.bfloat16
I32 = jnp.int32
U32 = jnp.uint32

GRID_W = 64
HEAD_DIM = 128
N_HEADS = 8
N_KV_HEADS = 2
GQA_GROUP = N_HEADS // N_KV_HEADS
ATTN_W = N_HEADS * HEAD_DIM
KV_W = N_KV_HEADS * HEAD_DIM
ROPE_THETA = 10000.0
RG_W = 1024
RG_HEADS = 8
RG_HD = RG_W // RG_HEADS
RG_C = 8.0
PROJ_W = ATTN_W + 2 * KV_W + 2 * RG_W
N_EXPERTS = 64
N_GROUPS = 8
GROUP_SIZE = N_EXPERTS // N_GROUPS
TOPK_GROUPS = 4
TOP_K = 8
ROUTED_SCALE = 2.5
NORM_EPS = 1e-6
DEPTH = 1
DEEPNORM_ALPHA = (2.0 * DEPTH) ** 0.25
LOG2E = 1.4426950408889634

LANES = 128
SUBLANES = 8
BF16_SUBLANES = 16
VMEM_LIMIT = 56 * 1024 * 1024

NEG_BIG = -1e30
KV_CHUNK = 512
ATTN_UNROLL = 16
VT_ROWS = HEAD_DIM + BF16_SUBLANES
EXPERT_BLOCK = 512


def _cparams(*sem):
    return pltpu.CompilerParams(dimension_semantics=sem, vmem_limit_bytes=VMEM_LIMIT)


def _const_spec(shape):
    nd = len(shape)
    return pl.BlockSpec(shape, lambda *_: (0,) * nd)


def _mod_kernel(ct_ref, w_ref, b_ref, o_ref, sb_ref, *, tn):
    d = w_ref.shape[0]

    @pl.when(pl.program_id(0) == 0)
    def _():
        ct = ct_ref[...]
        s = ct * jax.nn.sigmoid(ct)
        sb_ref[0] = jnp.broadcast_to(s[:, 0:1], (d, LANES))
        sb_ref[1] = jnp.broadcast_to(s[:, 1:2], (d, LANES))

    for c in range(tn // LANES):
        sl = slice(c * LANES, (c + 1) * LANES)
        wc = w_ref[:, sl]
        bc = b_ref[:, sl]
        o0 = jnp.sum(wc * sb_ref[0], axis=0, keepdims=True) + bc
        o1 = jnp.sum(wc * sb_ref[1], axis=0, keepdims=True) + bc
        o_ref[:, sl] = jnp.concatenate(
            [o0, o1, jnp.zeros((SUBLANES - 2, LANES), F32)], axis=0)


def _modulation(c2t, w_mod, b_mod):
    d, n = w_mod.shape
    tn = 1024
    return pl.pallas_call(
        functools.partial(_mod_kernel, tn=tn),
        grid=(n // tn,),
        in_specs=[_const_spec((d, 2)),
                  pl.BlockSpec((d, tn), lambda j: (0, j)),
                  pl.BlockSpec((1, tn), lambda j: (0, j))],
        out_specs=pl.BlockSpec((SUBLANES, tn), lambda j: (0, j)),
        out_shape=jax.ShapeDtypeStruct((SUBLANES, n), F32),
        scratch_shapes=[pltpu.VMEM((2, d, LANES), F32)],
        compiler_params=_cparams("arbitrary"),
        name="modulation",
    )(c2t, w_mod, b_mod)


def _swap_half(y):
    lane = lax.broadcasted_iota(I32, y.shape, 1)
    return jnp.where((lane % 64) < 32,
                     pltpu.roll(y, LANES - 32, 1), pltpu.roll(y, 32, 1))


def _norm_rope(ph, g, cos, sin_signed, scale):
    ms = jnp.mean(ph * ph, axis=-1, keepdims=True)
    y = (ph * lax.rsqrt(ms + NORM_EPS)) * g
    y = y * cos + _swap_half(y) * sin_signed
    if scale != 1.0:
        y = y * scale
    return y


def _inproj_kernel(x_ref, sc_ref, sh_ref, w_ref, qg_ref, kg_ref, cos_ref, sin_ref,
                   q_ref, k_ref, vt_ref, xr_ref, yr_ref, *, q_scale):
    tm = x_ref.shape[0]
    kc = vt_ref.shape[-1]
    u = (x_ref[...] * (1.0 + sc_ref[...]) + sh_ref[...]).astype(BF16)
    cos = cos_ref[...]
    sin = sin_ref[...]
    o = 0
    pq = jnp.dot(u, w_ref[:, o:o + ATTN_W], preferred_element_type=F32)
    for h in range(N_HEADS):
        sl = slice(h * HEAD_DIM, (h + 1) * HEAD_DIM)
        q_ref[:, sl] = _norm_rope(pq[:, sl], qg_ref[...], cos, sin, q_scale).astype(BF16)
    o += ATTN_W
    pk = jnp.dot(u, w_ref[:, o:o + KV_W], preferred_element_type=F32)
    for h in range(N_KV_HEADS):
        sl = slice(h * HEAD_DIM, (h + 1) * HEAD_DIM)
        k_ref[:, sl] = _norm_rope(pk[:, sl], kg_ref[...], cos, sin, 1.0).astype(BF16)
    o += KV_W
    pv = jnp.dot(u, w_ref[:, o:o + KV_W], preferred_element_type=F32)
    ones_rows = jnp.where(
        lax.broadcasted_iota(I32, (VT_ROWS - HEAD_DIM, kc), 0) == 0, 1.0, 0.0).astype(BF16)
    for h in range(N_KV_HEADS):
        for cc in range(tm // kc):
            blk = pv[cc * kc:(cc + 1) * kc, h * HEAD_DIM:(h + 1) * HEAD_DIM]
            vt_ref[h, cc, 0:HEAD_DIM, :] = blk.T.astype(BF16)
            vt_ref[h, cc, HEAD_DIM:VT_ROWS, :] = ones_rows
    o += KV_W
    xr_ref[...] = jnp.dot(u, w_ref[:, o:o + RG_W], preferred_element_type=F32)
    o += RG_W
    yr_ref[...] = jnp.dot(u, w_ref[:, o:o + RG_W], preferred_element_type=F32)


def _in_projection(x, sc, sh, w_in, qg, kg, cos, sin, tm, kv_chunk):
    s, d = x.shape
    row = lambda i: (i, 0)
    if kv_chunk >= tm:
        per = kv_chunk // tm
        vt_spec = pl.BlockSpec((N_KV_HEADS, 1, VT_ROWS, tm), lambda i: (0, i // per, 0, i % per))
    else:
        vt_spec = pl.BlockSpec((N_KV_HEADS, tm // kv_chunk, VT_ROWS, kv_chunk),
                               lambda i: (0, i, 0, 0))
    return pl.pallas_call(
        functools.partial(_inproj_kernel, q_scale=HEAD_DIM ** -0.5 * LOG2E),
        grid=(s // tm,),
        in_specs=[pl.BlockSpec((tm, d), row),
                  _const_spec((1, d)), _const_spec((1, d)),
                  pl.BlockSpec((d, PROJ_W), lambda i: (0, 0), pipeline_mode=pl.Buffered(1)),
                  _const_spec((1, HEAD_DIM)), _const_spec((1, HEAD_DIM)),
                  pl.BlockSpec((tm, HEAD_DIM), row), pl.BlockSpec((tm, HEAD_DIM), row)],
        out_specs=[pl.BlockSpec((tm, ATTN_W), row), pl.BlockSpec((tm, KV_W), row),
                   vt_spec,
                   pl.BlockSpec((tm, RG_W), row), pl.BlockSpec((tm, RG_W), row)],
        out_shape=[jax.ShapeDtypeStruct((s, ATTN_W), BF16),
                   jax.ShapeDtypeStruct((s, KV_W), BF16),
                   jax.ShapeDtypeStruct((N_KV_HEADS, s // kv_chunk, VT_ROWS, kv_chunk), BF16),
                   jax.ShapeDtypeStruct((s, RG_W), F32),
                   jax.ShapeDtypeStruct((s, RG_W), F32)],
        compiler_params=_cparams("arbitrary"),
        name="in_projection",
    )(x, sc, sh, w_in, qg, kg, cos, sin)


def _attn_kernel(q_ref, kc_ref, vtc_ref, kl_ref, vtl_ref, o_ref, sa_ref, sb_ref, pa_ref, pb_ref, sc_ref,
                 *, tq, tk, n_chunks, unroll):
    q = q_ref[...]
    qs = jnp.concatenate(
        [q[:, g * HEAD_DIM:(g + 1) * HEAD_DIM] for g in range(GQA_GROUP)], axis=0)
    cols = GQA_GROUP * tq

    def scores(k):
        return lax.dot_general(k, qs, (((1,), (1,)), ((), ())), preferred_element_type=F32)

    def softmax(s_ref, m):
        m_new = jnp.maximum(m, jnp.max(s_ref[...], axis=0, keepdims=True))
        return m_new, jnp.exp2(m - m_new), jnp.exp2(s_ref[...] - m_new).astype(BF16)

    def weighted_values(vt, p, alpha, acc):
        return alpha * acc + jnp.dot(vt, p, preferred_element_type=F32)

    m0 = jnp.full((1, cols), NEG_BIG, F32)
    a0 = jnp.zeros((VT_ROWS, cols), F32)
    sc_ref[...] = scores(kc_ref[...])
    m, alpha, p = softmax(sc_ref, m0)
    acc = weighted_values(vtc_ref[0], p, alpha, a0)

    def latent_scores(j):
        j = jnp.minimum(j, n_chunks - 1)
        return scores(kl_ref[pl.ds(pl.multiple_of(j * tk, tk), tk), :])

    s_refs = (sa_ref, sb_ref)
    p_refs = (pa_ref, pb_ref)
    sa_ref[...] = latent_scores(0)
    sb_ref[...] = latent_scores(1)
    m, alpha, pa_ref[...] = softmax(sa_ref, m)

    def body(i, carry):
        m, acc, alpha = carry
        for u in range(unroll):
            c = unroll * i + u
            cur, nxt = u % 2, (u + 1) % 2
            s_refs[cur][...] = latent_scores(c + 2)
            m, alpha_next, p_refs[nxt][...] = softmax(s_refs[nxt], m)
            acc = weighted_values(vtl_ref[c], p_refs[cur][...], alpha, acc)
            alpha = alpha_next
        return m, acc, alpha

    _, acc, _ = lax.fori_loop(0, n_chunks // unroll, body, (m, acc, alpha))
    out_t = acc[0:HEAD_DIM, :] / acc[HEAD_DIM:HEAD_DIM + 1, :]
    for g in range(GQA_GROUP):
        o_ref[:, g * HEAD_DIM:(g + 1) * HEAD_DIM] = out_t[:, g * tq:(g + 1) * tq].T.astype(BF16)


def _attention(q, k_c, vt_c, k_l, vt_l, tq):
    s = q.shape[0]
    n_ctx = k_c.shape[0]
    n_chunks, _, tk = vt_l.shape[1:]
    unroll = min(ATTN_UNROLL, n_chunks)
    assert n_chunks % unroll == 0 and unroll % 2 == 0
    gw = GQA_GROUP * HEAD_DIM
    return pl.pallas_call(
        functools.partial(_attn_kernel, tq=tq, tk=tk, n_chunks=n_chunks, unroll=unroll),
        grid=(N_KV_HEADS, s // tq),
        in_specs=[pl.BlockSpec((tq, gw), lambda h, i: (i, h)),
                  pl.BlockSpec((n_ctx, HEAD_DIM), lambda h, i: (0, h)),
                  pl.BlockSpec((None, 1, VT_ROWS, n_ctx), lambda h, i: (h, 0, 0, 0)),
                  pl.BlockSpec((s, HEAD_DIM), lambda h, i: (0, h)),
                  pl.BlockSpec((None, n_chunks, VT_ROWS, tk), lambda h, i: (h, 0, 0, 0))],
        out_specs=pl.BlockSpec((tq, gw), lambda h, i: (i, h)),
        out_shape=jax.ShapeDtypeStruct((s, ATTN_W), BF16),
        scratch_shapes=[pltpu.VMEM((tk, GQA_GROUP * tq), F32),
                        pltpu.VMEM((tk, GQA_GROUP * tq), F32),
                        pltpu.VMEM((tk, GQA_GROUP * tq), BF16),
                        pltpu.VMEM((tk, GQA_GROUP * tq), BF16),
                        pltpu.VMEM((n_ctx, GQA_GROUP * tq), F32)],
        compiler_params=_cparams("arbitrary", "arbitrary"),
        name="attention",
    )(q, k_c, vt_c, k_l, vt_l)


def _log_sigmoid(x):
    return jnp.minimum(x, 0.0) - jnp.log1p(jnp.exp(-jnp.abs(x)))


def _rg_kernel(x_ref, xp_ref, xn_ref, cw_ref, cb_ref, wg_ref, bg_ref, lam_ref, h0_ref,
               *rest, reverse, final, t, n_chunks):
    if final:
        hf_ref, yr_ref, o_ref, a_scr, b_scr, hc_scr, h_scr = rest
    else:
        o_ref, a_scr, b_scr, hc_scr = rest
        h_scr = o_ref
    i = pl.program_id(0)
    c = (n_chunks - 1 - i) if reverse else i
    w = x_ref.shape[1]

    @pl.when(i == 0)
    def _():
        hc_scr[...] = jnp.broadcast_to(h0_ref[...], (SUBLANES, w))

    x = x_ref[...]
    row = lax.broadcasted_iota(I32, (t, w), 0)
    pm = jnp.where(c == 0, 0.0, 1.0).astype(F32)
    nm = jnp.where(c == n_chunks - 1, 0.0, 1.0).astype(F32)
    p6 = xp_ref[SUBLANES - 2:SUBLANES - 1, :] * pm
    p7 = xp_ref[SUBLANES - 1:SUBLANES, :] * pm
    n0 = xn_ref[0:1, :] * nm
    x_m1 = jnp.where(row == 0, p7, pltpu.roll(x, 1, 0))
    x_m2 = jnp.where(row == 0, p6, jnp.where(row == 1, p7, pltpu.roll(x, 2, 0)))
    x_p1 = jnp.where(row == t - 1, n0, pltpu.roll(x, t - 1, 0))
    xc = cb_ref[...] + cw_ref[0:1, :] * x_m2
    xc = xc + cw_ref[1:2, :] * x_m1
    xc = xc + cw_ref[2:3, :] * x
    xc = xc + cw_ref[3:4, :] * x_p1

    xcb = xc.astype(BF16)
    clam = RG_C * _log_sigmoid(lam_ref[...])
    for h in range(RG_HEADS):
        sl = slice(h * RG_HD, (h + 1) * RG_HD)
        g = jnp.dot(xcb[:, sl], wg_ref[h], preferred_element_type=F32)
        r = jax.nn.sigmoid(g[:, :RG_HD] + bg_ref[0:1, sl])
        gi = jax.nn.sigmoid(g[:, RG_HD:] + bg_ref[1:2, sl])
        log_a = r * clam[:, sl]
        a = jnp.exp(log_a)
        a_scr[:, sl] = a
        b_scr[:, sl] = jnp.sqrt(-jnp.tanh(log_a) * (a * a + 1.0)) * (gi * xc[:, sl])

    srow = lax.broadcasted_iota(I32, (SUBLANES, w), 0)
    n_tiles = t // SUBLANES

    def tile_body(j, hprev):
        tile = (n_tiles - 1 - j) if reverse else j
        start = pl.multiple_of(tile * SUBLANES, SUBLANES)
        a = a_scr[pl.ds(start, SUBLANES), :]
        b = b_scr[pl.ds(start, SUBLANES), :]
        for k in (1, 2, 4):
            if reverse:
                keep = srow < SUBLANES - k
                shift = SUBLANES - k
            else:
                keep = srow >= k
                shift = k
            a_sh = jnp.where(keep, pltpu.roll(a, shift, 0), 1.0)
            b_sh = jnp.where(keep, pltpu.roll(b, shift, 0), 0.0)
            b = a * b_sh + b
            a = a * a_sh
        hh = a * hprev + b
        h_scr[pl.ds(start, SUBLANES), :] = hh
        last = hh[0:1, :] if reverse else hh[SUBLANES - 1:SUBLANES, :]
        return jnp.broadcast_to(last, (SUBLANES, w))

    hc_scr[...] = lax.fori_loop(0, n_tiles, tile_body, hc_scr[...])

    if final:
        gate = jax.nn.gelu(yr_ref[...], approximate=True)
        o_ref[...] = ((hf_ref[...] + h_scr[...]) * gate).astype(o_ref.dtype)


def _rg_scan(xr, conv_w, conv_b, wg, bg, lam, h0, *, reverse, t, hf=None, yr=None):
    s, w = xr.shape
    n_chunks = s // t
    final = hf is not None
    tb = t // SUBLANES
    last_blk = s // SUBLANES - 1
    if reverse:
        cidx = lambda i: n_chunks - 1 - i
    else:
        cidx = lambda i: i
    chunk_spec = pl.BlockSpec((t, w), lambda i: (cidx(i), 0))
    in_specs = [chunk_spec,
                pl.BlockSpec((SUBLANES, w), lambda i: (jnp.maximum(cidx(i) * tb - 1, 0), 0)),
                pl.BlockSpec((SUBLANES, w), lambda i: (jnp.minimum((cidx(i) + 1) * tb, last_blk), 0)),
                _const_spec((4, w)), _const_spec((1, w)),
                _const_spec((RG_HEADS, RG_HD, 2 * RG_HD)), _const_spec((2, w)),
                _const_spec((1, w)), _const_spec((1, w))]
    args = [xr, xr, xr, conv_w, conv_b, wg, bg, lam, h0]
    scratch = [pltpu.VMEM((t, w), F32), pltpu.VMEM((t, w), F32), pltpu.VMEM((SUBLANES, w), F32)]
    if final:
        in_specs += [chunk_spec, chunk_spec]
        args += [hf, yr]
        scratch.append(pltpu.VMEM((t, w), F32))
        out_dtype = BF16
    else:
        out_dtype = F32
    return pl.pallas_call(
        functools.partial(_rg_kernel, reverse=reverse, final=final, t=t, n_chunks=n_chunks),
        grid=(n_chunks,),
        in_specs=in_specs,
        out_specs=chunk_spec,
        out_shape=jax.ShapeDtypeStruct((s, w), out_dtype),
        scratch_shapes=scratch,
        compiler_params=_cparams("arbitrary"),
        name="rglru_bwd" if reverse else "rglru_fwd",
    )(*args)


def _layer_norm(y, g, b):
    mu = jnp.mean(y, axis=-1, keepdims=True)
    yc = y - mu
    var = jnp.mean(yc * yc, axis=-1, keepdims=True)
    return yc * lax.rsqrt(var + NORM_EPS) * g + b


def _outproj_kernel(attn_ref, rg_ref, x_ref, w_ref, g1_ref, lg_ref, lb_ref, sc2_ref, sh2_ref,
                    whi_ref, wlo_ref, eb_ref, tri_ref,
                    h1_ref, v_ref, idx_ref, wts_ref, rank_ref, cnt_ref, base_scr):
    mix = jnp.dot(attn_ref[...], w_ref[0:ATTN_W, :], preferred_element_type=F32)
    mix = mix + jnp.dot(rg_ref[...], w_ref[ATTN_W:, :], preferred_element_type=F32)
    h1 = _layer_norm(DEEPNORM_ALPHA * x_ref[...] + g1_ref[...] * mix, lg_ref[...], lb_ref[...])
    h1_ref[...] = h1
    v = h1 * (1.0 + sc2_ref[...]) + sh2_ref[...]
    v_ref[...] = v
    _route(v, whi_ref, wlo_ref, eb_ref, tri_ref, idx_ref, wts_ref, rank_ref, cnt_ref, base_scr)


def _out_projection(attn, rg, x, w_out, g1, ln_g, ln_b, sc2, sh2, w_hi, w_lo, e_bias_col, tm):
    s, d = x.shape
    row = lambda i: (i, 0)
    vec = _const_spec((1, d))
    tri = jnp.triu(jnp.ones((tm, tm), BF16), k=1)
    slot = pl.BlockSpec((TOP_K, tm), lambda i: (0, i))
    return pl.pallas_call(
        _outproj_kernel,
        grid=(s // tm,),
        in_specs=[pl.BlockSpec((tm, ATTN_W), row), pl.BlockSpec((tm, RG_W), row),
                  pl.BlockSpec((tm, d), row),
                  pl.BlockSpec((ATTN_W + RG_W, d), lambda i: (0, 0), pipeline_mode=pl.Buffered(1)),
                  vec, vec, vec, vec, vec,
                  _const_spec((d, LANES)), _const_spec((d, LANES)),
                  _const_spec((N_EXPERTS, 1)), _const_spec((tm, tm))],
        out_specs=[pl.BlockSpec((tm, d), row), pl.BlockSpec((tm, d), row),
                   slot, slot, slot, _const_spec((N_EXPERTS, LANES))],
        out_shape=[jax.ShapeDtypeStruct((s, d), F32), jax.ShapeDtypeStruct((s, d), F32),
                   jax.ShapeDtypeStruct((TOP_K, s), I32),
                   jax.ShapeDtypeStruct((TOP_K, s), F32),
                   jax.ShapeDtypeStruct((TOP_K, s), I32),
                   jax.ShapeDtypeStruct((N_EXPERTS, LANES), F32)],
        scratch_shapes=[pltpu.VMEM((N_EXPERTS, 1), F32)],
        compiler_params=_cparams("arbitrary"),
        name="out_projection",
    )(attn, rg, x, w_out, g1, ln_g, ln_b, sc2, sh2, w_hi, w_lo, e_bias_col, tri)


def _first_index_of_max(x, iota_f, axis):
    mx = jnp.max(x, axis=axis, keepdims=True)
    idx = jnp.min(jnp.where(x == mx, iota_f, float(N_EXPERTS)), axis=axis, keepdims=True)
    return mx, idx


def _route(v, whi_ref, wlo_ref, eb_ref, tri_ref, idx_ref, wts_ref, rank_ref, cnt_ref, base_scr):
    tm = v.shape[0]

    @pl.when(pl.program_id(0) == 0)
    def _():
        base_scr[...] = jnp.zeros(base_scr.shape, F32)

    v_hi = v.astype(BF16)
    v_lo = (v - v_hi.astype(F32)).astype(BF16)
    logits = jnp.dot(v_hi, whi_ref[...], preferred_element_type=F32)
    logits = logits + jnp.dot(v_lo, whi_ref[...], preferred_element_type=F32)
    logits = logits + jnp.dot(v_hi, wlo_ref[...], preferred_element_type=F32)
    lt = logits.T[0:N_EXPERTS, :]
    scores = jax.nn.sigmoid(lt)
    biased = scores + eb_ref[...]
    neg_inf = float("-inf")

    ig = lax.broadcasted_iota(I32, (GROUP_SIZE, tm), 0).astype(F32)
    groups = [biased[g * GROUP_SIZE:(g + 1) * GROUP_SIZE, :] for g in range(N_GROUPS)]
    gscore = []
    for bg in groups:
        top1, i1 = _first_index_of_max(bg, ig, 0)
        top2 = jnp.max(jnp.where(ig == i1, neg_inf, bg), axis=0, keepdims=True)
        gscore.append(top1 + top2)

    masked = []
    for g in range(N_GROUPS):
        ahead = jnp.zeros((1, tm), F32)
        for o in range(N_GROUPS):
            if o == g:
                continue
            before = (gscore[o] >= gscore[g]) if o < g else (gscore[o] > gscore[g])
            ahead = ahead + jnp.where(before, 1.0, 0.0)
        keep = jnp.broadcast_to(ahead < TOPK_GROUPS, (GROUP_SIZE, tm))
        masked.append(jnp.where(keep, groups[g], neg_inf))
    masked = jnp.concatenate(masked, axis=0)

    ie = lax.broadcasted_iota(I32, masked.shape, 0).astype(F32)
    seen = base_scr[...]
    ws = []
    for k in range(TOP_K):
        _, ei = _first_index_of_max(masked, ie, 0)
        hit = ie == ei
        idx_ref[k:k + 1, :] = ei.astype(I32)
        ws.append(jnp.sum(jnp.where(hit, scores, 0.0), axis=0, keepdims=True))
        masked = jnp.where(hit, neg_inf, masked)
        onehot = jnp.where(hit, 1.0, 0.0)
        before = jnp.dot(onehot.astype(BF16), tri_ref[...], preferred_element_type=F32)
        rank = jnp.sum(jnp.where(hit, before + seen, 0.0), axis=0, keepdims=True)
        rank_ref[k:k + 1, :] = rank.astype(I32)
        seen = seen + jnp.sum(onehot, axis=1, keepdims=True)
    base_scr[...] = seen
    cnt_ref[...] = jnp.broadcast_to(seen, cnt_ref.shape)
    total = ws[0]
    for k in range(1, TOP_K):
        total = total + ws[k]
    for k in range(TOP_K):
        wts_ref[k:k + 1, :] = ws[k] / total * ROUTED_SCALE


def _pack_pairs(x):
    h = x.shape[1] // 2
    lo = lax.bitcast_convert_type(x[:, :h].astype(BF16).astype(F32), U32)
    hi = lax.bitcast_convert_type(x[:, h:].astype(BF16).astype(F32), U32)
    return (lo >> 16) | (hi & jnp.uint32(0xFFFF0000))


def _unpack_pairs(w):
    lo = lax.bitcast_convert_type(w << 16, F32)
    hi = lax.bitcast_convert_type(w & jnp.uint32(0xFFFF0000), F32)
    return jnp.concatenate([lo, hi], axis=1)


def _dispatch_kernel(last_ref, has_ref, v_ref, idx_ref, rank_ref, pstart_ref,
                     dest_ref, xs_hbm, dest_v, dest_s, zeros_v, pk_v, sem_z, sem_i, sem_r, *, bm):
    tm = v_ref.shape[0]

    def zero_fill(e):
        row0 = pl.multiple_of(last_ref[e], bm)
        return pltpu.make_async_copy(zeros_v, xs_hbm.at[pl.ds(row0, bm), :], sem_z)

    @pl.when(pl.program_id(0) == 0)
    def _():
        zeros_v[...] = jnp.zeros(zeros_v.shape, zeros_v.dtype)

        def start(e, carry):
            @pl.when(has_ref[e] > 0)
            def _():
                zero_fill(e).start()
            return carry

        def wait(e, carry):
            @pl.when(has_ref[e] > 0)
            def _():
                zero_fill(e).wait()
            return carry

        lax.fori_loop(0, N_EXPERTS, start, 0)
        lax.fori_loop(0, N_EXPERTS, wait, 0)

    ie = lax.broadcasted_iota(I32, (N_EXPERTS, tm), 0)
    pstart = pstart_ref[...]
    for k in range(TOP_K):
        hit = ie == idx_ref[k:k + 1, :]
        seg = jnp.sum(jnp.where(hit, pstart, 0.0), axis=0, keepdims=True)
        dest_v[k:k + 1, :] = seg.astype(I32) + rank_ref[k:k + 1, :]
    dest_ref[...] = dest_v[...]
    cp = pltpu.make_async_copy(dest_v, dest_s, sem_i)
    cp.start()
    pk_v[...] = _pack_pairs(v_ref[...])
    cp.wait()

    for t in range(tm):
        for k in range(TOP_K):
            pltpu.make_async_copy(pk_v.at[pl.ds(t, 1), :],
                                  xs_hbm.at[pl.ds(dest_s[k, t], 1), :], sem_r).start(priority=k % 2)
    for k in range(TOP_K):
        pltpu.make_async_copy(pk_v, xs_hbm.at[pl.ds(0, tm), :], sem_r).wait()


def _dispatch(last_row, has_blk, v, idx, rank, pstart_col, n_rows, tm, bm):
    s, d = v.shape
    slot = pl.BlockSpec((TOP_K, tm), lambda i, *_: (0, i))
    grid_spec = pltpu.PrefetchScalarGridSpec(
        num_scalar_prefetch=2,
        grid=(s // tm,),
        in_specs=[pl.BlockSpec((tm, d), lambda i, *_: (i, 0)), slot, slot,
                  pl.BlockSpec((N_EXPERTS, 1), lambda i, *_: (0, 0))],
        out_specs=[slot, pl.BlockSpec(memory_space=pl.ANY)],
        scratch_shapes=[pltpu.VMEM((TOP_K, tm), I32), pltpu.SMEM((TOP_K, tm), I32),
                        pltpu.VMEM((bm, d // 2), U32), pltpu.VMEM((tm, d // 2), U32),
                        pltpu.SemaphoreType.DMA, pltpu.SemaphoreType.DMA,
                        pltpu.SemaphoreType.DMA])
    return pl.pallas_call(
        functools.partial(_dispatch_kernel, bm=bm),
        grid_spec=grid_spec,
        out_shape=[jax.ShapeDtypeStruct((TOP_K, s), I32),
                   jax.ShapeDtypeStruct((n_rows, d // 2), U32)],
        compiler_params=_cparams("arbitrary"),
        name="dispatch",
    )(last_row, has_blk, v, idx, rank, pstart_col)


def _expert_kernel(blk_e_ref, n_used_ref, x_ref, w1_ref, w3_ref, w2_ref, y_ref, w1b, w3b, w2b):
    b = pl.program_id(0)

    @pl.when(b < n_used_ref[0])
    def _():
        prev_e = blk_e_ref[jnp.maximum(b - 1, 0)]

        @pl.when(jnp.logical_or(b == 0, blk_e_ref[b] != prev_e))
        def _():
            w1b[...] = w1_ref[...].astype(BF16)
            w3b[...] = w3_ref[...].astype(BF16)
            w2b[...] = w2_ref[...].astype(BF16)

        xb = _unpack_pairs(x_ref[...]).astype(BF16)
        h1 = jnp.dot(xb, w1b[...], preferred_element_type=F32)
        h3 = jnp.dot(xb, w3b[...], preferred_element_type=F32)
        act = (h1 * jax.nn.sigmoid(h1) * h3).astype(BF16)
        y_ref[...] = _pack_pairs(jnp.dot(act, w2b[...], preferred_element_type=F32))


def _experts(blk_e, n_used, xs, w_e1, w_e3, w_e2, bm):
    n_rows = xs.shape[0]
    d, ff = w_e1.shape[-2:]
    rows = lambda b, be, nu: (jnp.minimum(b, nu[0] - 1), 0)
    wspec1 = pl.BlockSpec((None, d, ff), lambda b, be, nu: (be[b], 0, 0))
    wspec2 = pl.BlockSpec((None, ff, d), lambda b, be, nu: (be[b], 0, 0))
    grid_spec = pltpu.PrefetchScalarGridSpec(
        num_scalar_prefetch=2,
        grid=(n_rows // bm,),
        in_specs=[pl.BlockSpec((bm, d // 2), rows), wspec1, wspec1, wspec2],
        out_specs=pl.BlockSpec((bm, d // 2), rows),
        scratch_shapes=[pltpu.VMEM((d, ff), BF16), pltpu.VMEM((d, ff), BF16),
                        pltpu.VMEM((ff, d), BF16)])
    return pl.pallas_call(
        _expert_kernel,
        grid_spec=grid_spec,
        out_shape=jax.ShapeDtypeStruct((n_rows, d // 2), U32),
        compiler_params=_cparams("arbitrary"),
        name="experts",
    )(blk_e, n_used, xs, w_e1, w_e3, w_e2)


def _block_tables(counts, n_blocks, bm):
    padded = (counts + bm - 1) // bm * bm
    pad_end = jnp.cumsum(padded)
    pad_start = pad_end - padded
    blk_first = jnp.arange(n_blocks, dtype=I32) * bm
    blk_e = jnp.minimum(jnp.sum(pad_end[None, :] <= blk_first[:, None], axis=1), N_EXPERTS - 1)
    n_used = pad_end[-1] // bm
    blk_e = jnp.where(jnp.arange(n_blocks) < n_used, blk_e, blk_e[jnp.maximum(n_used - 1, 0)])
    last_row = jnp.maximum(pad_end - bm, 0)
    return (pad_start, last_row.astype(I32), (padded > 0).astype(I32),
            blk_e.astype(I32), n_used.astype(I32).reshape(1))


def _combine_kernel(dest_ref, wts_ref, v_ref, h1_ref, g2_ref, lg_ref, lb_ref,
                    ws1_ref, ws3_ref, ws2_ref, ys_hbm, o_ref, dest_s, ybuf, sem_i, sem_r):
    tm = v_ref.shape[0]
    cp = pltpu.make_async_copy(dest_ref, dest_s, sem_i)
    cp.start()
    cp.wait()

    for t in range(tm):
        for k in range(TOP_K):
            pltpu.make_async_copy(ys_hbm.at[pl.ds(dest_s[k, t], 1), :],
                                  ybuf.at[k, pl.ds(t, 1), :], sem_r).start(priority=k % 2)

    vb = v_ref[...].astype(BF16)
    a1 = jnp.dot(vb, ws1_ref[...], preferred_element_type=F32)
    a3 = jnp.dot(vb, ws3_ref[...], preferred_element_type=F32)
    act = (a1 * jax.nn.sigmoid(a1) * a3).astype(BF16)
    shared = jnp.dot(act, ws2_ref[...], preferred_element_type=F32)
    wpad = jnp.concatenate([wts_ref[...], jnp.zeros((LANES - TOP_K, tm), F32)], axis=0)
    wt = wpad.T

    for k in range(TOP_K):
        pltpu.make_async_copy(ys_hbm.at[pl.ds(0, tm), :], ybuf.at[k], sem_r).wait()
    ff = _unpack_pairs(ybuf[0]) * wt[:, 0:1]
    for k in range(1, TOP_K):
        ff = ff + _unpack_pairs(ybuf[k]) * wt[:, k:k + 1]
    ff = ff + shared
    o_ref[...] = _layer_norm(DEEPNORM_ALPHA * h1_ref[...] + g2_ref[...] * ff,
                             lg_ref[...], lb_ref[...])


def _combine(dest, wts, v, h1, g2, ln_g, ln_b, w_s1, w_s3, w_s2, ys, tm):
    s, d = v.shape
    row = lambda i: (i, 0)
    vec = _const_spec((1, d))
    slot = pl.BlockSpec((TOP_K, tm), lambda i: (0, i))
    return pl.pallas_call(
        _combine_kernel,
        grid=(s // tm,),
        in_specs=[slot, slot, pl.BlockSpec((tm, d), row), pl.BlockSpec((tm, d), row),
                  vec, vec, vec,
                  _const_spec(w_s1.shape), _const_spec(w_s3.shape), _const_spec(w_s2.shape),
                  pl.BlockSpec(memory_space=pl.ANY)],
        out_specs=pl.BlockSpec((tm, d), row),
        out_shape=jax.ShapeDtypeStruct((s, d), F32),
        scratch_shapes=[pltpu.SMEM((TOP_K, tm), I32), pltpu.VMEM((TOP_K, tm, d // 2), U32),
                        pltpu.SemaphoreType.DMA, pltpu.SemaphoreType.DMA],
        compiler_params=_cparams("arbitrary"),
        name="combine",
    )(dest, wts, v, h1, g2, ln_g, ln_b, w_s1, w_s3, w_s2, ys)


def _rope_tables(s):
    half = HEAD_DIM // 4
    inv_freq = ROPE_THETA ** (-np.arange(half, dtype=np.float32) / half)
    t = np.arange(s)
    ang_r = (t // GRID_W).astype(np.float32)[:, None] * inv_freq[None, :]
    ang_c = (t % GRID_W).astype(np.float32)[:, None] * inv_freq[None, :]
    cos = np.concatenate([np.cos(ang_r)] * 2 + [np.cos(ang_c)] * 2, axis=-1)
    sin = np.concatenate([-np.sin(ang_r), np.sin(ang_r), -np.sin(ang_c), np.sin(ang_c)], axis=-1)
    return jnp.asarray(cos, F32), jnp.asarray(sin, F32)


def kernel(x, c, ctx, c_ctx, w_mod, b_mod, w_in, q_norm, k_norm, conv_w, conv_b, rg_wa, rg_ba,
           rg_wx, rg_bx, rg_lam, w_out, ln1_g, ln1_b, w_router, e_bias, w_e1, w_e3, w_e2,
           w_s1, w_s3, w_s2, ln2_g, ln2_b):
    assert x.shape[0] == 1 and w_mod.shape[0] == DEPTH
    _, s, d = x.shape
    n_ctx = ctx.shape[1]
    x2 = x[0]

    c2t = jnp.stack([c[0], c_ctx], axis=1)
    mod = _modulation(c2t, w_mod[0], b_mod[0].reshape(1, -1))
    sh1, sc1, g1, sh2, sc2, g2 = [mod[0:1, j * d:(j + 1) * d] for j in range(6)]
    csh1, csc1 = mod[1:2, 0:d], mod[1:2, d:2 * d]

    w_in_b = w_in[0].astype(BF16)
    qg = q_norm[0].reshape(1, HEAD_DIM)
    kg = k_norm[0].reshape(1, HEAD_DIM)
    cos, sin = _rope_tables(s)
    q_l, k_l, vt_l, xr_l, yr_l = _in_projection(x2, sc1, sh1, w_in_b, qg, kg, cos, sin,
                                                tm=min(512, s), kv_chunk=min(KV_CHUNK, s))
    _, k_c, vt_c, xr_c, _ = _in_projection(
        ctx[0], csc1, csh1, w_in_b, qg, kg,
        jnp.ones((n_ctx, HEAD_DIM), F32), jnp.zeros((n_ctx, HEAD_DIM), F32),
        tm=n_ctx, kv_chunk=n_ctx)

    attn = _attention(q_l, k_c, vt_c, k_l, vt_l, tq=min(256, s))

    wg = jnp.concatenate([rg_wa[0], rg_wx[0]], axis=-1).astype(BF16)
    cb = conv_b[0].reshape(1, RG_W)
    zero_state = jnp.zeros((1, RG_W), F32)
    rg_args = []
    for dd in range(2):
        rg_args.append((conv_w[0], cb, wg[dd],
                        jnp.stack([rg_ba[0, dd], rg_bx[0, dd]], axis=0),
                        rg_lam[0, dd].reshape(1, RG_W)))
    t_rg = min(512, s)
    hc_f = _rg_scan(xr_c, *rg_args[0], zero_state, reverse=False, t=n_ctx)
    hc_b = _rg_scan(xr_c, *rg_args[1], zero_state, reverse=True, t=n_ctx)
    h_f = _rg_scan(xr_l, *rg_args[0], hc_f[n_ctx - 1:n_ctx], reverse=False, t=t_rg)
    rg = _rg_scan(xr_l, *rg_args[1], hc_b[0:1], reverse=True, t=t_rg, hf=h_f, yr=yr_l)

    wr = jnp.pad(w_router[0], ((0, 0), (0, LANES - N_EXPERTS)))
    wr_hi = wr.astype(BF16)
    wr_lo = (wr - wr_hi.astype(F32)).astype(BF16)
    h1, v, idx, wts, rank, cnt = _out_projection(
        attn, rg, x2, w_out[0].astype(BF16), g1, ln1_g[0].reshape(1, d), ln1_b[0].reshape(1, d),
        sc2, sh2, wr_hi, wr_lo, e_bias[0].reshape(N_EXPERTS, 1), tm=min(512, s))

    bm = min(EXPERT_BLOCK, s)
    n_blocks = s * TOP_K // bm + N_EXPERTS
    pad_start, last_row, has_blk, blk_e, n_used = _block_tables(cnt[:, 0].astype(I32), n_blocks, bm)
    dest, xs = _dispatch(last_row, has_blk, v, idx, rank,
                         pad_start.astype(F32).reshape(N_EXPERTS, 1), n_blocks * bm,
                         tm=min(256, s), bm=bm)
    ys = _experts(blk_e, n_used, xs, w_e1[0], w_e3[0], w_e2[0], bm)

    out = _combine(dest, wts, v, h1, g2, ln2_g[0].reshape(1, d), ln2_b[0].reshape(1, d),
                   w_s1[0].astype(BF16), w_s3[0].astype(BF16), w_s2[0].astype(BF16), ys,
                   tm=min(256, s))
    return out[None]
```

```python
import functools

import jax
import jax.numpy as jnp
import numpy as np
from jax import lax
from jax.experimental import pallas as pl
from jax.experimental.pallas import tpu as pltpu

F32 = jnp.float32
BF16 = jnp.bfloat16
I32 = jnp.int32
U32 = jnp.uint32

GRID_W = 64
HEAD_DIM = 128
N_HEADS = 8
N_KV_HEADS = 2
GQA_GROUP = N_HEADS // N_KV_HEADS
ATTN_W = N_HEADS * HEAD_DIM
KV_W = N_KV_HEADS * HEAD_DIM
ROPE_THETA = 10000.0
RG_W = 1024
RG_HEADS = 8
RG_HD = RG_W // RG_HEADS
RG_C = 8.0
PROJ_W = ATTN_W + 2 * KV_W + 2 * RG_W
N_EXPERTS = 64
N_GROUPS = 8
GROUP_SIZE = N_EXPERTS // N_GROUPS
TOPK_GROUPS = 4
TOP_K = 8
ROUTED_SCALE = 2.5
NORM_EPS = 1e-6
DEPTH = 1
DEEPNORM_ALPHA = (2.0 * DEPTH) ** 0.25
LOG2E = 1.4426950408889634

LANES = 128
SUBLANES = 8
BF16_SUBLANES = 16
VMEM_LIMIT = 56 * 1024 * 1024

NEG_BIG = -1e30
KV_CHUNK = 512
ATTN_UNROLL = 16
VT_ROWS = HEAD_DIM + BF16_SUBLANES
EXPERT_BLOCK = 512


def _cparams(*sem):
    return pltpu.CompilerParams(dimension_semantics=sem, vmem_limit_bytes=VMEM_LIMIT)


def _const_spec(shape):
    nd = len(shape)
    return pl.BlockSpec(shape, lambda *_: (0,) * nd)


def _mod_kernel(ct_ref, w_ref, b_ref, o_ref, sb_ref, *, tn):
    d = w_ref.shape[0]

    @pl.when(pl.program_id(0) == 0)
    def _():
        ct = ct_ref[...]
        s = ct * jax.nn.sigmoid(ct)
        sb_ref[0] = jnp.broadcast_to(s[:, 0:1], (d, LANES))
        sb_ref[1] = jnp.broadcast_to(s[:, 1:2], (d, LANES))

    for c in range(tn // LANES):
        sl = slice(c * LANES, (c + 1) * LANES)
        wc = w_ref[:, sl]
        bc = b_ref[:, sl]
        o0 = jnp.sum(wc * sb_ref[0], axis=0, keepdims=True) + bc
        o1 = jnp.sum(wc * sb_ref[1], axis=0, keepdims=True) + bc
        o_ref[:, sl] = jnp.concatenate(
            [o0, o1, jnp.zeros((SUBLANES - 2, LANES), F32)], axis=0)


def _modulation(c2t, w_mod, b_mod):
    d, n = w_mod.shape
    tn = 1024
    return pl.pallas_call(
        functools.partial(_mod_kernel, tn=tn),
        grid=(n // tn,),
        in_specs=[_const_spec((d, 2)),
                  pl.BlockSpec((d, tn), lambda j: (0, j)),
                  pl.BlockSpec((1, tn), lambda j: (0, j))],
        out_specs=pl.BlockSpec((SUBLANES, tn), lambda j: (0, j)),
        out_shape=jax.ShapeDtypeStruct((SUBLANES, n), F32),
        scratch_shapes=[pltpu.VMEM((2, d, LANES), F32)],
        compiler_params=_cparams("arbitrary"),
        name="modulation",
    )(c2t, w_mod, b_mod)


def _swap_half(y):
    lane = lax.broadcasted_iota(I32, y.shape, 1)
    return jnp.where((lane % 64) < 32,
                     pltpu.roll(y, LANES - 32, 1), pltpu.roll(y, 32, 1))


def _norm_rope(ph, g, cos, sin_signed, scale):
    ms = jnp.mean(ph * ph, axis=-1, keepdims=True)
    y = (ph * lax.rsqrt(ms + NORM_EPS)) * g
    y = y * cos + _swap_half(y) * sin_signed
    if scale != 1.0:
        y = y * scale
    return y


def _inproj_kernel(x_ref, sc_ref, sh_ref, w_ref, qg_ref, kg_ref, cos_ref, sin_ref,
                   q_ref, k_ref, vt_ref, xr_ref, yr_ref, *, q_scale):
    tm = x_ref.shape[0]
    kc = vt_ref.shape[-1]
    u = (x_ref[...] * (1.0 + sc_ref[...]) + sh_ref[...]).astype(BF16)
    cos = cos_ref[...]
    sin = sin_ref[...]
    o = 0
    pq = jnp.dot(u, w_ref[:, o:o + ATTN_W], preferred_element_type=F32)
    for h in range(N_HEADS):
        sl = slice(h * HEAD_DIM, (h + 1) * HEAD_DIM)
        q_ref[:, sl] = _norm_rope(pq[:, sl], qg_ref[...], cos, sin, q_scale).astype(BF16)
    o += ATTN_W
    pk = jnp.dot(u, w_ref[:, o:o + KV_W], preferred_element_type=F32)
    for h in range(N_KV_HEADS):
        sl = slice(h * HEAD_DIM, (h + 1) * HEAD_DIM)
        k_ref[:, sl] = _norm_rope(pk[:, sl], kg_ref[...], cos, sin, 1.0).astype(BF16)
    o += KV_W
    pv = jnp.dot(u, w_ref[:, o:o + KV_W], preferred_element_type=F32)
    ones_rows = jnp.where(
        lax.broadcasted_iota(I32, (VT_ROWS - HEAD_DIM, kc), 0) == 0, 1.0, 0.0).astype(BF16)
    for h in range(N_KV_HEADS):
        for cc in range(tm // kc):
            blk = pv[cc * kc:(cc + 1) * kc, h * HEAD_DIM:(h + 1) * HEAD_DIM]
            vt_ref[h, cc, 0:HEAD_DIM, :] = blk.T.astype(BF16)
            vt_ref[h, cc, HEAD_DIM:VT_ROWS, :] = ones_rows
    o += KV_W
    xr_ref[...] = jnp.dot(u, w_ref[:, o:o + RG_W], preferred_element_type=F32)
    o += RG_W
    yr_ref[...] = jnp.dot(u, w_ref[:, o:o + RG_W], preferred_element_type=F32)


def _in_projection(x, sc, sh, w_in, qg, kg, cos, sin, tm, kv_chunk):
    s, d = x.shape
    row = lambda i: (i, 0)
    if kv_chunk >= tm:
        per = kv_chunk // tm
        vt_spec = pl.BlockSpec((N_KV_HEADS, 1, VT_ROWS, tm), lambda i: (0, i // per, 0, i % per))
    else:
        vt_spec = pl.BlockSpec((N_KV_HEADS, tm // kv_chunk, VT_ROWS, kv_chunk),
                               lambda i: (0, i, 0, 0))
    return pl.pallas_call(
        functools.partial(_inproj_kernel, q_scale=HEAD_DIM ** -0.5 * LOG2E),
        grid=(s // tm,),
        in_specs=[pl.BlockSpec((tm, d), row),
                  _const_spec((1, d)), _const_spec((1, d)),
                  pl.BlockSpec((d, PROJ_W), lambda i: (0, 0), pipeline_mode=pl.Buffered(1)),
                  _const_spec((1, HEAD_DIM)), _const_spec((1, HEAD_DIM)),
                  pl.BlockSpec((tm, HEAD_DIM), row), pl.BlockSpec((tm, HEAD_DIM), row)],
        out_specs=[pl.BlockSpec((tm, ATTN_W), row), pl.BlockSpec((tm, KV_W), row),
                   vt_spec,
                   pl.BlockSpec((tm, RG_W), row), pl.BlockSpec((tm, RG_W), row)],
        out_shape=[jax.ShapeDtypeStruct((s, ATTN_W), BF16),
                   jax.ShapeDtypeStruct((s, KV_W), BF16),
                   jax.ShapeDtypeStruct((N_KV_HEADS, s // kv_chunk, VT_ROWS, kv_chunk), BF16),
                   jax.ShapeDtypeStruct((s, RG_W), F32),
                   jax.ShapeDtypeStruct((s, RG_W), F32)],
        compiler_params=_cparams("arbitrary"),
        name="in_projection",
    )(x, sc, sh, w_in, qg, kg, cos, sin)


def _attn_kernel(q_ref, kc_ref, vtc_ref, kl_ref, vtl_ref, o_ref, sa_ref, sb_ref, pa_ref, pb_ref, sc_ref,
                 *, tq, tk, n_chunks, unroll):
    q = q_ref[...]
    qs = jnp.concatenate(
        [q[:, g * HEAD_DIM:(g + 1) * HEAD_DIM] for g in range(GQA_GROUP)], axis=0)
    cols = GQA_GROUP * tq

    def scores(k):
        return lax.dot_general(k, qs, (((1,), (1,)), ((), ())), preferred_element_type=F32)

    def softmax(s_ref, m):
        m_new = jnp.maximum(m, jnp.max(s_ref[...], axis=0, keepdims=True))
        return m_new, jnp.exp2(m - m_new), jnp.exp2(s_ref[...] - m_new).astype(BF16)

    def weighted_values(vt, p, alpha, acc):
        return alpha * acc + jnp.dot(vt, p, preferred_element_type=F32)

    m0 = jnp.full((1, cols), NEG_BIG, F32)
    a0 = jnp.zeros((VT_ROWS, cols), F32)
    sc_ref[...] = scores(kc_ref[...])
    m, alpha, p = softmax(sc_ref, m0)
    acc = weighted_values(vtc_ref[0], p, alpha, a0)

    def latent_scores(j):
        j = jnp.minimum(j, n_chunks - 1)
        return scores(kl_ref[pl.ds(pl.multiple_of(j * tk, tk), tk), :])

    s_refs = (sa_ref, sb_ref)
    p_refs = (pa_ref, pb_ref)
    sa_ref[...] = latent_scores(0)
    sb_ref[...] = latent_scores(1)
    m, alpha, pa_ref[...] = softmax(sa_ref, m)

    def body(i, carry):
        m, acc, alpha = carry
        for u in range(unroll):
            c = unroll * i + u
            cur, nxt = u % 2, (u + 1) % 2
            s_refs[cur][...] = latent_scores(c + 2)
            m, alpha_next, p_refs[nxt][...] = softmax(s_refs[nxt], m)
            acc = weighted_values(vtl_ref[c], p_refs[cur][...], alpha, acc)
            alpha = alpha_next
        return m, acc, alpha

    _, acc, _ = lax.fori_loop(0, n_chunks // unroll, body, (m, acc, alpha))
    out_t = acc[0:HEAD_DIM, :] / acc[HEAD_DIM:HEAD_DIM + 1, :]
    for g in range(GQA_GROUP):
        o_ref[:, g * HEAD_DIM:(g + 1) * HEAD_DIM] = out_t[:, g * tq:(g + 1) * tq].T.astype(BF16)


def _attention(q, k_c, vt_c, k_l, vt_l, tq):
    s = q.shape[0]
    n_ctx = k_c.shape[0]
    n_chunks, _, tk = vt_l.shape[1:]
    unroll = min(ATTN_UNROLL, n_chunks)
    assert n_chunks % unroll == 0 and unroll % 2 == 0
    gw = GQA_GROUP * HEAD_DIM
    return pl.pallas_call(
        functools.partial(_attn_kernel, tq=tq, tk=tk, n_chunks=n_chunks, unroll=unroll),
        grid=(N_KV_HEADS, s // tq),
        in_specs=[pl.BlockSpec((tq, gw), lambda h, i: (i, h)),
                  pl.BlockSpec((n_ctx, HEAD_DIM), lambda h, i: (0, h)),
                  pl.BlockSpec((None, 1, VT_ROWS, n_ctx), lambda h, i: (h, 0, 0, 0)),
                  pl.BlockSpec((s, HEAD_DIM), lambda h, i: (0, h)),
                  pl.BlockSpec((None, n_chunks, VT_ROWS, tk), lambda h, i: (h, 0, 0, 0))],
        out_specs=pl.BlockSpec((tq, gw), lambda h, i: (i, h)),
        out_shape=jax.ShapeDtypeStruct((s, ATTN_W), BF16),
        scratch_shapes=[pltpu.VMEM((tk, GQA_GROUP * tq), F32),
                        pltpu.VMEM((tk, GQA_GROUP * tq), F32),
                        pltpu.VMEM((tk, GQA_GROUP * tq), BF16),
                        pltpu.VMEM((tk, GQA_GROUP * tq), BF16),
                        pltpu.VMEM((n_ctx, GQA_GROUP * tq), F32)],
        compiler_params=_cparams("arbitrary", "arbitrary"),
        name="attention",
    )(q, k_c, vt_c, k_l, vt_l)


def _log_sigmoid(x):
    return jnp.minimum(x, 0.0) - jnp.log1p(jnp.exp(-jnp.abs(x)))


def _rg_kernel(x_ref, xp_ref, xn_ref, cw_ref, cb_ref, wg_ref, bg_ref, lam_ref, h0_ref,
               *rest, reverse, final, t, n_chunks):
    if final:
        hf_ref, yr_ref, o_ref, a_scr, b_scr, hc_scr, h_scr = rest
    else:
        o_ref, a_scr, b_scr, hc_scr = rest
        h_scr = o_ref
    i = pl.program_id(0)
    c = (n_chunks - 1 - i) if reverse else i
    w = x_ref.shape[1]

    @pl.when(i == 0)
    def _():
        hc_scr[...] = jnp.broadcast_to(h0_ref[...], (SUBLANES, w))

    x = x_ref[...]
    row = lax.broadcasted_iota(I32, (t, w), 0)
    pm = jnp.where(c == 0, 0.0, 1.0).astype(F32)
    nm = jnp.where(c == n_chunks - 1, 0.0, 1.0).astype(F32)
    p6 = xp_ref[SUBLANES - 2:SUBLANES - 1, :] * pm
    p7 = xp_ref[SUBLANES - 1:SUBLANES, :] * pm
    n0 = xn_ref[0:1, :] * nm
    x_m1 = jnp.where(row == 0, p7, pltpu.roll(x, 1, 0))
    x_m2 = jnp.where(row == 0, p6, jnp.where(row == 1, p7, pltpu.roll(x, 2, 0)))
    x_p1 = jnp.where(row == t - 1, n0, pltpu.roll(x, t - 1, 0))
    xc = cb_ref[...] + cw_ref[0:1, :] * x_m2
    xc = xc + cw_ref[1:2, :] * x_m1
    xc = xc + cw_ref[2:3, :] * x
    xc = xc + cw_ref[3:4, :] * x_p1

    xcb = xc.astype(BF16)
    clam = RG_C * _log_sigmoid(lam_ref[...])
    for h in range(RG_HEADS):
        sl = slice(h * RG_HD, (h + 1) * RG_HD)
        g = jnp.dot(xcb[:, sl], wg_ref[h], preferred_element_type=F32)
        r = jax.nn.sigmoid(g[:, :RG_HD] + bg_ref[0:1, sl])
        gi = jax.nn.sigmoid(g[:, RG_HD:] + bg_ref[1:2, sl])
        log_a = r * clam[:, sl]
        a = jnp.exp(log_a)
        a_scr[:, sl] = a
        b_scr[:, sl] = jnp.sqrt(-jnp.tanh(log_a) * (a * a + 1.0)) * (gi * xc[:, sl])

    srow = lax.broadcasted_iota(I32, (SUBLANES, w), 0)
    n_tiles = t // SUBLANES

    def tile_body(j, hprev):
        tile = (n_tiles - 1 - j) if reverse else j
        start = pl.multiple_of(tile * SUBLANES, SUBLANES)
        a = a_scr[pl.ds(start, SUBLANES), :]
        b = b_scr[pl.ds(start, SUBLANES), :]
        for k in (1, 2, 4):
            if reverse:
                keep = srow < SUBLANES - k
                shift = SUBLANES - k
            else:
                keep = srow >= k
                shift = k
            a_sh = jnp.where(keep, pltpu.roll(a, shift, 0), 1.0)
            b_sh = jnp.where(keep, pltpu.roll(b, shift, 0), 0.0)
            b = a * b_sh + b
            a = a * a_sh
        hh = a * hprev + b
        h_scr[pl.ds(start, SUBLANES), :] = hh
        last = hh[0:1, :] if reverse else hh[SUBLANES - 1:SUBLANES, :]
        return jnp.broadcast_to(last, (SUBLANES, w))

    hc_scr[...] = lax.fori_loop(0, n_tiles, tile_body, hc_scr[...])

    if final:
        gate = jax.nn.gelu(yr_ref[...], approximate=True)
        o_ref[...] = ((hf_ref[...] + h_scr[...]) * gate).astype(o_ref.dtype)


def _rg_scan(xr, conv_w, conv_b, wg, bg, lam, h0, *, reverse, t, hf=None, yr=None):
    s, w = xr.shape
    n_chunks = s // t
    final = hf is not None
    tb = t // SUBLANES
    last_blk = s // SUBLANES - 1
    if reverse:
        cidx = lambda i: n_chunks - 1 - i
    else:
        cidx = lambda i: i
    chunk_spec = pl.BlockSpec((t, w), lambda i: (cidx(i), 0))
    in_specs = [chunk_spec,
                pl.BlockSpec((SUBLANES, w), lambda i: (jnp.maximum(cidx(i) * tb - 1, 0), 0)),
                pl.BlockSpec((SUBLANES, w), lambda i: (jnp.minimum((cidx(i) + 1) * tb, last_blk), 0)),
                _const_spec((4, w)), _const_spec((1, w)),
                _const_spec((RG_HEADS, RG_HD, 2 * RG_HD)), _const_spec((2, w)),
                _const_spec((1, w)), _const_spec((1, w))]
    args = [xr, xr, xr, conv_w, conv_b, wg, bg, lam, h0]
    scratch = [pltpu.VMEM((t, w), F32), pltpu.VMEM((t, w), F32), pltpu.VMEM((SUBLANES, w), F32)]
    if final:
        in_specs += [chunk_spec, chunk_spec]
        args += [hf, yr]
        scratch.append(pltpu.VMEM((t, w), F32))
        out_dtype = BF16
    else:
        out_dtype = F32
    return pl.pallas_call(
        functools.partial(_rg_kernel, reverse=reverse, final=final, t=t, n_chunks=n_chunks),
        grid=(n_chunks,),
        in_specs=in_specs,
        out_specs=chunk_spec,
        out_shape=jax.ShapeDtypeStruct((s, w), out_dtype),
        scratch_shapes=scratch,
        compiler_params=_cparams("arbitrary"),
        name="rglru_bwd" if reverse else "rglru_fwd",
    )(*args)


def _layer_norm(y, g, b):
    mu = jnp.mean(y, axis=-1, keepdims=True)
    yc = y - mu
    var = jnp.mean(yc * yc, axis=-1, keepdims=True)
    return yc * lax.rsqrt(var + NORM_EPS) * g + b


def _outproj_kernel(attn_ref, rg_ref, x_ref, w_ref, g1_ref, lg_ref, lb_ref, sc2_ref, sh2_ref,
                    whi_ref, wlo_ref, eb_ref, tri_ref,
                    h1_ref, v_ref, idx_ref, wts_ref, rank_ref, cnt_ref, base_scr):
    mix = jnp.dot(attn_ref[...], w_ref[0:ATTN_W, :], preferred_element_type=F32)
    mix = mix + jnp.dot(rg_ref[...], w_ref[ATTN_W:, :], preferred_element_type=F32)
    h1 = _layer_norm(DEEPNORM_ALPHA * x_ref[...] + g1_ref[...] * mix, lg_ref[...], lb_ref[...])
    h1_ref[...] = h1
    v = h1 * (1.0 + sc2_ref[...]) + sh2_ref[...]
    v_ref[...] = v
    _route(v, whi_ref, wlo_ref, eb_ref, tri_ref, idx_ref, wts_ref, rank_ref, cnt_ref, base_scr)


def _out_projection(attn, rg, x, w_out, g1, ln_g, ln_b, sc2, sh2, w_hi, w_lo, e_bias_col, tm):
    s, d = x.shape
    row = lambda i: (i, 0)
    vec = _const_spec((1, d))
    tri = jnp.triu(jnp.ones((tm, tm), BF16), k=1)
    slot = pl.BlockSpec((TOP_K, tm), lambda i: (0, i))
    return pl.pallas_call(
        _outproj_kernel,
        grid=(s // tm,),
        in_specs=[pl.BlockSpec((tm, ATTN_W), row), pl.BlockSpec((tm, RG_W), row),
                  pl.BlockSpec((tm, d), row),
                  pl.BlockSpec((ATTN_W + RG_W, d), lambda i: (0, 0), pipeline_mode=pl.Buffered(1)),
                  vec, vec, vec, vec, vec,
                  _const_spec((d, LANES)), _const_spec((d, LANES)),
                  _const_spec((N_EXPERTS, 1)), _const_spec((tm, tm))],
        out_specs=[pl.BlockSpec((tm, d), row), pl.BlockSpec((tm, d), row),
                   slot, slot, slot, _const_spec((N_EXPERTS, LANES))],
        out_shape=[jax.ShapeDtypeStruct((s, d), F32), jax.ShapeDtypeStruct((s, d), F32),
                   jax.ShapeDtypeStruct((TOP_K, s), I32),
                   jax.ShapeDtypeStruct((TOP_K, s), F32),
                   jax.ShapeDtypeStruct((TOP_K, s), I32),
                   jax.ShapeDtypeStruct((N_EXPERTS, LANES), F32)],
        scratch_shapes=[pltpu.VMEM((N_EXPERTS, 1), F32)],
        compiler_params=_cparams("arbitrary"),
        name="out_projection",
    )(attn, rg, x, w_out, g1, ln_g, ln_b, sc2, sh2, w_hi, w_lo, e_bias_col, tri)


def _first_index_of_max(x, iota_f, axis):
    mx = jnp.max(x, axis=axis, keepdims=True)
    idx = jnp.min(jnp.where(x == mx, iota_f, float(N_EXPERTS)), axis=axis, keepdims=True)
    return mx, idx


def _route(v, whi_ref, wlo_ref, eb_ref, tri_ref, idx_ref, wts_ref, rank_ref, cnt_ref, base_scr):
    tm = v.shape[0]

    @pl.when(pl.program_id(0) == 0)
    def _():
        base_scr[...] = jnp.zeros(base_scr.shape, F32)

    v_hi = v.astype(BF16)
    v_lo = (v - v_hi.astype(F32)).astype(BF16)
    logits = jnp.dot(v_hi, whi_ref[...], preferred_element_type=F32)
    logits = logits + jnp.dot(v_lo, whi_ref[...], preferred_element_type=F32)
    logits = logits + jnp.dot(v_hi, wlo_ref[...], preferred_element_type=F32)
    lt = logits.T[0:N_EXPERTS, :]
    scores = jax.nn.sigmoid(lt)
    biased = scores + eb_ref[...]
    neg_inf = float("-inf")

    ig = lax.broadcasted_iota(I32, (GROUP_SIZE, tm), 0).astype(F32)
    groups = [biased[g * GROUP_SIZE:(g + 1) * GROUP_SIZE, :] for g in range(N_GROUPS)]
    gscore = []
    for bg in groups:
        top1, i1 = _first_index_of_max(bg, ig, 0)
        top2 = jnp.max(jnp.where(ig == i1, neg_inf, bg), axis=0, keepdims=True)
        gscore.append(top1 + top2)

    masked = []
    for g in range(N_GROUPS):
        ahead = jnp.zeros((1, tm), F32)
        for o in range(N_GROUPS):
            if o == g:
                continue
            before = (gscore[o] >= gscore[g]) if o < g else (gscore[o] > gscore[g])
            ahead = ahead + jnp.where(before, 1.0, 0.0)
        keep = jnp.broadcast_to(ahead < TOPK_GROUPS, (GROUP_SIZE, tm))
        masked.append(jnp.where(keep, groups[g], neg_inf))
    masked = jnp.concatenate(masked, axis=0)

    ie = lax.broadcasted_iota(I32, masked.shape, 0).astype(F32)
    seen = base_scr[...]
    ws = []
    for k in range(TOP_K):
        _, ei = _first_index_of_max(masked, ie, 0)
        hit = ie == ei
        idx_ref[k:k + 1, :] = ei.astype(I32)
        ws.append(jnp.sum(jnp.where(hit, scores, 0.0), axis=0, keepdims=True))
        masked = jnp.where(hit, neg_inf, masked)
        onehot = jnp.where(hit, 1.0, 0.0)
        before = jnp.dot(onehot.astype(BF16), tri_ref[...], preferred_element_type=F32)
        rank = jnp.sum(jnp.where(hit, before + seen, 0.0), axis=0, keepdims=True)
        rank_ref[k:k + 1, :] = rank.astype(I32)
        seen = seen + jnp.sum(onehot, axis=1, keepdims=True)
    base_scr[...] = seen
    cnt_ref[...] = jnp.broadcast_to(seen, cnt_ref.shape)
    total = ws[0]
    for k in range(1, TOP_K):
        total = total + ws[k]
    for k in range(TOP_K):
        wts_ref[k:k + 1, :] = ws[k] / total * ROUTED_SCALE


def _pack_pairs(x):
    h = x.shape[1] // 2
    lo = lax.bitcast_convert_type(x[:, :h].astype(BF16).astype(F32), U32)
    hi = lax.bitcast_convert_type(x[:, h:].astype(BF16).astype(F32), U32)
    return (lo >> 16) | (hi & jnp.uint32(0xFFFF0000))


def _unpack_pairs(w):
    lo = lax.bitcast_convert_type(w << 16, F32)
    hi = lax.bitcast_convert_type(w & jnp.uint32(0xFFFF0000), F32)
    return jnp.concatenate([lo, hi], axis=1)


def _dispatch_kernel(last_ref, has_ref, v_ref, idx_ref, rank_ref, pstart_ref,
                     dest_ref, xs_hbm, dest_v, dest_s, zeros_v, pk_v, sem_z, sem_i, sem_r, *, bm):
    tm = v_ref.shape[0]

    def zero_fill(e):
        row0 = pl.multiple_of(last_ref[e], bm)
        return pltpu.make_async_copy(zeros_v, xs_hbm.at[pl.ds(row0, bm), :], sem_z)

    @pl.when(pl.program_id(0) == 0)
    def _():
        zeros_v[...] = jnp.zeros(zeros_v.shape, zeros_v.dtype)

        def start(e, carry):
            @pl.when(has_ref[e] > 0)
            def _():
                zero_fill(e).start()
            return carry

        def wait(e, carry):
            @pl.when(has_ref[e] > 0)
            def _():
                zero_fill(e).wait()
            return carry

        lax.fori_loop(0, N_EXPERTS, start, 0)
        lax.fori_loop(0, N_EXPERTS, wait, 0)

    ie = lax.broadcasted_iota(I32, (N_EXPERTS, tm), 0)
    pstart = pstart_ref[...]
    for k in range(TOP_K):
        hit = ie == idx_ref[k:k + 1, :]
        seg = jnp.sum(jnp.where(hit, pstart, 0.0), axis=0, keepdims=True)
        dest_v[k:k + 1, :] = seg.astype(I32) + rank_ref[k:k + 1, :]
    dest_ref[...] = dest_v[...]
    cp = pltpu.make_async_copy(dest_v, dest_s, sem_i)
    cp.start()
    pk_v[...] = _pack_pairs(v_ref[...])
    cp.wait()

    for t in range(tm):
        for k in range(TOP_K):
            pltpu.make_async_copy(pk_v.at[pl.ds(t, 1), :],
                                  xs_hbm.at[pl.ds(dest_s[k, t], 1), :], sem_r).start(priority=k % 2)
    for k in range(TOP_K):
        pltpu.make_async_copy(pk_v, xs_hbm.at[pl.ds(0, tm), :], sem_r).wait()


def _dispatch(last_row, has_blk, v, idx, rank, pstart_col, n_rows, tm, bm):
    s, d = v.shape
    slot = pl.BlockSpec((TOP_K, tm), lambda i, *_: (0, i))
    grid_spec = pltpu.PrefetchScalarGridSpec(
        num_scalar_prefetch=2,
        grid=(s // tm,),
        in_specs=[pl.BlockSpec((tm, d), lambda i, *_: (i, 0)), slot, slot,
                  pl.BlockSpec((N_EXPERTS, 1), lambda i, *_: (0, 0))],
        out_specs=[slot, pl.BlockSpec(memory_space=pl.ANY)],
        scratch_shapes=[pltpu.VMEM((TOP_K, tm), I32), pltpu.SMEM((TOP_K, tm), I32),
                        pltpu.VMEM((bm, d // 2), U32), pltpu.VMEM((tm, d // 2), U32),
                        pltpu.SemaphoreType.DMA, pltpu.SemaphoreType.DMA,
                        pltpu.SemaphoreType.DMA])
    return pl.pallas_call(
        functools.partial(_dispatch_kernel, bm=bm),
        grid_spec=grid_spec,
        out_shape=[jax.ShapeDtypeStruct((TOP_K, s), I32),
                   jax.ShapeDtypeStruct((n_rows, d // 2), U32)],
        compiler_params=_cparams("arbitrary"),
        name="dispatch",
    )(last_row, has_blk, v, idx, rank, pstart_col)


def _expert_kernel(blk_e_ref, n_used_ref, x_ref, w1_ref, w3_ref, w2_ref, y_ref, w1b, w3b, w2b):
    b = pl.program_id(0)

    @pl.when(b < n_used_ref[0])
    def _():
        prev_e = blk_e_ref[jnp.maximum(b - 1, 0)]

        @pl.when(jnp.logical_or(b == 0, blk_e_ref[b] != prev_e))
        def _():
            w1b[...] = w1_ref[...].astype(BF16)
            w3b[...] = w3_ref[...].astype(BF16)
            w2b[...] = w2_ref[...].astype(BF16)

        xb = _unpack_pairs(x_ref[...]).astype(BF16)
        h1 = jnp.dot(xb, w1b[...], preferred_element_type=F32)
        h3 = jnp.dot(xb, w3b[...], preferred_element_type=F32)
        act = (h1 * jax.nn.sigmoid(h1) * h3).astype(BF16)
        y_ref[...] = _pack_pairs(jnp.dot(act, w2b[...], preferred_element_type=F32))


def _experts(blk_e, n_used, xs, w_e1, w_e3, w_e2, bm):
    n_rows = xs.shape[0]
    d, ff = w_e1.shape[-2:]
    rows = lambda b, be, nu: (jnp.minimum(b, nu[0] - 1), 0)
    wspec1 = pl.BlockSpec((None, d, ff), lambda b, be, nu: (be[b], 0, 0))
    wspec2 = pl.BlockSpec((None, ff, d), lambda b, be, nu: (be[b], 0, 0))
    grid_spec = pltpu.PrefetchScalarGridSpec(
        num_scalar_prefetch=2,
        grid=(n_rows // bm,),
        in_specs=[pl.BlockSpec((bm, d // 2), rows), wspec1, wspec1, wspec2],
        out_specs=pl.BlockSpec((bm, d // 2), rows),
        scratch_shapes=[pltpu.VMEM((d, ff), BF16), pltpu.VMEM((d, ff), BF16),
                        pltpu.VMEM((ff, d), BF16)])
    return pl.pallas_call(
        _expert_kernel,
        grid_spec=grid_spec,
        out_shape=jax.ShapeDtypeStruct((n_rows, d // 2), U32),
        compiler_params=_cparams("arbitrary"),
        name="experts",
    )(blk_e, n_used, xs, w_e1, w_e3, w_e2)


def _block_tables(counts, n_blocks, bm):
    padded = (counts + bm - 1) // bm * bm
    pad_end = jnp.cumsum(padded)
    pad_start = pad_end - padded
    blk_first = jnp.arange(n_blocks, dtype=I32) * bm
    blk_e = jnp.minimum(jnp.sum(pad_end[None, :] <= blk_first[:, None], axis=1), N_EXPERTS - 1)
    n_used = pad_end[-1] // bm
    blk_e = jnp.where(jnp.arange(n_blocks) < n_used, blk_e, blk_e[jnp.maximum(n_used - 1, 0)])
    last_row = jnp.maximum(pad_end - bm, 0)
    return (pad_start, last_row.astype(I32), (padded > 0).astype(I32),
            blk_e.astype(I32), n_used.astype(I32).reshape(1))


def _combine_kernel(dest0_ref, dnext_ref, wts_ref, v_ref, h1_ref, g2_ref, lg_ref, lb_ref,
                    ws1_ref, ws3_ref, ws2_ref, ys_hbm, o_ref, dest_s, ybuf, sem_i, sem_r):
    tm = v_ref.shape[0]
    i = pl.program_id(0)

    def start_gathers(slot):
        for t in range(tm):
            for k in range(TOP_K):
                pltpu.make_async_copy(ys_hbm.at[pl.ds(dest_s[slot, k, t], 1), :],
                                      ybuf.at[slot, k, pl.ds(t, 1), :],
                                      sem_r.at[slot]).start(priority=k % 2)

    def wait_gathers(slot):
        for k in range(TOP_K):
            pltpu.make_async_copy(ys_hbm.at[pl.ds(0, tm), :], ybuf.at[slot, k], sem_r.at[slot]).wait()

    @pl.when(i == 0)
    def _():
        cp = pltpu.make_async_copy(dest0_ref, dest_s.at[0], sem_i)
        cp.start()
        cp.wait()

        def first_tile(t, carry):
            for k in range(TOP_K):
                pltpu.make_async_copy(ys_hbm.at[pl.ds(dest_s[0, k, t], 1), :],
                                      ybuf.at[0, k, pl.ds(t, 1), :], sem_r.at[0]).start()
            return carry

        lax.fori_loop(0, tm, first_tile, 0)

    def tile(slot):
        other = 1 - slot
        cp = pltpu.make_async_copy(dnext_ref, dest_s.at[other], sem_i)
        cp.start()
        cp.wait()
        wait_gathers(slot)
        start_gathers(other)
        vb = v_ref[...].astype(BF16)
        a1 = jnp.dot(vb, ws1_ref[...], preferred_element_type=F32)
        a3 = jnp.dot(vb, ws3_ref[...], preferred_element_type=F32)
        act = (a1 * jax.nn.sigmoid(a1) * a3).astype(BF16)
        ff = jnp.dot(act, ws2_ref[...], preferred_element_type=F32)
        wpad = jnp.concatenate([wts_ref[...], jnp.zeros((LANES - TOP_K, tm), F32)], axis=0)
        wt = wpad.T
        for k in range(TOP_K):
            ff = ff + _unpack_pairs(ybuf[slot, k]) * wt[:, k:k + 1]
        o_ref[...] = _layer_norm(DEEPNORM_ALPHA * h1_ref[...] + g2_ref[...] * ff,
                                 lg_ref[...], lb_ref[...])

        @pl.when(i == pl.num_programs(0) - 1)
        def _():
            wait_gathers(other)

    @pl.when(i % 2 == 0)
    def _():
        tile(0)

    @pl.when(i % 2 == 1)
    def _():
        tile(1)


def _combine(dest, wts, v, h1, g2, ln_g, ln_b, w_s1, w_s3, w_s2, ys, tm):
    s, d = v.shape
    n = s // tm
    row = lambda i: (i, 0)
    vec = _const_spec((1, d))
    slot = pl.BlockSpec((TOP_K, tm), lambda i: (0, i))
    single = lambda shape: pl.BlockSpec(shape, lambda i: (0, 0), pipeline_mode=pl.Buffered(1))
    return pl.pallas_call(
        _combine_kernel,
        grid=(n,),
        in_specs=[pl.BlockSpec((TOP_K, tm), lambda i: (0, 0)),
                  pl.BlockSpec((TOP_K, tm), lambda i: (0, jnp.minimum(i + 1, n - 1))),
                  slot, pl.BlockSpec((tm, d), row), pl.BlockSpec((tm, d), row),
                  vec, vec, vec,
                  single(w_s1.shape), single(w_s3.shape), single(w_s2.shape),
                  pl.BlockSpec(memory_space=pl.ANY)],
        out_specs=pl.BlockSpec((tm, d), row),
        out_shape=jax.ShapeDtypeStruct((s, d), F32),
        scratch_shapes=[pltpu.SMEM((2, TOP_K, tm), I32), pltpu.VMEM((2, TOP_K, tm, d // 2), U32),
                        pltpu.SemaphoreType.DMA, pltpu.SemaphoreType.DMA((2,))],
        compiler_params=_cparams("arbitrary"),
        name="combine",
    )(dest, dest, wts, v, h1, g2, ln_g, ln_b, w_s1, w_s3, w_s2, ys)


def _rope_tables(s):
    half = HEAD_DIM // 4
    inv_freq = ROPE_THETA ** (-np.arange(half, dtype=np.float32) / half)
    t = np.arange(s)
    ang_r = (t // GRID_W).astype(np.float32)[:, None] * inv_freq[None, :]
    ang_c = (t % GRID_W).astype(np.float32)[:, None] * inv_freq[None, :]
    cos = np.concatenate([np.cos(ang_r)] * 2 + [np.cos(ang_c)] * 2, axis=-1)
    sin = np.concatenate([-np.sin(ang_r), np.sin(ang_r), -np.sin(ang_c), np.sin(ang_c)], axis=-1)
    return jnp.asarray(cos, F32), jnp.asarray(sin, F32)


def kernel(x, c, ctx, c_ctx, w_mod, b_mod, w_in, q_norm, k_norm, conv_w, conv_b, rg_wa, rg_ba,
           rg_wx, rg_bx, rg_lam, w_out, ln1_g, ln1_b, w_router, e_bias, w_e1, w_e3, w_e2,
           w_s1, w_s3, w_s2, ln2_g, ln2_b):
    assert x.shape[0] == 1 and w_mod.shape[0] == DEPTH
    _, s, d = x.shape
    n_ctx = ctx.shape[1]
    x2 = x[0]

    c2t = jnp.stack([c[0], c_ctx], axis=1)
    mod = _modulation(c2t, w_mod[0], b_mod[0].reshape(1, -1))
    sh1, sc1, g1, sh2, sc2, g2 = [mod[0:1, j * d:(j + 1) * d] for j in range(6)]
    csh1, csc1 = mod[1:2, 0:d], mod[1:2, d:2 * d]

    w_in_b = w_in[0].astype(BF16)
    qg = q_norm[0].reshape(1, HEAD_DIM)
    kg = k_norm[0].reshape(1, HEAD_DIM)
    cos, sin = _rope_tables(s)
    q_l, k_l, vt_l, xr_l, yr_l = _in_projection(x2, sc1, sh1, w_in_b, qg, kg, cos, sin,
                                                tm=min(512, s), kv_chunk=min(KV_CHUNK, s))
    _, k_c, vt_c, xr_c, _ = _in_projection(
        ctx[0], csc1, csh1, w_in_b, qg, kg,
        jnp.ones((n_ctx, HEAD_DIM), F32), jnp.zeros((n_ctx, HEAD_DIM), F32),
        tm=n_ctx, kv_chunk=n_ctx)

    attn = _attention(q_l, k_c, vt_c, k_l, vt_l, tq=min(256, s))

    wg = jnp.concatenate([rg_wa[0], rg_wx[0]], axis=-1).astype(BF16)
    cb = conv_b[0].reshape(1, RG_W)
    zero_state = jnp.zeros((1, RG_W), F32)
    rg_args = []
    for dd in range(2):
        rg_args.append((conv_w[0], cb, wg[dd],
                        jnp.stack([rg_ba[0, dd], rg_bx[0, dd]], axis=0),
                        rg_lam[0, dd].reshape(1, RG_W)))
    t_rg = min(512, s)
    hc_f = _rg_scan(xr_c, *rg_args[0], zero_state, reverse=False, t=n_ctx)
    hc_b = _rg_scan(xr_c, *rg_args[1], zero_state, reverse=True, t=n_ctx)
    h_f = _rg_scan(xr_l, *rg_args[0], hc_f[n_ctx - 1:n_ctx], reverse=False, t=t_rg)
    rg = _rg_scan(xr_l, *rg_args[1], hc_b[0:1], reverse=True, t=t_rg, hf=h_f, yr=yr_l)

    wr = jnp.pad(w_router[0], ((0, 0), (0, LANES - N_EXPERTS)))
    wr_hi = wr.astype(BF16)
    wr_lo = (wr - wr_hi.astype(F32)).astype(BF16)
    h1, v, idx, wts, rank, cnt = _out_projection(
        attn, rg, x2, w_out[0].astype(BF16), g1, ln1_g[0].reshape(1, d), ln1_b[0].reshape(1, d),
        sc2, sh2, wr_hi, wr_lo, e_bias[0].reshape(N_EXPERTS, 1), tm=min(512, s))

    bm = min(EXPERT_BLOCK, s)
    n_blocks = s * TOP_K // bm + N_EXPERTS
    pad_start, last_row, has_blk, blk_e, n_used = _block_tables(cnt[:, 0].astype(I32), n_blocks, bm)
    dest, xs = _dispatch(last_row, has_blk, v, idx, rank,
                         pad_start.astype(F32).reshape(N_EXPERTS, 1), n_blocks * bm,
                         tm=min(256, s), bm=bm)
    ys = _experts(blk_e, n_used, xs, w_e1[0], w_e3[0], w_e2[0], bm)

    out = _combine(dest, wts, v, h1, g2, ln2_g[0].reshape(1, d), ln2_b[0].reshape(1, d),
                   w_s1[0].astype(BF16), w_s3[0].astype(BF16), w_s2[0].astype(BF16), ys,
                   tm=min(256, s))
    return out[None]
```

```python
import functools

import jax
import jax.numpy as jnp
import numpy as np
from jax import lax
from jax.experimental import pallas as pl
from jax.experimental.pallas import tpu as pltpu

F32 = jnp.float32
BF16 = jnp.bfloat16
I32 = jnp.int32
U32 = jnp.uint32

GRID_W = 64
HEAD_DIM = 128
N_HEADS = 8
N_KV_HEADS = 2
GQA_GROUP = N_HEADS // N_KV_HEADS
ATTN_W = N_HEADS * HEAD_DIM
KV_W = N_KV_HEADS * HEAD_DIM
ROPE_THETA = 10000.0
RG_W = 1024
RG_HEADS = 8
RG_HD = RG_W // RG_HEADS
RG_C = 8.0
PROJ_W = ATTN_W + 2 * KV_W + 2 * RG_W
N_EXPERTS = 64
N_GROUPS = 8
GROUP_SIZE = N_EXPERTS // N_GROUPS
TOPK_GROUPS = 4
TOP_K = 8
ROUTED_SCALE = 2.5
NORM_EPS = 1e-6
DEPTH = 1
DEEPNORM_ALPHA = (2.0 * DEPTH) ** 0.25
LOG2E = 1.4426950408889634

LANES = 128
SUBLANES = 8
BF16_SUBLANES = 16
VMEM_LIMIT = 56 * 1024 * 1024

NEG_BIG = -1e30
KV_CHUNK = 512
ATTN_UNROLL = 16
VT_ROWS = HEAD_DIM + BF16_SUBLANES
EXPERT_BLOCK = 512


def _cparams(*sem):
    return pltpu.CompilerParams(dimension_semantics=sem, vmem_limit_bytes=VMEM_LIMIT)


def _const_spec(shape):
    nd = len(shape)
    return pl.BlockSpec(shape, lambda *_: (0,) * nd)


def _mod_kernel(ct_ref, w_ref, b_ref, o_ref, sb_ref, *, tn):
    d = w_ref.shape[0]

    @pl.when(pl.program_id(0) == 0)
    def _():
        ct = ct_ref[...]
        s = ct * jax.nn.sigmoid(ct)
        sb_ref[0] = jnp.broadcast_to(s[:, 0:1], (d, LANES))
        sb_ref[1] = jnp.broadcast_to(s[:, 1:2], (d, LANES))

    for c in range(tn // LANES):
        sl = slice(c * LANES, (c + 1) * LANES)
        wc = w_ref[:, sl]
        bc = b_ref[:, sl]
        o0 = jnp.sum(wc * sb_ref[0], axis=0, keepdims=True) + bc
        o1 = jnp.sum(wc * sb_ref[1], axis=0, keepdims=True) + bc
        o_ref[:, sl] = jnp.concatenate(
            [o0, o1, jnp.zeros((SUBLANES - 2, LANES), F32)], axis=0)


def _modulation(c2t, w_mod, b_mod):
    d, n = w_mod.shape
    tn = 1024
    return pl.pallas_call(
        functools.partial(_mod_kernel, tn=tn),
        grid=(n // tn,),
        in_specs=[_const_spec((d, 2)),
                  pl.BlockSpec((d, tn), lambda j: (0, j)),
                  pl.BlockSpec((1, tn), lambda j: (0, j))],
        out_specs=pl.BlockSpec((SUBLANES, tn), lambda j: (0, j)),
        out_shape=jax.ShapeDtypeStruct((SUBLANES, n), F32),
        scratch_shapes=[pltpu.VMEM((2, d, LANES), F32)],
        compiler_params=_cparams("arbitrary"),
        name="modulation",
    )(c2t, w_mod, b_mod)


def _swap_half(y):
    lane = lax.broadcasted_iota(I32, y.shape, 1)
    return jnp.where((lane % 64) < 32,
                     pltpu.roll(y, LANES - 32, 1), pltpu.roll(y, 32, 1))


def _norm_rope(ph, g, cos, sin_signed, scale):
    ms = jnp.mean(ph * ph, axis=-1, keepdims=True)
    y = (ph * lax.rsqrt(ms + NORM_EPS)) * g
    y = y * cos + _swap_half(y) * sin_signed
    if scale != 1.0:
        y = y * scale
    return y


def _inproj_kernel(x_ref, sc_ref, sh_ref, w_ref, qg_ref, kg_ref, cos_ref, sin_ref,
                   q_ref, k_ref, vt_ref, xr_ref, yr_ref, *, q_scale):
    tm = x_ref.shape[0]
    kc = vt_ref.shape[-1]
    u = (x_ref[...] * (1.0 + sc_ref[...]) + sh_ref[...]).astype(BF16)
    cos = cos_ref[...]
    sin = sin_ref[...]
    o = 0
    pq = jnp.dot(u, w_ref[:, o:o + ATTN_W], preferred_element_type=F32)
    for h in range(N_HEADS):
        sl = slice(h * HEAD_DIM, (h + 1) * HEAD_DIM)
        q_ref[:, sl] = _norm_rope(pq[:, sl], qg_ref[...], cos, sin, q_scale).astype(BF16)
    o += ATTN_W
    pk = jnp.dot(u, w_ref[:, o:o + KV_W], preferred_element_type=F32)
    for h in range(N_KV_HEADS):
        sl = slice(h * HEAD_DIM, (h + 1) * HEAD_DIM)
        k_ref[:, sl] = _norm_rope(pk[:, sl], kg_ref[...], cos, sin, 1.0).astype(BF16)
    o += KV_W
    pv = jnp.dot(u, w_ref[:, o:o + KV_W], preferred_element_type=F32)
    ones_rows = jnp.where(
        lax.broadcasted_iota(I32, (VT_ROWS - HEAD_DIM, kc), 0) == 0, 1.0, 0.0).astype(BF16)
    for h in range(N_KV_HEADS):
        for cc in range(tm // kc):
            blk = pv[cc * kc:(cc + 1) * kc, h * HEAD_DIM:(h + 1) * HEAD_DIM]
            vt_ref[h, cc, 0:HEAD_DIM, :] = blk.T.astype(BF16)
            vt_ref[h, cc, HEAD_DIM:VT_ROWS, :] = ones_rows
    o += KV_W
    xr_ref[...] = jnp.dot(u, w_ref[:, o:o + RG_W], preferred_element_type=F32)
    o += RG_W
    yr_ref[...] = jnp.dot(u, w_ref[:, o:o + RG_W], preferred_element_type=F32)


def _in_projection(x, sc, sh, w_in, qg, kg, cos, sin, tm, kv_chunk):
    s, d = x.shape
    row = lambda i: (i, 0)
    if kv_chunk >= tm:
        per = kv_chunk // tm
        vt_spec = pl.BlockSpec((N_KV_HEADS, 1, VT_ROWS, tm), lambda i: (0, i // per, 0, i % per))
    else:
        vt_spec = pl.BlockSpec((N_KV_HEADS, tm // kv_chunk, VT_ROWS, kv_chunk),
                               lambda i: (0, i, 0, 0))
    return pl.pallas_call(
        functools.partial(_inproj_kernel, q_scale=HEAD_DIM ** -0.5 * LOG2E),
        grid=(s // tm,),
        in_specs=[pl.BlockSpec((tm, d), row),
                  _const_spec((1, d)), _const_spec((1, d)),
                  pl.BlockSpec((d, PROJ_W), lambda i: (0, 0), pipeline_mode=pl.Buffered(1)),
                  _const_spec((1, HEAD_DIM)), _const_spec((1, HEAD_DIM)),
                  pl.BlockSpec((tm, HEAD_DIM), row), pl.BlockSpec((tm, HEAD_DIM), row)],
        out_specs=[pl.BlockSpec((tm, ATTN_W), row), pl.BlockSpec((tm, KV_W), row),
                   vt_spec,
                   pl.BlockSpec((tm, RG_W), row), pl.BlockSpec((tm, RG_W), row)],
        out_shape=[jax.ShapeDtypeStruct((s, ATTN_W), BF16),
                   jax.ShapeDtypeStruct((s, KV_W), BF16),
                   jax.ShapeDtypeStruct((N_KV_HEADS, s // kv_chunk, VT_ROWS, kv_chunk), BF16),
                   jax.ShapeDtypeStruct((s, RG_W), F32),
                   jax.ShapeDtypeStruct((s, RG_W), F32)],
        compiler_params=_cparams("arbitrary"),
        name="in_projection",
    )(x, sc, sh, w_in, qg, kg, cos, sin)


def _attn_kernel(q_ref, kc_ref, vtc_ref, kl_ref, vtl_ref, o_ref, sa_ref, sb_ref, pa_ref, pb_ref, sc_ref,
                 *, tq, tk, n_chunks, unroll):
    q = q_ref[...]
    qs = jnp.concatenate(
        [q[:, g * HEAD_DIM:(g + 1) * HEAD_DIM] for g in range(GQA_GROUP)], axis=0)
    cols = GQA_GROUP * tq

    def scores(k):
        return lax.dot_general(k, qs, (((1,), (1,)), ((), ())), preferred_element_type=F32)

    def softmax(s_ref, m):
        m_new = jnp.maximum(m, jnp.max(s_ref[...], axis=0, keepdims=True))
        return m_new, jnp.exp2(m - m_new), jnp.exp2(s_ref[...] - m_new).astype(BF16)

    def weighted_values(vt, p, alpha, acc):
        return alpha * acc + jnp.dot(vt, p, preferred_element_type=F32)

    m0 = jnp.full((1, cols), NEG_BIG, F32)
    a0 = jnp.zeros((VT_ROWS, cols), F32)
    sc_ref[...] = scores(kc_ref[...])
    m, alpha, p = softmax(sc_ref, m0)
    acc = weighted_values(vtc_ref[0], p, alpha, a0)

    def latent_scores(j):
        j = jnp.minimum(j, n_chunks - 1)
        return scores(kl_ref[pl.ds(pl.multiple_of(j * tk, tk), tk), :])

    s_refs = (sa_ref, sb_ref)
    p_refs = (pa_ref, pb_ref)
    sa_ref[...] = latent_scores(0)
    sb_ref[...] = latent_scores(1)
    m, alpha, pa_ref[...] = softmax(sa_ref, m)

    def body(i, carry):
        m, acc, alpha = carry
        for u in range(unroll):
            c = unroll * i + u
            cur, nxt = u % 2, (u + 1) % 2
            s_refs[cur][...] = latent_scores(c + 2)
            m, alpha_next, p_refs[nxt][...] = softmax(s_refs[nxt], m)
            acc = weighted_values(vtl_ref[c], p_refs[cur][...], alpha, acc)
            alpha = alpha_next
        return m, acc, alpha

    _, acc, _ = lax.fori_loop(0, n_chunks // unroll, body, (m, acc, alpha))
    out_t = acc[0:HEAD_DIM, :] / acc[HEAD_DIM:HEAD_DIM + 1, :]
    for g in range(GQA_GROUP):
        o_ref[:, g * HEAD_DIM:(g + 1) * HEAD_DIM] = out_t[:, g * tq:(g + 1) * tq].T.astype(BF16)


def _attention(q, k_c, vt_c, k_l, vt_l, tq):
    s = q.shape[0]
    n_ctx = k_c.shape[0]
    n_chunks, _, tk = vt_l.shape[1:]
    unroll = min(ATTN_UNROLL, n_chunks)
    assert n_chunks % unroll == 0 and unroll % 2 == 0
    gw = GQA_GROUP * HEAD_DIM
    return pl.pallas_call(
        functools.partial(_attn_kernel, tq=tq, tk=tk, n_chunks=n_chunks, unroll=unroll),
        grid=(N_KV_HEADS, s // tq),
        in_specs=[pl.BlockSpec((tq, gw), lambda h, i: (i, h)),
                  pl.BlockSpec((n_ctx, HEAD_DIM), lambda h, i: (0, h)),
                  pl.BlockSpec((None, 1, VT_ROWS, n_ctx), lambda h, i: (h, 0, 0, 0)),
                  pl.BlockSpec((s, HEAD_DIM), lambda h, i: (0, h)),
                  pl.BlockSpec((None, n_chunks, VT_ROWS, tk), lambda h, i: (h, 0, 0, 0))],
        out_specs=pl.BlockSpec((tq, gw), lambda h, i: (i, h)),
        out_shape=jax.ShapeDtypeStruct((s, ATTN_W), BF16),
        scratch_shapes=[pltpu.VMEM((tk, GQA_GROUP * tq), F32),
                        pltpu.VMEM((tk, GQA_GROUP * tq), F32),
                        pltpu.VMEM((tk, GQA_GROUP * tq), BF16),
                        pltpu.VMEM((tk, GQA_GROUP * tq), BF16),
                        pltpu.VMEM((n_ctx, GQA_GROUP * tq), F32)],
        compiler_params=_cparams("arbitrary", "arbitrary"),
        name="attention",
    )(q, k_c, vt_c, k_l, vt_l)


def _log_sigmoid(x):
    return jnp.minimum(x, 0.0) - jnp.log1p(jnp.exp(-jnp.abs(x)))


def _rg_kernel(x_ref, xp_ref, xn_ref, cw_ref, cb_ref, wg_ref, bg_ref, lam_ref, h0_ref,
               *rest, reverse, final, t, n_chunks):
    if final:
        hf_ref, yr_ref, o_ref, a_scr, b_scr, hc_scr, h_scr = rest
    else:
        o_ref, a_scr, b_scr, hc_scr = rest
        h_scr = o_ref
    i = pl.program_id(0)
    c = (n_chunks - 1 - i) if reverse else i
    w = x_ref.shape[1]

    @pl.when(i == 0)
    def _():
        hc_scr[...] = jnp.broadcast_to(h0_ref[...], (SUBLANES, w))

    x = x_ref[...]
    row = lax.broadcasted_iota(I32, (t, w), 0)
    pm = jnp.where(c == 0, 0.0, 1.0).astype(F32)
    nm = jnp.where(c == n_chunks - 1, 0.0, 1.0).astype(F32)
    p6 = xp_ref[SUBLANES - 2:SUBLANES - 1, :] * pm
    p7 = xp_ref[SUBLANES - 1:SUBLANES, :] * pm
    n0 = xn_ref[0:1, :] * nm
    x_m1 = jnp.where(row == 0, p7, pltpu.roll(x, 1, 0))
    x_m2 = jnp.where(row == 0, p6, jnp.where(row == 1, p7, pltpu.roll(x, 2, 0)))
    x_p1 = jnp.where(row == t - 1, n0, pltpu.roll(x, t - 1, 0))
    xc = cb_ref[...] + cw_ref[0:1, :] * x_m2
    xc = xc + cw_ref[1:2, :] * x_m1
    xc = xc + cw_ref[2:3, :] * x
    xc = xc + cw_ref[3:4, :] * x_p1

    xcb = xc.astype(BF16)
    clam = RG_C * _log_sigmoid(lam_ref[...])
    for h in range(RG_HEADS):
        sl = slice(h * RG_HD, (h + 1) * RG_HD)
        g = jnp.dot(xcb[:, sl], wg_ref[h], preferred_element_type=F32)
        r = jax.nn.sigmoid(g[:, :RG_HD] + bg_ref[0:1, sl])
        gi = jax.nn.sigmoid(g[:, RG_HD:] + bg_ref[1:2, sl])
        log_a = r * clam[:, sl]
        a = jnp.exp(log_a)
        a_scr[:, sl] = a
        b_scr[:, sl] = jnp.sqrt(-jnp.tanh(log_a) * (a * a + 1.0)) * (gi * xc[:, sl])

    srow = lax.broadcasted_iota(I32, (SUBLANES, w), 0)
    n_tiles = t // SUBLANES

    def tile_body(j, hprev):
        tile = (n_tiles - 1 - j) if reverse else j
        start = pl.multiple_of(tile * SUBLANES, SUBLANES)
        a = a_scr[pl.ds(start, SUBLANES), :]
        b = b_scr[pl.ds(start, SUBLANES), :]
        for k in (1, 2, 4):
            if reverse:
                keep = srow < SUBLANES - k
                shift = SUBLANES - k
            else:
                keep = srow >= k
                shift = k
            a_sh = jnp.where(keep, pltpu.roll(a, shift, 0), 1.0)
            b_sh = jnp.where(keep, pltpu.roll(b, shift, 0), 0.0)
            b = a * b_sh + b
            a = a * a_sh
        hh = a * hprev + b
        h_scr[pl.ds(start, SUBLANES), :] = hh
        last = hh[0:1, :] if reverse else hh[SUBLANES - 1:SUBLANES, :]
        return jnp.broadcast_to(last, (SUBLANES, w))

    hc_scr[...] = lax.fori_loop(0, n_tiles, tile_body, hc_scr[...])

    if final:
        gate = jax.nn.gelu(yr_ref[...], approximate=True)
        o_ref[...] = ((hf_ref[...] + h_scr[...]) * gate).astype(o_ref.dtype)


def _rg_scan(xr, conv_w, conv_b, wg, bg, lam, h0, *, reverse, t, hf=None, yr=None):
    s, w = xr.shape
    n_chunks = s // t
    final = hf is not None
    tb = t // SUBLANES
    last_blk = s // SUBLANES - 1
    if reverse:
        cidx = lambda i: n_chunks - 1 - i
    else:
        cidx = lambda i: i
    chunk_spec = pl.BlockSpec((t, w), lambda i: (cidx(i), 0))
    in_specs = [chunk_spec,
                pl.BlockSpec((SUBLANES, w), lambda i: (jnp.maximum(cidx(i) * tb - 1, 0), 0)),
                pl.BlockSpec((SUBLANES, w), lambda i: (jnp.minimum((cidx(i) + 1) * tb, last_blk), 0)),
                _const_spec((4, w)), _const_spec((1, w)),
                _const_spec((RG_HEADS, RG_HD, 2 * RG_HD)), _const_spec((2, w)),
                _const_spec((1, w)), _const_spec((1, w))]
    args = [xr, xr, xr, conv_w, conv_b, wg, bg, lam, h0]
    scratch = [pltpu.VMEM((t, w), F32), pltpu.VMEM((t, w), F32), pltpu.VMEM((SUBLANES, w), F32)]
    if final:
        in_specs += [chunk_spec, chunk_spec]
        args += [hf, yr]
        scratch.append(pltpu.VMEM((t, w), F32))
        out_dtype = BF16
    else:
        out_dtype = F32
    return pl.pallas_call(
        functools.partial(_rg_kernel, reverse=reverse, final=final, t=t, n_chunks=n_chunks),
        grid=(n_chunks,),
        in_specs=in_specs,
        out_specs=chunk_spec,
        out_shape=jax.ShapeDtypeStruct((s, w), out_dtype),
        scratch_shapes=scratch,
        compiler_params=_cparams("arbitrary"),
        name="rglru_bwd" if reverse else "rglru_fwd",
    )(*args)


def _layer_norm(y, g, b):
    mu = jnp.mean(y, axis=-1, keepdims=True)
    yc = y - mu
    var = jnp.mean(yc * yc, axis=-1, keepdims=True)
    return yc * lax.rsqrt(var + NORM_EPS) * g + b


def _outproj_kernel(attn_ref, rg_ref, x_ref, w_ref, g1_ref, lg_ref, lb_ref, sc2_ref, sh2_ref,
                    h1_ref, v_ref):
    mix = jnp.dot(attn_ref[...], w_ref[0:ATTN_W, :], preferred_element_type=F32)
    mix = mix + jnp.dot(rg_ref[...], w_ref[ATTN_W:, :], preferred_element_type=F32)
    h1 = _layer_norm(DEEPNORM_ALPHA * x_ref[...] + g1_ref[...] * mix, lg_ref[...], lb_ref[...])
    h1_ref[...] = h1
    v_ref[...] = h1 * (1.0 + sc2_ref[...]) + sh2_ref[...]


def _out_projection(attn, rg, x, w_out, g1, ln_g, ln_b, sc2, sh2, tm):
    s, d = x.shape
    row = lambda i: (i, 0)
    vec = _const_spec((1, d))
    return pl.pallas_call(
        _outproj_kernel,
        grid=(s // tm,),
        in_specs=[pl.BlockSpec((tm, ATTN_W), row), pl.BlockSpec((tm, RG_W), row),
                  pl.BlockSpec((tm, d), row),
                  pl.BlockSpec((ATTN_W + RG_W, d), lambda i: (0, 0), pipeline_mode=pl.Buffered(1)),
                  vec, vec, vec, vec, vec],
        out_specs=[pl.BlockSpec((tm, d), row), pl.BlockSpec((tm, d), row)],
        out_shape=[jax.ShapeDtypeStruct((s, d), F32), jax.ShapeDtypeStruct((s, d), F32)],
        compiler_params=_cparams("arbitrary"),
        name="out_projection",
    )(attn, rg, x, w_out, g1, ln_g, ln_b, sc2, sh2)


def _first_index_of_max(x, iota_f, axis):
    mx = jnp.max(x, axis=axis, keepdims=True)
    idx = jnp.min(jnp.where(x == mx, iota_f, float(N_EXPERTS)), axis=axis, keepdims=True)
    return mx, idx


def _router_kernel(v_ref, whi_ref, wlo_ref, eb_ref, tri_ref,
                   idx_ref, wts_ref, rank_ref, cnt_ref, base_scr):
    v = v_ref[...]
    tm = v.shape[0]

    @pl.when(pl.program_id(0) == 0)
    def _():
        base_scr[...] = jnp.zeros(base_scr.shape, F32)

    v_hi = v.astype(BF16)
    v_lo = (v - v_hi.astype(F32)).astype(BF16)
    logits = jnp.dot(v_hi, whi_ref[...], preferred_element_type=F32)
    logits = logits + jnp.dot(v_lo, whi_ref[...], preferred_element_type=F32)
    logits = logits + jnp.dot(v_hi, wlo_ref[...], preferred_element_type=F32)
    lt = logits.T[0:N_EXPERTS, :]
    scores = jax.nn.sigmoid(lt)
    biased = scores + eb_ref[...]
    neg_inf = float("-inf")

    ig = lax.broadcasted_iota(I32, (GROUP_SIZE, tm), 0).astype(F32)
    groups = [biased[g * GROUP_SIZE:(g + 1) * GROUP_SIZE, :] for g in range(N_GROUPS)]
    gscore = []
    for bg in groups:
        top1, i1 = _first_index_of_max(bg, ig, 0)
        top2 = jnp.max(jnp.where(ig == i1, neg_inf, bg), axis=0, keepdims=True)
        gscore.append(top1 + top2)

    masked = []
    for g in range(N_GROUPS):
        ahead = jnp.zeros((1, tm), F32)
        for o in range(N_GROUPS):
            if o == g:
                continue
            before = (gscore[o] >= gscore[g]) if o < g else (gscore[o] > gscore[g])
            ahead = ahead + jnp.where(before, 1.0, 0.0)
        keep = jnp.broadcast_to(ahead < TOPK_GROUPS, (GROUP_SIZE, tm))
        masked.append(jnp.where(keep, groups[g], neg_inf))
    masked = jnp.concatenate(masked, axis=0)

    ie = lax.broadcasted_iota(I32, masked.shape, 0).astype(F32)
    seen = base_scr[...]
    ws = []
    for k in range(TOP_K):
        _, ei = _first_index_of_max(masked, ie, 0)
        hit = ie == ei
        idx_ref[k:k + 1, :] = ei.astype(I32)
        ws.append(jnp.sum(jnp.where(hit, scores, 0.0), axis=0, keepdims=True))
        masked = jnp.where(hit, neg_inf, masked)
        onehot = jnp.where(hit, 1.0, 0.0)
        before = jnp.dot(onehot.astype(BF16), tri_ref[...], preferred_element_type=F32)
        rank = jnp.sum(jnp.where(hit, before + seen, 0.0), axis=0, keepdims=True)
        rank_ref[k:k + 1, :] = rank.astype(I32)
        seen = seen + jnp.sum(onehot, axis=1, keepdims=True)
    base_scr[...] = seen
    cnt_ref[...] = jnp.broadcast_to(seen, cnt_ref.shape)
    total = ws[0]
    for k in range(1, TOP_K):
        total = total + ws[k]
    for k in range(TOP_K):
        wts_ref[k:k + 1, :] = ws[k] / total * ROUTED_SCALE


def _router(v, w_hi, w_lo, e_bias_col, tm):
    s, d = v.shape
    tri = jnp.triu(jnp.ones((tm, tm), BF16), k=1)
    slot = pl.BlockSpec((TOP_K, tm), lambda i: (0, i))
    return pl.pallas_call(
        _router_kernel,
        grid=(s // tm,),
        in_specs=[pl.BlockSpec((tm, d), lambda i: (i, 0)),
                  _const_spec((d, LANES)), _const_spec((d, LANES)),
                  _const_spec((N_EXPERTS, 1)), _const_spec((tm, tm))],
        out_specs=[slot, slot, slot, _const_spec((N_EXPERTS, LANES))],
        out_shape=[jax.ShapeDtypeStruct((TOP_K, s), I32),
                   jax.ShapeDtypeStruct((TOP_K, s), F32),
                   jax.ShapeDtypeStruct((TOP_K, s), I32),
                   jax.ShapeDtypeStruct((N_EXPERTS, LANES), F32)],
        scratch_shapes=[pltpu.VMEM((N_EXPERTS, 1), F32)],
        compiler_params=_cparams("arbitrary"),
        name="router",
    )(v, w_hi, w_lo, e_bias_col, tri)


def _pack_pairs(x):
    h = x.shape[1] // 2
    lo = lax.bitcast_convert_type(x[:, :h].astype(BF16).astype(F32), U32)
    hi = lax.bitcast_convert_type(x[:, h:].astype(BF16).astype(F32), U32)
    return (lo >> 16) | (hi & jnp.uint32(0xFFFF0000))


def _unpack_pairs(w):
    lo = lax.bitcast_convert_type(w << 16, F32)
    hi = lax.bitcast_convert_type(w & jnp.uint32(0xFFFF0000), F32)
    return jnp.concatenate([lo, hi], axis=1)


def _dispatch_kernel(last_ref, has_ref, v_ref, idx_ref, rank_ref, pstart_ref,
                     dest_ref, xs_hbm, dest_v, dest_s, zeros_v, pk_v, sem_z, sem_i, sem_r, *, bm):
    tm = v_ref.shape[0]

    def zero_fill(e):
        row0 = pl.multiple_of(last_ref[e], bm)
        return pltpu.make_async_copy(zeros_v, xs_hbm.at[pl.ds(row0, bm), :], sem_z)

    @pl.when(pl.program_id(0) == 0)
    def _():
        zeros_v[...] = jnp.zeros(zeros_v.shape, zeros_v.dtype)

        def start(e, carry):
            @pl.when(has_ref[e] > 0)
            def _():
                zero_fill(e).start()
            return carry

        def wait(e, carry):
            @pl.when(has_ref[e] > 0)
            def _():
                zero_fill(e).wait()
            return carry

        lax.fori_loop(0, N_EXPERTS, start, 0)
        lax.fori_loop(0, N_EXPERTS, wait, 0)

    ie = lax.broadcasted_iota(I32, (N_EXPERTS, tm), 0)
    pstart = pstart_ref[...]
    for k in range(TOP_K):
        hit = ie == idx_ref[k:k + 1, :]
        seg = jnp.sum(jnp.where(hit, pstart, 0.0), axis=0, keepdims=True)
        dest_v[k:k + 1, :] = seg.astype(I32) + rank_ref[k:k + 1, :]
    dest_ref[...] = dest_v[...]
    cp = pltpu.make_async_copy(dest_v, dest_s, sem_i)
    cp.start()
    pk_v[...] = _pack_pairs(v_ref[...])
    cp.wait()

    for t in range(tm):
        for k in range(TOP_K):
            pltpu.make_async_copy(pk_v.at[pl.ds(t, 1), :],
                                  xs_hbm.at[pl.ds(dest_s[k, t], 1), :], sem_r).start(priority=k % 2)
    for k in range(TOP_K):
        pltpu.make_async_copy(pk_v, xs_hbm.at[pl.ds(0, tm), :], sem_r).wait()


def _dispatch(last_row, has_blk, v, idx, rank, pstart_col, n_rows, tm, bm):
    s, d = v.shape
    slot = pl.BlockSpec((TOP_K, tm), lambda i, *_: (0, i))
    grid_spec = pltpu.PrefetchScalarGridSpec(
        num_scalar_prefetch=2,
        grid=(s // tm,),
        in_specs=[pl.BlockSpec((tm, d), lambda i, *_: (i, 0)), slot, slot,
                  pl.BlockSpec((N_EXPERTS, 1), lambda i, *_: (0, 0))],
        out_specs=[slot, pl.BlockSpec(memory_space=pl.ANY)],
        scratch_shapes=[pltpu.VMEM((TOP_K, tm), I32), pltpu.SMEM((TOP_K, tm), I32),
                        pltpu.VMEM((bm, d // 2), U32), pltpu.VMEM((tm, d // 2), U32),
                        pltpu.SemaphoreType.DMA, pltpu.SemaphoreType.DMA,
                        pltpu.SemaphoreType.DMA])
    return pl.pallas_call(
        functools.partial(_dispatch_kernel, bm=bm),
        grid_spec=grid_spec,
        out_shape=[jax.ShapeDtypeStruct((TOP_K, s), I32),
                   jax.ShapeDtypeStruct((n_rows, d // 2), U32)],
        compiler_params=_cparams("arbitrary"),
        name="dispatch",
    )(last_row, has_blk, v, idx, rank, pstart_col)


def _expert_kernel(blk_e_ref, n_used_ref, x_ref, w1_ref, w3_ref, w2_ref, y_ref, w1b, w3b, w2b):
    b = pl.program_id(0)

    @pl.when(b < n_used_ref[0])
    def _():
        prev_e = blk_e_ref[jnp.maximum(b - 1, 0)]

        @pl.when(jnp.logical_or(b == 0, blk_e_ref[b] != prev_e))
        def _():
            w1b[...] = w1_ref[...].astype(BF16)
            w3b[...] = w3_ref[...].astype(BF16)
            w2b[...] = w2_ref[...].astype(BF16)

        xb = _unpack_pairs(x_ref[...]).astype(BF16)
        h1 = jnp.dot(xb, w1b[...], preferred_element_type=F32)
        h3 = jnp.dot(xb, w3b[...], preferred_element_type=F32)
        act = (h1 * jax.nn.sigmoid(h1) * h3).astype(BF16)
        y_ref[...] = _pack_pairs(jnp.dot(act, w2b[...], preferred_element_type=F32))


def _experts(blk_e, n_used, xs, w_e1, w_e3, w_e2, bm):
    n_rows = xs.shape[0]
    d, ff = w_e1.shape[-2:]
    rows = lambda b, be, nu: (jnp.minimum(b, nu[0] - 1), 0)
    wspec1 = pl.BlockSpec((None, d, ff), lambda b, be, nu: (be[b], 0, 0))
    wspec2 = pl.BlockSpec((None, ff, d), lambda b, be, nu: (be[b], 0, 0))
    grid_spec = pltpu.PrefetchScalarGridSpec(
        num_scalar_prefetch=2,
        grid=(n_rows // bm,),
        in_specs=[pl.BlockSpec((bm, d // 2), rows), wspec1, wspec1, wspec2],
        out_specs=pl.BlockSpec((bm, d // 2), rows),
        scratch_shapes=[pltpu.VMEM((d, ff), BF16), pltpu.VMEM((d, ff), BF16),
                        pltpu.VMEM((ff, d), BF16)])
    return pl.pallas_call(
        _expert_kernel,
        grid_spec=grid_spec,
        out_shape=jax.ShapeDtypeStruct((n_rows, d // 2), U32),
        compiler_params=_cparams("arbitrary"),
        name="experts",
    )(blk_e, n_used, xs, w_e1, w_e3, w_e2)


def _block_tables(counts, n_blocks, bm):
    padded = (counts + bm - 1) // bm * bm
    pad_end = jnp.cumsum(padded)
    pad_start = pad_end - padded
    blk_first = jnp.arange(n_blocks, dtype=I32) * bm
    blk_e = jnp.minimum(jnp.sum(pad_end[None, :] <= blk_first[:, None], axis=1), N_EXPERTS - 1)
    n_used = pad_end[-1] // bm
    blk_e = jnp.where(jnp.arange(n_blocks) < n_used, blk_e, blk_e[jnp.maximum(n_used - 1, 0)])
    last_row = jnp.maximum(pad_end - bm, 0)
    return (pad_start, last_row.astype(I32), (padded > 0).astype(I32),
            blk_e.astype(I32), n_used.astype(I32).reshape(1))


def _combine_kernel(dest0_ref, dnext_ref, wts_ref, v_ref, h1_ref, g2_ref, lg_ref, lb_ref,
                    ws1_ref, ws3_ref, ws2_ref, ys_hbm, o_ref, dest_s, ybuf_a, ybuf_b, sem_i, sem_r):
    tm = v_ref.shape[0]
    i = pl.program_id(0)
    ybufs = (ybuf_a, ybuf_b)

    def start_gathers(slot):
        for t in range(tm):
            for k in range(TOP_K):
                pltpu.make_async_copy(ys_hbm.at[pl.ds(dest_s[slot, k, t], 1), :],
                                      ybufs[slot].at[k, pl.ds(t, 1), :],
                                      sem_r.at[slot]).start(priority=k % 2)

    def wait_gathers(slot):
        for k in range(TOP_K):
            pltpu.make_async_copy(ys_hbm.at[pl.ds(0, tm), :], ybufs[slot].at[k], sem_r.at[slot]).wait()

    @pl.when(i == 0)
    def _():
        cp = pltpu.make_async_copy(dest0_ref, dest_s.at[0], sem_i)
        cp.start()
        cp.wait()

        def first_tile(t, carry):
            for k in range(TOP_K):
                pltpu.make_async_copy(ys_hbm.at[pl.ds(dest_s[0, k, t], 1), :],
                                      ybuf_a.at[k, pl.ds(t, 1), :], sem_r.at[0]).start()
            return carry

        lax.fori_loop(0, tm, first_tile, 0)

    def tile(slot):
        other = 1 - slot
        cp = pltpu.make_async_copy(dnext_ref, dest_s.at[other], sem_i)
        cp.start()
        cp.wait()
        wait_gathers(slot)
        start_gathers(other)
        vb = v_ref[...].astype(BF16)
        a1 = jnp.dot(vb, ws1_ref[...], preferred_element_type=F32)
        a3 = jnp.dot(vb, ws3_ref[...], preferred_element_type=F32)
        act = (a1 * jax.nn.sigmoid(a1) * a3).astype(BF16)
        ff = jnp.dot(act, ws2_ref[...], preferred_element_type=F32)
        wpad = jnp.concatenate([wts_ref[...], jnp.zeros((LANES - TOP_K, tm), F32)], axis=0)
        wt = wpad.T
        for k in range(TOP_K):
            ff = ff + _unpack_pairs(ybufs[slot][k]) * wt[:, k:k + 1]
        o_ref[...] = _layer_norm(DEEPNORM_ALPHA * h1_ref[...] + g2_ref[...] * ff,
                                 lg_ref[...], lb_ref[...])

        @pl.when(i == pl.num_programs(0) - 1)
        def _():
            wait_gathers(other)

    @pl.when(i % 2 == 0)
    def _():
        tile(0)

    @pl.when(i % 2 == 1)
    def _():
        tile(1)


def _combine(dest, wts, v, h1, g2, ln_g, ln_b, w_s1, w_s3, w_s2, ys, tm):
    s, d = v.shape
    n = s // tm
    row = lambda i: (i, 0)
    vec = _const_spec((1, d))
    slot = pl.BlockSpec((TOP_K, tm), lambda i: (0, i))
    single = lambda shape: pl.BlockSpec(shape, lambda i: (0, 0), pipeline_mode=pl.Buffered(1))
    return pl.pallas_call(
        _combine_kernel,
        grid=(n,),
        in_specs=[pl.BlockSpec((TOP_K, tm), lambda i: (0, 0)),
                  pl.BlockSpec((TOP_K, tm), lambda i: (0, jnp.minimum(i + 1, n - 1))),
                  slot, pl.BlockSpec((tm, d), row), pl.BlockSpec((tm, d), row),
                  vec, vec, vec,
                  single(w_s1.shape), single(w_s3.shape), single(w_s2.shape),
                  pl.BlockSpec(memory_space=pl.ANY)],
        out_specs=pl.BlockSpec((tm, d), row),
        out_shape=jax.ShapeDtypeStruct((s, d), F32),
        scratch_shapes=[pltpu.SMEM((2, TOP_K, tm), I32),
                        pltpu.VMEM((TOP_K, tm, d // 2), U32), pltpu.VMEM((TOP_K, tm, d // 2), U32),
                        pltpu.SemaphoreType.DMA, pltpu.SemaphoreType.DMA((2,))],
        compiler_params=_cparams("arbitrary"),
        name="combine",
    )(dest, dest, wts, v, h1, g2, ln_g, ln_b, w_s1, w_s3, w_s2, ys)


def _rope_tables(s):
    half = HEAD_DIM // 4
    inv_freq = ROPE_THETA ** (-np.arange(half, dtype=np.float32) / half)
    t = np.arange(s)
    ang_r = (t // GRID_W).astype(np.float32)[:, None] * inv_freq[None, :]
    ang_c = (t % GRID_W).astype(np.float32)[:, None] * inv_freq[None, :]
    cos = np.concatenate([np.cos(ang_r)] * 2 + [np.cos(ang_c)] * 2, axis=-1)
    sin = np.concatenate([-np.sin(ang_r), np.sin(ang_r), -np.sin(ang_c), np.sin(ang_c)], axis=-1)
    return jnp.asarray(cos, F32), jnp.asarray(sin, F32)


def kernel(x, c, ctx, c_ctx, w_mod, b_mod, w_in, q_norm, k_norm, conv_w, conv_b, rg_wa, rg_ba,
           rg_wx, rg_bx, rg_lam, w_out, ln1_g, ln1_b, w_router, e_bias, w_e1, w_e3, w_e2,
           w_s1, w_s3, w_s2, ln2_g, ln2_b):
    assert x.shape[0] == 1 and w_mod.shape[0] == DEPTH
    _, s, d = x.shape
    n_ctx = ctx.shape[1]
    x2 = x[0]

    c2t = jnp.stack([c[0], c_ctx], axis=1)
    mod = _modulation(c2t, w_mod[0], b_mod[0].reshape(1, -1))
    sh1, sc1, g1, sh2, sc2, g2 = [mod[0:1, j * d:(j + 1) * d] for j in range(6)]
    csh1, csc1 = mod[1:2, 0:d], mod[1:2, d:2 * d]

    w_in_b = w_in[0].astype(BF16)
    qg = q_norm[0].reshape(1, HEAD_DIM)
    kg = k_norm[0].reshape(1, HEAD_DIM)
    cos, sin = _rope_tables(s)
    q_l, k_l, vt_l, xr_l, yr_l = _in_projection(x2, sc1, sh1, w_in_b, qg, kg, cos, sin,
                                                tm=min(512, s), kv_chunk=min(KV_CHUNK, s))
    _, k_c, vt_c, xr_c, _ = _in_projection(
        ctx[0], csc1, csh1, w_in_b, qg, kg,
        jnp.ones((n_ctx, HEAD_DIM), F32), jnp.zeros((n_ctx, HEAD_DIM), F32),
        tm=n_ctx, kv_chunk=n_ctx)

    attn = _attention(q_l, k_c, vt_c, k_l, vt_l, tq=min(256, s))

    wg = jnp.concatenate([rg_wa[0], rg_wx[0]], axis=-1).astype(BF16)
    cb = conv_b[0].reshape(1, RG_W)
    zero_state = jnp.zeros((1, RG_W), F32)
    rg_args = []
    for dd in range(2):
        rg_args.append((conv_w[0], cb, wg[dd],
                        jnp.stack([rg_ba[0, dd], rg_bx[0, dd]], axis=0),
                        rg_lam[0, dd].reshape(1, RG_W)))
    t_rg = min(512, s)
    hc_f = _rg_scan(xr_c, *rg_args[0], zero_state, reverse=False, t=n_ctx)
    hc_b = _rg_scan(xr_c, *rg_args[1], zero_state, reverse=True, t=n_ctx)
    h_f = _rg_scan(xr_l, *rg_args[0], hc_f[n_ctx - 1:n_ctx], reverse=False, t=t_rg)
    rg = _rg_scan(xr_l, *rg_args[1], hc_b[0:1], reverse=True, t=t_rg, hf=h_f, yr=yr_l)

    h1, v = _out_projection(attn, rg, x2, w_out[0].astype(BF16), g1,
                            ln1_g[0].reshape(1, d), ln1_b[0].reshape(1, d), sc2, sh2,
                            tm=min(256, s))

    wr = jnp.pad(w_router[0], ((0, 0), (0, LANES - N_EXPERTS)))
    wr_hi = wr.astype(BF16)
    wr_lo = (wr - wr_hi.astype(F32)).astype(BF16)
    idx, wts, rank, cnt = _router(v, wr_hi, wr_lo, e_bias[0].reshape(N_EXPERTS, 1), tm=min(256, s))

    bm = min(EXPERT_BLOCK, s)
    n_blocks = s * TOP_K // bm + N_EXPERTS
    pad_start, last_row, has_blk, blk_e, n_used = _block_tables(cnt[:, 0].astype(I32), n_blocks, bm)
    dest, xs = _dispatch(last_row, has_blk, v, idx, rank,
                         pad_start.astype(F32).reshape(N_EXPERTS, 1), n_blocks * bm,
                         tm=min(256, s), bm=bm)
    ys = _experts(blk_e, n_used, xs, w_e1[0], w_e3[0], w_e2[0], bm)

    out = _combine(dest, wts, v, h1, g2, ln2_g[0].reshape(1, d), ln2_b[0].reshape(1, d),
                   w_s1[0].astype(BF16), w_s3[0].astype(BF16), w_s2[0].astype(BF16), ys,
                   tm=min(256, s))
    return out[None]
```

```python
import functools

import jax
import jax.numpy as jnp
import numpy as np
from jax import lax
from jax.experimental import pallas as pl
from jax.experimental.pallas import tpu as pltpu

F32 = jnp.float32
BF16 = jnp.bfloat16
I32 = jnp.int32
U32 = jnp.uint32

GRID_W = 64
HEAD_DIM = 128
N_HEADS = 8
N_KV_HEADS = 2
GQA_GROUP = N_HEADS // N_KV_HEADS
ATTN_W = N_HEADS * HEAD_DIM
KV_W = N_KV_HEADS * HEAD_DIM
ROPE_THETA = 10000.0
RG_W = 1024
RG_HEADS = 8
RG_HD = RG_W // RG_HEADS
RG_C = 8.0
PROJ_W = ATTN_W + 2 * KV_W + 2 * RG_W
N_EXPERTS = 64
N_GROUPS = 8
GROUP_SIZE = N_EXPERTS // N_GROUPS
TOPK_GROUPS = 4
TOP_K = 8
ROUTED_SCALE = 2.5
NORM_EPS = 1e-6
DEPTH = 1
DEEPNORM_ALPHA = (2.0 * DEPTH) ** 0.25
LOG2E = 1.4426950408889634

LANES = 128
SUBLANES = 8
BF16_SUBLANES = 16
VMEM_LIMIT = 56 * 1024 * 1024

NEG_BIG = -1e30
ROPE_AXIS_DIM = HEAD_DIM // 2
VT_ROWS = HEAD_DIM + BF16_SUBLANES

MOD_COLS = 1024
INPROJ_ROWS = 512
ATTN_QUERIES = 256
KV_CHUNK = 512
ATTN_UNROLL = 16
RG_ROWS = 512
OUTPROJ_ROWS = 256
ROUTER_ROWS = 256
EXPERT_BLOCK = 512
DISPATCH_ROWS = 256
COMBINE_ROWS = 256


def _cparams(*sem):
    return pltpu.CompilerParams(dimension_semantics=sem, vmem_limit_bytes=VMEM_LIMIT)


def _const_spec(shape):
    nd = len(shape)
    return pl.BlockSpec(shape, lambda *_: (0,) * nd)


def _mod_kernel(ct_ref, w_ref, b_ref, o_ref, sb_ref, *, tn):
    d = w_ref.shape[0]

    @pl.when(pl.program_id(0) == 0)
    def _():
        ct = ct_ref[...]
        s = ct * jax.nn.sigmoid(ct)
        sb_ref[0] = jnp.broadcast_to(s[:, 0:1], (d, LANES))
        sb_ref[1] = jnp.broadcast_to(s[:, 1:2], (d, LANES))

    for c in range(tn // LANES):
        sl = slice(c * LANES, (c + 1) * LANES)
        wc = w_ref[:, sl]
        bc = b_ref[:, sl]
        o0 = jnp.sum(wc * sb_ref[0], axis=0, keepdims=True) + bc
        o1 = jnp.sum(wc * sb_ref[1], axis=0, keepdims=True) + bc
        o_ref[:, sl] = jnp.concatenate(
            [o0, o1, jnp.zeros((SUBLANES - 2, LANES), F32)], axis=0)


def _modulation(c2t, w_mod, b_mod):
    d, n = w_mod.shape
    tn = MOD_COLS
    return pl.pallas_call(
        functools.partial(_mod_kernel, tn=tn),
        grid=(n // tn,),
        in_specs=[_const_spec((d, 2)),
                  pl.BlockSpec((d, tn), lambda j: (0, j)),
                  pl.BlockSpec((1, tn), lambda j: (0, j))],
        out_specs=pl.BlockSpec((SUBLANES, tn), lambda j: (0, j)),
        out_shape=jax.ShapeDtypeStruct((SUBLANES, n), F32),
        scratch_shapes=[pltpu.VMEM((2, d, LANES), F32)],
        compiler_params=_cparams("arbitrary"),
        name="modulation",
    )(c2t, w_mod, b_mod)


def _swap_half(y):
    half = ROPE_AXIS_DIM // 2
    lane = lax.broadcasted_iota(I32, y.shape, 1)
    return jnp.where((lane % ROPE_AXIS_DIM) < half,
                     pltpu.roll(y, LANES - half, 1), pltpu.roll(y, half, 1))


def _norm_rope(ph, g, cos, sin_signed, scale):
    ms = jnp.mean(ph * ph, axis=-1, keepdims=True)
    y = (ph * lax.rsqrt(ms + NORM_EPS)) * g
    y = y * cos + _swap_half(y) * sin_signed
    if scale != 1.0:
        y = y * scale
    return y


def _inproj_kernel(x_ref, sc_ref, sh_ref, w_ref, qg_ref, kg_ref, cos_ref, sin_ref,
                   q_ref, k_ref, vt_ref, xr_ref, yr_ref, *, q_scale):
    tm = x_ref.shape[0]
    kc = vt_ref.shape[-1]
    u = (x_ref[...] * (1.0 + sc_ref[...]) + sh_ref[...]).astype(BF16)
    cos = cos_ref[...]
    sin = sin_ref[...]
    o = 0
    pq = jnp.dot(u, w_ref[:, o:o + ATTN_W], preferred_element_type=F32)
    for h in range(N_HEADS):
        sl = slice(h * HEAD_DIM, (h + 1) * HEAD_DIM)
        q_ref[:, sl] = _norm_rope(pq[:, sl], qg_ref[...], cos, sin, q_scale).astype(BF16)
    o += ATTN_W
    pk = jnp.dot(u, w_ref[:, o:o + KV_W], preferred_element_type=F32)
    for h in range(N_KV_HEADS):
        sl = slice(h * HEAD_DIM, (h + 1) * HEAD_DIM)
        k_ref[:, sl] = _norm_rope(pk[:, sl], kg_ref[...], cos, sin, 1.0).astype(BF16)
    o += KV_W
    pv = jnp.dot(u, w_ref[:, o:o + KV_W], preferred_element_type=F32)
    ones_rows = jnp.where(
        lax.broadcasted_iota(I32, (VT_ROWS - HEAD_DIM, kc), 0) == 0, 1.0, 0.0).astype(BF16)
    for h in range(N_KV_HEADS):
        for cc in range(tm // kc):
            blk = pv[cc * kc:(cc + 1) * kc, h * HEAD_DIM:(h + 1) * HEAD_DIM]
            vt_ref[h, cc, 0:HEAD_DIM, :] = blk.T.astype(BF16)
            vt_ref[h, cc, HEAD_DIM:VT_ROWS, :] = ones_rows
    o += KV_W
    xr_ref[...] = jnp.dot(u, w_ref[:, o:o + RG_W], preferred_element_type=F32)
    o += RG_W
    yr_ref[...] = jnp.dot(u, w_ref[:, o:o + RG_W], preferred_element_type=F32)


def _in_projection(x, sc, sh, w_in, qg, kg, cos, sin, tm, kv_chunk):
    s, d = x.shape
    row = lambda i: (i, 0)
    if kv_chunk >= tm:
        per = kv_chunk // tm
        vt_spec = pl.BlockSpec((N_KV_HEADS, 1, VT_ROWS, tm), lambda i: (0, i // per, 0, i % per))
    else:
        vt_spec = pl.BlockSpec((N_KV_HEADS, tm // kv_chunk, VT_ROWS, kv_chunk),
                               lambda i: (0, i, 0, 0))
    return pl.pallas_call(
        functools.partial(_inproj_kernel, q_scale=HEAD_DIM ** -0.5 * LOG2E),
        grid=(s // tm,),
        in_specs=[pl.BlockSpec((tm, d), row),
                  _const_spec((1, d)), _const_spec((1, d)),
                  pl.BlockSpec((d, PROJ_W), lambda i: (0, 0), pipeline_mode=pl.Buffered(1)),
                  _const_spec((1, HEAD_DIM)), _const_spec((1, HEAD_DIM)),
                  pl.BlockSpec((tm, HEAD_DIM), row), pl.BlockSpec((tm, HEAD_DIM), row)],
        out_specs=[pl.BlockSpec((tm, ATTN_W), row), pl.BlockSpec((tm, KV_W), row),
                   vt_spec,
                   pl.BlockSpec((tm, RG_W), row), pl.BlockSpec((tm, RG_W), row)],
        out_shape=[jax.ShapeDtypeStruct((s, ATTN_W), BF16),
                   jax.ShapeDtypeStruct((s, KV_W), BF16),
                   jax.ShapeDtypeStruct((N_KV_HEADS, s // kv_chunk, VT_ROWS, kv_chunk), BF16),
                   jax.ShapeDtypeStruct((s, RG_W), F32),
                   jax.ShapeDtypeStruct((s, RG_W), F32)],
        compiler_params=_cparams("arbitrary"),
        name="in_projection",
    )(x, sc, sh, w_in, qg, kg, cos, sin)


def _attn_kernel(q_ref, kc_ref, vtc_ref, kl_ref, vtl_ref, o_ref, sa_ref, sb_ref, pa_ref, pb_ref, sc_ref,
                 *, tq, tk, n_chunks, unroll):
    q = q_ref[...]
    qs = jnp.concatenate(
        [q[:, g * HEAD_DIM:(g + 1) * HEAD_DIM] for g in range(GQA_GROUP)], axis=0)
    cols = GQA_GROUP * tq

    def scores(k):
        return lax.dot_general(k, qs, (((1,), (1,)), ((), ())), preferred_element_type=F32)

    def softmax(s_ref, m):
        m_new = jnp.maximum(m, jnp.max(s_ref[...], axis=0, keepdims=True))
        return m_new, jnp.exp2(m - m_new), jnp.exp2(s_ref[...] - m_new).astype(BF16)

    def weighted_values(vt, p, alpha, acc):
        return alpha * acc + jnp.dot(vt, p, preferred_element_type=F32)

    m0 = jnp.full((1, cols), NEG_BIG, F32)
    a0 = jnp.zeros((VT_ROWS, cols), F32)
    sc_ref[...] = scores(kc_ref[...])
    m, alpha, p = softmax(sc_ref, m0)
    acc = weighted_values(vtc_ref[0], p, alpha, a0)

    def latent_scores(j):
        j = jnp.minimum(j, n_chunks - 1)
        return scores(kl_ref[pl.ds(pl.multiple_of(j * tk, tk), tk), :])

    s_refs = (sa_ref, sb_ref)
    p_refs = (pa_ref, pb_ref)
    sa_ref[...] = latent_scores(0)
    sb_ref[...] = latent_scores(1)
    m, alpha, pa_ref[...] = softmax(sa_ref, m)

    def body(i, carry):
        m, acc, alpha = carry
        for u in range(unroll):
            c = unroll * i + u
            cur, nxt = u % 2, (u + 1) % 2
            s_refs[cur][...] = latent_scores(c + 2)
            m, alpha_next, p_refs[nxt][...] = softmax(s_refs[nxt], m)
            acc = weighted_values(vtl_ref[c], p_refs[cur][...], alpha, acc)
            alpha = alpha_next
        return m, acc, alpha

    _, acc, _ = lax.fori_loop(0, n_chunks // unroll, body, (m, acc, alpha))
    out_t = acc[0:HEAD_DIM, :] / acc[HEAD_DIM:HEAD_DIM + 1, :]
    for g in range(GQA_GROUP):
        o_ref[:, g * HEAD_DIM:(g + 1) * HEAD_DIM] = out_t[:, g * tq:(g + 1) * tq].T.astype(BF16)


def _attention(q, k_c, vt_c, k_l, vt_l, tq):
    s = q.shape[0]
    n_ctx = k_c.shape[0]
    n_chunks, _, tk = vt_l.shape[1:]
    unroll = min(ATTN_UNROLL, n_chunks)
    assert n_chunks % unroll == 0 and unroll % 2 == 0
    gw = GQA_GROUP * HEAD_DIM
    return pl.pallas_call(
        functools.partial(_attn_kernel, tq=tq, tk=tk, n_chunks=n_chunks, unroll=unroll),
        grid=(N_KV_HEADS, s // tq),
        in_specs=[pl.BlockSpec((tq, gw), lambda h, i: (i, h)),
                  pl.BlockSpec((n_ctx, HEAD_DIM), lambda h, i: (0, h)),
                  pl.BlockSpec((None, 1, VT_ROWS, n_ctx), lambda h, i: (h, 0, 0, 0)),
                  pl.BlockSpec((s, HEAD_DIM), lambda h, i: (0, h)),
                  pl.BlockSpec((None, n_chunks, VT_ROWS, tk), lambda h, i: (h, 0, 0, 0))],
        out_specs=pl.BlockSpec((tq, gw), lambda h, i: (i, h)),
        out_shape=jax.ShapeDtypeStruct((s, ATTN_W), BF16),
        scratch_shapes=[pltpu.VMEM((tk, GQA_GROUP * tq), F32),
                        pltpu.VMEM((tk, GQA_GROUP * tq), F32),
                        pltpu.VMEM((tk, GQA_GROUP * tq), BF16),
                        pltpu.VMEM((tk, GQA_GROUP * tq), BF16),
                        pltpu.VMEM((n_ctx, GQA_GROUP * tq), F32)],
        compiler_params=_cparams("arbitrary", "arbitrary"),
        name="attention",
    )(q, k_c, vt_c, k_l, vt_l)


def _log_sigmoid(x):
    return jnp.minimum(x, 0.0) - jnp.log1p(jnp.exp(-jnp.abs(x)))


def _rg_kernel(x_ref, xp_ref, xn_ref, cw_ref, cb_ref, wg_ref, bg_ref, lam_ref, h0_ref,
               *rest, reverse, final, t, n_chunks):
    if final:
        hf_ref, yr_ref, o_ref, a_scr, b_scr, hc_scr, h_scr = rest
    else:
        o_ref, a_scr, b_scr, hc_scr = rest
        h_scr = o_ref
    i = pl.program_id(0)
    c = (n_chunks - 1 - i) if reverse else i
    w = x_ref.shape[1]

    @pl.when(i == 0)
    def _():
        hc_scr[...] = jnp.broadcast_to(h0_ref[...], (SUBLANES, w))

    x = x_ref[...]
    row = lax.broadcasted_iota(I32, (t, w), 0)
    pm = jnp.where(c == 0, 0.0, 1.0).astype(F32)
    nm = jnp.where(c == n_chunks - 1, 0.0, 1.0).astype(F32)
    p6 = xp_ref[SUBLANES - 2:SUBLANES - 1, :] * pm
    p7 = xp_ref[SUBLANES - 1:SUBLANES, :] * pm
    n0 = xn_ref[0:1, :] * nm
    x_m1 = jnp.where(row == 0, p7, pltpu.roll(x, 1, 0))
    x_m2 = jnp.where(row == 0, p6, jnp.where(row == 1, p7, pltpu.roll(x, 2, 0)))
    x_p1 = jnp.where(row == t - 1, n0, pltpu.roll(x, t - 1, 0))
    xc = cb_ref[...] + cw_ref[0:1, :] * x_m2
    xc = xc + cw_ref[1:2, :] * x_m1
    xc = xc + cw_ref[2:3, :] * x
    xc = xc + cw_ref[3:4, :] * x_p1

    xcb = xc.astype(BF16)
    clam = RG_C * _log_sigmoid(lam_ref[...])
    for h in range(RG_HEADS):
        sl = slice(h * RG_HD, (h + 1) * RG_HD)
        g = jnp.dot(xcb[:, sl], wg_ref[h], preferred_element_type=F32)
        r = jax.nn.sigmoid(g[:, :RG_HD] + bg_ref[0:1, sl])
        gi = jax.nn.sigmoid(g[:, RG_HD:] + bg_ref[1:2, sl])
        log_a = r * clam[:, sl]
        a = jnp.exp(log_a)
        a_scr[:, sl] = a
        b_scr[:, sl] = jnp.sqrt(-jnp.tanh(log_a) * (a * a + 1.0)) * (gi * xc[:, sl])

    srow = lax.broadcasted_iota(I32, (SUBLANES, w), 0)
    n_tiles = t // SUBLANES

    def tile_body(j, hprev):
        tile = (n_tiles - 1 - j) if reverse else j
        start = pl.multiple_of(tile * SUBLANES, SUBLANES)
        a = a_scr[pl.ds(start, SUBLANES), :]
        b = b_scr[pl.ds(start, SUBLANES), :]
        for k in (1, 2, 4):
            if reverse:
                keep = srow < SUBLANES - k
                shift = SUBLANES - k
            else:
                keep = srow >= k
                shift = k
            a_sh = jnp.where(keep, pltpu.roll(a, shift, 0), 1.0)
            b_sh = jnp.where(keep, pltpu.roll(b, shift, 0), 0.0)
            b = a * b_sh + b
            a = a * a_sh
        hh = a * hprev + b
        h_scr[pl.ds(start, SUBLANES), :] = hh
        last = hh[0:1, :] if reverse else hh[SUBLANES - 1:SUBLANES, :]
        return jnp.broadcast_to(last, (SUBLANES, w))

    hc_scr[...] = lax.fori_loop(0, n_tiles, tile_body, hc_scr[...])

    if final:
        gate = jax.nn.gelu(yr_ref[...], approximate=True)
        o_ref[...] = ((hf_ref[...] + h_scr[...]) * gate).astype(o_ref.dtype)


def _rg_scan(xr, conv_w, conv_b, wg, bg, lam, h0, *, reverse, t, hf=None, yr=None):
    s, w = xr.shape
    n_chunks = s // t
    final = hf is not None
    tb = t // SUBLANES
    last_blk = s // SUBLANES - 1
    if reverse:
        cidx = lambda i: n_chunks - 1 - i
    else:
        cidx = lambda i: i
    chunk_spec = pl.BlockSpec((t, w), lambda i: (cidx(i), 0))
    in_specs = [chunk_spec,
                pl.BlockSpec((SUBLANES, w), lambda i: (jnp.maximum(cidx(i) * tb - 1, 0), 0)),
                pl.BlockSpec((SUBLANES, w), lambda i: (jnp.minimum((cidx(i) + 1) * tb, last_blk), 0)),
                _const_spec((4, w)), _const_spec((1, w)),
                _const_spec((RG_HEADS, RG_HD, 2 * RG_HD)), _const_spec((2, w)),
                _const_spec((1, w)), _const_spec((1, w))]
    args = [xr, xr, xr, conv_w, conv_b, wg, bg, lam, h0]
    scratch = [pltpu.VMEM((t, w), F32), pltpu.VMEM((t, w), F32), pltpu.VMEM((SUBLANES, w), F32)]
    if final:
        in_specs += [chunk_spec, chunk_spec]
        args += [hf, yr]
        scratch.append(pltpu.VMEM((t, w), F32))
        out_dtype = BF16
    else:
        out_dtype = F32
    return pl.pallas_call(
        functools.partial(_rg_kernel, reverse=reverse, final=final, t=t, n_chunks=n_chunks),
        grid=(n_chunks,),
        in_specs=in_specs,
        out_specs=chunk_spec,
        out_shape=jax.ShapeDtypeStruct((s, w), out_dtype),
        scratch_shapes=scratch,
        compiler_params=_cparams("arbitrary"),
        name="rglru_bwd" if reverse else "rglru_fwd",
    )(*args)


def _layer_norm(y, g, b):
    mu = jnp.mean(y, axis=-1, keepdims=True)
    yc = y - mu
    var = jnp.mean(yc * yc, axis=-1, keepdims=True)
    return yc * lax.rsqrt(var + NORM_EPS) * g + b


def _outproj_kernel(attn_ref, rg_ref, x_ref, w_ref, g1_ref, lg_ref, lb_ref, sc2_ref, sh2_ref,
                    h1_ref, v_ref):
    mix = jnp.dot(attn_ref[...], w_ref[0:ATTN_W, :], preferred_element_type=F32)
    mix = mix + jnp.dot(rg_ref[...], w_ref[ATTN_W:, :], preferred_element_type=F32)
    h1 = _layer_norm(DEEPNORM_ALPHA * x_ref[...] + g1_ref[...] * mix, lg_ref[...], lb_ref[...])
    h1_ref[...] = h1
    v_ref[...] = h1 * (1.0 + sc2_ref[...]) + sh2_ref[...]


def _out_projection(attn, rg, x, w_out, g1, ln_g, ln_b, sc2, sh2, tm):
    s, d = x.shape
    row = lambda i: (i, 0)
    vec = _const_spec((1, d))
    return pl.pallas_call(
        _outproj_kernel,
        grid=(s // tm,),
        in_specs=[pl.BlockSpec((tm, ATTN_W), row), pl.BlockSpec((tm, RG_W), row),
                  pl.BlockSpec((tm, d), row),
                  pl.BlockSpec((ATTN_W + RG_W, d), lambda i: (0, 0), pipeline_mode=pl.Buffered(1)),
                  vec, vec, vec, vec, vec],
        out_specs=[pl.BlockSpec((tm, d), row), pl.BlockSpec((tm, d), row)],
        out_shape=[jax.ShapeDtypeStruct((s, d), F32), jax.ShapeDtypeStruct((s, d), F32)],
        compiler_params=_cparams("arbitrary"),
        name="out_projection",
    )(attn, rg, x, w_out, g1, ln_g, ln_b, sc2, sh2)


def _first_index_of_max(x, iota_f, axis):
    mx = jnp.max(x, axis=axis, keepdims=True)
    idx = jnp.min(jnp.where(x == mx, iota_f, float(N_EXPERTS)), axis=axis, keepdims=True)
    return mx, idx


def _router_kernel(v_ref, whi_ref, wlo_ref, eb_ref, tri_ref,
                   idx_ref, wts_ref, rank_ref, cnt_ref, base_scr):
    v = v_ref[...]
    tm = v.shape[0]

    @pl.when(pl.program_id(0) == 0)
    def _():
        base_scr[...] = jnp.zeros(base_scr.shape, F32)

    v_hi = v.astype(BF16)
    v_lo = (v - v_hi.astype(F32)).astype(BF16)
    logits = jnp.dot(v_hi, whi_ref[...], preferred_element_type=F32)
    logits = logits + jnp.dot(v_lo, whi_ref[...], preferred_element_type=F32)
    logits = logits + jnp.dot(v_hi, wlo_ref[...], preferred_element_type=F32)
    lt = logits.T[0:N_EXPERTS, :]
    scores = jax.nn.sigmoid(lt)
    biased = scores + eb_ref[...]
    neg_inf = float("-inf")

    ig = lax.broadcasted_iota(I32, (GROUP_SIZE, tm), 0).astype(F32)
    groups = [biased[g * GROUP_SIZE:(g + 1) * GROUP_SIZE, :] for g in range(N_GROUPS)]
    gscore = []
    for bg in groups:
        top1, i1 = _first_index_of_max(bg, ig, 0)
        top2 = jnp.max(jnp.where(ig == i1, neg_inf, bg), axis=0, keepdims=True)
        gscore.append(top1 + top2)

    masked = []
    for g in range(N_GROUPS):
        ahead = jnp.zeros((1, tm), F32)
        for o in range(N_GROUPS):
            if o == g:
                continue
            before = (gscore[o] >= gscore[g]) if o < g else (gscore[o] > gscore[g])
            ahead = ahead + jnp.where(before, 1.0, 0.0)
        keep = jnp.broadcast_to(ahead < TOPK_GROUPS, (GROUP_SIZE, tm))
        masked.append(jnp.where(keep, groups[g], neg_inf))
    masked = jnp.concatenate(masked, axis=0)

    ie = lax.broadcasted_iota(I32, masked.shape, 0).astype(F32)
    seen = base_scr[...]
    ws = []
    for k in range(TOP_K):
        _, ei = _first_index_of_max(masked, ie, 0)
        hit = ie == ei
        idx_ref[k:k + 1, :] = ei.astype(I32)
        ws.append(jnp.sum(jnp.where(hit, scores, 0.0), axis=0, keepdims=True))
        masked = jnp.where(hit, neg_inf, masked)
        onehot = jnp.where(hit, 1.0, 0.0)
        before = jnp.dot(onehot.astype(BF16), tri_ref[...], preferred_element_type=F32)
        rank = jnp.sum(jnp.where(hit, before + seen, 0.0), axis=0, keepdims=True)
        rank_ref[k:k + 1, :] = rank.astype(I32)
        seen = seen + jnp.sum(onehot, axis=1, keepdims=True)
    base_scr[...] = seen
    cnt_ref[...] = jnp.broadcast_to(seen, cnt_ref.shape)
    total = ws[0]
    for k in range(1, TOP_K):
        total = total + ws[k]
    for k in range(TOP_K):
        wts_ref[k:k + 1, :] = ws[k] / total * ROUTED_SCALE


def _router(v, w_hi, w_lo, e_bias_col, tm):
    s, d = v.shape
    tri = jnp.triu(jnp.ones((tm, tm), BF16), k=1)
    slot = pl.BlockSpec((TOP_K, tm), lambda i: (0, i))
    return pl.pallas_call(
        _router_kernel,
        grid=(s // tm,),
        in_specs=[pl.BlockSpec((tm, d), lambda i: (i, 0)),
                  _const_spec((d, LANES)), _const_spec((d, LANES)),
                  _const_spec((N_EXPERTS, 1)), _const_spec((tm, tm))],
        out_specs=[slot, slot, slot, _const_spec((N_EXPERTS, LANES))],
        out_shape=[jax.ShapeDtypeStruct((TOP_K, s), I32),
                   jax.ShapeDtypeStruct((TOP_K, s), F32),
                   jax.ShapeDtypeStruct((TOP_K, s), I32),
                   jax.ShapeDtypeStruct((N_EXPERTS, LANES), F32)],
        scratch_shapes=[pltpu.VMEM((N_EXPERTS, 1), F32)],
        compiler_params=_cparams("arbitrary"),
        name="router",
    )(v, w_hi, w_lo, e_bias_col, tri)


def _pack_pairs(x):
    h = x.shape[1] // 2
    lo = lax.bitcast_convert_type(x[:, :h].astype(BF16).astype(F32), U32)
    hi = lax.bitcast_convert_type(x[:, h:].astype(BF16).astype(F32), U32)
    return (lo >> 16) | (hi & jnp.uint32(0xFFFF0000))


def _unpack_pairs(w):
    lo = lax.bitcast_convert_type(w << 16, F32)
    hi = lax.bitcast_convert_type(w & jnp.uint32(0xFFFF0000), F32)
    return jnp.concatenate([lo, hi], axis=1)


def _dispatch_kernel(last_ref, has_ref, v_ref, idx_ref, rank_ref, pstart_ref,
                     dest_ref, xs_hbm, dest_v, dest_s, zeros_v, pk_v, sem_z, sem_i, sem_r, *, bm):
    tm = v_ref.shape[0]

    def zero_fill(e):
        row0 = pl.multiple_of(last_ref[e], bm)
        return pltpu.make_async_copy(zeros_v, xs_hbm.at[pl.ds(row0, bm), :], sem_z)

    @pl.when(pl.program_id(0) == 0)
    def _():
        zeros_v[...] = jnp.zeros(zeros_v.shape, zeros_v.dtype)

        def start(e, carry):
            @pl.when(has_ref[e] > 0)
            def _():
                zero_fill(e).start()
            return carry

        def wait(e, carry):
            @pl.when(has_ref[e] > 0)
            def _():
                zero_fill(e).wait()
            return carry

        lax.fori_loop(0, N_EXPERTS, start, 0)
        lax.fori_loop(0, N_EXPERTS, wait, 0)

    ie = lax.broadcasted_iota(I32, (N_EXPERTS, tm), 0)
    pstart = pstart_ref[...]
    for k in range(TOP_K):
        hit = ie == idx_ref[k:k + 1, :]
        seg = jnp.sum(jnp.where(hit, pstart, 0.0), axis=0, keepdims=True)
        dest_v[k:k + 1, :] = seg.astype(I32) + rank_ref[k:k + 1, :]
    dest_ref[...] = dest_v[...]
    cp = pltpu.make_async_copy(dest_v, dest_s, sem_i)
    cp.start()
    pk_v[...] = _pack_pairs(v_ref[...])
    cp.wait()

    for t in range(tm):
        for k in range(TOP_K):
            pltpu.make_async_copy(pk_v.at[pl.ds(t, 1), :],
                                  xs_hbm.at[pl.ds(dest_s[k, t], 1), :], sem_r).start(priority=k % 2)
    for k in range(TOP_K):
        pltpu.make_async_copy(pk_v, xs_hbm.at[pl.ds(0, tm), :], sem_r).wait()


def _dispatch(last_row, has_blk, v, idx, rank, pstart_col, n_rows, tm, bm):
    s, d = v.shape
    slot = pl.BlockSpec((TOP_K, tm), lambda i, *_: (0, i))
    grid_spec = pltpu.PrefetchScalarGridSpec(
        num_scalar_prefetch=2,
        grid=(s // tm,),
        in_specs=[pl.BlockSpec((tm, d), lambda i, *_: (i, 0)), slot, slot,
                  pl.BlockSpec((N_EXPERTS, 1), lambda i, *_: (0, 0))],
        out_specs=[slot, pl.BlockSpec(memory_space=pl.ANY)],
        scratch_shapes=[pltpu.VMEM((TOP_K, tm), I32), pltpu.SMEM((TOP_K, tm), I32),
                        pltpu.VMEM((bm, d // 2), U32), pltpu.VMEM((tm, d // 2), U32),
                        pltpu.SemaphoreType.DMA, pltpu.SemaphoreType.DMA,
                        pltpu.SemaphoreType.DMA])
    return pl.pallas_call(
        functools.partial(_dispatch_kernel, bm=bm),
        grid_spec=grid_spec,
        out_shape=[jax.ShapeDtypeStruct((TOP_K, s), I32),
                   jax.ShapeDtypeStruct((n_rows, d // 2), U32)],
        compiler_params=_cparams("arbitrary"),
        name="dispatch",
    )(last_row, has_blk, v, idx, rank, pstart_col)


def _expert_kernel(blk_e_ref, n_used_ref, x_ref, w1_ref, w3_ref, w2_ref, y_ref, w1b, w3b, w2b):
    b = pl.program_id(0)

    @pl.when(b < n_used_ref[0])
    def _():
        prev_e = blk_e_ref[jnp.maximum(b - 1, 0)]

        @pl.when(jnp.logical_or(b == 0, blk_e_ref[b] != prev_e))
        def _():
            w1b[...] = w1_ref[...].astype(BF16)
            w3b[...] = w3_ref[...].astype(BF16)
            w2b[...] = w2_ref[...].astype(BF16)

        xb = _unpack_pairs(x_ref[...]).astype(BF16)
        h1 = jnp.dot(xb, w1b[...], preferred_element_type=F32)
        h3 = jnp.dot(xb, w3b[...], preferred_element_type=F32)
        act = (h1 * jax.nn.sigmoid(h1) * h3).astype(BF16)
        y_ref[...] = _pack_pairs(jnp.dot(act, w2b[...], preferred_element_type=F32))


def _experts(blk_e, n_used, xs, w_e1, w_e3, w_e2, bm):
    n_rows = xs.shape[0]
    d, ff = w_e1.shape[-2:]
    rows = lambda b, be, nu: (jnp.minimum(b, nu[0] - 1), 0)
    wspec1 = pl.BlockSpec((None, d, ff), lambda b, be, nu: (be[b], 0, 0))
    wspec2 = pl.BlockSpec((None, ff, d), lambda b, be, nu: (be[b], 0, 0))
    grid_spec = pltpu.PrefetchScalarGridSpec(
        num_scalar_prefetch=2,
        grid=(n_rows // bm,),
        in_specs=[pl.BlockSpec((bm, d // 2), rows), wspec1, wspec1, wspec2],
        out_specs=pl.BlockSpec((bm, d // 2), rows),
        scratch_shapes=[pltpu.VMEM((d, ff), BF16), pltpu.VMEM((d, ff), BF16),
                        pltpu.VMEM((ff, d), BF16)])
    return pl.pallas_call(
        _expert_kernel,
        grid_spec=grid_spec,
        out_shape=jax.ShapeDtypeStruct((n_rows, d // 2), U32),
        compiler_params=_cparams("arbitrary"),
        name="experts",
    )(blk_e, n_used, xs, w_e1, w_e3, w_e2)


def _block_tables(counts, n_blocks, bm):
    padded = (counts + bm - 1) // bm * bm
    pad_end = jnp.cumsum(padded)
    pad_start = pad_end - padded
    blk_first = jnp.arange(n_blocks, dtype=I32) * bm
    blk_e = jnp.minimum(jnp.sum(pad_end[None, :] <= blk_first[:, None], axis=1), N_EXPERTS - 1)
    n_used = pad_end[-1] // bm
    blk_e = jnp.where(jnp.arange(n_blocks) < n_used, blk_e, blk_e[jnp.maximum(n_used - 1, 0)])
    last_row = jnp.maximum(pad_end - bm, 0)
    return (pad_start, last_row.astype(I32), (padded > 0).astype(I32),
            blk_e.astype(I32), n_used.astype(I32).reshape(1))


def _combine_kernel(dest0_ref, dnext_ref, wts_ref, v_ref, h1_ref, g2_ref, lg_ref, lb_ref,
                    ws1_ref, ws3_ref, ws2_ref, ys_hbm, o_ref, dest_s, ybuf_a, ybuf_b, sem_i, sem_r):
    tm = v_ref.shape[0]
    i = pl.program_id(0)
    ybufs = (ybuf_a, ybuf_b)

    def start_gathers(slot):
        for t in range(tm):
            for k in range(TOP_K):
                pltpu.make_async_copy(ys_hbm.at[pl.ds(dest_s[slot, k, t], 1), :],
                                      ybufs[slot].at[k, pl.ds(t, 1), :],
                                      sem_r.at[slot]).start(priority=k % 2)

    def wait_gathers(slot):
        for k in range(TOP_K):
            pltpu.make_async_copy(ys_hbm.at[pl.ds(0, tm), :], ybufs[slot].at[k], sem_r.at[slot]).wait()

    @pl.when(i == 0)
    def _():
        cp = pltpu.make_async_copy(dest0_ref, dest_s.at[0], sem_i)
        cp.start()
        cp.wait()

        def first_tile(t, carry):
            for k in range(TOP_K):
                pltpu.make_async_copy(ys_hbm.at[pl.ds(dest_s[0, k, t], 1), :],
                                      ybuf_a.at[k, pl.ds(t, 1), :], sem_r.at[0]).start()
            return carry

        lax.fori_loop(0, tm, first_tile, 0)

    def tile(slot):
        other = 1 - slot
        cp = pltpu.make_async_copy(dnext_ref, dest_s.at[other], sem_i)
        cp.start()
        cp.wait()
        wait_gathers(slot)
        start_gathers(other)
        vb = v_ref[...].astype(BF16)
        a1 = jnp.dot(vb, ws1_ref[...], preferred_element_type=F32)
        a3 = jnp.dot(vb, ws3_ref[...], preferred_element_type=F32)
        act = (a1 * jax.nn.sigmoid(a1) * a3).astype(BF16)
        ff = jnp.dot(act, ws2_ref[...], preferred_element_type=F32)
        wpad = jnp.concatenate([wts_ref[...], jnp.zeros((LANES - TOP_K, tm), F32)], axis=0)
        wt = wpad.T
        for k in range(TOP_K):
            ff = ff + _unpack_pairs(ybufs[slot][k]) * wt[:, k:k + 1]
        o_ref[...] = _layer_norm(DEEPNORM_ALPHA * h1_ref[...] + g2_ref[...] * ff,
                                 lg_ref[...], lb_ref[...])

        @pl.when(i == pl.num_programs(0) - 1)
        def _():
            wait_gathers(other)

    @pl.when(i % 2 == 0)
    def _():
        tile(0)

    @pl.when(i % 2 == 1)
    def _():
        tile(1)


def _combine(dest, wts, v, h1, g2, ln_g, ln_b, w_s1, w_s3, w_s2, ys, tm):
    s, d = v.shape
    n = s // tm
    row = lambda i: (i, 0)
    vec = _const_spec((1, d))
    slot = pl.BlockSpec((TOP_K, tm), lambda i: (0, i))
    single = lambda shape: pl.BlockSpec(shape, lambda i: (0, 0), pipeline_mode=pl.Buffered(1))
    return pl.pallas_call(
        _combine_kernel,
        grid=(n,),
        in_specs=[pl.BlockSpec((TOP_K, tm), lambda i: (0, 0)),
                  pl.BlockSpec((TOP_K, tm), lambda i: (0, jnp.minimum(i + 1, n - 1))),
                  slot, pl.BlockSpec((tm, d), row), pl.BlockSpec((tm, d), row),
                  vec, vec, vec,
                  single(w_s1.shape), single(w_s3.shape), single(w_s2.shape),
                  pl.BlockSpec(memory_space=pl.ANY)],
        out_specs=pl.BlockSpec((tm, d), row),
        out_shape=jax.ShapeDtypeStruct((s, d), F32),
        scratch_shapes=[pltpu.SMEM((2, TOP_K, tm), I32),
                        pltpu.VMEM((TOP_K, tm, d // 2), U32), pltpu.VMEM((TOP_K, tm, d // 2), U32),
                        pltpu.SemaphoreType.DMA, pltpu.SemaphoreType.DMA((2,))],
        compiler_params=_cparams("arbitrary"),
        name="combine",
    )(dest, dest, wts, v, h1, g2, ln_g, ln_b, w_s1, w_s3, w_s2, ys)


def _rope_tables(s):
    half = HEAD_DIM // 4
    inv_freq = ROPE_THETA ** (-np.arange(half, dtype=np.float32) / half)
    t = np.arange(s)
    ang_r = (t // GRID_W).astype(np.float32)[:, None] * inv_freq[None, :]
    ang_c = (t % GRID_W).astype(np.float32)[:, None] * inv_freq[None, :]
    cos = np.concatenate([np.cos(ang_r)] * 2 + [np.cos(ang_c)] * 2, axis=-1)
    sin = np.concatenate([-np.sin(ang_r), np.sin(ang_r), -np.sin(ang_c), np.sin(ang_c)], axis=-1)
    return jnp.asarray(cos, F32), jnp.asarray(sin, F32)


def kernel(x, c, ctx, c_ctx, w_mod, b_mod, w_in, q_norm, k_norm, conv_w, conv_b, rg_wa, rg_ba,
           rg_wx, rg_bx, rg_lam, w_out, ln1_g, ln1_b, w_router, e_bias, w_e1, w_e3, w_e2,
           w_s1, w_s3, w_s2, ln2_g, ln2_b):
    assert x.shape[0] == 1 and w_mod.shape[0] == DEPTH
    _, s, d = x.shape
    n_ctx = ctx.shape[1]
    x2 = x[0]

    c2t = jnp.stack([c[0], c_ctx], axis=1)
    mod = _modulation(c2t, w_mod[0], b_mod[0].reshape(1, -1))
    sh1, sc1, g1, sh2, sc2, g2 = [mod[0:1, j * d:(j + 1) * d] for j in range(6)]
    csh1, csc1 = mod[1:2, 0:d], mod[1:2, d:2 * d]

    w_in_b = w_in[0].astype(BF16)
    qg = q_norm[0].reshape(1, HEAD_DIM)
    kg = k_norm[0].reshape(1, HEAD_DIM)
    cos, sin = _rope_tables(s)
    q_l, k_l, vt_l, xr_l, yr_l = _in_projection(x2, sc1, sh1, w_in_b, qg, kg, cos, sin,
                                                tm=min(INPROJ_ROWS, s), kv_chunk=min(KV_CHUNK, s))
    _, k_c, vt_c, xr_c, _ = _in_projection(
        ctx[0], csc1, csh1, w_in_b, qg, kg,
        jnp.ones((n_ctx, HEAD_DIM), F32), jnp.zeros((n_ctx, HEAD_DIM), F32),
        tm=n_ctx, kv_chunk=n_ctx)

    attn = _attention(q_l, k_c, vt_c, k_l, vt_l, tq=min(ATTN_QUERIES, s))

    wg = jnp.concatenate([rg_wa[0], rg_wx[0]], axis=-1).astype(BF16)
    cb = conv_b[0].reshape(1, RG_W)
    zero_state = jnp.zeros((1, RG_W), F32)
    rg_args = []
    for dd in range(2):
        rg_args.append((conv_w[0], cb, wg[dd],
                        jnp.stack([rg_ba[0, dd], rg_bx[0, dd]], axis=0),
                        rg_lam[0, dd].reshape(1, RG_W)))
    t_rg = min(RG_ROWS, s)
    hc_f = _rg_scan(xr_c, *rg_args[0], zero_state, reverse=False, t=n_ctx)
    hc_b = _rg_scan(xr_c, *rg_args[1], zero_state, reverse=True, t=n_ctx)
    h_f = _rg_scan(xr_l, *rg_args[0], hc_f[n_ctx - 1:n_ctx], reverse=False, t=t_rg)
    rg = _rg_scan(xr_l, *rg_args[1], hc_b[0:1], reverse=True, t=t_rg, hf=h_f, yr=yr_l)

    h1, v = _out_projection(attn, rg, x2, w_out[0].astype(BF16), g1,
                            ln1_g[0].reshape(1, d), ln1_b[0].reshape(1, d), sc2, sh2,
                            tm=min(OUTPROJ_ROWS, s))

    wr = jnp.pad(w_router[0], ((0, 0), (0, LANES - N_EXPERTS)))
    wr_hi = wr.astype(BF16)
    wr_lo = (wr - wr_hi.astype(F32)).astype(BF16)
    idx, wts, rank, cnt = _router(v, wr_hi, wr_lo, e_bias[0].reshape(N_EXPERTS, 1),
                                  tm=min(ROUTER_ROWS, s))

    bm = min(EXPERT_BLOCK, s)
    n_blocks = s * TOP_K // bm + N_EXPERTS
    pad_start, last_row, has_blk, blk_e, n_used = _block_tables(cnt[:, 0].astype(I32), n_blocks, bm)
    dest, xs = _dispatch(last_row, has_blk, v, idx, rank,
                         pad_start.astype(F32).reshape(N_EXPERTS, 1), n_blocks * bm,
                         tm=min(DISPATCH_ROWS, s), bm=bm)
    ys = _experts(blk_e, n_used, xs, w_e1[0], w_e3[0], w_e2[0], bm)

    out = _combine(dest, wts, v, h1, g2, ln2_g[0].reshape(1, d), ln2_b[0].reshape(1, d),
                   w_s1[0].astype(BF16), w_s3[0].astype(BF16), w_s2[0].astype(BF16), ys,
                   tm=min(COMBINE_ROWS, s))
    return out[None]
```

```python
import functools

import jax
import jax.numpy as jnp
import numpy as np
from jax import lax
from jax.experimental import pallas as pl
from jax.experimental.pallas import tpu as pltpu

F32 = jnp.float32
BF16 = jnp.bfloat16
I32 = jnp.int32
U32 = jnp.uint32

GRID_W = 64
HEAD_DIM = 128
N_HEADS = 8
N_KV_HEADS = 2
GQA_GROUP = N_HEADS // N_KV_HEADS
ATTN_W = N_HEADS * HEAD_DIM
KV_W = N_KV_HEADS * HEAD_DIM
ROPE_THETA = 10000.0
RG_W = 1024
RG_HEADS = 8
RG_HD = RG_W // RG_HEADS
RG_C = 8.0
PROJ_W = ATTN_W + 2 * KV_W + 2 * RG_W
N_EXPERTS = 64
N_GROUPS = 8
GROUP_SIZE = N_EXPERTS // N_GROUPS
TOPK_GROUPS = 4
TOP_K = 8
ROUTED_SCALE = 2.5
NORM_EPS = 1e-6
DEPTH = 1
DEEPNORM_ALPHA = (2.0 * DEPTH) ** 0.25
LOG2E = 1.4426950408889634

LANES = 128
SUBLANES = 8
BF16_SUBLANES = 16
VMEM_LIMIT = 56 * 1024 * 1024

NEG_BIG = -1e30
ROPE_AXIS_DIM = HEAD_DIM // 2
VT_ROWS = HEAD_DIM + BF16_SUBLANES

MOD_COLS = 1024
INPROJ_ROWS = 512
ATTN_QUERIES = 256
KV_CHUNK = 512
ATTN_UNROLL = 16
RG_ROWS = 512
OUTPROJ_ROWS = 256
ROUTER_ROWS = 256
EXPERT_BLOCK = 512
DISPATCH_ROWS = 256
COMBINE_ROWS = 256


def _cparams(*sem):
    return pltpu.CompilerParams(dimension_semantics=sem, vmem_limit_bytes=VMEM_LIMIT)


def _const_spec(shape):
    nd = len(shape)
    return pl.BlockSpec(shape, lambda *_: (0,) * nd)


def _mod_kernel(ct_ref, w_ref, b_ref, o_ref, sb_ref, *, tn):
    d = w_ref.shape[0]

    @pl.when(pl.program_id(0) == 0)
    def _():
        ct = ct_ref[...]
        s = ct * jax.nn.sigmoid(ct)
        sb_ref[0] = jnp.broadcast_to(s[:, 0:1], (d, LANES))
        sb_ref[1] = jnp.broadcast_to(s[:, 1:2], (d, LANES))

    for c in range(tn // LANES):
        sl = slice(c * LANES, (c + 1) * LANES)
        wc = w_ref[:, sl]
        bc = b_ref[:, sl]
        o0 = jnp.sum(wc * sb_ref[0], axis=0, keepdims=True) + bc
        o1 = jnp.sum(wc * sb_ref[1], axis=0, keepdims=True) + bc
        o_ref[:, sl] = jnp.concatenate(
            [o0, o1, jnp.zeros((SUBLANES - 2, LANES), F32)], axis=0)


def _modulation(c2t, w_mod, b_mod):
    d, n = w_mod.shape
    tn = MOD_COLS
    return pl.pallas_call(
        functools.partial(_mod_kernel, tn=tn),
        grid=(n // tn,),
        in_specs=[_const_spec((d, 2)),
                  pl.BlockSpec((d, tn), lambda j: (0, j)),
                  pl.BlockSpec((1, tn), lambda j: (0, j))],
        out_specs=pl.BlockSpec((SUBLANES, tn), lambda j: (0, j)),
        out_shape=jax.ShapeDtypeStruct((SUBLANES, n), F32),
        scratch_shapes=[pltpu.VMEM((2, d, LANES), F32)],
        compiler_params=_cparams("arbitrary"),
        name="modulation",
    )(c2t, w_mod, b_mod)


def _swap_half(y):
    half = ROPE_AXIS_DIM // 2
    lane = lax.broadcasted_iota(I32, y.shape, 1)
    return jnp.where((lane % ROPE_AXIS_DIM) < half,
                     pltpu.roll(y, LANES - half, 1), pltpu.roll(y, half, 1))


def _norm_rope(ph, g, cos, sin_signed, scale):
    ms = jnp.mean(ph * ph, axis=-1, keepdims=True)
    y = (ph * lax.rsqrt(ms + NORM_EPS)) * g
    y = y * cos + _swap_half(y) * sin_signed
    if scale != 1.0:
        y = y * scale
    return y


def _inproj_kernel(x_ref, sc_ref, sh_ref, w_ref, qg_ref, kg_ref, cos_ref, sin_ref,
                   q_ref, k_ref, vt_ref, xr_ref, yr_ref, *, q_scale):
    tm = x_ref.shape[0]
    kc = vt_ref.shape[-1]
    u = (x_ref[...] * (1.0 + sc_ref[...]) + sh_ref[...]).astype(BF16)
    cos = cos_ref[...]
    sin = sin_ref[...]
    o = 0
    pq = jnp.dot(u, w_ref[:, o:o + ATTN_W], preferred_element_type=F32)
    for h in range(N_HEADS):
        sl = slice(h * HEAD_DIM, (h + 1) * HEAD_DIM)
        q_ref[:, sl] = _norm_rope(pq[:, sl], qg_ref[...], cos, sin, q_scale).astype(BF16)
    o += ATTN_W
    pk = jnp.dot(u, w_ref[:, o:o + KV_W], preferred_element_type=F32)
    for h in range(N_KV_HEADS):
        sl = slice(h * HEAD_DIM, (h + 1) * HEAD_DIM)
        k_ref[:, sl] = _norm_rope(pk[:, sl], kg_ref[...], cos, sin, 1.0).astype(BF16)
    o += KV_W
    pv = jnp.dot(u, w_ref[:, o:o + KV_W], preferred_element_type=F32)
    ones_rows = jnp.where(
        lax.broadcasted_iota(I32, (VT_ROWS - HEAD_DIM, kc), 0) == 0, 1.0, 0.0).astype(BF16)
    for h in range(N_KV_HEADS):
        for cc in range(tm // kc):
            blk = pv[cc * kc:(cc + 1) * kc, h * HEAD_DIM:(h + 1) * HEAD_DIM]
            vt_ref[h, cc, 0:HEAD_DIM, :] = blk.T.astype(BF16)
            vt_ref[h, cc, HEAD_DIM:VT_ROWS, :] = ones_rows
    o += KV_W
    xr_ref[...] = jnp.dot(u, w_ref[:, o:o + RG_W], preferred_element_type=F32)
    o += RG_W
    yr_ref[...] = jnp.dot(u, w_ref[:, o:o + RG_W], preferred_element_type=F32)


def _in_projection(x, sc, sh, w_in, qg, kg, cos, sin, tm, kv_chunk):
    s, d = x.shape
    row = lambda i: (i, 0)
    if kv_chunk >= tm:
        per = kv_chunk // tm
        vt_spec = pl.BlockSpec((N_KV_HEADS, 1, VT_ROWS, tm), lambda i: (0, i // per, 0, i % per))
    else:
        vt_spec = pl.BlockSpec((N_KV_HEADS, tm // kv_chunk, VT_ROWS, kv_chunk),
                               lambda i: (0, i, 0, 0))
    return pl.pallas_call(
        functools.partial(_inproj_kernel, q_scale=HEAD_DIM ** -0.5 * LOG2E),
        grid=(s // tm,),
        in_specs=[pl.BlockSpec((tm, d), row),
                  _const_spec((1, d)), _const_spec((1, d)),
                  pl.BlockSpec((d, PROJ_W), lambda i: (0, 0), pipeline_mode=pl.Buffered(1)),
                  _const_spec((1, HEAD_DIM)), _const_spec((1, HEAD_DIM)),
                  pl.BlockSpec((tm, HEAD_DIM), row), pl.BlockSpec((tm, HEAD_DIM), row)],
        out_specs=[pl.BlockSpec((tm, ATTN_W), row), pl.BlockSpec((tm, KV_W), row),
                   vt_spec,
                   pl.BlockSpec((tm, RG_W), row), pl.BlockSpec((tm, RG_W), row)],
        out_shape=[jax.ShapeDtypeStruct((s, ATTN_W), BF16),
                   jax.ShapeDtypeStruct((s, KV_W), BF16),
                   jax.ShapeDtypeStruct((N_KV_HEADS, s // kv_chunk, VT_ROWS, kv_chunk), BF16),
                   jax.ShapeDtypeStruct((s, RG_W), F32),
                   jax.ShapeDtypeStruct((s, RG_W), F32)],
        compiler_params=_cparams("arbitrary"),
        name="in_projection",
    )(x, sc, sh, w_in, qg, kg, cos, sin)


def _attn_kernel(q_ref, kc_ref, vtc_ref, kl_ref, vtl_ref, o_ref, sa_ref, sb_ref, pa_ref, pb_ref, sc_ref,
                 *, tq, tk, n_chunks, unroll):
    q = q_ref[...]
    qs = jnp.concatenate(
        [q[:, g * HEAD_DIM:(g + 1) * HEAD_DIM] for g in range(GQA_GROUP)], axis=0)
    cols = GQA_GROUP * tq

    def scores(k):
        return lax.dot_general(k, qs, (((1,), (1,)), ((), ())), preferred_element_type=F32)

    def softmax(s_ref, m):
        m_new = jnp.maximum(m, jnp.max(s_ref[...], axis=0, keepdims=True))
        return m_new, jnp.exp2(m - m_new), jnp.exp2(s_ref[...] - m_new).astype(BF16)

    def weighted_values(vt, p, alpha, acc):
        return alpha * acc + jnp.dot(vt, p, preferred_element_type=F32)

    m0 = jnp.full((1, cols), NEG_BIG, F32)
    a0 = jnp.zeros((VT_ROWS, cols), F32)
    sc_ref[...] = scores(kc_ref[...])
    m, alpha, p = softmax(sc_ref, m0)
    acc = weighted_values(vtc_ref[0], p, alpha, a0)

    def latent_scores(j):
        j = jnp.minimum(j, n_chunks - 1)
        return scores(kl_ref[pl.ds(pl.multiple_of(j * tk, tk), tk), :])

    s_refs = (sa_ref, sb_ref)
    p_refs = (pa_ref, pb_ref)
    sa_ref[...] = latent_scores(0)
    sb_ref[...] = latent_scores(1)
    m, alpha, pa_ref[...] = softmax(sa_ref, m)

    def body(i, carry):
        m, acc, alpha = carry
        for u in range(unroll):
            c = unroll * i + u
            cur, nxt = u % 2, (u + 1) % 2
            s_refs[cur][...] = latent_scores(c + 2)
            m, alpha_next, p_refs[nxt][...] = softmax(s_refs[nxt], m)
            acc = weighted_values(vtl_ref[c], p_refs[cur][...], alpha, acc)
            alpha = alpha_next
        return m, acc, alpha

    _, acc, _ = lax.fori_loop(0, n_chunks // unroll, body, (m, acc, alpha))
    out_t = acc[0:HEAD_DIM, :] / acc[HEAD_DIM:HEAD_DIM + 1, :]
    for g in range(GQA_GROUP):
        o_ref[:, g * HEAD_DIM:(g + 1) * HEAD_DIM] = out_t[:, g * tq:(g + 1) * tq].T.astype(BF16)


def _attention(q, k_c, vt_c, k_l, vt_l, tq):
    s = q.shape[0]
    n_ctx = k_c.shape[0]
    n_chunks, _, tk = vt_l.shape[1:]
    unroll = min(ATTN_UNROLL, n_chunks)
    assert n_chunks % unroll == 0 and unroll % 2 == 0
    gw = GQA_GROUP * HEAD_DIM
    return pl.pallas_call(
        functools.partial(_attn_kernel, tq=tq, tk=tk, n_chunks=n_chunks, unroll=unroll),
        grid=(N_KV_HEADS, s // tq),
        in_specs=[pl.BlockSpec((tq, gw), lambda h, i: (i, h)),
                  pl.BlockSpec((n_ctx, HEAD_DIM), lambda h, i: (0, h)),
                  pl.BlockSpec((None, 1, VT_ROWS, n_ctx), lambda h, i: (h, 0, 0, 0)),
                  pl.BlockSpec((s, HEAD_DIM), lambda h, i: (0, h)),
                  pl.BlockSpec((None, n_chunks, VT_ROWS, tk), lambda h, i: (h, 0, 0, 0))],
        out_specs=pl.BlockSpec((tq, gw), lambda h, i: (i, h)),
        out_shape=jax.ShapeDtypeStruct((s, ATTN_W), BF16),
        scratch_shapes=[pltpu.VMEM((tk, GQA_GROUP * tq), F32),
                        pltpu.VMEM((tk, GQA_GROUP * tq), F32),
                        pltpu.VMEM((tk, GQA_GROUP * tq), BF16),
                        pltpu.VMEM((tk, GQA_GROUP * tq), BF16),
                        pltpu.VMEM((n_ctx, GQA_GROUP * tq), F32)],
        compiler_params=_cparams("arbitrary", "arbitrary"),
        name="attention",
    )(q, k_c, vt_c, k_l, vt_l)


def _log_sigmoid(x):
    return jnp.minimum(x, 0.0) - jnp.log1p(jnp.exp(-jnp.abs(x)))


def _rg_kernel(x_ref, xp_ref, xn_ref, cw_ref, cb_ref, wg_ref, bg_ref, lam_ref, h0_ref,
               *rest, reverse, final, t, n_chunks):
    if final:
        hf_ref, yr_ref, o_ref, a_scr, b_scr, hc_scr, h_scr = rest
    else:
        o_ref, a_scr, b_scr, hc_scr = rest
        h_scr = o_ref
    i = pl.program_id(0)
    c = (n_chunks - 1 - i) if reverse else i
    w = x_ref.shape[1]

    @pl.when(i == 0)
    def _():
        hc_scr[...] = jnp.broadcast_to(h0_ref[...], (SUBLANES, w))

    x = x_ref[...]
    row = lax.broadcasted_iota(I32, (t, w), 0)
    pm = jnp.where(c == 0, 0.0, 1.0).astype(F32)
    nm = jnp.where(c == n_chunks - 1, 0.0, 1.0).astype(F32)
    p6 = xp_ref[SUBLANES - 2:SUBLANES - 1, :] * pm
    p7 = xp_ref[SUBLANES - 1:SUBLANES, :] * pm
    n0 = xn_ref[0:1, :] * nm
    x_m1 = jnp.where(row == 0, p7, pltpu.roll(x, 1, 0))
    x_m2 = jnp.where(row == 0, p6, jnp.where(row == 1, p7, pltpu.roll(x, 2, 0)))
    x_p1 = jnp.where(row == t - 1, n0, pltpu.roll(x, t - 1, 0))
    xc = cb_ref[...] + cw_ref[0:1, :] * x_m2
    xc = xc + cw_ref[1:2, :] * x_m1
    xc = xc + cw_ref[2:3, :] * x
    xc = xc + cw_ref[3:4, :] * x_p1

    xcb = xc.astype(BF16)
    clam = RG_C * _log_sigmoid(lam_ref[...])
    for h in range(RG_HEADS):
        sl = slice(h * RG_HD, (h + 1) * RG_HD)
        g = jnp.dot(xcb[:, sl], wg_ref[h], preferred_element_type=F32)
        r = jax.nn.sigmoid(g[:, :RG_HD] + bg_ref[0:1, sl])
        gi = jax.nn.sigmoid(g[:, RG_HD:] + bg_ref[1:2, sl])
        log_a = r * clam[:, sl]
        a = jnp.exp(log_a)
        a_scr[:, sl] = a
        b_scr[:, sl] = jnp.sqrt(-jnp.tanh(log_a) * (a * a + 1.0)) * (gi * xc[:, sl])

    srow = lax.broadcasted_iota(I32, (SUBLANES, w), 0)
    n_tiles = t // SUBLANES

    def tile_body(j, hprev):
        tile = (n_tiles - 1 - j) if reverse else j
        start = pl.multiple_of(tile * SUBLANES, SUBLANES)
        a = a_scr[pl.ds(start, SUBLANES), :]
        b = b_scr[pl.ds(start, SUBLANES), :]
        for k in (1, 2, 4):
            if reverse:
                keep = srow < SUBLANES - k
                shift = SUBLANES - k
            else:
                keep = srow >= k
                shift = k
            a_sh = jnp.where(keep, pltpu.roll(a, shift, 0), 1.0)
            b_sh = jnp.where(keep, pltpu.roll(b, shift, 0), 0.0)
            b = a * b_sh + b
            a = a * a_sh
        hh = a * hprev + b
        h_scr[pl.ds(start, SUBLANES), :] = hh
        last = hh[0:1, :] if reverse else hh[SUBLANES - 1:SUBLANES, :]
        return jnp.broadcast_to(last, (SUBLANES, w))

    hc_scr[...] = lax.fori_loop(0, n_tiles, tile_body, hc_scr[...])

    if final:
        gate = jax.nn.gelu(yr_ref[...], approximate=True)
        o_ref[...] = ((hf_ref[...] + h_scr[...]) * gate).astype(o_ref.dtype)


def _rg_scan(xr, conv_w, conv_b, wg, bg, lam, h0, *, reverse, t, hf=None, yr=None):
    s, w = xr.shape
    n_chunks = s // t
    final = hf is not None
    tb = t // SUBLANES
    last_blk = s // SUBLANES - 1
    if reverse:
        cidx = lambda i: n_chunks - 1 - i
    else:
        cidx = lambda i: i
    chunk_spec = pl.BlockSpec((t, w), lambda i: (cidx(i), 0))
    in_specs = [chunk_spec,
                pl.BlockSpec((SUBLANES, w), lambda i: (jnp.maximum(cidx(i) * tb - 1, 0), 0)),
                pl.BlockSpec((SUBLANES, w), lambda i: (jnp.minimum((cidx(i) + 1) * tb, last_blk), 0)),
                _const_spec((4, w)), _const_spec((1, w)),
                _const_spec((RG_HEADS, RG_HD, 2 * RG_HD)), _const_spec((2, w)),
                _const_spec((1, w)), _const_spec((1, w))]
    args = [xr, xr, xr, conv_w, conv_b, wg, bg, lam, h0]
    scratch = [pltpu.VMEM((t, w), F32), pltpu.VMEM((t, w), F32), pltpu.VMEM((SUBLANES, w), F32)]
    if final:
        in_specs += [chunk_spec, chunk_spec]
        args += [hf, yr]
        scratch.append(pltpu.VMEM((t, w), F32))
        out_dtype = BF16
    else:
        out_dtype = F32
    return pl.pallas_call(
        functools.partial(_rg_kernel, reverse=reverse, final=final, t=t, n_chunks=n_chunks),
        grid=(n_chunks,),
        in_specs=in_specs,
        out_specs=chunk_spec,
        out_shape=jax.ShapeDtypeStruct((s, w), out_dtype),
        scratch_shapes=scratch,
        compiler_params=_cparams("arbitrary"),
        name="rglru_bwd" if reverse else "rglru_fwd",
    )(*args)


def _layer_norm(y, g, b):
    mu = jnp.mean(y, axis=-1, keepdims=True)
    yc = y - mu
    var = jnp.mean(yc * yc, axis=-1, keepdims=True)
    return yc * lax.rsqrt(var + NORM_EPS) * g + b


def _outproj_kernel(attn_ref, rg_ref, x_ref, w_ref, g1_ref, lg_ref, lb_ref, sc2_ref, sh2_ref,
                    h1_ref, v_ref):
    mix = jnp.dot(attn_ref[...], w_ref[0:ATTN_W, :], preferred_element_type=F32)
    mix = mix + jnp.dot(rg_ref[...], w_ref[ATTN_W:, :], preferred_element_type=F32)
    h1 = _layer_norm(DEEPNORM_ALPHA * x_ref[...] + g1_ref[...] * mix, lg_ref[...], lb_ref[...])
    h1_ref[...] = h1
    v_ref[...] = h1 * (1.0 + sc2_ref[...]) + sh2_ref[...]


def _out_projection(attn, rg, x, w_out, g1, ln_g, ln_b, sc2, sh2, tm):
    s, d = x.shape
    row = lambda i: (i, 0)
    vec = _const_spec((1, d))
    return pl.pallas_call(
        _outproj_kernel,
        grid=(s // tm,),
        in_specs=[pl.BlockSpec((tm, ATTN_W), row), pl.BlockSpec((tm, RG_W), row),
                  pl.BlockSpec((tm, d), row),
                  pl.BlockSpec((ATTN_W + RG_W, d), lambda i: (0, 0), pipeline_mode=pl.Buffered(1)),
                  vec, vec, vec, vec, vec],
        out_specs=[pl.BlockSpec((tm, d), row), pl.BlockSpec((tm, d), row)],
        out_shape=[jax.ShapeDtypeStruct((s, d), F32), jax.ShapeDtypeStruct((s, d), F32)],
        compiler_params=_cparams("arbitrary"),
        name="out_projection",
    )(attn, rg, x, w_out, g1, ln_g, ln_b, sc2, sh2)


def _first_index_of_max(x, iota_f, axis):
    mx = jnp.max(x, axis=axis, keepdims=True)
    idx = jnp.min(jnp.where(x == mx, iota_f, float(N_EXPERTS)), axis=axis, keepdims=True)
    return mx, idx


def _router_kernel(v_ref, whi_ref, wlo_ref, eb_ref, tri_ref,
                   idx_ref, wts_ref, rank_ref, cnt_ref, base_scr):
    v = v_ref[...]
    tm = v.shape[0]

    @pl.when(pl.program_id(0) == 0)
    def _():
        base_scr[...] = jnp.zeros(base_scr.shape, F32)

    v_hi = v.astype(BF16)
    v_lo = (v - v_hi.astype(F32)).astype(BF16)
    logits = jnp.dot(v_hi, whi_ref[...], preferred_element_type=F32)
    logits = logits + jnp.dot(v_lo, whi_ref[...], preferred_element_type=F32)
    logits = logits + jnp.dot(v_hi, wlo_ref[...], preferred_element_type=F32)
    lt = logits.T[0:N_EXPERTS, :]
    scores = jax.nn.sigmoid(lt)
    biased = scores + eb_ref[...]
    neg_inf = float("-inf")

    ig = lax.broadcasted_iota(I32, (GROUP_SIZE, tm), 0).astype(F32)
    groups = [biased[g * GROUP_SIZE:(g + 1) * GROUP_SIZE, :] for g in range(N_GROUPS)]
    gscore = []
    for bg in groups:
        top1, i1 = _first_index_of_max(bg, ig, 0)
        top2 = jnp.max(jnp.where(ig == i1, neg_inf, bg), axis=0, keepdims=True)
        gscore.append(top1 + top2)

    masked = []
    for g in range(N_GROUPS):
        ahead = jnp.zeros((1, tm), F32)
        for o in range(N_GROUPS):
            if o == g:
                continue
            before = (gscore[o] >= gscore[g]) if o < g else (gscore[o] > gscore[g])
            ahead = ahead + jnp.where(before, 1.0, 0.0)
        keep = jnp.broadcast_to(ahead < TOPK_GROUPS, (GROUP_SIZE, tm))
        masked.append(jnp.where(keep, groups[g], neg_inf))
    masked = jnp.concatenate(masked, axis=0)

    ie = lax.broadcasted_iota(I32, masked.shape, 0).astype(F32)
    seen = base_scr[...]
    ws = []
    for k in range(TOP_K):
        _, ei = _first_index_of_max(masked, ie, 0)
        hit = ie == ei
        idx_ref[k:k + 1, :] = ei.astype(I32)
        ws.append(jnp.sum(jnp.where(hit, scores, 0.0), axis=0, keepdims=True))
        masked = jnp.where(hit, neg_inf, masked)
        onehot = jnp.where(hit, 1.0, 0.0)
        before = jnp.dot(onehot.astype(BF16), tri_ref[...], preferred_element_type=F32)
        rank = jnp.sum(jnp.where(hit, before + seen, 0.0), axis=0, keepdims=True)
        rank_ref[k:k + 1, :] = rank.astype(I32)
        seen = seen + jnp.sum(onehot, axis=1, keepdims=True)
    base_scr[...] = seen
    cnt_ref[...] = jnp.broadcast_to(seen, cnt_ref.shape)
    total = ws[0]
    for k in range(1, TOP_K):
        total = total + ws[k]
    for k in range(TOP_K):
        wts_ref[k:k + 1, :] = ws[k] / total * ROUTED_SCALE


def _router(v, w_hi, w_lo, e_bias_col, tm):
    s, d = v.shape
    tri = jnp.triu(jnp.ones((tm, tm), BF16), k=1)
    slot = pl.BlockSpec((TOP_K, tm), lambda i: (0, i))
    return pl.pallas_call(
        _router_kernel,
        grid=(s // tm,),
        in_specs=[pl.BlockSpec((tm, d), lambda i: (i, 0)),
                  _const_spec((d, LANES)), _const_spec((d, LANES)),
                  _const_spec((N_EXPERTS, 1)), _const_spec((tm, tm))],
        out_specs=[slot, slot, slot, _const_spec((N_EXPERTS, LANES))],
        out_shape=[jax.ShapeDtypeStruct((TOP_K, s), I32),
                   jax.ShapeDtypeStruct((TOP_K, s), F32),
                   jax.ShapeDtypeStruct((TOP_K, s), I32),
                   jax.ShapeDtypeStruct((N_EXPERTS, LANES), F32)],
        scratch_shapes=[pltpu.VMEM((N_EXPERTS, 1), F32)],
        compiler_params=_cparams("arbitrary"),
        name="router",
    )(v, w_hi, w_lo, e_bias_col, tri)


def _pack_pairs(x):
    h = x.shape[1] // 2
    lo = lax.bitcast_convert_type(x[:, :h].astype(BF16).astype(F32), U32)
    hi = lax.bitcast_convert_type(x[:, h:].astype(BF16).astype(F32), U32)
    return (lo >> 16) | (hi & jnp.uint32(0xFFFF0000))


def _unpack_pairs(w):
    lo = lax.bitcast_convert_type(w << 16, F32)
    hi = lax.bitcast_convert_type(w & jnp.uint32(0xFFFF0000), F32)
    return jnp.concatenate([lo, hi], axis=1)


def _dispatch_kernel(last_ref, has_ref, v_ref, idx_ref, rank_ref, pstart_ref,
                     dest_ref, xs_hbm, dest_v, dest_s, zeros_v, pk_v, sem_z, sem_i, sem_r, *, bm):
    tm = v_ref.shape[0]

    def zero_fill(e):
        row0 = pl.multiple_of(last_ref[e], bm)
        return pltpu.make_async_copy(zeros_v, xs_hbm.at[pl.ds(row0, bm), :], sem_z)

    @pl.when(pl.program_id(0) == 0)
    def _():
        zeros_v[...] = jnp.zeros(zeros_v.shape, zeros_v.dtype)

        def start(e, carry):
            @pl.when(has_ref[e] > 0)
            def _():
                zero_fill(e).start()
            return carry

        def wait(e, carry):
            @pl.when(has_ref[e] > 0)
            def _():
                zero_fill(e).wait()
            return carry

        lax.fori_loop(0, N_EXPERTS, start, 0)
        lax.fori_loop(0, N_EXPERTS, wait, 0)

    ie = lax.broadcasted_iota(I32, (N_EXPERTS, tm), 0)
    pstart = pstart_ref[...]
    for k in range(TOP_K):
        hit = ie == idx_ref[k:k + 1, :]
        seg = jnp.sum(jnp.where(hit, pstart, 0.0), axis=0, keepdims=True)
        dest_v[k:k + 1, :] = seg.astype(I32) + rank_ref[k:k + 1, :]
    dest_ref[...] = dest_v[...]
    cp = pltpu.make_async_copy(dest_v, dest_s, sem_i)
    cp.start()
    pk_v[...] = _pack_pairs(v_ref[...])
    cp.wait()

    for t in range(tm):
        for k in range(TOP_K):
            pltpu.make_async_copy(pk_v.at[pl.ds(t, 1), :],
                                  xs_hbm.at[pl.ds(dest_s[k, t], 1), :], sem_r).start(priority=k % 2)
    for k in range(TOP_K):
        pltpu.make_async_copy(pk_v, xs_hbm.at[pl.ds(0, tm), :], sem_r).wait()


def _dispatch(last_row, has_blk, v, idx, rank, pstart_col, n_rows, tm, bm):
    s, d = v.shape
    slot = pl.BlockSpec((TOP_K, tm), lambda i, *_: (0, i))
    grid_spec = pltpu.PrefetchScalarGridSpec(
        num_scalar_prefetch=2,
        grid=(s // tm,),
        in_specs=[pl.BlockSpec((tm, d), lambda i, *_: (i, 0)), slot, slot,
                  pl.BlockSpec((N_EXPERTS, 1), lambda i, *_: (0, 0))],
        out_specs=[slot, pl.BlockSpec(memory_space=pl.ANY)],
        scratch_shapes=[pltpu.VMEM((TOP_K, tm), I32), pltpu.SMEM((TOP_K, tm), I32),
                        pltpu.VMEM((bm, d // 2), U32), pltpu.VMEM((tm, d // 2), U32),
                        pltpu.SemaphoreType.DMA, pltpu.SemaphoreType.DMA,
                        pltpu.SemaphoreType.DMA])
    return pl.pallas_call(
        functools.partial(_dispatch_kernel, bm=bm),
        grid_spec=grid_spec,
        out_shape=[jax.ShapeDtypeStruct((TOP_K, s), I32),
                   jax.ShapeDtypeStruct((n_rows, d // 2), U32)],
        compiler_params=_cparams("arbitrary"),
        name="dispatch",
    )(last_row, has_blk, v, idx, rank, pstart_col)


def _expert_kernel(blk_e_ref, n_used_ref, grp_ref, nxt_ref, x_ref, w1_hbm, w3_hbm, w2_hbm, y_ref,
                   wf1, wf3, wf2, w1b, w3b, w2b, sem_w):
    b = pl.program_id(0)

    def weight_copies(e, slot):
        return (pltpu.make_async_copy(w1_hbm.at[e], wf1.at[slot], sem_w.at[slot, 0]),
                pltpu.make_async_copy(w3_hbm.at[e], wf3.at[slot], sem_w.at[slot, 1]),
                pltpu.make_async_copy(w2_hbm.at[e], wf2.at[slot], sem_w.at[slot, 2]))

    @pl.when(b < n_used_ref[0])
    def _():
        e = blk_e_ref[b]
        slot = grp_ref[b] % 2
        prev_e = blk_e_ref[jnp.maximum(b - 1, 0)]

        @pl.when(b == 0)
        def _():
            for cp in weight_copies(e, slot):
                cp.start()

        @pl.when(jnp.logical_or(b == 0, e != prev_e))
        def _():
            for cp in weight_copies(e, slot):
                cp.wait()
            w1b[...] = wf1[slot].astype(BF16)
            w3b[...] = wf3[slot].astype(BF16)
            w2b[...] = wf2[slot].astype(BF16)
            nxt = nxt_ref[b]

            @pl.when(nxt != e)
            def _():
                for cp in weight_copies(nxt, 1 - slot):
                    cp.start()

        xb = _unpack_pairs(x_ref[...]).astype(BF16)
        h1 = jnp.dot(xb, w1b[...], preferred_element_type=F32)
        h3 = jnp.dot(xb, w3b[...], preferred_element_type=F32)
        act = (h1 * jax.nn.sigmoid(h1) * h3).astype(BF16)
        y_ref[...] = _pack_pairs(jnp.dot(act, w2b[...], preferred_element_type=F32))


def _experts(blk_e, n_used, xs, w_e1, w_e3, w_e2, bm):
    n_rows = xs.shape[0]
    d, ff = w_e1.shape[-2:]
    rows = lambda b, be, nu, gr, nx: (jnp.minimum(b, nu[0] - 1), 0)
    grp = jnp.cumsum(jnp.concatenate([jnp.zeros((1,), I32),
                                      (blk_e[1:] != blk_e[:-1]).astype(I32)])).astype(I32)
    later = jnp.where(blk_e[None, :] > blk_e[:, None], blk_e[None, :], N_EXPERTS)
    nxt = jnp.min(later, axis=1)
    nxt = jnp.where(nxt == N_EXPERTS, blk_e, nxt).astype(I32)
    any_spec = pl.BlockSpec(memory_space=pl.ANY)
    grid_spec = pltpu.PrefetchScalarGridSpec(
        num_scalar_prefetch=4,
        grid=(n_rows // bm,),
        in_specs=[pl.BlockSpec((bm, d // 2), rows), any_spec, any_spec, any_spec],
        out_specs=pl.BlockSpec((bm, d // 2), rows),
        scratch_shapes=[pltpu.VMEM((2, d, ff), F32), pltpu.VMEM((2, d, ff), F32),
                        pltpu.VMEM((2, ff, d), F32),
                        pltpu.VMEM((d, ff), BF16), pltpu.VMEM((d, ff), BF16),
                        pltpu.VMEM((ff, d), BF16),
                        pltpu.SemaphoreType.DMA((2, 3))])
    return pl.pallas_call(
        _expert_kernel,
        grid_spec=grid_spec,
        out_shape=jax.ShapeDtypeStruct((n_rows, d // 2), U32),
        compiler_params=_cparams("arbitrary"),
        name="experts",
    )(blk_e, n_used, grp, nxt, xs, w_e1, w_e3, w_e2)


def _block_tables(counts, n_blocks, bm):
    padded = (counts + bm - 1) // bm * bm
    pad_end = jnp.cumsum(padded)
    pad_start = pad_end - padded
    blk_first = jnp.arange(n_blocks, dtype=I32) * bm
    blk_e = jnp.minimum(jnp.sum(pad_end[None, :] <= blk_first[:, None], axis=1), N_EXPERTS - 1)
    n_used = pad_end[-1] // bm
    blk_e = jnp.where(jnp.arange(n_blocks) < n_used, blk_e, blk_e[jnp.maximum(n_used - 1, 0)])
    last_row = jnp.maximum(pad_end - bm, 0)
    return (pad_start, last_row.astype(I32), (padded > 0).astype(I32),
            blk_e.astype(I32), n_used.astype(I32).reshape(1))


def _combine_kernel(dest0_ref, dnext_ref, wts_ref, v_ref, h1_ref, g2_ref, lg_ref, lb_ref,
                    ws1_ref, ws3_ref, ws2_ref, ys_hbm, o_ref, dest_s, ybuf_a, ybuf_b, sem_i, sem_r):
    tm = v_ref.shape[0]
    i = pl.program_id(0)
    ybufs = (ybuf_a, ybuf_b)

    def start_gathers(slot):
        for t in range(tm):
            for k in range(TOP_K):
                pltpu.make_async_copy(ys_hbm.at[pl.ds(dest_s[slot, k, t], 1), :],
                                      ybufs[slot].at[k, pl.ds(t, 1), :],
                                      sem_r.at[slot]).start(priority=k % 2)

    def wait_gathers(slot):
        for k in range(TOP_K):
            pltpu.make_async_copy(ys_hbm.at[pl.ds(0, tm), :], ybufs[slot].at[k], sem_r.at[slot]).wait()

    @pl.when(i == 0)
    def _():
        cp = pltpu.make_async_copy(dest0_ref, dest_s.at[0], sem_i)
        cp.start()
        cp.wait()

        def first_tile(t, carry):
            for k in range(TOP_K):
                pltpu.make_async_copy(ys_hbm.at[pl.ds(dest_s[0, k, t], 1), :],
                                      ybuf_a.at[k, pl.ds(t, 1), :], sem_r.at[0]).start()
            return carry

        lax.fori_loop(0, tm, first_tile, 0)

    def tile(slot):
        other = 1 - slot
        cp = pltpu.make_async_copy(dnext_ref, dest_s.at[other], sem_i)
        cp.start()
        cp.wait()
        wait_gathers(slot)
        start_gathers(other)
        vb = v_ref[...].astype(BF16)
        a1 = jnp.dot(vb, ws1_ref[...], preferred_element_type=F32)
        a3 = jnp.dot(vb, ws3_ref[...], preferred_element_type=F32)
        act = (a1 * jax.nn.sigmoid(a1) * a3).astype(BF16)
        ff = jnp.dot(act, ws2_ref[...], preferred_element_type=F32)
        wpad = jnp.concatenate([wts_ref[...], jnp.zeros((LANES - TOP_K, tm), F32)], axis=0)
        wt = wpad.T
        for k in range(TOP_K):
            ff = ff + _unpack_pairs(ybufs[slot][k]) * wt[:, k:k + 1]
        o_ref[...] = _layer_norm(DEEPNORM_ALPHA * h1_ref[...] + g2_ref[...] * ff,
                                 lg_ref[...], lb_ref[...])

        @pl.when(i == pl.num_programs(0) - 1)
        def _():
            wait_gathers(other)

    @pl.when(i % 2 == 0)
    def _():
        tile(0)

    @pl.when(i % 2 == 1)
    def _():
        tile(1)


def _combine(dest, wts, v, h1, g2, ln_g, ln_b, w_s1, w_s3, w_s2, ys, tm):
    s, d = v.shape
    n = s // tm
    row = lambda i: (i, 0)
    vec = _const_spec((1, d))
    slot = pl.BlockSpec((TOP_K, tm), lambda i: (0, i))
    single = lambda shape: pl.BlockSpec(shape, lambda i: (0, 0), pipeline_mode=pl.Buffered(1))
    return pl.pallas_call(
        _combine_kernel,
        grid=(n,),
        in_specs=[pl.BlockSpec((TOP_K, tm), lambda i: (0, 0)),
                  pl.BlockSpec((TOP_K, tm), lambda i: (0, jnp.minimum(i + 1, n - 1))),
                  slot, pl.BlockSpec((tm, d), row), pl.BlockSpec((tm, d), row),
                  vec, vec, vec,
                  single(w_s1.shape), single(w_s3.shape), single(w_s2.shape),
                  pl.BlockSpec(memory_space=pl.ANY)],
        out_specs=pl.BlockSpec((tm, d), row),
        out_shape=jax.ShapeDtypeStruct((s, d), F32),
        scratch_shapes=[pltpu.SMEM((2, TOP_K, tm), I32),
                        pltpu.VMEM((TOP_K, tm, d // 2), U32), pltpu.VMEM((TOP_K, tm, d // 2), U32),
                        pltpu.SemaphoreType.DMA, pltpu.SemaphoreType.DMA((2,))],
        compiler_params=_cparams("arbitrary"),
        name="combine",
    )(dest, dest, wts, v, h1, g2, ln_g, ln_b, w_s1, w_s3, w_s2, ys)


def _rope_tables(s):
    half = HEAD_DIM // 4
    inv_freq = ROPE_THETA ** (-np.arange(half, dtype=np.float32) / half)
    t = np.arange(s)
    ang_r = (t // GRID_W).astype(np.float32)[:, None] * inv_freq[None, :]
    ang_c = (t % GRID_W).astype(np.float32)[:, None] * inv_freq[None, :]
    cos = np.concatenate([np.cos(ang_r)] * 2 + [np.cos(ang_c)] * 2, axis=-1)
    sin = np.concatenate([-np.sin(ang_r), np.sin(ang_r), -np.sin(ang_c), np.sin(ang_c)], axis=-1)
    return jnp.asarray(cos, F32), jnp.asarray(sin, F32)


def kernel(x, c, ctx, c_ctx, w_mod, b_mod, w_in, q_norm, k_norm, conv_w, conv_b, rg_wa, rg_ba,
           rg_wx, rg_bx, rg_lam, w_out, ln1_g, ln1_b, w_router, e_bias, w_e1, w_e3, w_e2,
           w_s1, w_s3, w_s2, ln2_g, ln2_b):
    assert x.shape[0] == 1 and w_mod.shape[0] == DEPTH
    _, s, d = x.shape
    n_ctx = ctx.shape[1]
    x2 = x[0]

    c2t = jnp.stack([c[0], c_ctx], axis=1)
    mod = _modulation(c2t, w_mod[0], b_mod[0].reshape(1, -1))
    sh1, sc1, g1, sh2, sc2, g2 = [mod[0:1, j * d:(j + 1) * d] for j in range(6)]
    csh1, csc1 = mod[1:2, 0:d], mod[1:2, d:2 * d]

    w_in_b = w_in[0].astype(BF16)
    qg = q_norm[0].reshape(1, HEAD_DIM)
    kg = k_norm[0].reshape(1, HEAD_DIM)
    cos, sin = _rope_tables(s)
    q_l, k_l, vt_l, xr_l, yr_l = _in_projection(x2, sc1, sh1, w_in_b, qg, kg, cos, sin,
                                                tm=min(INPROJ_ROWS, s), kv_chunk=min(KV_CHUNK, s))
    _, k_c, vt_c, xr_c, _ = _in_projection(
        ctx[0], csc1, csh1, w_in_b, qg, kg,
        jnp.ones((n_ctx, HEAD_DIM), F32), jnp.zeros((n_ctx, HEAD_DIM), F32),
        tm=n_ctx, kv_chunk=n_ctx)

    attn = _attention(q_l, k_c, vt_c, k_l, vt_l, tq=min(ATTN_QUERIES, s))

    wg = jnp.concatenate([rg_wa[0], rg_wx[0]], axis=-1).astype(BF16)
    cb = conv_b[0].reshape(1, RG_W)
    zero_state = jnp.zeros((1, RG_W), F32)
    rg_args = []
    for dd in range(2):
        rg_args.append((conv_w[0], cb, wg[dd],
                        jnp.stack([rg_ba[0, dd], rg_bx[0, dd]], axis=0),
                        rg_lam[0, dd].reshape(1, RG_W)))
    t_rg = min(RG_ROWS, s)
    hc_f = _rg_scan(xr_c, *rg_args[0], zero_state, reverse=False, t=n_ctx)
    hc_b = _rg_scan(xr_c, *rg_args[1], zero_state, reverse=True, t=n_ctx)
    h_f = _rg_scan(xr_l, *rg_args[0], hc_f[n_ctx - 1:n_ctx], reverse=False, t=t_rg)
    rg = _rg_scan(xr_l, *rg_args[1], hc_b[0:1], reverse=True, t=t_rg, hf=h_f, yr=yr_l)

    h1, v = _out_projection(attn, rg, x2, w_out[0].astype(BF16), g1,
                            ln1_g[0].reshape(1, d), ln1_b[0].reshape(1, d), sc2, sh2,
                            tm=min(OUTPROJ_ROWS, s))

    wr = jnp.pad(w_router[0], ((0, 0), (0, LANES - N_EXPERTS)))
    wr_hi = wr.astype(BF16)
    wr_lo = (wr - wr_hi.astype(F32)).astype(BF16)
    idx, wts, rank, cnt = _router(v, wr_hi, wr_lo, e_bias[0].reshape(N_EXPERTS, 1),
                                  tm=min(ROUTER_ROWS, s))

    bm = min(EXPERT_BLOCK, s)
    n_blocks = s * TOP_K // bm + N_EXPERTS
    pad_start, last_row, has_blk, blk_e, n_used = _block_tables(cnt[:, 0].astype(I32), n_blocks, bm)
    dest, xs = _dispatch(last_row, has_blk, v, idx, rank,
                         pad_start.astype(F32).reshape(N_EXPERTS, 1), n_blocks * bm,
                         tm=min(DISPATCH_ROWS, s), bm=bm)
    ys = _experts(blk_e, n_used, xs, w_e1[0], w_e3[0], w_e2[0], bm)

    out = _combine(dest, wts, v, h1, g2, ln2_g[0].reshape(1, d), ln2_b[0].reshape(1, d),
                   w_s1[0].astype(BF16), w_s3[0].astype(BF16), w_s2[0].astype(BF16), ys,
                   tm=min(COMBINE_ROWS, s))
    return out[None]
```

```python
import functools

import jax
import jax.numpy as jnp
import numpy as np
from jax import lax
from jax.experimental import pallas as pl
from jax.experimental.pallas import tpu as pltpu

F32 = jnp.float32
BF16 = jnp.bfloat16
I32 = jnp.int32
U32 = jnp.uint32

GRID_W = 64
HEAD_DIM = 128
N_HEADS = 8
N_KV_HEADS = 2
GQA_GROUP = N_HEADS // N_KV_HEADS
ATTN_W = N_HEADS * HEAD_DIM
KV_W = N_KV_HEADS * HEAD_DIM
ROPE_THETA = 10000.0
RG_W = 1024
RG_HEADS = 8
RG_HD = RG_W // RG_HEADS
RG_C = 8.0
PROJ_W = ATTN_W + 2 * KV_W + 2 * RG_W
N_EXPERTS = 64
N_GROUPS = 8
GROUP_SIZE = N_EXPERTS // N_GROUPS
TOPK_GROUPS = 4
TOP_K = 8
ROUTED_SCALE = 2.5
NORM_EPS = 1e-6
DEPTH = 1
DEEPNORM_ALPHA = (2.0 * DEPTH) ** 0.25
LOG2E = 1.4426950408889634

LANES = 128
SUBLANES = 8
BF16_SUBLANES = 16
VMEM_LIMIT = 56 * 1024 * 1024

NEG_BIG = -1e30
ROPE_AXIS_DIM = HEAD_DIM // 2
VT_ROWS = HEAD_DIM + BF16_SUBLANES

MOD_COLS = 1024
INPROJ_ROWS = 512
ATTN_QUERIES = 256
KV_CHUNK = 512
ATTN_UNROLL = 16
RG_ROWS = 512
OUTPROJ_ROWS = 256
ROUTER_ROWS = 256
EXPERT_BLOCK = 512
DISPATCH_ROWS = 256
COMBINE_ROWS = 256


def _cparams(*sem):
    return pltpu.CompilerParams(dimension_semantics=sem, vmem_limit_bytes=VMEM_LIMIT)


def _const_spec(shape):
    nd = len(shape)
    return pl.BlockSpec(shape, lambda *_: (0,) * nd)


def _mod_kernel(ct_ref, w_ref, b_ref, o_ref, sb_ref, *, tn):
    d = w_ref.shape[0]

    @pl.when(pl.program_id(0) == 0)
    def _():
        ct = ct_ref[...]
        s = ct * jax.nn.sigmoid(ct)
        sb_ref[0] = jnp.broadcast_to(s[:, 0:1], (d, LANES))
        sb_ref[1] = jnp.broadcast_to(s[:, 1:2], (d, LANES))

    for c in range(tn // LANES):
        sl = slice(c * LANES, (c + 1) * LANES)
        wc = w_ref[:, sl]
        bc = b_ref[:, sl]
        o0 = jnp.sum(wc * sb_ref[0], axis=0, keepdims=True) + bc
        o1 = jnp.sum(wc * sb_ref[1], axis=0, keepdims=True) + bc
        o_ref[:, sl] = jnp.concatenate(
            [o0, o1, jnp.zeros((SUBLANES - 2, LANES), F32)], axis=0)


def _modulation(c2t, w_mod, b_mod):
    d, n = w_mod.shape
    tn = MOD_COLS
    return pl.pallas_call(
        functools.partial(_mod_kernel, tn=tn),
        grid=(n // tn,),
        in_specs=[_const_spec((d, 2)),
                  pl.BlockSpec((d, tn), lambda j: (0, j)),
                  pl.BlockSpec((1, tn), lambda j: (0, j))],
        out_specs=pl.BlockSpec((SUBLANES, tn), lambda j: (0, j)),
        out_shape=jax.ShapeDtypeStruct((SUBLANES, n), F32),
        scratch_shapes=[pltpu.VMEM((2, d, LANES), F32)],
        compiler_params=_cparams("arbitrary"),
        name="modulation",
    )(c2t, w_mod, b_mod)


def _swap_half(y):
    half = ROPE_AXIS_DIM // 2
    lane = lax.broadcasted_iota(I32, y.shape, 1)
    return jnp.where((lane % ROPE_AXIS_DIM) < half,
                     pltpu.roll(y, LANES - half, 1), pltpu.roll(y, half, 1))


def _norm_rope(ph, g, cos, sin_signed, scale):
    ms = jnp.mean(ph * ph, axis=-1, keepdims=True)
    y = (ph * lax.rsqrt(ms + NORM_EPS)) * g
    y = y * cos + _swap_half(y) * sin_signed
    if scale != 1.0:
        y = y * scale
    return y


def _inproj_kernel(x_ref, sc_ref, sh_ref, w_ref, qg_ref, kg_ref, cos_ref, sin_ref,
                   q_ref, k_ref, vt_ref, xr_ref, yr_ref, *, q_scale):
    tm = x_ref.shape[0]
    kc = vt_ref.shape[-1]
    u = (x_ref[...] * (1.0 + sc_ref[...]) + sh_ref[...]).astype(BF16)
    cos = cos_ref[...]
    sin = sin_ref[...]
    o = 0
    pq = jnp.dot(u, w_ref[:, o:o + ATTN_W], preferred_element_type=F32)
    for h in range(N_HEADS):
        sl = slice(h * HEAD_DIM, (h + 1) * HEAD_DIM)
        q_ref[:, sl] = _norm_rope(pq[:, sl], qg_ref[...], cos, sin, q_scale).astype(BF16)
    o += ATTN_W
    pk = jnp.dot(u, w_ref[:, o:o + KV_W], preferred_element_type=F32)
    for h in range(N_KV_HEADS):
        sl = slice(h * HEAD_DIM, (h + 1) * HEAD_DIM)
        k_ref[:, sl] = _norm_rope(pk[:, sl], kg_ref[...], cos, sin, 1.0).astype(BF16)
    o += KV_W
    pv = jnp.dot(u, w_ref[:, o:o + KV_W], preferred_element_type=F32)
    ones_rows = jnp.where(
        lax.broadcasted_iota(I32, (VT_ROWS - HEAD_DIM, kc), 0) == 0, 1.0, 0.0).astype(BF16)
    for h in range(N_KV_HEADS):
        for cc in range(tm // kc):
            blk = pv[cc * kc:(cc + 1) * kc, h * HEAD_DIM:(h + 1) * HEAD_DIM]
            vt_ref[h, cc, 0:HEAD_DIM, :] = blk.T.astype(BF16)
            vt_ref[h, cc, HEAD_DIM:VT_ROWS, :] = ones_rows
    o += KV_W
    xr_ref[...] = jnp.dot(u, w_ref[:, o:o + RG_W], preferred_element_type=F32)
    o += RG_W
    yr_ref[...] = jnp.dot(u, w_ref[:, o:o + RG_W], preferred_element_type=F32)


def _in_projection(x, sc, sh, w_in, qg, kg, cos, sin, tm, kv_chunk):
    s, d = x.shape
    row = lambda i: (i, 0)
    if kv_chunk >= tm:
        per = kv_chunk // tm
        vt_spec = pl.BlockSpec((N_KV_HEADS, 1, VT_ROWS, tm), lambda i: (0, i // per, 0, i % per))
    else:
        vt_spec = pl.BlockSpec((N_KV_HEADS, tm // kv_chunk, VT_ROWS, kv_chunk),
                               lambda i: (0, i, 0, 0))
    return pl.pallas_call(
        functools.partial(_inproj_kernel, q_scale=HEAD_DIM ** -0.5 * LOG2E),
        grid=(s // tm,),
        in_specs=[pl.BlockSpec((tm, d), row),
                  _const_spec((1, d)), _const_spec((1, d)),
                  pl.BlockSpec((d, PROJ_W), lambda i: (0, 0), pipeline_mode=pl.Buffered(1)),
                  _const_spec((1, HEAD_DIM)), _const_spec((1, HEAD_DIM)),
                  pl.BlockSpec((tm, HEAD_DIM), row), pl.BlockSpec((tm, HEAD_DIM), row)],
        out_specs=[pl.BlockSpec((tm, ATTN_W), row), pl.BlockSpec((tm, KV_W), row),
                   vt_spec,
                   pl.BlockSpec((tm, RG_W), row), pl.BlockSpec((tm, RG_W), row)],
        out_shape=[jax.ShapeDtypeStruct((s, ATTN_W), BF16),
                   jax.ShapeDtypeStruct((s, KV_W), BF16),
                   jax.ShapeDtypeStruct((N_KV_HEADS, s // kv_chunk, VT_ROWS, kv_chunk), BF16),
                   jax.ShapeDtypeStruct((s, RG_W), F32),
                   jax.ShapeDtypeStruct((s, RG_W), F32)],
        compiler_params=_cparams("arbitrary"),
        name="in_projection",
    )(x, sc, sh, w_in, qg, kg, cos, sin)


def _attn_kernel(q_ref, kc_ref, vtc_ref, kl_ref, vtl_ref, o_ref, sa_ref, sb_ref, pa_ref, pb_ref, sc_ref,
                 *, tq, tk, n_chunks, unroll):
    q = q_ref[...]
    qs = jnp.concatenate(
        [q[:, g * HEAD_DIM:(g + 1) * HEAD_DIM] for g in range(GQA_GROUP)], axis=0)
    cols = GQA_GROUP * tq

    def scores(k):
        return lax.dot_general(k, qs, (((1,), (1,)), ((), ())), preferred_element_type=F32)

    def softmax(s_ref, m):
        m_new = jnp.maximum(m, jnp.max(s_ref[...], axis=0, keepdims=True))
        return m_new, jnp.exp2(m - m_new), jnp.exp2(s_ref[...] - m_new).astype(BF16)

    def weighted_values(vt, p, alpha, acc):
        return alpha * acc + jnp.dot(vt, p, preferred_element_type=F32)

    m0 = jnp.full((1, cols), NEG_BIG, F32)
    a0 = jnp.zeros((VT_ROWS, cols), F32)
    sc_ref[...] = scores(kc_ref[...])
    m, alpha, p = softmax(sc_ref, m0)
    acc = weighted_values(vtc_ref[0], p, alpha, a0)

    def latent_scores(j):
        j = jnp.minimum(j, n_chunks - 1)
        return scores(kl_ref[pl.ds(pl.multiple_of(j * tk, tk), tk), :])

    s_refs = (sa_ref, sb_ref)
    p_refs = (pa_ref, pb_ref)
    sa_ref[...] = latent_scores(0)
    sb_ref[...] = latent_scores(1)
    m, alpha, pa_ref[...] = softmax(sa_ref, m)

    def body(i, carry):
        m, acc, alpha = carry
        for u in range(unroll):
            c = unroll * i + u
            cur, nxt = u % 2, (u + 1) % 2
            s_refs[cur][...] = latent_scores(c + 2)
            m, alpha_next, p_refs[nxt][...] = softmax(s_refs[nxt], m)
            acc = weighted_values(vtl_ref[c], p_refs[cur][...], alpha, acc)
            alpha = alpha_next
        return m, acc, alpha

    _, acc, _ = lax.fori_loop(0, n_chunks // unroll, body, (m, acc, alpha))
    out_t = acc[0:HEAD_DIM, :] / acc[HEAD_DIM:HEAD_DIM + 1, :]
    for g in range(GQA_GROUP):
        o_ref[:, g * HEAD_DIM:(g + 1) * HEAD_DIM] = out_t[:, g * tq:(g + 1) * tq].T.astype(BF16)


def _attention(q, k_c, vt_c, k_l, vt_l, tq):
    s = q.shape[0]
    n_ctx = k_c.shape[0]
    n_chunks, _, tk = vt_l.shape[1:]
    unroll = min(ATTN_UNROLL, n_chunks)
    assert n_chunks % unroll == 0 and unroll % 2 == 0
    gw = GQA_GROUP * HEAD_DIM
    return pl.pallas_call(
        functools.partial(_attn_kernel, tq=tq, tk=tk, n_chunks=n_chunks, unroll=unroll),
        grid=(N_KV_HEADS, s // tq),
        in_specs=[pl.BlockSpec((tq, gw), lambda h, i: (i, h)),
                  pl.BlockSpec((n_ctx, HEAD_DIM), lambda h, i: (0, h)),
                  pl.BlockSpec((None, 1, VT_ROWS, n_ctx), lambda h, i: (h, 0, 0, 0)),
                  pl.BlockSpec((s, HEAD_DIM), lambda h, i: (0, h)),
                  pl.BlockSpec((None, n_chunks, VT_ROWS, tk), lambda h, i: (h, 0, 0, 0))],
        out_specs=pl.BlockSpec((tq, gw), lambda h, i: (i, h)),
        out_shape=jax.ShapeDtypeStruct((s, ATTN_W), BF16),
        scratch_shapes=[pltpu.VMEM((tk, GQA_GROUP * tq), F32),
                        pltpu.VMEM((tk, GQA_GROUP * tq), F32),
                        pltpu.VMEM((tk, GQA_GROUP * tq), BF16),
                        pltpu.VMEM((tk, GQA_GROUP * tq), BF16),
                        pltpu.VMEM((n_ctx, GQA_GROUP * tq), F32)],
        compiler_params=_cparams("arbitrary", "arbitrary"),
        name="attention",
    )(q, k_c, vt_c, k_l, vt_l)


def _log_sigmoid(x):
    return jnp.minimum(x, 0.0) - jnp.log1p(jnp.exp(-jnp.abs(x)))


def _rg_kernel(x_ref, xp_ref, xn_ref, cw_ref, cb_ref, wg_ref, bg_ref, lam_ref, h0_ref,
               *rest, reverse, final, t, n_chunks):
    if final:
        hf_ref, yr_ref, o_ref, a_scr, b_scr, hc_scr, h_scr = rest
    else:
        o_ref, a_scr, b_scr, hc_scr = rest
        h_scr = o_ref
    i = pl.program_id(0)
    c = (n_chunks - 1 - i) if reverse else i
    w = x_ref.shape[1]

    @pl.when(i == 0)
    def _():
        hc_scr[...] = jnp.broadcast_to(h0_ref[...], (SUBLANES, w))

    x = x_ref[...]
    row = lax.broadcasted_iota(I32, (t, w), 0)
    pm = jnp.where(c == 0, 0.0, 1.0).astype(F32)
    nm = jnp.where(c == n_chunks - 1, 0.0, 1.0).astype(F32)
    p6 = xp_ref[SUBLANES - 2:SUBLANES - 1, :] * pm
    p7 = xp_ref[SUBLANES - 1:SUBLANES, :] * pm
    n0 = xn_ref[0:1, :] * nm
    x_m1 = jnp.where(row == 0, p7, pltpu.roll(x, 1, 0))
    x_m2 = jnp.where(row == 0, p6, jnp.where(row == 1, p7, pltpu.roll(x, 2, 0)))
    x_p1 = jnp.where(row == t - 1, n0, pltpu.roll(x, t - 1, 0))
    xc = cb_ref[...] + cw_ref[0:1, :] * x_m2
    xc = xc + cw_ref[1:2, :] * x_m1
    xc = xc + cw_ref[2:3, :] * x
    xc = xc + cw_ref[3:4, :] * x_p1

    xcb = xc.astype(BF16)
    clam = RG_C * _log_sigmoid(lam_ref[...])
    for h in range(RG_HEADS):
        sl = slice(h * RG_HD, (h + 1) * RG_HD)
        g = jnp.dot(xcb[:, sl], wg_ref[h], preferred_element_type=F32)
        r = jax.nn.sigmoid(g[:, :RG_HD] + bg_ref[0:1, sl])
        gi = jax.nn.sigmoid(g[:, RG_HD:] + bg_ref[1:2, sl])
        log_a = r * clam[:, sl]
        a = jnp.exp(log_a)
        a_scr[:, sl] = a
        b_scr[:, sl] = jnp.sqrt(-jnp.tanh(log_a) * (a * a + 1.0)) * (gi * xc[:, sl])

    srow = lax.broadcasted_iota(I32, (SUBLANES, w), 0)
    n_tiles = t // SUBLANES

    def tile_body(j, hprev):
        tile = (n_tiles - 1 - j) if reverse else j
        start = pl.multiple_of(tile * SUBLANES, SUBLANES)
        a = a_scr[pl.ds(start, SUBLANES), :]
        b = b_scr[pl.ds(start, SUBLANES), :]
        for k in (1, 2, 4):
            if reverse:
                keep = srow < SUBLANES - k
                shift = SUBLANES - k
            else:
                keep = srow >= k
                shift = k
            a_sh = jnp.where(keep, pltpu.roll(a, shift, 0), 1.0)
            b_sh = jnp.where(keep, pltpu.roll(b, shift, 0), 0.0)
            b = a * b_sh + b
            a = a * a_sh
        hh = a * hprev + b
        h_scr[pl.ds(start, SUBLANES), :] = hh
        last = hh[0:1, :] if reverse else hh[SUBLANES - 1:SUBLANES, :]
        return jnp.broadcast_to(last, (SUBLANES, w))

    hc_scr[...] = lax.fori_loop(0, n_tiles, tile_body, hc_scr[...])

    if final:
        gate = jax.nn.gelu(yr_ref[...], approximate=True)
        o_ref[...] = ((hf_ref[...] + h_scr[...]) * gate).astype(o_ref.dtype)


def _rg_scan(xr, conv_w, conv_b, wg, bg, lam, h0, *, reverse, t, hf=None, yr=None):
    s, w = xr.shape
    n_chunks = s // t
    final = hf is not None
    tb = t // SUBLANES
    last_blk = s // SUBLANES - 1
    if reverse:
        cidx = lambda i: n_chunks - 1 - i
    else:
        cidx = lambda i: i
    chunk_spec = pl.BlockSpec((t, w), lambda i: (cidx(i), 0))
    in_specs = [chunk_spec,
                pl.BlockSpec((SUBLANES, w), lambda i: (jnp.maximum(cidx(i) * tb - 1, 0), 0)),
                pl.BlockSpec((SUBLANES, w), lambda i: (jnp.minimum((cidx(i) + 1) * tb, last_blk), 0)),
                _const_spec((4, w)), _const_spec((1, w)),
                _const_spec((RG_HEADS, RG_HD, 2 * RG_HD)), _const_spec((2, w)),
                _const_spec((1, w)), _const_spec((1, w))]
    args = [xr, xr, xr, conv_w, conv_b, wg, bg, lam, h0]
    scratch = [pltpu.VMEM((t, w), F32), pltpu.VMEM((t, w), F32), pltpu.VMEM((SUBLANES, w), F32)]
    if final:
        in_specs += [chunk_spec, chunk_spec]
        args += [hf, yr]
        scratch.append(pltpu.VMEM((t, w), F32))
        out_dtype = BF16
    else:
        out_dtype = F32
    return pl.pallas_call(
        functools.partial(_rg_kernel, reverse=reverse, final=final, t=t, n_chunks=n_chunks),
        grid=(n_chunks,),
        in_specs=in_specs,
        out_specs=chunk_spec,
        out_shape=jax.ShapeDtypeStruct((s, w), out_dtype),
        scratch_shapes=scratch,
        compiler_params=_cparams("arbitrary"),
        name="rglru_bwd" if reverse else "rglru_fwd",
    )(*args)


def _layer_norm(y, g, b):
    mu = jnp.mean(y, axis=-1, keepdims=True)
    yc = y - mu
    var = jnp.mean(yc * yc, axis=-1, keepdims=True)
    return yc * lax.rsqrt(var + NORM_EPS) * g + b


def _outproj_kernel(attn_ref, rg_ref, x_ref, w_ref, g1_ref, lg_ref, lb_ref, sc2_ref, sh2_ref,
                    h1_ref, v_ref):
    mix = jnp.dot(attn_ref[...], w_ref[0:ATTN_W, :], preferred_element_type=F32)
    mix = mix + jnp.dot(rg_ref[...], w_ref[ATTN_W:, :], preferred_element_type=F32)
    h1 = _layer_norm(DEEPNORM_ALPHA * x_ref[...] + g1_ref[...] * mix, lg_ref[...], lb_ref[...])
    h1_ref[...] = h1
    v_ref[...] = h1 * (1.0 + sc2_ref[...]) + sh2_ref[...]


def _out_projection(attn, rg, x, w_out, g1, ln_g, ln_b, sc2, sh2, tm):
    s, d = x.shape
    row = lambda i: (i, 0)
    vec = _const_spec((1, d))
    return pl.pallas_call(
        _outproj_kernel,
        grid=(s // tm,),
        in_specs=[pl.BlockSpec((tm, ATTN_W), row), pl.BlockSpec((tm, RG_W), row),
                  pl.BlockSpec((tm, d), row),
                  pl.BlockSpec((ATTN_W + RG_W, d), lambda i: (0, 0), pipeline_mode=pl.Buffered(1)),
                  vec, vec, vec, vec, vec],
        out_specs=[pl.BlockSpec((tm, d), row), pl.BlockSpec((tm, d), row)],
        out_shape=[jax.ShapeDtypeStruct((s, d), F32), jax.ShapeDtypeStruct((s, d), F32)],
        compiler_params=_cparams("arbitrary"),
        name="out_projection",
    )(attn, rg, x, w_out, g1, ln_g, ln_b, sc2, sh2)


def _first_index_of_max(x, iota_f, axis):
    mx = jnp.max(x, axis=axis, keepdims=True)
    idx = jnp.min(jnp.where(x == mx, iota_f, float(N_EXPERTS)), axis=axis, keepdims=True)
    return mx, idx


def _router_kernel(v_ref, whi_ref, wlo_ref, eb_ref, tri_ref,
                   idx_ref, wts_ref, rank_ref, cnt_ref, base_scr):
    v = v_ref[...]
    tm = v.shape[0]

    @pl.when(pl.program_id(0) == 0)
    def _():
        base_scr[...] = jnp.zeros(base_scr.shape, F32)

    v_hi = v.astype(BF16)
    v_lo = (v - v_hi.astype(F32)).astype(BF16)
    logits = jnp.dot(v_hi, whi_ref[...], preferred_element_type=F32)
    logits = logits + jnp.dot(v_lo, whi_ref[...], preferred_element_type=F32)
    logits = logits + jnp.dot(v_hi, wlo_ref[...], preferred_element_type=F32)
    lt = logits.T[0:N_EXPERTS, :]
    scores = jax.nn.sigmoid(lt)
    biased = scores + eb_ref[...]
    neg_inf = float("-inf")

    ig = lax.broadcasted_iota(I32, (GROUP_SIZE, tm), 0).astype(F32)
    groups = [biased[g * GROUP_SIZE:(g + 1) * GROUP_SIZE, :] for g in range(N_GROUPS)]
    gscore = []
    for bg in groups:
        top1, i1 = _first_index_of_max(bg, ig, 0)
        top2 = jnp.max(jnp.where(ig == i1, neg_inf, bg), axis=0, keepdims=True)
        gscore.append(top1 + top2)

    masked = []
    for g in range(N_GROUPS):
        ahead = jnp.zeros((1, tm), F32)
        for o in range(N_GROUPS):
            if o == g:
                continue
            before = (gscore[o] >= gscore[g]) if o < g else (gscore[o] > gscore[g])
            ahead = ahead + jnp.where(before, 1.0, 0.0)
        keep = jnp.broadcast_to(ahead < TOPK_GROUPS, (GROUP_SIZE, tm))
        masked.append(jnp.where(keep, groups[g], neg_inf))
    masked = jnp.concatenate(masked, axis=0)

    ie = lax.broadcasted_iota(I32, masked.shape, 0).astype(F32)
    seen = base_scr[...]
    ws = []
    for k in range(TOP_K):
        _, ei = _first_index_of_max(masked, ie, 0)
        hit = ie == ei
        idx_ref[k:k + 1, :] = ei.astype(I32)
        ws.append(jnp.sum(jnp.where(hit, scores, 0.0), axis=0, keepdims=True))
        masked = jnp.where(hit, neg_inf, masked)
        onehot = jnp.where(hit, 1.0, 0.0)
        before = jnp.dot(onehot.astype(BF16), tri_ref[...], preferred_element_type=F32)
        rank = jnp.sum(jnp.where(hit, before + seen, 0.0), axis=0, keepdims=True)
        rank_ref[k:k + 1, :] = rank.astype(I32)
        seen = seen + jnp.sum(onehot, axis=1, keepdims=True)
    base_scr[...] = seen
    cnt_ref[...] = jnp.broadcast_to(seen, cnt_ref.shape)
    total = ws[0]
    for k in range(1, TOP_K):
        total = total + ws[k]
    for k in range(TOP_K):
        wts_ref[k:k + 1, :] = ws[k] / total * ROUTED_SCALE


def _router(v, w_hi, w_lo, e_bias_col, tm):
    s, d = v.shape
    tri = jnp.triu(jnp.ones((tm, tm), BF16), k=1)
    slot = pl.BlockSpec((TOP_K, tm), lambda i: (0, i))
    return pl.pallas_call(
        _router_kernel,
        grid=(s // tm,),
        in_specs=[pl.BlockSpec((tm, d), lambda i: (i, 0)),
                  _const_spec((d, LANES)), _const_spec((d, LANES)),
                  _const_spec((N_EXPERTS, 1)), _const_spec((tm, tm))],
        out_specs=[slot, slot, slot, _const_spec((N_EXPERTS, LANES))],
        out_shape=[jax.ShapeDtypeStruct((TOP_K, s), I32),
                   jax.ShapeDtypeStruct((TOP_K, s), F32),
                   jax.ShapeDtypeStruct((TOP_K, s), I32),
                   jax.ShapeDtypeStruct((N_EXPERTS, LANES), F32)],
        scratch_shapes=[pltpu.VMEM((N_EXPERTS, 1), F32)],
        compiler_params=_cparams("arbitrary"),
        name="router",
    )(v, w_hi, w_lo, e_bias_col, tri)


def _pack_pairs(x):
    h = x.shape[1] // 2
    lo = lax.bitcast_convert_type(x[:, :h].astype(BF16).astype(F32), U32)
    hi = lax.bitcast_convert_type(x[:, h:].astype(BF16).astype(F32), U32)
    return (lo >> 16) | (hi & jnp.uint32(0xFFFF0000))


def _unpack_pairs(w):
    lo = lax.bitcast_convert_type(w << 16, F32)
    hi = lax.bitcast_convert_type(w & jnp.uint32(0xFFFF0000), F32)
    return jnp.concatenate([lo, hi], axis=1)


def _dispatch_kernel(last_ref, has_ref, v_ref, idx_ref, rank_ref, pstart_ref,
                     dest_ref, xs_hbm, dest_v, dest_s, zeros_v, pk_a, pk_b, sem_z, sem_i, sem_r,
                     *, bm):
    tm = v_ref.shape[0]

    def zero_fill(e):
        row0 = pl.multiple_of(last_ref[e], bm)
        return pltpu.make_async_copy(zeros_v, xs_hbm.at[pl.ds(row0, bm), :], sem_z)

    @pl.when(pl.program_id(0) == 0)
    def _():
        zeros_v[...] = jnp.zeros(zeros_v.shape, zeros_v.dtype)

        def start(e, carry):
            @pl.when(has_ref[e] > 0)
            def _():
                zero_fill(e).start()
            return carry

        def wait(e, carry):
            @pl.when(has_ref[e] > 0)
            def _():
                zero_fill(e).wait()
            return carry

        lax.fori_loop(0, N_EXPERTS, start, 0)
        lax.fori_loop(0, N_EXPERTS, wait, 0)

    ie = lax.broadcasted_iota(I32, (N_EXPERTS, tm), 0)
    pstart = pstart_ref[...]
    for k in range(TOP_K):
        hit = ie == idx_ref[k:k + 1, :]
        seg = jnp.sum(jnp.where(hit, pstart, 0.0), axis=0, keepdims=True)
        dest_v[k:k + 1, :] = seg.astype(I32) + rank_ref[k:k + 1, :]
    dest_ref[...] = dest_v[...]
    cp = pltpu.make_async_copy(dest_v, dest_s, sem_i)
    cp.start()
    packed = _pack_pairs(v_ref[...])
    cp.wait()
    i = pl.program_id(0)
    pk_bufs = (pk_a, pk_b)

    def drain(slot):
        for k in range(TOP_K):
            pltpu.make_async_copy(pk_bufs[slot], xs_hbm.at[pl.ds(0, tm), :], sem_r.at[slot]).wait()

    def tile(slot):
        pk = pk_bufs[slot]
        pk[...] = packed
        for t in range(tm):
            for k in range(TOP_K):
                pltpu.make_async_copy(pk.at[pl.ds(t, 1), :], xs_hbm.at[pl.ds(dest_s[k, t], 1), :],
                                      sem_r.at[slot]).start(priority=k % 2)

        @pl.when(i > 0)
        def _():
            drain(1 - slot)

        @pl.when(i == pl.num_programs(0) - 1)
        def _():
            drain(slot)

    @pl.when(i % 2 == 0)
    def _():
        tile(0)

    @pl.when(i % 2 == 1)
    def _():
        tile(1)


def _dispatch(last_row, has_blk, v, idx, rank, pstart_col, n_rows, tm, bm):
    s, d = v.shape
    slot = pl.BlockSpec((TOP_K, tm), lambda i, *_: (0, i))
    grid_spec = pltpu.PrefetchScalarGridSpec(
        num_scalar_prefetch=2,
        grid=(s // tm,),
        in_specs=[pl.BlockSpec((tm, d), lambda i, *_: (i, 0)), slot, slot,
                  pl.BlockSpec((N_EXPERTS, 1), lambda i, *_: (0, 0))],
        out_specs=[slot, pl.BlockSpec(memory_space=pl.ANY)],
        scratch_shapes=[pltpu.VMEM((TOP_K, tm), I32), pltpu.SMEM((TOP_K, tm), I32),
                        pltpu.VMEM((bm, d // 2), U32),
                        pltpu.VMEM((tm, d // 2), U32), pltpu.VMEM((tm, d // 2), U32),
                        pltpu.SemaphoreType.DMA, pltpu.SemaphoreType.DMA,
                        pltpu.SemaphoreType.DMA((2,))])
    return pl.pallas_call(
        functools.partial(_dispatch_kernel, bm=bm),
        grid_spec=grid_spec,
        out_shape=[jax.ShapeDtypeStruct((TOP_K, s), I32),
                   jax.ShapeDtypeStruct((n_rows, d // 2), U32)],
        compiler_params=_cparams("arbitrary"),
        name="dispatch",
    )(last_row, has_blk, v, idx, rank, pstart_col)


def _expert_kernel(blk_e_ref, n_used_ref, grp_ref, nxt_ref, x_ref, w1_hbm, w3_hbm, w2_hbm, y_ref,
                   wf1, wf3, wf2, w1b, w3b, w2b, sem_w):
    b = pl.program_id(0)

    def weight_copies(e, slot):
        return (pltpu.make_async_copy(w1_hbm.at[e], wf1.at[slot], sem_w.at[slot, 0]),
                pltpu.make_async_copy(w3_hbm.at[e], wf3.at[slot], sem_w.at[slot, 1]),
                pltpu.make_async_copy(w2_hbm.at[e], wf2.at[slot], sem_w.at[slot, 2]))

    @pl.when(b < n_used_ref[0])
    def _():
        e = blk_e_ref[b]
        slot = grp_ref[b] % 2
        prev_e = blk_e_ref[jnp.maximum(b - 1, 0)]

        @pl.when(b == 0)
        def _():
            for cp in weight_copies(e, slot):
                cp.start()

        @pl.when(jnp.logical_or(b == 0, e != prev_e))
        def _():
            for cp in weight_copies(e, slot):
                cp.wait()
            w1b[...] = wf1[slot].astype(BF16)
            w3b[...] = wf3[slot].astype(BF16)
            w2b[...] = wf2[slot].astype(BF16)
            nxt = nxt_ref[b]

            @pl.when(nxt != e)
            def _():
                for cp in weight_copies(nxt, 1 - slot):
                    cp.start()

        xb = _unpack_pairs(x_ref[...]).astype(BF16)
        h1 = jnp.dot(xb, w1b[...], preferred_element_type=F32)
        h3 = jnp.dot(xb, w3b[...], preferred_element_type=F32)
        act = (h1 * jax.nn.sigmoid(h1) * h3).astype(BF16)
        y_ref[...] = _pack_pairs(jnp.dot(act, w2b[...], preferred_element_type=F32))


def _experts(blk_e, n_used, xs, w_e1, w_e3, w_e2, bm):
    n_rows = xs.shape[0]
    d, ff = w_e1.shape[-2:]
    rows = lambda b, be, nu, gr, nx: (jnp.minimum(b, nu[0] - 1), 0)
    grp = jnp.cumsum(jnp.concatenate([jnp.zeros((1,), I32),
                                      (blk_e[1:] != blk_e[:-1]).astype(I32)])).astype(I32)
    later = jnp.where(blk_e[None, :] > blk_e[:, None], blk_e[None, :], N_EXPERTS)
    nxt = jnp.min(later, axis=1)
    nxt = jnp.where(nxt == N_EXPERTS, blk_e, nxt).astype(I32)
    any_spec = pl.BlockSpec(memory_space=pl.ANY)
    grid_spec = pltpu.PrefetchScalarGridSpec(
        num_scalar_prefetch=4,
        grid=(n_rows // bm,),
        in_specs=[pl.BlockSpec((bm, d // 2), rows), any_spec, any_spec, any_spec],
        out_specs=pl.BlockSpec((bm, d // 2), rows),
        scratch_shapes=[pltpu.VMEM((2, d, ff), F32), pltpu.VMEM((2, d, ff), F32),
                        pltpu.VMEM((2, ff, d), F32),
                        pltpu.VMEM((d, ff), BF16), pltpu.VMEM((d, ff), BF16),
                        pltpu.VMEM((ff, d), BF16),
                        pltpu.SemaphoreType.DMA((2, 3))])
    return pl.pallas_call(
        _expert_kernel,
        grid_spec=grid_spec,
        out_shape=jax.ShapeDtypeStruct((n_rows, d // 2), U32),
        compiler_params=_cparams("arbitrary"),
        name="experts",
    )(blk_e, n_used, grp, nxt, xs, w_e1, w_e3, w_e2)


def _block_tables(counts, n_blocks, bm):
    padded = (counts + bm - 1) // bm * bm
    pad_end = jnp.cumsum(padded)
    pad_start = pad_end - padded
    blk_first = jnp.arange(n_blocks, dtype=I32) * bm
    blk_e = jnp.minimum(jnp.sum(pad_end[None, :] <= blk_first[:, None], axis=1), N_EXPERTS - 1)
    n_used = pad_end[-1] // bm
    blk_e = jnp.where(jnp.arange(n_blocks) < n_used, blk_e, blk_e[jnp.maximum(n_used - 1, 0)])
    last_row = jnp.maximum(pad_end - bm, 0)
    return (pad_start, last_row.astype(I32), (padded > 0).astype(I32),
            blk_e.astype(I32), n_used.astype(I32).reshape(1))


def _combine_kernel(dest0_ref, dnext_ref, wts_ref, v_ref, h1_ref, g2_ref, lg_ref, lb_ref,
                    ws1_ref, ws3_ref, ws2_ref, ys_hbm, o_ref, dest_s, ybuf_a, ybuf_b, sem_i, sem_r):
    tm = v_ref.shape[0]
    i = pl.program_id(0)
    ybufs = (ybuf_a, ybuf_b)

    def start_gathers(slot):
        for t in range(tm):
            for k in range(TOP_K):
                pltpu.make_async_copy(ys_hbm.at[pl.ds(dest_s[slot, k, t], 1), :],
                                      ybufs[slot].at[k, pl.ds(t, 1), :],
                                      sem_r.at[slot]).start(priority=k % 2)

    def wait_gathers(slot):
        for k in range(TOP_K):
            pltpu.make_async_copy(ys_hbm.at[pl.ds(0, tm), :], ybufs[slot].at[k], sem_r.at[slot]).wait()

    @pl.when(i == 0)
    def _():
        cp = pltpu.make_async_copy(dest0_ref, dest_s.at[0], sem_i)
        cp.start()
        cp.wait()

        def first_tile(t, carry):
            for k in range(TOP_K):
                pltpu.make_async_copy(ys_hbm.at[pl.ds(dest_s[0, k, t], 1), :],
                                      ybuf_a.at[k, pl.ds(t, 1), :], sem_r.at[0]).start()
            return carry

        lax.fori_loop(0, tm, first_tile, 0)

    def tile(slot):
        other = 1 - slot
        cp = pltpu.make_async_copy(dnext_ref, dest_s.at[other], sem_i)
        cp.start()
        cp.wait()
        wait_gathers(slot)
        start_gathers(other)
        vb = v_ref[...].astype(BF16)
        a1 = jnp.dot(vb, ws1_ref[...], preferred_element_type=F32)
        a3 = jnp.dot(vb, ws3_ref[...], preferred_element_type=F32)
        act = (a1 * jax.nn.sigmoid(a1) * a3).astype(BF16)
        ff = jnp.dot(act, ws2_ref[...], preferred_element_type=F32)
        wpad = jnp.concatenate([wts_ref[...], jnp.zeros((LANES - TOP_K, tm), F32)], axis=0)
        wt = wpad.T
        for k in range(TOP_K):
            ff = ff + _unpack_pairs(ybufs[slot][k]) * wt[:, k:k + 1]
        o_ref[...] = _layer_norm(DEEPNORM_ALPHA * h1_ref[...] + g2_ref[...] * ff,
                                 lg_ref[...], lb_ref[...])

        @pl.when(i == pl.num_programs(0) - 1)
        def _():
            wait_gathers(other)

    @pl.when(i % 2 == 0)
    def _():
        tile(0)

    @pl.when(i % 2 == 1)
    def _():
        tile(1)


def _combine(dest, wts, v, h1, g2, ln_g, ln_b, w_s1, w_s3, w_s2, ys, tm):
    s, d = v.shape
    n = s // tm
    row = lambda i: (i, 0)
    vec = _const_spec((1, d))
    slot = pl.BlockSpec((TOP_K, tm), lambda i: (0, i))
    single = lambda shape: pl.BlockSpec(shape, lambda i: (0, 0), pipeline_mode=pl.Buffered(1))
    return pl.pallas_call(
        _combine_kernel,
        grid=(n,),
        in_specs=[pl.BlockSpec((TOP_K, tm), lambda i: (0, 0)),
                  pl.BlockSpec((TOP_K, tm), lambda i: (0, jnp.minimum(i + 1, n - 1))),
                  slot, pl.BlockSpec((tm, d), row), pl.BlockSpec((tm, d), row),
                  vec, vec, vec,
                  single(w_s1.shape), single(w_s3.shape), single(w_s2.shape),
                  pl.BlockSpec(memory_space=pl.ANY)],
        out_specs=pl.BlockSpec((tm, d), row),
        out_shape=jax.ShapeDtypeStruct((s, d), F32),
        scratch_shapes=[pltpu.SMEM((2, TOP_K, tm), I32),
                        pltpu.VMEM((TOP_K, tm, d // 2), U32), pltpu.VMEM((TOP_K, tm, d // 2), U32),
                        pltpu.SemaphoreType.DMA, pltpu.SemaphoreType.DMA((2,))],
        compiler_params=_cparams("arbitrary"),
        name="combine",
    )(dest, dest, wts, v, h1, g2, ln_g, ln_b, w_s1, w_s3, w_s2, ys)


def _rope_tables(s):
    half = HEAD_DIM // 4
    inv_freq = ROPE_THETA ** (-np.arange(half, dtype=np.float32) / half)
    t = np.arange(s)
    ang_r = (t // GRID_W).astype(np.float32)[:, None] * inv_freq[None, :]
    ang_c = (t % GRID_W).astype(np.float32)[:, None] * inv_freq[None, :]
    cos = np.concatenate([np.cos(ang_r)] * 2 + [np.cos(ang_c)] * 2, axis=-1)
    sin = np.concatenate([-np.sin(ang_r), np.sin(ang_r), -np.sin(ang_c), np.sin(ang_c)], axis=-1)
    return jnp.asarray(cos, F32), jnp.asarray(sin, F32)


def kernel(x, c, ctx, c_ctx, w_mod, b_mod, w_in, q_norm, k_norm, conv_w, conv_b, rg_wa, rg_ba,
           rg_wx, rg_bx, rg_lam, w_out, ln1_g, ln1_b, w_router, e_bias, w_e1, w_e3, w_e2,
           w_s1, w_s3, w_s2, ln2_g, ln2_b):
    assert x.shape[0] == 1 and w_mod.shape[0] == DEPTH
    _, s, d = x.shape
    n_ctx = ctx.shape[1]
    x2 = x[0]

    c2t = jnp.stack([c[0], c_ctx], axis=1)
    mod = _modulation(c2t, w_mod[0], b_mod[0].reshape(1, -1))
    sh1, sc1, g1, sh2, sc2, g2 = [mod[0:1, j * d:(j + 1) * d] for j in range(6)]
    csh1, csc1 = mod[1:2, 0:d], mod[1:2, d:2 * d]

    w_in_b = w_in[0].astype(BF16)
    qg = q_norm[0].reshape(1, HEAD_DIM)
    kg = k_norm[0].reshape(1, HEAD_DIM)
    cos, sin = _rope_tables(s)
    q_l, k_l, vt_l, xr_l, yr_l = _in_projection(x2, sc1, sh1, w_in_b, qg, kg, cos, sin,
                                                tm=min(INPROJ_ROWS, s), kv_chunk=min(KV_CHUNK, s))
    _, k_c, vt_c, xr_c, _ = _in_projection(
        ctx[0], csc1, csh1, w_in_b, qg, kg,
        jnp.ones((n_ctx, HEAD_DIM), F32), jnp.zeros((n_ctx, HEAD_DIM), F32),
        tm=n_ctx, kv_chunk=n_ctx)

    attn = _attention(q_l, k_c, vt_c, k_l, vt_l, tq=min(ATTN_QUERIES, s))

    wg = jnp.concatenate([rg_wa[0], rg_wx[0]], axis=-1).astype(BF16)
    cb = conv_b[0].reshape(1, RG_W)
    zero_state = jnp.zeros((1, RG_W), F32)
    rg_args = []
    for dd in range(2):
        rg_args.append((conv_w[0], cb, wg[dd],
                        jnp.stack([rg_ba[0, dd], rg_bx[0, dd]], axis=0),
                        rg_lam[0, dd].reshape(1, RG_W)))
    t_rg = min(RG_ROWS, s)
    hc_f = _rg_scan(xr_c, *rg_args[0], zero_state, reverse=False, t=n_ctx)
    hc_b = _rg_scan(xr_c, *rg_args[1], zero_state, reverse=True, t=n_ctx)
    h_f = _rg_scan(xr_l, *rg_args[0], hc_f[n_ctx - 1:n_ctx], reverse=False, t=t_rg)
    rg = _rg_scan(xr_l, *rg_args[1], hc_b[0:1], reverse=True, t=t_rg, hf=h_f, yr=yr_l)

    h1, v = _out_projection(attn, rg, x2, w_out[0].astype(BF16), g1,
                            ln1_g[0].reshape(1, d), ln1_b[0].reshape(1, d), sc2, sh2,
                            tm=min(OUTPROJ_ROWS, s))

    wr = jnp.pad(w_router[0], ((0, 0), (0, LANES - N_EXPERTS)))
    wr_hi = wr.astype(BF16)
    wr_lo = (wr - wr_hi.astype(F32)).astype(BF16)
    idx, wts, rank, cnt = _router(v, wr_hi, wr_lo, e_bias[0].reshape(N_EXPERTS, 1),
                                  tm=min(ROUTER_ROWS, s))

    bm = min(EXPERT_BLOCK, s)
    n_blocks = s * TOP_K // bm + N_EXPERTS
    pad_start, last_row, has_blk, blk_e, n_used = _block_tables(cnt[:, 0].astype(I32), n_blocks, bm)
    dest, xs = _dispatch(last_row, has_blk, v, idx, rank,
                         pad_start.astype(F32).reshape(N_EXPERTS, 1), n_blocks * bm,
                         tm=min(DISPATCH_ROWS, s), bm=bm)
    ys = _experts(blk_e, n_used, xs, w_e1[0], w_e3[0], w_e2[0], bm)

    out = _combine(dest, wts, v, h1, g2, ln2_g[0].reshape(1, d), ln2_b[0].reshape(1, d),
                   w_s1[0].astype(BF16), w_s3[0].astype(BF16), w_s2[0].astype(BF16), ys,
                   tm=min(COMBINE_ROWS, s))
    return out[None]
```

```python
import functools

import jax
import jax.numpy as jnp
import numpy as np
from jax import lax
from jax.experimental import pallas as pl
from jax.experimental.pallas import tpu as pltpu

F32 = jnp.float32
BF16 = jnp.bfloat16
I32 = jnp.int32
U32 = jnp.uint32

GRID_W = 64
HEAD_DIM = 128
N_HEADS = 8
N_KV_HEADS = 2
GQA_GROUP = N_HEADS // N_KV_HEADS
ATTN_W = N_HEADS * HEAD_DIM
KV_W = N_KV_HEADS * HEAD_DIM
ROPE_THETA = 10000.0
RG_W = 1024
RG_HEADS = 8
RG_HD = RG_W // RG_HEADS
RG_C = 8.0
PROJ_W = ATTN_W + 2 * KV_W + 2 * RG_W
N_EXPERTS = 64
N_GROUPS = 8
GROUP_SIZE = N_EXPERTS // N_GROUPS
TOPK_GROUPS = 4
TOP_K = 8
ROUTED_SCALE = 2.5
NORM_EPS = 1e-6
DEPTH = 1
DEEPNORM_ALPHA = (2.0 * DEPTH) ** 0.25
LOG2E = 1.4426950408889634

LANES = 128
SUBLANES = 8
BF16_SUBLANES = 16
VMEM_LIMIT = 56 * 1024 * 1024

NEG_BIG = -1e30
ROPE_AXIS_DIM = HEAD_DIM // 2
VT_ROWS = HEAD_DIM + BF16_SUBLANES

MOD_COLS = 1024
INPROJ_ROWS = 512
ATTN_QUERIES = 256
KV_CHUNK = 512
ATTN_UNROLL = 16
RG_ROWS = 512
OUTPROJ_ROWS = 512
ROUTER_ROWS = 256
EXPERT_BLOCK = 512
DISPATCH_ROWS = 256
COMBINE_ROWS = 256


def _cparams(*sem):
    return pltpu.CompilerParams(dimension_semantics=sem, vmem_limit_bytes=VMEM_LIMIT)


def _const_spec(shape):
    nd = len(shape)
    return pl.BlockSpec(shape, lambda *_: (0,) * nd)


def _mod_kernel(ct_ref, w_ref, b_ref, o_ref, sb_ref, *, tn):
    d = w_ref.shape[0]

    @pl.when(pl.program_id(0) == 0)
    def _():
        ct = ct_ref[...]
        s = ct * jax.nn.sigmoid(ct)
        sb_ref[0] = jnp.broadcast_to(s[:, 0:1], (d, LANES))
        sb_ref[1] = jnp.broadcast_to(s[:, 1:2], (d, LANES))

    for c in range(tn // LANES):
        sl = slice(c * LANES, (c + 1) * LANES)
        wc = w_ref[:, sl]
        bc = b_ref[:, sl]
        o0 = jnp.sum(wc * sb_ref[0], axis=0, keepdims=True) + bc
        o1 = jnp.sum(wc * sb_ref[1], axis=0, keepdims=True) + bc
        o_ref[:, sl] = jnp.concatenate(
            [o0, o1, jnp.zeros((SUBLANES - 2, LANES), F32)], axis=0)


def _modulation(c2t, w_mod, b_mod):
    d, n = w_mod.shape
    tn = MOD_COLS
    return pl.pallas_call(
        functools.partial(_mod_kernel, tn=tn),
        grid=(n // tn,),
        in_specs=[_const_spec((d, 2)),
                  pl.BlockSpec((d, tn), lambda j: (0, j)),
                  pl.BlockSpec((1, tn), lambda j: (0, j))],
        out_specs=pl.BlockSpec((SUBLANES, tn), lambda j: (0, j)),
        out_shape=jax.ShapeDtypeStruct((SUBLANES, n), F32),
        scratch_shapes=[pltpu.VMEM((2, d, LANES), F32)],
        compiler_params=_cparams("arbitrary"),
        name="modulation",
    )(c2t, w_mod, b_mod)


def _swap_half(y):
    half = ROPE_AXIS_DIM // 2
    lane = lax.broadcasted_iota(I32, y.shape, 1)
    return jnp.where((lane % ROPE_AXIS_DIM) < half,
                     pltpu.roll(y, LANES - half, 1), pltpu.roll(y, half, 1))


def _norm_rope(ph, g, cos, sin_signed, scale):
    ms = jnp.mean(ph * ph, axis=-1, keepdims=True)
    y = (ph * lax.rsqrt(ms + NORM_EPS)) * g
    y = y * cos + _swap_half(y) * sin_signed
    if scale != 1.0:
        y = y * scale
    return y


def _inproj_kernel(x_ref, sc_ref, sh_ref, w_ref, qg_ref, kg_ref, cos_ref, sin_ref,
                   q_ref, k_ref, vt_ref, xr_ref, yr_ref, *, q_scale):
    tm = x_ref.shape[0]
    kc = vt_ref.shape[-1]
    u = (x_ref[...] * (1.0 + sc_ref[...]) + sh_ref[...]).astype(BF16)
    cos = cos_ref[...]
    sin = sin_ref[...]
    o = 0
    pq = jnp.dot(u, w_ref[:, o:o + ATTN_W], preferred_element_type=F32)
    for h in range(N_HEADS):
        sl = slice(h * HEAD_DIM, (h + 1) * HEAD_DIM)
        q_ref[:, sl] = _norm_rope(pq[:, sl], qg_ref[...], cos, sin, q_scale).astype(BF16)
    o += ATTN_W
    pk = jnp.dot(u, w_ref[:, o:o + KV_W], preferred_element_type=F32)
    for h in range(N_KV_HEADS):
        sl = slice(h * HEAD_DIM, (h + 1) * HEAD_DIM)
        k_ref[:, sl] = _norm_rope(pk[:, sl], kg_ref[...], cos, sin, 1.0).astype(BF16)
    o += KV_W
    pv = jnp.dot(u, w_ref[:, o:o + KV_W], preferred_element_type=F32)
    ones_rows = jnp.where(
        lax.broadcasted_iota(I32, (VT_ROWS - HEAD_DIM, kc), 0) == 0, 1.0, 0.0).astype(BF16)
    for h in range(N_KV_HEADS):
        for cc in range(tm // kc):
            blk = pv[cc * kc:(cc + 1) * kc, h * HEAD_DIM:(h + 1) * HEAD_DIM]
            vt_ref[h, cc, 0:HEAD_DIM, :] = blk.T.astype(BF16)
            vt_ref[h, cc, HEAD_DIM:VT_ROWS, :] = ones_rows
    o += KV_W
    xr_ref[...] = jnp.dot(u, w_ref[:, o:o + RG_W], preferred_element_type=F32)
    o += RG_W
    yr_ref[...] = jnp.dot(u, w_ref[:, o:o + RG_W], preferred_element_type=F32)


def _in_projection(x, sc, sh, w_in, qg, kg, cos, sin, tm, kv_chunk):
    s, d = x.shape
    row = lambda i: (i, 0)
    if kv_chunk >= tm:
        per = kv_chunk // tm
        vt_spec = pl.BlockSpec((N_KV_HEADS, 1, VT_ROWS, tm), lambda i: (0, i // per, 0, i % per))
    else:
        vt_spec = pl.BlockSpec((N_KV_HEADS, tm // kv_chunk, VT_ROWS, kv_chunk),
                               lambda i: (0, i, 0, 0))
    return pl.pallas_call(
        functools.partial(_inproj_kernel, q_scale=HEAD_DIM ** -0.5 * LOG2E),
        grid=(s // tm,),
        in_specs=[pl.BlockSpec((tm, d), row),
                  _const_spec((1, d)), _const_spec((1, d)),
                  pl.BlockSpec((d, PROJ_W), lambda i: (0, 0), pipeline_mode=pl.Buffered(1)),
                  _const_spec((1, HEAD_DIM)), _const_spec((1, HEAD_DIM)),
                  pl.BlockSpec((tm, HEAD_DIM), row), pl.BlockSpec((tm, HEAD_DIM), row)],
        out_specs=[pl.BlockSpec((tm, ATTN_W), row), pl.BlockSpec((tm, KV_W), row),
                   vt_spec,
                   pl.BlockSpec((tm, RG_W), row), pl.BlockSpec((tm, RG_W), row)],
        out_shape=[jax.ShapeDtypeStruct((s, ATTN_W), BF16),
                   jax.ShapeDtypeStruct((s, KV_W), BF16),
                   jax.ShapeDtypeStruct((N_KV_HEADS, s // kv_chunk, VT_ROWS, kv_chunk), BF16),
                   jax.ShapeDtypeStruct((s, RG_W), F32),
                   jax.ShapeDtypeStruct((s, RG_W), F32)],
        compiler_params=_cparams("arbitrary"),
        name="in_projection",
    )(x, sc, sh, w_in, qg, kg, cos, sin)


def _attn_kernel(q_ref, kc_ref, vtc_ref, kl_ref, vtl_ref, o_ref, sa_ref, sb_ref, pa_ref, pb_ref, sc_ref,
                 *, tq, tk, n_chunks, unroll):
    q = q_ref[...]
    qs = jnp.concatenate(
        [q[:, g * HEAD_DIM:(g + 1) * HEAD_DIM] for g in range(GQA_GROUP)], axis=0)
    cols = GQA_GROUP * tq

    def scores(k):
        return lax.dot_general(k, qs, (((1,), (1,)), ((), ())), preferred_element_type=F32)

    def softmax(s_ref, m):
        m_new = jnp.maximum(m, jnp.max(s_ref[...], axis=0, keepdims=True))
        return m_new, jnp.exp2(m - m_new), jnp.exp2(s_ref[...] - m_new).astype(BF16)

    def weighted_values(vt, p, alpha, acc):
        return alpha * acc + jnp.dot(vt, p, preferred_element_type=F32)

    m0 = jnp.full((1, cols), NEG_BIG, F32)
    a0 = jnp.zeros((VT_ROWS, cols), F32)
    sc_ref[...] = scores(kc_ref[...])
    m, alpha, p = softmax(sc_ref, m0)
    acc = weighted_values(vtc_ref[0], p, alpha, a0)

    def latent_scores(j):
        j = jnp.minimum(j, n_chunks - 1)
        return scores(kl_ref[pl.ds(pl.multiple_of(j * tk, tk), tk), :])

    s_refs = (sa_ref, sb_ref)
    p_refs = (pa_ref, pb_ref)
    sa_ref[...] = latent_scores(0)
    sb_ref[...] = latent_scores(1)
    m, alpha, pa_ref[...] = softmax(sa_ref, m)

    def body(i, carry):
        m, acc, alpha = carry
        for u in range(unroll):
            c = unroll * i + u
            cur, nxt = u % 2, (u + 1) % 2
            s_refs[cur][...] = latent_scores(c + 2)
            m, alpha_next, p_refs[nxt][...] = softmax(s_refs[nxt], m)
            acc = weighted_values(vtl_ref[c], p_refs[cur][...], alpha, acc)
            alpha = alpha_next
        return m, acc, alpha

    _, acc, _ = lax.fori_loop(0, n_chunks // unroll, body, (m, acc, alpha))
    out_t = acc[0:HEAD_DIM, :] / acc[HEAD_DIM:HEAD_DIM + 1, :]
    for g in range(GQA_GROUP):
        o_ref[:, g * HEAD_DIM:(g + 1) * HEAD_DIM] = out_t[:, g * tq:(g + 1) * tq].T.astype(BF16)


def _attention(q, k_c, vt_c, k_l, vt_l, tq):
    s = q.shape[0]
    n_ctx = k_c.shape[0]
    n_chunks, _, tk = vt_l.shape[1:]
    unroll = min(ATTN_UNROLL, n_chunks)
    assert n_chunks % unroll == 0 and unroll % 2 == 0
    gw = GQA_GROUP * HEAD_DIM
    return pl.pallas_call(
        functools.partial(_attn_kernel, tq=tq, tk=tk, n_chunks=n_chunks, unroll=unroll),
        grid=(N_KV_HEADS, s // tq),
        in_specs=[pl.BlockSpec((tq, gw), lambda h, i: (i, h)),
                  pl.BlockSpec((n_ctx, HEAD_DIM), lambda h, i: (0, h)),
                  pl.BlockSpec((None, 1, VT_ROWS, n_ctx), lambda h, i: (h, 0, 0, 0)),
                  pl.BlockSpec((s, HEAD_DIM), lambda h, i: (0, h)),
                  pl.BlockSpec((None, n_chunks, VT_ROWS, tk), lambda h, i: (h, 0, 0, 0))],
        out_specs=pl.BlockSpec((tq, gw), lambda h, i: (i, h)),
        out_shape=jax.ShapeDtypeStruct((s, ATTN_W), BF16),
        scratch_shapes=[pltpu.VMEM((tk, GQA_GROUP * tq), F32),
                        pltpu.VMEM((tk, GQA_GROUP * tq), F32),
                        pltpu.VMEM((tk, GQA_GROUP * tq), BF16),
                        pltpu.VMEM((tk, GQA_GROUP * tq), BF16),
                        pltpu.VMEM((n_ctx, GQA_GROUP * tq), F32)],
        compiler_params=_cparams("arbitrary", "arbitrary"),
        name="attention",
    )(q, k_c, vt_c, k_l, vt_l)


def _log_sigmoid(x):
    return jnp.minimum(x, 0.0) - jnp.log1p(jnp.exp(-jnp.abs(x)))


def _rg_kernel(x_ref, xp_ref, xn_ref, cw_ref, cb_ref, wg_ref, bg_ref, lam_ref, h0_ref,
               *rest, reverse, final, t, n_chunks):
    if final:
        hf_ref, yr_ref, o_ref, a_scr, b_scr, hc_scr, h_scr = rest
    else:
        o_ref, a_scr, b_scr, hc_scr = rest
        h_scr = o_ref
    i = pl.program_id(0)
    c = (n_chunks - 1 - i) if reverse else i
    w = x_ref.shape[1]

    @pl.when(i == 0)
    def _():
        hc_scr[...] = jnp.broadcast_to(h0_ref[...], (SUBLANES, w))

    x = x_ref[...]
    row = lax.broadcasted_iota(I32, (t, w), 0)
    pm = jnp.where(c == 0, 0.0, 1.0).astype(F32)
    nm = jnp.where(c == n_chunks - 1, 0.0, 1.0).astype(F32)
    p6 = xp_ref[SUBLANES - 2:SUBLANES - 1, :] * pm
    p7 = xp_ref[SUBLANES - 1:SUBLANES, :] * pm
    n0 = xn_ref[0:1, :] * nm
    x_m1 = jnp.where(row == 0, p7, pltpu.roll(x, 1, 0))
    x_m2 = jnp.where(row == 0, p6, jnp.where(row == 1, p7, pltpu.roll(x, 2, 0)))
    x_p1 = jnp.where(row == t - 1, n0, pltpu.roll(x, t - 1, 0))
    xc = cb_ref[...] + cw_ref[0:1, :] * x_m2
    xc = xc + cw_ref[1:2, :] * x_m1
    xc = xc + cw_ref[2:3, :] * x
    xc = xc + cw_ref[3:4, :] * x_p1

    xcb = xc.astype(BF16)
    clam = RG_C * _log_sigmoid(lam_ref[...])
    for h in range(RG_HEADS):
        sl = slice(h * RG_HD, (h + 1) * RG_HD)
        g = jnp.dot(xcb[:, sl], wg_ref[h], preferred_element_type=F32)
        r = jax.nn.sigmoid(g[:, :RG_HD] + bg_ref[0:1, sl])
        gi = jax.nn.sigmoid(g[:, RG_HD:] + bg_ref[1:2, sl])
        log_a = r * clam[:, sl]
        a = jnp.exp(log_a)
        a_scr[:, sl] = a
        b_scr[:, sl] = jnp.sqrt(-jnp.tanh(log_a) * (a * a + 1.0)) * (gi * xc[:, sl])

    srow = lax.broadcasted_iota(I32, (SUBLANES, w), 0)
    n_tiles = t // SUBLANES

    def tile_body(j, hprev):
        tile = (n_tiles - 1 - j) if reverse else j
        start = pl.multiple_of(tile * SUBLANES, SUBLANES)
        a = a_scr[pl.ds(start, SUBLANES), :]
        b = b_scr[pl.ds(start, SUBLANES), :]
        for k in (1, 2, 4):
            if reverse:
                keep = srow < SUBLANES - k
                shift = SUBLANES - k
            else:
                keep = srow >= k
                shift = k
            a_sh = jnp.where(keep, pltpu.roll(a, shift, 0), 1.0)
            b_sh = jnp.where(keep, pltpu.roll(b, shift, 0), 0.0)
            b = a * b_sh + b
            a = a * a_sh
        hh = a * hprev + b
        h_scr[pl.ds(start, SUBLANES), :] = hh
        last = hh[0:1, :] if reverse else hh[SUBLANES - 1:SUBLANES, :]
        return jnp.broadcast_to(last, (SUBLANES, w))

    hc_scr[...] = lax.fori_loop(0, n_tiles, tile_body, hc_scr[...])

    if final:
        gate = jax.nn.gelu(yr_ref[...], approximate=True)
        o_ref[...] = ((hf_ref[...] + h_scr[...]) * gate).astype(o_ref.dtype)


def _rg_scan(xr, conv_w, conv_b, wg, bg, lam, h0, *, reverse, t, hf=None, yr=None):
    s, w = xr.shape
    n_chunks = s // t
    final = hf is not None
    tb = t // SUBLANES
    last_blk = s // SUBLANES - 1
    if reverse:
        cidx = lambda i: n_chunks - 1 - i
    else:
        cidx = lambda i: i
    chunk_spec = pl.BlockSpec((t, w), lambda i: (cidx(i), 0))
    in_specs = [chunk_spec,
                pl.BlockSpec((SUBLANES, w), lambda i: (jnp.maximum(cidx(i) * tb - 1, 0), 0)),
                pl.BlockSpec((SUBLANES, w), lambda i: (jnp.minimum((cidx(i) + 1) * tb, last_blk), 0)),
                _const_spec((4, w)), _const_spec((1, w)),
                _const_spec((RG_HEADS, RG_HD, 2 * RG_HD)), _const_spec((2, w)),
                _const_spec((1, w)), _const_spec((1, w))]
    args = [xr, xr, xr, conv_w, conv_b, wg, bg, lam, h0]
    scratch = [pltpu.VMEM((t, w), F32), pltpu.VMEM((t, w), F32), pltpu.VMEM((SUBLANES, w), F32)]
    if final:
        in_specs += [chunk_spec, chunk_spec]
        args += [hf, yr]
        scratch.append(pltpu.VMEM((t, w), F32))
        out_dtype = BF16
    else:
        out_dtype = F32
    return pl.pallas_call(
        functools.partial(_rg_kernel, reverse=reverse, final=final, t=t, n_chunks=n_chunks),
        grid=(n_chunks,),
        in_specs=in_specs,
        out_specs=chunk_spec,
        out_shape=jax.ShapeDtypeStruct((s, w), out_dtype),
        scratch_shapes=scratch,
        compiler_params=_cparams("arbitrary"),
        name="rglru_bwd" if reverse else "rglru_fwd",
    )(*args)


def _layer_norm(y, g, b):
    mu = jnp.mean(y, axis=-1, keepdims=True)
    yc = y - mu
    var = jnp.mean(yc * yc, axis=-1, keepdims=True)
    return yc * lax.rsqrt(var + NORM_EPS) * g + b


def _outproj_kernel(attn_ref, rg_ref, x_ref, w_ref, g1_ref, lg_ref, lb_ref, sc2_ref, sh2_ref,
                    h1_ref, v_ref):
    mix = jnp.dot(attn_ref[...], w_ref[0:ATTN_W, :], preferred_element_type=F32)
    mix = mix + jnp.dot(rg_ref[...], w_ref[ATTN_W:, :], preferred_element_type=F32)
    h1 = _layer_norm(DEEPNORM_ALPHA * x_ref[...] + g1_ref[...] * mix, lg_ref[...], lb_ref[...])
    h1_ref[...] = h1
    v_ref[...] = h1 * (1.0 + sc2_ref[...]) + sh2_ref[...]


def _out_projection(attn, rg, x, w_out, g1, ln_g, ln_b, sc2, sh2, tm):
    s, d = x.shape
    row = lambda i: (i, 0)
    vec = _const_spec((1, d))
    return pl.pallas_call(
        _outproj_kernel,
        grid=(s // tm,),
        in_specs=[pl.BlockSpec((tm, ATTN_W), row), pl.BlockSpec((tm, RG_W), row),
                  pl.BlockSpec((tm, d), row),
                  pl.BlockSpec((ATTN_W + RG_W, d), lambda i: (0, 0), pipeline_mode=pl.Buffered(1)),
                  vec, vec, vec, vec, vec],
        out_specs=[pl.BlockSpec((tm, d), row), pl.BlockSpec((tm, d), row)],
        out_shape=[jax.ShapeDtypeStruct((s, d), F32), jax.ShapeDtypeStruct((s, d), F32)],
        compiler_params=_cparams("arbitrary"),
        name="out_projection",
    )(attn, rg, x, w_out, g1, ln_g, ln_b, sc2, sh2)


def _first_index_of_max(x, iota_f, axis):
    mx = jnp.max(x, axis=axis, keepdims=True)
    idx = jnp.min(jnp.where(x == mx, iota_f, float(N_EXPERTS)), axis=axis, keepdims=True)
    return mx, idx


def _router_kernel(v_ref, whi_ref, wlo_ref, eb_ref, tri_ref,
                   idx_ref, wts_ref, rank_ref, cnt_ref, base_scr):
    v = v_ref[...]
    tm = v.shape[0]

    @pl.when(pl.program_id(0) == 0)
    def _():
        base_scr[...] = jnp.zeros(base_scr.shape, F32)

    v_hi = v.astype(BF16)
    v_lo = (v - v_hi.astype(F32)).astype(BF16)
    logits = jnp.dot(v_hi, whi_ref[...], preferred_element_type=F32)
    logits = logits + jnp.dot(v_lo, whi_ref[...], preferred_element_type=F32)
    logits = logits + jnp.dot(v_hi, wlo_ref[...], preferred_element_type=F32)
    lt = logits.T[0:N_EXPERTS, :]
    scores = jax.nn.sigmoid(lt)
    biased = scores + eb_ref[...]
    neg_inf = float("-inf")

    ig = lax.broadcasted_iota(I32, (GROUP_SIZE, tm), 0).astype(F32)
    groups = [biased[g * GROUP_SIZE:(g + 1) * GROUP_SIZE, :] for g in range(N_GROUPS)]
    gscore = []
    for bg in groups:
        top1, i1 = _first_index_of_max(bg, ig, 0)
        top2 = jnp.max(jnp.where(ig == i1, neg_inf, bg), axis=0, keepdims=True)
        gscore.append(top1 + top2)

    masked = []
    for g in range(N_GROUPS):
        ahead = jnp.zeros((1, tm), F32)
        for o in range(N_GROUPS):
            if o == g:
                continue
            before = (gscore[o] >= gscore[g]) if o < g else (gscore[o] > gscore[g])
            ahead = ahead + jnp.where(before, 1.0, 0.0)
        keep = jnp.broadcast_to(ahead < TOPK_GROUPS, (GROUP_SIZE, tm))
        masked.append(jnp.where(keep, groups[g], neg_inf))
    masked = jnp.concatenate(masked, axis=0)

    ie = lax.broadcasted_iota(I32, masked.shape, 0).astype(F32)
    seen = base_scr[...]
    ws = []
    for k in range(TOP_K):
        _, ei = _first_index_of_max(masked, ie, 0)
        hit = ie == ei
        idx_ref[k:k + 1, :] = ei.astype(I32)
        ws.append(jnp.sum(jnp.where(hit, scores, 0.0), axis=0, keepdims=True))
        masked = jnp.where(hit, neg_inf, masked)
        onehot = jnp.where(hit, 1.0, 0.0)
        before = jnp.dot(onehot.astype(BF16), tri_ref[...], preferred_element_type=F32)
        rank = jnp.sum(jnp.where(hit, before + seen, 0.0), axis=0, keepdims=True)
        rank_ref[k:k + 1, :] = rank.astype(I32)
        seen = seen + jnp.sum(onehot, axis=1, keepdims=True)
    base_scr[...] = seen
    cnt_ref[...] = jnp.broadcast_to(seen, cnt_ref.shape)
    total = ws[0]
    for k in range(1, TOP_K):
        total = total + ws[k]
    for k in range(TOP_K):
        wts_ref[k:k + 1, :] = ws[k] / total * ROUTED_SCALE


def _router(v, w_hi, w_lo, e_bias_col, tm):
    s, d = v.shape
    tri = jnp.triu(jnp.ones((tm, tm), BF16), k=1)
    slot = pl.BlockSpec((TOP_K, tm), lambda i: (0, i))
    return pl.pallas_call(
        _router_kernel,
        grid=(s // tm,),
        in_specs=[pl.BlockSpec((tm, d), lambda i: (i, 0)),
                  _const_spec((d, LANES)), _const_spec((d, LANES)),
                  _const_spec((N_EXPERTS, 1)), _const_spec((tm, tm))],
        out_specs=[slot, slot, slot, _const_spec((N_EXPERTS, LANES))],
        out_shape=[jax.ShapeDtypeStruct((TOP_K, s), I32),
                   jax.ShapeDtypeStruct((TOP_K, s), F32),
                   jax.ShapeDtypeStruct((TOP_K, s), I32),
                   jax.ShapeDtypeStruct((N_EXPERTS, LANES), F32)],
        scratch_shapes=[pltpu.VMEM((N_EXPERTS, 1), F32)],
        compiler_params=_cparams("arbitrary"),
        name="router",
    )(v, w_hi, w_lo, e_bias_col, tri)


def _pack_pairs(x):
    h = x.shape[1] // 2
    lo = lax.bitcast_convert_type(x[:, :h].astype(BF16).astype(F32), U32)
    hi = lax.bitcast_convert_type(x[:, h:].astype(BF16).astype(F32), U32)
    return (lo >> 16) | (hi & jnp.uint32(0xFFFF0000))


def _unpack_pairs(w):
    lo = lax.bitcast_convert_type(w << 16, F32)
    hi = lax.bitcast_convert_type(w & jnp.uint32(0xFFFF0000), F32)
    return jnp.concatenate([lo, hi], axis=1)


def _dispatch_kernel(last_ref, has_ref, v_ref, idx_ref, rank_ref, pstart_ref,
                     dest_ref, xs_hbm, dest_v, dest_s, zeros_v, pk_a, pk_b, sem_z, sem_i, sem_r,
                     *, bm):
    tm = v_ref.shape[0]

    def zero_fill(e):
        row0 = pl.multiple_of(last_ref[e], bm)
        return pltpu.make_async_copy(zeros_v, xs_hbm.at[pl.ds(row0, bm), :], sem_z)

    @pl.when(pl.program_id(0) == 0)
    def _():
        zeros_v[...] = jnp.zeros(zeros_v.shape, zeros_v.dtype)

        def start(e, carry):
            @pl.when(has_ref[e] > 0)
            def _():
                zero_fill(e).start()
            return carry

        def wait(e, carry):
            @pl.when(has_ref[e] > 0)
            def _():
                zero_fill(e).wait()
            return carry

        lax.fori_loop(0, N_EXPERTS, start, 0)
        lax.fori_loop(0, N_EXPERTS, wait, 0)

    ie = lax.broadcasted_iota(I32, (N_EXPERTS, tm), 0)
    pstart = pstart_ref[...]
    for k in range(TOP_K):
        hit = ie == idx_ref[k:k + 1, :]
        seg = jnp.sum(jnp.where(hit, pstart, 0.0), axis=0, keepdims=True)
        dest_v[k:k + 1, :] = seg.astype(I32) + rank_ref[k:k + 1, :]
    dest_ref[...] = dest_v[...]
    cp = pltpu.make_async_copy(dest_v, dest_s, sem_i)
    cp.start()
    packed = _pack_pairs(v_ref[...])
    cp.wait()
    i = pl.program_id(0)
    pk_bufs = (pk_a, pk_b)

    def drain(slot):
        for k in range(TOP_K):
            pltpu.make_async_copy(pk_bufs[slot], xs_hbm.at[pl.ds(0, tm), :], sem_r.at[slot]).wait()

    def tile(slot):
        pk = pk_bufs[slot]
        pk[...] = packed
        for t in range(tm):
            for k in range(TOP_K):
                pltpu.make_async_copy(pk.at[pl.ds(t, 1), :], xs_hbm.at[pl.ds(dest_s[k, t], 1), :],
                                      sem_r.at[slot]).start(priority=k % 2)

        @pl.when(i > 0)
        def _():
            drain(1 - slot)

        @pl.when(i == pl.num_programs(0) - 1)
        def _():
            drain(slot)

    @pl.when(i % 2 == 0)
    def _():
        tile(0)

    @pl.when(i % 2 == 1)
    def _():
        tile(1)


def _dispatch(last_row, has_blk, v, idx, rank, pstart_col, n_rows, tm, bm):
    s, d = v.shape
    slot = pl.BlockSpec((TOP_K, tm), lambda i, *_: (0, i))
    grid_spec = pltpu.PrefetchScalarGridSpec(
        num_scalar_prefetch=2,
        grid=(s // tm,),
        in_specs=[pl.BlockSpec((tm, d), lambda i, *_: (i, 0)), slot, slot,
                  pl.BlockSpec((N_EXPERTS, 1), lambda i, *_: (0, 0))],
        out_specs=[slot, pl.BlockSpec(memory_space=pl.ANY)],
        scratch_shapes=[pltpu.VMEM((TOP_K, tm), I32), pltpu.SMEM((TOP_K, tm), I32),
                        pltpu.VMEM((bm, d // 2), U32),
                        pltpu.VMEM((tm, d // 2), U32), pltpu.VMEM((tm, d // 2), U32),
                        pltpu.SemaphoreType.DMA, pltpu.SemaphoreType.DMA,
                        pltpu.SemaphoreType.DMA((2,))])
    return pl.pallas_call(
        functools.partial(_dispatch_kernel, bm=bm),
        grid_spec=grid_spec,
        out_shape=[jax.ShapeDtypeStruct((TOP_K, s), I32),
                   jax.ShapeDtypeStruct((n_rows, d // 2), U32)],
        compiler_params=_cparams("arbitrary"),
        name="dispatch",
    )(last_row, has_blk, v, idx, rank, pstart_col)


def _expert_kernel(blk_e_ref, n_used_ref, grp_ref, nxt_ref, x_ref, w1_hbm, w3_hbm, w2_hbm, y_ref,
                   wf1, wf3, wf2, w1b, w3b, w2b, sem_w):
    b = pl.program_id(0)

    def weight_copies(e, slot):
        return (pltpu.make_async_copy(w1_hbm.at[e], wf1.at[slot], sem_w.at[slot, 0]),
                pltpu.make_async_copy(w3_hbm.at[e], wf3.at[slot], sem_w.at[slot, 1]),
                pltpu.make_async_copy(w2_hbm.at[e], wf2.at[slot], sem_w.at[slot, 2]))

    @pl.when(b < n_used_ref[0])
    def _():
        e = blk_e_ref[b]
        slot = grp_ref[b] % 2
        prev_e = blk_e_ref[jnp.maximum(b - 1, 0)]

        @pl.when(b == 0)
        def _():
            for cp in weight_copies(e, slot):
                cp.start()

        @pl.when(jnp.logical_or(b == 0, e != prev_e))
        def _():
            for cp in weight_copies(e, slot):
                cp.wait()
            w1b[...] = wf1[slot].astype(BF16)
            w3b[...] = wf3[slot].astype(BF16)
            w2b[...] = wf2[slot].astype(BF16)
            nxt = nxt_ref[b]

            @pl.when(nxt != e)
            def _():
                for cp in weight_copies(nxt, 1 - slot):
                    cp.start()

        xb = _unpack_pairs(x_ref[...]).astype(BF16)
        h1 = jnp.dot(xb, w1b[...], preferred_element_type=F32)
        h3 = jnp.dot(xb, w3b[...], preferred_element_type=F32)
        act = (h1 * jax.nn.sigmoid(h1) * h3).astype(BF16)
        y_ref[...] = _pack_pairs(jnp.dot(act, w2b[...], preferred_element_type=F32))


def _experts(blk_e, n_used, xs, w_e1, w_e3, w_e2, bm):
    n_rows = xs.shape[0]
    d, ff = w_e1.shape[-2:]
    rows = lambda b, be, nu, gr, nx: (jnp.minimum(b, nu[0] - 1), 0)
    grp = jnp.cumsum(jnp.concatenate([jnp.zeros((1,), I32),
                                      (blk_e[1:] != blk_e[:-1]).astype(I32)])).astype(I32)
    later = jnp.where(blk_e[None, :] > blk_e[:, None], blk_e[None, :], N_EXPERTS)
    nxt = jnp.min(later, axis=1)
    nxt = jnp.where(nxt == N_EXPERTS, blk_e, nxt).astype(I32)
    any_spec = pl.BlockSpec(memory_space=pl.ANY)
    grid_spec = pltpu.PrefetchScalarGridSpec(
        num_scalar_prefetch=4,
        grid=(n_rows // bm,),
        in_specs=[pl.BlockSpec((bm, d // 2), rows), any_spec, any_spec, any_spec],
        out_specs=pl.BlockSpec((bm, d // 2), rows),
        scratch_shapes=[pltpu.VMEM((2, d, ff), F32), pltpu.VMEM((2, d, ff), F32),
                        pltpu.VMEM((2, ff, d), F32),
                        pltpu.VMEM((d, ff), BF16), pltpu.VMEM((d, ff), BF16),
                        pltpu.VMEM((ff, d), BF16),
                        pltpu.SemaphoreType.DMA((2, 3))])
    return pl.pallas_call(
        _expert_kernel,
        grid_spec=grid_spec,
        out_shape=jax.ShapeDtypeStruct((n_rows, d // 2), U32),
        compiler_params=_cparams("arbitrary"),
        name="experts",
    )(blk_e, n_used, grp, nxt, xs, w_e1, w_e3, w_e2)


def _block_tables(counts, n_blocks, bm):
    padded = (counts + bm - 1) // bm * bm
    pad_end = jnp.cumsum(padded)
    pad_start = pad_end - padded
    blk_first = jnp.arange(n_blocks, dtype=I32) * bm
    blk_e = jnp.minimum(jnp.sum(pad_end[None, :] <= blk_first[:, None], axis=1), N_EXPERTS - 1)
    n_used = pad_end[-1] // bm
    blk_e = jnp.where(jnp.arange(n_blocks) < n_used, blk_e, blk_e[jnp.maximum(n_used - 1, 0)])
    last_row = jnp.maximum(pad_end - bm, 0)
    return (pad_start, last_row.astype(I32), (padded > 0).astype(I32),
            blk_e.astype(I32), n_used.astype(I32).reshape(1))


def _combine_kernel(dest0_ref, dnext_ref, wts_ref, v_ref, h1_ref, g2_ref, lg_ref, lb_ref,
                    ws1_ref, ws3_ref, ws2_ref, ys_hbm, o_ref, dest_s, ybuf_a, ybuf_b, sem_i, sem_r):
    tm = v_ref.shape[0]
    i = pl.program_id(0)
    ybufs = (ybuf_a, ybuf_b)

    def start_gathers(slot):
        for t in range(tm):
            for k in range(TOP_K):
                pltpu.make_async_copy(ys_hbm.at[pl.ds(dest_s[slot, k, t], 1), :],
                                      ybufs[slot].at[k, pl.ds(t, 1), :],
                                      sem_r.at[slot]).start(priority=k % 2)

    def wait_gathers(slot):
        for k in range(TOP_K):
            pltpu.make_async_copy(ys_hbm.at[pl.ds(0, tm), :], ybufs[slot].at[k], sem_r.at[slot]).wait()

    @pl.when(i == 0)
    def _():
        cp = pltpu.make_async_copy(dest0_ref, dest_s.at[0], sem_i)
        cp.start()
        cp.wait()

        def first_tile(t, carry):
            for k in range(TOP_K):
                pltpu.make_async_copy(ys_hbm.at[pl.ds(dest_s[0, k, t], 1), :],
                                      ybuf_a.at[k, pl.ds(t, 1), :], sem_r.at[0]).start()
            return carry

        lax.fori_loop(0, tm, first_tile, 0)

    def tile(slot):
        other = 1 - slot
        cp = pltpu.make_async_copy(dnext_ref, dest_s.at[other], sem_i)
        cp.start()
        cp.wait()
        wait_gathers(slot)
        start_gathers(other)
        vb = v_ref[...].astype(BF16)
        a1 = jnp.dot(vb, ws1_ref[...], preferred_element_type=F32)
        a3 = jnp.dot(vb, ws3_ref[...], preferred_element_type=F32)
        act = (a1 * jax.nn.sigmoid(a1) * a3).astype(BF16)
        ff = jnp.dot(act, ws2_ref[...], preferred_element_type=F32)
        wpad = jnp.concatenate([wts_ref[...], jnp.zeros((LANES - TOP_K, tm), F32)], axis=0)
        wt = wpad.T
        for k in range(TOP_K):
            ff = ff + _unpack_pairs(ybufs[slot][k]) * wt[:, k:k + 1]
        o_ref[...] = _layer_norm(DEEPNORM_ALPHA * h1_ref[...] + g2_ref[...] * ff,
                                 lg_ref[...], lb_ref[...])

        @pl.when(i == pl.num_programs(0) - 1)
        def _():
            wait_gathers(other)

    @pl.when(i % 2 == 0)
    def _():
        tile(0)

    @pl.when(i % 2 == 1)
    def _():
        tile(1)


def _combine(dest, wts, v, h1, g2, ln_g, ln_b, w_s1, w_s3, w_s2, ys, tm):
    s, d = v.shape
    n = s // tm
    row = lambda i: (i, 0)
    vec = _const_spec((1, d))
    slot = pl.BlockSpec((TOP_K, tm), lambda i: (0, i))
    single = lambda shape: pl.BlockSpec(shape, lambda i: (0, 0), pipeline_mode=pl.Buffered(1))
    return pl.pallas_call(
        _combine_kernel,
        grid=(n,),
        in_specs=[pl.BlockSpec((TOP_K, tm), lambda i: (0, 0)),
                  pl.BlockSpec((TOP_K, tm), lambda i: (0, jnp.minimum(i + 1, n - 1))),
                  slot, pl.BlockSpec((tm, d), row), pl.BlockSpec((tm, d), row),
                  vec, vec, vec,
                  single(w_s1.shape), single(w_s3.shape), single(w_s2.shape),
                  pl.BlockSpec(memory_space=pl.ANY)],
        out_specs=pl.BlockSpec((tm, d), row),
        out_shape=jax.ShapeDtypeStruct((s, d), F32),
        scratch_shapes=[pltpu.SMEM((2, TOP_K, tm), I32),
                        pltpu.VMEM((TOP_K, tm, d // 2), U32), pltpu.VMEM((TOP_K, tm, d // 2), U32),
                        pltpu.SemaphoreType.DMA, pltpu.SemaphoreType.DMA((2,))],
        compiler_params=_cparams("arbitrary"),
        name="combine",
    )(dest, dest, wts, v, h1, g2, ln_g, ln_b, w_s1, w_s3, w_s2, ys)


def _rope_tables(s):
    half = HEAD_DIM // 4
    inv_freq = ROPE_THETA ** (-np.arange(half, dtype=np.float32) / half)
    t = np.arange(s)
    ang_r = (t // GRID_W).astype(np.float32)[:, None] * inv_freq[None, :]
    ang_c = (t % GRID_W).astype(np.float32)[:, None] * inv_freq[None, :]
    cos = np.concatenate([np.cos(ang_r)] * 2 + [np.cos(ang_c)] * 2, axis=-1)
    sin = np.concatenate([-np.sin(ang_r), np.sin(ang_r), -np.sin(ang_c), np.sin(ang_c)], axis=-1)
    return jnp.asarray(cos, F32), jnp.asarray(sin, F32)


def kernel(x, c, ctx, c_ctx, w_mod, b_mod, w_in, q_norm, k_norm, conv_w, conv_b, rg_wa, rg_ba,
           rg_wx, rg_bx, rg_lam, w_out, ln1_g, ln1_b, w_router, e_bias, w_e1, w_e3, w_e2,
           w_s1, w_s3, w_s2, ln2_g, ln2_b):
    assert x.shape[0] == 1 and w_mod.shape[0] == DEPTH
    _, s, d = x.shape
    n_ctx = ctx.shape[1]
    x2 = x[0]

    c2t = jnp.stack([c[0], c_ctx], axis=1)
    mod = _modulation(c2t, w_mod[0], b_mod[0].reshape(1, -1))
    sh1, sc1, g1, sh2, sc2, g2 = [mod[0:1, j * d:(j + 1) * d] for j in range(6)]
    csh1, csc1 = mod[1:2, 0:d], mod[1:2, d:2 * d]

    w_in_b = w_in[0].astype(BF16)
    qg = q_norm[0].reshape(1, HEAD_DIM)
    kg = k_norm[0].reshape(1, HEAD_DIM)
    cos, sin = _rope_tables(s)
    q_l, k_l, vt_l, xr_l, yr_l = _in_projection(x2, sc1, sh1, w_in_b, qg, kg, cos, sin,
                                                tm=min(INPROJ_ROWS, s), kv_chunk=min(KV_CHUNK, s))
    _, k_c, vt_c, xr_c, _ = _in_projection(
        ctx[0], csc1, csh1, w_in_b, qg, kg,
        jnp.ones((n_ctx, HEAD_DIM), F32), jnp.zeros((n_ctx, HEAD_DIM), F32),
        tm=n_ctx, kv_chunk=n_ctx)

    attn = _attention(q_l, k_c, vt_c, k_l, vt_l, tq=min(ATTN_QUERIES, s))

    wg = jnp.concatenate([rg_wa[0], rg_wx[0]], axis=-1).astype(BF16)
    cb = conv_b[0].reshape(1, RG_W)
    zero_state = jnp.zeros((1, RG_W), F32)
    rg_args = []
    for dd in range(2):
        rg_args.append((conv_w[0], cb, wg[dd],
                        jnp.stack([rg_ba[0, dd], rg_bx[0, dd]], axis=0),
                        rg_lam[0, dd].reshape(1, RG_W)))
    t_rg = min(RG_ROWS, s)
    hc_f = _rg_scan(xr_c, *rg_args[0], zero_state, reverse=False, t=n_ctx)
    hc_b = _rg_scan(xr_c, *rg_args[1], zero_state, reverse=True, t=n_ctx)
    h_f = _rg_scan(xr_l, *rg_args[0], hc_f[n_ctx - 1:n_ctx], reverse=False, t=t_rg)
    rg = _rg_scan(xr_l, *rg_args[1], hc_b[0:1], reverse=True, t=t_rg, hf=h_f, yr=yr_l)

    h1, v = _out_projection(attn, rg, x2, w_out[0].astype(BF16), g1,
                            ln1_g[0].reshape(1, d), ln1_b[0].reshape(1, d), sc2, sh2,
                            tm=min(OUTPROJ_ROWS, s))

    wr = jnp.pad(w_router[0], ((0, 0), (0, LANES - N_EXPERTS)))
    wr_hi = wr.astype(BF16)
    wr_lo = (wr - wr_hi.astype(F32)).astype(BF16)
    idx, wts, rank, cnt = _router(v, wr_hi, wr_lo, e_bias[0].reshape(N_EXPERTS, 1),
                                  tm=min(ROUTER_ROWS, s))

    bm = min(EXPERT_BLOCK, s)
    n_blocks = s * TOP_K // bm + N_EXPERTS
    pad_start, last_row, has_blk, blk_e, n_used = _block_tables(cnt[:, 0].astype(I32), n_blocks, bm)
    dest, xs = _dispatch(last_row, has_blk, v, idx, rank,
                         pad_start.astype(F32).reshape(N_EXPERTS, 1), n_blocks * bm,
                         tm=min(DISPATCH_ROWS, s), bm=bm)
    ys = _experts(blk_e, n_used, xs, w_e1[0], w_e3[0], w_e2[0], bm)

    out = _combine(dest, wts, v, h1, g2, ln2_g[0].reshape(1, d), ln2_b[0].reshape(1, d),
                   w_s1[0].astype(BF16), w_s3[0].astype(BF16), w_s2[0].astype(BF16), ys,
                   tm=min(COMBINE_ROWS, s))
    return out[None]
```

```python
import functools

import jax
import jax.numpy as jnp
import numpy as np
from jax import lax
from jax.experimental import pallas as pl
from jax.experimental.pallas import tpu as pltpu

F32 = jnp.float32
BF16 = jnp.bfloat16
I32 = jnp.int32
U32 = jnp.uint32

GRID_W = 64
HEAD_DIM = 128
N_HEADS = 8
N_KV_HEADS = 2
GQA_GROUP = N_HEADS // N_KV_HEADS
ATTN_W = N_HEADS * HEAD_DIM
KV_W = N_KV_HEADS * HEAD_DIM
ROPE_THETA = 10000.0
RG_W = 1024
RG_HEADS = 8
RG_HD = RG_W // RG_HEADS
RG_C = 8.0
PROJ_W = ATTN_W + 2 * KV_W + 2 * RG_W
N_EXPERTS = 64
N_GROUPS = 8
GROUP_SIZE = N_EXPERTS // N_GROUPS
TOPK_GROUPS = 4
TOP_K = 8
ROUTED_SCALE = 2.5
NORM_EPS = 1e-6
DEPTH = 1
DEEPNORM_ALPHA = (2.0 * DEPTH) ** 0.25
LOG2E = 1.4426950408889634

LANES = 128
SUBLANES = 8
BF16_SUBLANES = 16
VMEM_LIMIT = 56 * 1024 * 1024

NEG_BIG = -1e30
ROPE_AXIS_DIM = HEAD_DIM // 2
VT_ROWS = HEAD_DIM + BF16_SUBLANES

MOD_COLS = 1024
INPROJ_ROWS = 512
ATTN_QUERIES = 256
KV_CHUNK = 512
ATTN_UNROLL = 32
RG_ROWS = 512
OUTPROJ_ROWS = 512
ROUTER_ROWS = 256
EXPERT_BLOCK = 512
DISPATCH_ROWS = 256
COMBINE_ROWS = 256


def _cparams(*sem):
    return pltpu.CompilerParams(dimension_semantics=sem, vmem_limit_bytes=VMEM_LIMIT)


def _const_spec(shape):
    nd = len(shape)
    return pl.BlockSpec(shape, lambda *_: (0,) * nd)


def _mod_kernel(ct_ref, w_ref, b_ref, o_ref, sb_ref, *, tn):
    d = w_ref.shape[0]

    @pl.when(pl.program_id(0) == 0)
    def _():
        ct = ct_ref[...]
        s = ct * jax.nn.sigmoid(ct)
        sb_ref[0] = jnp.broadcast_to(s[:, 0:1], (d, LANES))
        sb_ref[1] = jnp.broadcast_to(s[:, 1:2], (d, LANES))

    for c in range(tn // LANES):
        sl = slice(c * LANES, (c + 1) * LANES)
        wc = w_ref[:, sl]
        bc = b_ref[:, sl]
        o0 = jnp.sum(wc * sb_ref[0], axis=0, keepdims=True) + bc
        o1 = jnp.sum(wc * sb_ref[1], axis=0, keepdims=True) + bc
        o_ref[:, sl] = jnp.concatenate(
            [o0, o1, jnp.zeros((SUBLANES - 2, LANES), F32)], axis=0)


def _modulation(c2t, w_mod, b_mod):
    d, n = w_mod.shape
    tn = MOD_COLS
    return pl.pallas_call(
        functools.partial(_mod_kernel, tn=tn),
        grid=(n // tn,),
        in_specs=[_const_spec((d, 2)),
                  pl.BlockSpec((d, tn), lambda j: (0, j)),
                  pl.BlockSpec((1, tn), lambda j: (0, j))],
        out_specs=pl.BlockSpec((SUBLANES, tn), lambda j: (0, j)),
        out_shape=jax.ShapeDtypeStruct((SUBLANES, n), F32),
        scratch_shapes=[pltpu.VMEM((2, d, LANES), F32)],
        compiler_params=_cparams("arbitrary"),
        name="modulation",
    )(c2t, w_mod, b_mod)


def _swap_half(y):
    half = ROPE_AXIS_DIM // 2
    lane = lax.broadcasted_iota(I32, y.shape, 1)
    return jnp.where((lane % ROPE_AXIS_DIM) < half,
                     pltpu.roll(y, LANES - half, 1), pltpu.roll(y, half, 1))


def _norm_rope(ph, g, cos, sin_signed, scale):
    ms = jnp.mean(ph * ph, axis=-1, keepdims=True)
    y = (ph * lax.rsqrt(ms + NORM_EPS)) * g
    y = y * cos + _swap_half(y) * sin_signed
    if scale != 1.0:
        y = y * scale
    return y


def _inproj_kernel(x_ref, sc_ref, sh_ref, w_ref, qg_ref, kg_ref, cos_ref, sin_ref,
                   q_ref, k_ref, vt_ref, xr_ref, yr_ref, *, q_scale):
    tm = x_ref.shape[0]
    kc = vt_ref.shape[-1]
    u = (x_ref[...] * (1.0 + sc_ref[...]) + sh_ref[...]).astype(BF16)
    cos = cos_ref[...]
    sin = sin_ref[...]
    o = 0
    pq = jnp.dot(u, w_ref[:, o:o + ATTN_W], preferred_element_type=F32)
    for h in range(N_HEADS):
        sl = slice(h * HEAD_DIM, (h + 1) * HEAD_DIM)
        q_ref[:, sl] = _norm_rope(pq[:, sl], qg_ref[...], cos, sin, q_scale).astype(BF16)
    o += ATTN_W
    pk = jnp.dot(u, w_ref[:, o:o + KV_W], preferred_element_type=F32)
    for h in range(N_KV_HEADS):
        sl = slice(h * HEAD_DIM, (h + 1) * HEAD_DIM)
        k_ref[:, sl] = _norm_rope(pk[:, sl], kg_ref[...], cos, sin, 1.0).astype(BF16)
    o += KV_W
    pv = jnp.dot(u, w_ref[:, o:o + KV_W], preferred_element_type=F32)
    ones_rows = jnp.where(
        lax.broadcasted_iota(I32, (VT_ROWS - HEAD_DIM, kc), 0) == 0, 1.0, 0.0).astype(BF16)
    for h in range(N_KV_HEADS):
        for cc in range(tm // kc):
            blk = pv[cc * kc:(cc + 1) * kc, h * HEAD_DIM:(h + 1) * HEAD_DIM]
            vt_ref[h, cc, 0:HEAD_DIM, :] = blk.T.astype(BF16)
            vt_ref[h, cc, HEAD_DIM:VT_ROWS, :] = ones_rows
    o += KV_W
    xr_ref[...] = jnp.dot(u, w_ref[:, o:o + RG_W], preferred_element_type=F32)
    o += RG_W
    yr_ref[...] = jnp.dot(u, w_ref[:, o:o + RG_W], preferred_element_type=F32)


def _in_projection(x, sc, sh, w_in, qg, kg, cos, sin, tm, kv_chunk):
    s, d = x.shape
    row = lambda i: (i, 0)
    if kv_chunk >= tm:
        per = kv_chunk // tm
        vt_spec = pl.BlockSpec((N_KV_HEADS, 1, VT_ROWS, tm), lambda i: (0, i // per, 0, i % per))
    else:
        vt_spec = pl.BlockSpec((N_KV_HEADS, tm // kv_chunk, VT_ROWS, kv_chunk),
                               lambda i: (0, i, 0, 0))
    return pl.pallas_call(
        functools.partial(_inproj_kernel, q_scale=HEAD_DIM ** -0.5 * LOG2E),
        grid=(s // tm,),
        in_specs=[pl.BlockSpec((tm, d), row),
                  _const_spec((1, d)), _const_spec((1, d)),
                  pl.BlockSpec((d, PROJ_W), lambda i: (0, 0), pipeline_mode=pl.Buffered(1)),
                  _const_spec((1, HEAD_DIM)), _const_spec((1, HEAD_DIM)),
                  pl.BlockSpec((tm, HEAD_DIM), row), pl.BlockSpec((tm, HEAD_DIM), row)],
        out_specs=[pl.BlockSpec((tm, ATTN_W), row), pl.BlockSpec((tm, KV_W), row),
                   vt_spec,
                   pl.BlockSpec((tm, RG_W), row), pl.BlockSpec((tm, RG_W), row)],
        out_shape=[jax.ShapeDtypeStruct((s, ATTN_W), BF16),
                   jax.ShapeDtypeStruct((s, KV_W), BF16),
                   jax.ShapeDtypeStruct((N_KV_HEADS, s // kv_chunk, VT_ROWS, kv_chunk), BF16),
                   jax.ShapeDtypeStruct((s, RG_W), F32),
                   jax.ShapeDtypeStruct((s, RG_W), F32)],
        compiler_params=_cparams("arbitrary"),
        name="in_projection",
    )(x, sc, sh, w_in, qg, kg, cos, sin)


def _attn_kernel(q_ref, kc_ref, vtc_ref, kl_ref, vtl_ref, o_ref, sa_ref, sb_ref, pa_ref, pb_ref, sc_ref,
                 *, tq, tk, n_chunks, unroll):
    q = q_ref[...]
    qs = jnp.concatenate(
        [q[:, g * HEAD_DIM:(g + 1) * HEAD_DIM] for g in range(GQA_GROUP)], axis=0)
    cols = GQA_GROUP * tq

    def scores(k):
        return lax.dot_general(k, qs, (((1,), (1,)), ((), ())), preferred_element_type=F32)

    def softmax(s_ref, m):
        m_new = jnp.maximum(m, jnp.max(s_ref[...], axis=0, keepdims=True))
        return m_new, jnp.exp2(m - m_new), jnp.exp2(s_ref[...] - m_new).astype(BF16)

    def weighted_values(vt, p, alpha, acc):
        return alpha * acc + jnp.dot(vt, p, preferred_element_type=F32)

    m0 = jnp.full((1, cols), NEG_BIG, F32)
    a0 = jnp.zeros((VT_ROWS, cols), F32)
    sc_ref[...] = scores(kc_ref[...])
    m, alpha, p = softmax(sc_ref, m0)
    acc = weighted_values(vtc_ref[0], p, alpha, a0)

    def latent_scores(j):
        j = jnp.minimum(j, n_chunks - 1)
        return scores(kl_ref[pl.ds(pl.multiple_of(j * tk, tk), tk), :])

    s_refs = (sa_ref, sb_ref)
    p_refs = (pa_ref, pb_ref)
    sa_ref[...] = latent_scores(0)
    sb_ref[...] = latent_scores(1)
    m, alpha, pa_ref[...] = softmax(sa_ref, m)

    def body(i, carry):
        m, acc, alpha = carry
        for u in range(unroll):
            c = unroll * i + u
            cur, nxt = u % 2, (u + 1) % 2
            s_refs[cur][...] = latent_scores(c + 2)
            m, alpha_next, p_refs[nxt][...] = softmax(s_refs[nxt], m)
            acc = weighted_values(vtl_ref[c], p_refs[cur][...], alpha, acc)
            alpha = alpha_next
        return m, acc, alpha

    _, acc, _ = lax.fori_loop(0, n_chunks // unroll, body, (m, acc, alpha))
    out_t = acc[0:HEAD_DIM, :] / acc[HEAD_DIM:HEAD_DIM + 1, :]
    for g in range(GQA_GROUP):
        o_ref[:, g * HEAD_DIM:(g + 1) * HEAD_DIM] = out_t[:, g * tq:(g + 1) * tq].T.astype(BF16)


def _attention(q, k_c, vt_c, k_l, vt_l, tq):
    s = q.shape[0]
    n_ctx = k_c.shape[0]
    n_chunks, _, tk = vt_l.shape[1:]
    unroll = min(ATTN_UNROLL, n_chunks)
    assert n_chunks % unroll == 0 and unroll % 2 == 0
    gw = GQA_GROUP * HEAD_DIM
    return pl.pallas_call(
        functools.partial(_attn_kernel, tq=tq, tk=tk, n_chunks=n_chunks, unroll=unroll),
        grid=(N_KV_HEADS, s // tq),
        in_specs=[pl.BlockSpec((tq, gw), lambda h, i: (i, h)),
                  pl.BlockSpec((n_ctx, HEAD_DIM), lambda h, i: (0, h)),
                  pl.BlockSpec((None, 1, VT_ROWS, n_ctx), lambda h, i: (h, 0, 0, 0)),
                  pl.BlockSpec((s, HEAD_DIM), lambda h, i: (0, h)),
                  pl.BlockSpec((None, n_chunks, VT_ROWS, tk), lambda h, i: (h, 0, 0, 0))],
        out_specs=pl.BlockSpec((tq, gw), lambda h, i: (i, h)),
        out_shape=jax.ShapeDtypeStruct((s, ATTN_W), BF16),
        scratch_shapes=[pltpu.VMEM((tk, GQA_GROUP * tq), F32),
                        pltpu.VMEM((tk, GQA_GROUP * tq), F32),
                        pltpu.VMEM((tk, GQA_GROUP * tq), BF16),
                        pltpu.VMEM((tk, GQA_GROUP * tq), BF16),
                        pltpu.VMEM((n_ctx, GQA_GROUP * tq), F32)],
        compiler_params=_cparams("arbitrary", "arbitrary"),
        name="attention",
    )(q, k_c, vt_c, k_l, vt_l)


def _log_sigmoid(x):
    return jnp.minimum(x, 0.0) - jnp.log1p(jnp.exp(-jnp.abs(x)))


def _rg_kernel(x_ref, xp_ref, xn_ref, cw_ref, cb_ref, wg_ref, bg_ref, lam_ref, h0_ref,
               *rest, reverse, final, t, n_chunks):
    if final:
        hf_ref, yr_ref, o_ref, a_scr, b_scr, hc_scr, h_scr = rest
    else:
        o_ref, a_scr, b_scr, hc_scr = rest
        h_scr = o_ref
    i = pl.program_id(0)
    c = (n_chunks - 1 - i) if reverse else i
    w = x_ref.shape[1]

    @pl.when(i == 0)
    def _():
        hc_scr[...] = jnp.broadcast_to(h0_ref[...], (SUBLANES, w))

    x = x_ref[...]
    row = lax.broadcasted_iota(I32, (t, w), 0)
    pm = jnp.where(c == 0, 0.0, 1.0).astype(F32)
    nm = jnp.where(c == n_chunks - 1, 0.0, 1.0).astype(F32)
    p6 = xp_ref[SUBLANES - 2:SUBLANES - 1, :] * pm
    p7 = xp_ref[SUBLANES - 1:SUBLANES, :] * pm
    n0 = xn_ref[0:1, :] * nm
    x_m1 = jnp.where(row == 0, p7, pltpu.roll(x, 1, 0))
    x_m2 = jnp.where(row == 0, p6, jnp.where(row == 1, p7, pltpu.roll(x, 2, 0)))
    x_p1 = jnp.where(row == t - 1, n0, pltpu.roll(x, t - 1, 0))
    xc = cb_ref[...] + cw_ref[0:1, :] * x_m2
    xc = xc + cw_ref[1:2, :] * x_m1
    xc = xc + cw_ref[2:3, :] * x
    xc = xc + cw_ref[3:4, :] * x_p1

    xcb = xc.astype(BF16)
    clam = RG_C * _log_sigmoid(lam_ref[...])
    for h in range(RG_HEADS):
        sl = slice(h * RG_HD, (h + 1) * RG_HD)
        g = jnp.dot(xcb[:, sl], wg_ref[h], preferred_element_type=F32)
        r = jax.nn.sigmoid(g[:, :RG_HD] + bg_ref[0:1, sl])
        gi = jax.nn.sigmoid(g[:, RG_HD:] + bg_ref[1:2, sl])
        log_a = r * clam[:, sl]
        a = jnp.exp(log_a)
        a_scr[:, sl] = a
        b_scr[:, sl] = jnp.sqrt(-jnp.tanh(log_a) * (a * a + 1.0)) * (gi * xc[:, sl])

    srow = lax.broadcasted_iota(I32, (SUBLANES, w), 0)
    n_tiles = t // SUBLANES

    def tile_body(j, hprev):
        tile = (n_tiles - 1 - j) if reverse else j
        start = pl.multiple_of(tile * SUBLANES, SUBLANES)
        a = a_scr[pl.ds(start, SUBLANES), :]
        b = b_scr[pl.ds(start, SUBLANES), :]
        for k in (1, 2, 4):
            if reverse:
                keep = srow < SUBLANES - k
                shift = SUBLANES - k
            else:
                keep = srow >= k
                shift = k
            a_sh = jnp.where(keep, pltpu.roll(a, shift, 0), 1.0)
            b_sh = jnp.where(keep, pltpu.roll(b, shift, 0), 0.0)
            b = a * b_sh + b
            a = a * a_sh
        hh = a * hprev + b
        h_scr[pl.ds(start, SUBLANES), :] = hh
        last = hh[0:1, :] if reverse else hh[SUBLANES - 1:SUBLANES, :]
        return jnp.broadcast_to(last, (SUBLANES, w))

    hc_scr[...] = lax.fori_loop(0, n_tiles, tile_body, hc_scr[...])

    if final:
        gate = jax.nn.gelu(yr_ref[...], approximate=True)
        o_ref[...] = ((hf_ref[...] + h_scr[...]) * gate).astype(o_ref.dtype)


def _rg_scan(xr, conv_w, conv_b, wg, bg, lam, h0, *, reverse, t, hf=None, yr=None):
    s, w = xr.shape
    n_chunks = s // t
    final = hf is not None
    tb = t // SUBLANES
    last_blk = s // SUBLANES - 1
    if reverse:
        cidx = lambda i: n_chunks - 1 - i
    else:
        cidx = lambda i: i
    chunk_spec = pl.BlockSpec((t, w), lambda i: (cidx(i), 0))
    in_specs = [chunk_spec,
                pl.BlockSpec((SUBLANES, w), lambda i: (jnp.maximum(cidx(i) * tb - 1, 0), 0)),
                pl.BlockSpec((SUBLANES, w), lambda i: (jnp.minimum((cidx(i) + 1) * tb, last_blk), 0)),
                _const_spec((4, w)), _const_spec((1, w)),
                _const_spec((RG_HEADS, RG_HD, 2 * RG_HD)), _const_spec((2, w)),
                _const_spec((1, w)), _const_spec((1, w))]
    args = [xr, xr, xr, conv_w, conv_b, wg, bg, lam, h0]
    scratch = [pltpu.VMEM((t, w), F32), pltpu.VMEM((t, w), F32), pltpu.VMEM((SUBLANES, w), F32)]
    if final:
        in_specs += [chunk_spec, chunk_spec]
        args += [hf, yr]
        scratch.append(pltpu.VMEM((t, w), F32))
        out_dtype = BF16
    else:
        out_dtype = F32
    return pl.pallas_call(
        functools.partial(_rg_kernel, reverse=reverse, final=final, t=t, n_chunks=n_chunks),
        grid=(n_chunks,),
        in_specs=in_specs,
        out_specs=chunk_spec,
        out_shape=jax.ShapeDtypeStruct((s, w), out_dtype),
        scratch_shapes=scratch,
        compiler_params=_cparams("arbitrary"),
        name="rglru_bwd" if reverse else "rglru_fwd",
    )(*args)


def _layer_norm(y, g, b):
    mu = jnp.mean(y, axis=-1, keepdims=True)
    yc = y - mu
    var = jnp.mean(yc * yc, axis=-1, keepdims=True)
    return yc * lax.rsqrt(var + NORM_EPS) * g + b


def _outproj_kernel(attn_ref, rg_ref, x_ref, w_ref, g1_ref, lg_ref, lb_ref, sc2_ref, sh2_ref,
                    h1_ref, v_ref):
    mix = jnp.dot(attn_ref[...], w_ref[0:ATTN_W, :], preferred_element_type=F32)
    mix = mix + jnp.dot(rg_ref[...], w_ref[ATTN_W:, :], preferred_element_type=F32)
    h1 = _layer_norm(DEEPNORM_ALPHA * x_ref[...] + g1_ref[...] * mix, lg_ref[...], lb_ref[...])
    h1_ref[...] = h1
    v_ref[...] = h1 * (1.0 + sc2_ref[...]) + sh2_ref[...]


def _out_projection(attn, rg, x, w_out, g1, ln_g, ln_b, sc2, sh2, tm):
    s, d = x.shape
    row = lambda i: (i, 0)
    vec = _const_spec((1, d))
    return pl.pallas_call(
        _outproj_kernel,
        grid=(s // tm,),
        in_specs=[pl.BlockSpec((tm, ATTN_W), row), pl.BlockSpec((tm, RG_W), row),
                  pl.BlockSpec((tm, d), row),
                  pl.BlockSpec((ATTN_W + RG_W, d), lambda i: (0, 0), pipeline_mode=pl.Buffered(1)),
                  vec, vec, vec, vec, vec],
        out_specs=[pl.BlockSpec((tm, d), row), pl.BlockSpec((tm, d), row)],
        out_shape=[jax.ShapeDtypeStruct((s, d), F32), jax.ShapeDtypeStruct((s, d), F32)],
        compiler_params=_cparams("arbitrary"),
        name="out_projection",
    )(attn, rg, x, w_out, g1, ln_g, ln_b, sc2, sh2)


def _first_index_of_max(x, iota_f, axis):
    mx = jnp.max(x, axis=axis, keepdims=True)
    idx = jnp.min(jnp.where(x == mx, iota_f, float(N_EXPERTS)), axis=axis, keepdims=True)
    return mx, idx


def _router_kernel(v_ref, whi_ref, wlo_ref, eb_ref, tri_ref,
                   idx_ref, wts_ref, rank_ref, cnt_ref, base_scr):
    v = v_ref[...]
    tm = v.shape[0]

    @pl.when(pl.program_id(0) == 0)
    def _():
        base_scr[...] = jnp.zeros(base_scr.shape, F32)

    v_hi = v.astype(BF16)
    v_lo = (v - v_hi.astype(F32)).astype(BF16)
    logits = jnp.dot(v_hi, whi_ref[...], preferred_element_type=F32)
    logits = logits + jnp.dot(v_lo, whi_ref[...], preferred_element_type=F32)
    logits = logits + jnp.dot(v_hi, wlo_ref[...], preferred_element_type=F32)
    lt = logits.T[0:N_EXPERTS, :]
    scores = jax.nn.sigmoid(lt)
    biased = scores + eb_ref[...]
    neg_inf = float("-inf")

    ig = lax.broadcasted_iota(I32, (GROUP_SIZE, tm), 0).astype(F32)
    groups = [biased[g * GROUP_SIZE:(g + 1) * GROUP_SIZE, :] for g in range(N_GROUPS)]
    gscore = []
    for bg in groups:
        top1, i1 = _first_index_of_max(bg, ig, 0)
        top2 = jnp.max(jnp.where(ig == i1, neg_inf, bg), axis=0, keepdims=True)
        gscore.append(top1 + top2)

    masked = []
    for g in range(N_GROUPS):
        ahead = jnp.zeros((1, tm), F32)
        for o in range(N_GROUPS):
            if o == g:
                continue
            before = (gscore[o] >= gscore[g]) if o < g else (gscore[o] > gscore[g])
            ahead = ahead + jnp.where(before, 1.0, 0.0)
        keep = jnp.broadcast_to(ahead < TOPK_GROUPS, (GROUP_SIZE, tm))
        masked.append(jnp.where(keep, groups[g], neg_inf))
    masked = jnp.concatenate(masked, axis=0)

    ie = lax.broadcasted_iota(I32, masked.shape, 0).astype(F32)
    seen = base_scr[...]
    ws = []
    for k in range(TOP_K):
        _, ei = _first_index_of_max(masked, ie, 0)
        hit = ie == ei
        idx_ref[k:k + 1, :] = ei.astype(I32)
        ws.append(jnp.sum(jnp.where(hit, scores, 0.0), axis=0, keepdims=True))
        masked = jnp.where(hit, neg_inf, masked)
        onehot = jnp.where(hit, 1.0, 0.0)
        before = jnp.dot(onehot.astype(BF16), tri_ref[...], preferred_element_type=F32)
        rank = jnp.sum(jnp.where(hit, before + seen, 0.0), axis=0, keepdims=True)
        rank_ref[k:k + 1, :] = rank.astype(I32)
        seen = seen + jnp.sum(onehot, axis=1, keepdims=True)
    base_scr[...] = seen
    cnt_ref[...] = jnp.broadcast_to(seen, cnt_ref.shape)
    total = ws[0]
    for k in range(1, TOP_K):
        total = total + ws[k]
    for k in range(TOP_K):
        wts_ref[k:k + 1, :] = ws[k] / total * ROUTED_SCALE


def _router(v, w_hi, w_lo, e_bias_col, tm):
    s, d = v.shape
    tri = jnp.triu(jnp.ones((tm, tm), BF16), k=1)
    slot = pl.BlockSpec((TOP_K, tm), lambda i: (0, i))
    return pl.pallas_call(
        _router_kernel,
        grid=(s // tm,),
        in_specs=[pl.BlockSpec((tm, d), lambda i: (i, 0)),
                  _const_spec((d, LANES)), _const_spec((d, LANES)),
                  _const_spec((N_EXPERTS, 1)), _const_spec((tm, tm))],
        out_specs=[slot, slot, slot, _const_spec((N_EXPERTS, LANES))],
        out_shape=[jax.ShapeDtypeStruct((TOP_K, s), I32),
                   jax.ShapeDtypeStruct((TOP_K, s), F32),
                   jax.ShapeDtypeStruct((TOP_K, s), I32),
                   jax.ShapeDtypeStruct((N_EXPERTS, LANES), F32)],
        scratch_shapes=[pltpu.VMEM((N_EXPERTS, 1), F32)],
        compiler_params=_cparams("arbitrary"),
        name="router",
    )(v, w_hi, w_lo, e_bias_col, tri)


def _pack_pairs(x):
    h = x.shape[1] // 2
    lo = lax.bitcast_convert_type(x[:, :h].astype(BF16).astype(F32), U32)
    hi = lax.bitcast_convert_type(x[:, h:].astype(BF16).astype(F32), U32)
    return (lo >> 16) | (hi & jnp.uint32(0xFFFF0000))


def _unpack_pairs(w):
    lo = lax.bitcast_convert_type(w << 16, F32)
    hi = lax.bitcast_convert_type(w & jnp.uint32(0xFFFF0000), F32)
    return jnp.concatenate([lo, hi], axis=1)


def _dispatch_kernel(last_ref, has_ref, v_ref, idx_ref, rank_ref, pstart_ref,
                     dest_ref, xs_hbm, dest_v, dest_s, zeros_v, pk_a, pk_b, sem_z, sem_i, sem_r,
                     *, bm):
    tm = v_ref.shape[0]

    def zero_fill(e):
        row0 = pl.multiple_of(last_ref[e], bm)
        return pltpu.make_async_copy(zeros_v, xs_hbm.at[pl.ds(row0, bm), :], sem_z)

    @pl.when(pl.program_id(0) == 0)
    def _():
        zeros_v[...] = jnp.zeros(zeros_v.shape, zeros_v.dtype)

        def start(e, carry):
            @pl.when(has_ref[e] > 0)
            def _():
                zero_fill(e).start()
            return carry

        def wait(e, carry):
            @pl.when(has_ref[e] > 0)
            def _():
                zero_fill(e).wait()
            return carry

        lax.fori_loop(0, N_EXPERTS, start, 0)
        lax.fori_loop(0, N_EXPERTS, wait, 0)

    ie = lax.broadcasted_iota(I32, (N_EXPERTS, tm), 0)
    pstart = pstart_ref[...]
    for k in range(TOP_K):
        hit = ie == idx_ref[k:k + 1, :]
        seg = jnp.sum(jnp.where(hit, pstart, 0.0), axis=0, keepdims=True)
        dest_v[k:k + 1, :] = seg.astype(I32) + rank_ref[k:k + 1, :]
    dest_ref[...] = dest_v[...]
    cp = pltpu.make_async_copy(dest_v, dest_s, sem_i)
    cp.start()
    packed = _pack_pairs(v_ref[...])
    cp.wait()
    i = pl.program_id(0)
    pk_bufs = (pk_a, pk_b)

    def drain(slot):
        for k in range(TOP_K):
            pltpu.make_async_copy(pk_bufs[slot], xs_hbm.at[pl.ds(0, tm), :], sem_r.at[slot]).wait()

    def tile(slot):
        pk = pk_bufs[slot]
        pk[...] = packed
        for t in range(tm):
            for k in range(TOP_K):
                pltpu.make_async_copy(pk.at[pl.ds(t, 1), :], xs_hbm.at[pl.ds(dest_s[k, t], 1), :],
                                      sem_r.at[slot]).start(priority=k % 2)

        @pl.when(i > 0)
        def _():
            drain(1 - slot)

        @pl.when(i == pl.num_programs(0) - 1)
        def _():
            drain(slot)

    @pl.when(i % 2 == 0)
    def _():
        tile(0)

    @pl.when(i % 2 == 1)
    def _():
        tile(1)


def _dispatch(last_row, has_blk, v, idx, rank, pstart_col, n_rows, tm, bm):
    s, d = v.shape
    slot = pl.BlockSpec((TOP_K, tm), lambda i, *_: (0, i))
    grid_spec = pltpu.PrefetchScalarGridSpec(
        num_scalar_prefetch=2,
        grid=(s // tm,),
        in_specs=[pl.BlockSpec((tm, d), lambda i, *_: (i, 0)), slot, slot,
                  pl.BlockSpec((N_EXPERTS, 1), lambda i, *_: (0, 0))],
        out_specs=[slot, pl.BlockSpec(memory_space=pl.ANY)],
        scratch_shapes=[pltpu.VMEM((TOP_K, tm), I32), pltpu.SMEM((TOP_K, tm), I32),
                        pltpu.VMEM((bm, d // 2), U32),
                        pltpu.VMEM((tm, d // 2), U32), pltpu.VMEM((tm, d // 2), U32),
                        pltpu.SemaphoreType.DMA, pltpu.SemaphoreType.DMA,
                        pltpu.SemaphoreType.DMA((2,))])
    return pl.pallas_call(
        functools.partial(_dispatch_kernel, bm=bm),
        grid_spec=grid_spec,
        out_shape=[jax.ShapeDtypeStruct((TOP_K, s), I32),
                   jax.ShapeDtypeStruct((n_rows, d // 2), U32)],
        compiler_params=_cparams("arbitrary"),
        name="dispatch",
    )(last_row, has_blk, v, idx, rank, pstart_col)


def _expert_kernel(blk_e_ref, n_used_ref, grp_ref, nxt_ref, x_ref, w1_hbm, w3_hbm, w2_hbm, y_ref,
                   wf1, wf3, wf2, w1b, w3b, w2b, sem_w):
    b = pl.program_id(0)

    def weight_copies(e, slot):
        return (pltpu.make_async_copy(w1_hbm.at[e], wf1.at[slot], sem_w.at[slot, 0]),
                pltpu.make_async_copy(w3_hbm.at[e], wf3.at[slot], sem_w.at[slot, 1]),
                pltpu.make_async_copy(w2_hbm.at[e], wf2.at[slot], sem_w.at[slot, 2]))

    @pl.when(b < n_used_ref[0])
    def _():
        e = blk_e_ref[b]
        slot = grp_ref[b] % 2
        prev_e = blk_e_ref[jnp.maximum(b - 1, 0)]

        @pl.when(b == 0)
        def _():
            for cp in weight_copies(e, slot):
                cp.start()

        @pl.when(jnp.logical_or(b == 0, e != prev_e))
        def _():
            for cp in weight_copies(e, slot):
                cp.wait()
            w1b[...] = wf1[slot].astype(BF16)
            w3b[...] = wf3[slot].astype(BF16)
            w2b[...] = wf2[slot].astype(BF16)
            nxt = nxt_ref[b]

            @pl.when(nxt != e)
            def _():
                for cp in weight_copies(nxt, 1 - slot):
                    cp.start()

        xb = _unpack_pairs(x_ref[...]).astype(BF16)
        h1 = jnp.dot(xb, w1b[...], preferred_element_type=F32)
        h3 = jnp.dot(xb, w3b[...], preferred_element_type=F32)
        act = (h1 * jax.nn.sigmoid(h1) * h3).astype(BF16)
        y_ref[...] = _pack_pairs(jnp.dot(act, w2b[...], preferred_element_type=F32))


def _experts(blk_e, n_used, xs, w_e1, w_e3, w_e2, bm):
    n_rows = xs.shape[0]
    d, ff = w_e1.shape[-2:]
    rows = lambda b, be, nu, gr, nx: (jnp.minimum(b, nu[0] - 1), 0)
    grp = jnp.cumsum(jnp.concatenate([jnp.zeros((1,), I32),
                                      (blk_e[1:] != blk_e[:-1]).astype(I32)])).astype(I32)
    later = jnp.where(blk_e[None, :] > blk_e[:, None], blk_e[None, :], N_EXPERTS)
    nxt = jnp.min(later, axis=1)
    nxt = jnp.where(nxt == N_EXPERTS, blk_e, nxt).astype(I32)
    any_spec = pl.BlockSpec(memory_space=pl.ANY)
    grid_spec = pltpu.PrefetchScalarGridSpec(
        num_scalar_prefetch=4,
        grid=(n_rows // bm,),
        in_specs=[pl.BlockSpec((bm, d // 2), rows), any_spec, any_spec, any_spec],
        out_specs=pl.BlockSpec((bm, d // 2), rows),
        scratch_shapes=[pltpu.VMEM((2, d, ff), F32), pltpu.VMEM((2, d, ff), F32),
                        pltpu.VMEM((2, ff, d), F32),
                        pltpu.VMEM((d, ff), BF16), pltpu.VMEM((d, ff), BF16),
                        pltpu.VMEM((ff, d), BF16),
                        pltpu.SemaphoreType.DMA((2, 3))])
    return pl.pallas_call(
        _expert_kernel,
        grid_spec=grid_spec,
        out_shape=jax.ShapeDtypeStruct((n_rows, d // 2), U32),
        compiler_params=_cparams("arbitrary"),
        name="experts",
    )(blk_e, n_used, grp, nxt, xs, w_e1, w_e3, w_e2)


def _block_tables(counts, n_blocks, bm):
    padded = (counts + bm - 1) // bm * bm
    pad_end = jnp.cumsum(padded)
    pad_start = pad_end - padded
    blk_first = jnp.arange(n_blocks, dtype=I32) * bm
    blk_e = jnp.minimum(jnp.sum(pad_end[None, :] <= blk_first[:, None], axis=1), N_EXPERTS - 1)
    n_used = pad_end[-1] // bm
    blk_e = jnp.where(jnp.arange(n_blocks) < n_used, blk_e, blk_e[jnp.maximum(n_used - 1, 0)])
    last_row = jnp.maximum(pad_end - bm, 0)
    return (pad_start, last_row.astype(I32), (padded > 0).astype(I32),
            blk_e.astype(I32), n_used.astype(I32).reshape(1))


def _combine_kernel(dest0_ref, dnext_ref, wts_ref, v_ref, h1_ref, g2_ref, lg_ref, lb_ref,
                    ws1_ref, ws3_ref, ws2_ref, ys_hbm, o_ref, dest_s, ybuf_a, ybuf_b, sem_i, sem_r):
    tm = v_ref.shape[0]
    i = pl.program_id(0)
    ybufs = (ybuf_a, ybuf_b)

    def start_gathers(slot):
        for t in range(tm):
            for k in range(TOP_K):
                pltpu.make_async_copy(ys_hbm.at[pl.ds(dest_s[slot, k, t], 1), :],
                                      ybufs[slot].at[k, pl.ds(t, 1), :],
                                      sem_r.at[slot]).start(priority=k % 2)

    def wait_gathers(slot):
        for k in range(TOP_K):
            pltpu.make_async_copy(ys_hbm.at[pl.ds(0, tm), :], ybufs[slot].at[k], sem_r.at[slot]).wait()

    @pl.when(i == 0)
    def _():
        cp = pltpu.make_async_copy(dest0_ref, dest_s.at[0], sem_i)
        cp.start()
        cp.wait()

        def first_tile(t, carry):
            for k in range(TOP_K):
                pltpu.make_async_copy(ys_hbm.at[pl.ds(dest_s[0, k, t], 1), :],
                                      ybuf_a.at[k, pl.ds(t, 1), :], sem_r.at[0]).start()
            return carry

        lax.fori_loop(0, tm, first_tile, 0)

    def tile(slot):
        other = 1 - slot
        cp = pltpu.make_async_copy(dnext_ref, dest_s.at[other], sem_i)
        cp.start()
        cp.wait()
        wait_gathers(slot)
        start_gathers(other)
        vb = v_ref[...].astype(BF16)
        a1 = jnp.dot(vb, ws1_ref[...], preferred_element_type=F32)
        a3 = jnp.dot(vb, ws3_ref[...], preferred_element_type=F32)
        act = (a1 * jax.nn.sigmoid(a1) * a3).astype(BF16)
        ff = jnp.dot(act, ws2_ref[...], preferred_element_type=F32)
        wpad = jnp.concatenate([wts_ref[...], jnp.zeros((LANES - TOP_K, tm), F32)], axis=0)
        wt = wpad.T
        for k in range(TOP_K):
            ff = ff + _unpack_pairs(ybufs[slot][k]) * wt[:, k:k + 1]
        o_ref[...] = _layer_norm(DEEPNORM_ALPHA * h1_ref[...] + g2_ref[...] * ff,
                                 lg_ref[...], lb_ref[...])

        @pl.when(i == pl.num_programs(0) - 1)
        def _():
            wait_gathers(other)

    @pl.when(i % 2 == 0)
    def _():
        tile(0)

    @pl.when(i % 2 == 1)
    def _():
        tile(1)


def _combine(dest, wts, v, h1, g2, ln_g, ln_b, w_s1, w_s3, w_s2, ys, tm):
    s, d = v.shape
    n = s // tm
    row = lambda i: (i, 0)
    vec = _const_spec((1, d))
    slot = pl.BlockSpec((TOP_K, tm), lambda i: (0, i))
    single = lambda shape: pl.BlockSpec(shape, lambda i: (0, 0), pipeline_mode=pl.Buffered(1))
    return pl.pallas_call(
        _combine_kernel,
        grid=(n,),
        in_specs=[pl.BlockSpec((TOP_K, tm), lambda i: (0, 0)),
                  pl.BlockSpec((TOP_K, tm), lambda i: (0, jnp.minimum(i + 1, n - 1))),
                  slot, pl.BlockSpec((tm, d), row), pl.BlockSpec((tm, d), row),
                  vec, vec, vec,
                  single(w_s1.shape), single(w_s3.shape), single(w_s2.shape),
                  pl.BlockSpec(memory_space=pl.ANY)],
        out_specs=pl.BlockSpec((tm, d), row),
        out_shape=jax.ShapeDtypeStruct((s, d), F32),
        scratch_shapes=[pltpu.SMEM((2, TOP_K, tm), I32),
                        pltpu.VMEM((TOP_K, tm, d // 2), U32), pltpu.VMEM((TOP_K, tm, d // 2), U32),
                        pltpu.SemaphoreType.DMA, pltpu.SemaphoreType.DMA((2,))],
        compiler_params=_cparams("arbitrary"),
        name="combine",
    )(dest, dest, wts, v, h1, g2, ln_g, ln_b, w_s1, w_s3, w_s2, ys)


def _rope_tables(s):
    half = HEAD_DIM // 4
    inv_freq = ROPE_THETA ** (-np.arange(half, dtype=np.float32) / half)
    t = np.arange(s)
    ang_r = (t // GRID_W).astype(np.float32)[:, None] * inv_freq[None, :]
    ang_c = (t % GRID_W).astype(np.float32)[:, None] * inv_freq[None, :]
    cos = np.concatenate([np.cos(ang_r)] * 2 + [np.cos(ang_c)] * 2, axis=-1)
    sin = np.concatenate([-np.sin(ang_r), np.sin(ang_r), -np.sin(ang_c), np.sin(ang_c)], axis=-1)
    return jnp.asarray(cos, F32), jnp.asarray(sin, F32)


def kernel(x, c, ctx, c_ctx, w_mod, b_mod, w_in, q_norm, k_norm, conv_w, conv_b, rg_wa, rg_ba,
           rg_wx, rg_bx, rg_lam, w_out, ln1_g, ln1_b, w_router, e_bias, w_e1, w_e3, w_e2,
           w_s1, w_s3, w_s2, ln2_g, ln2_b):
    assert x.shape[0] == 1 and w_mod.shape[0] == DEPTH
    _, s, d = x.shape
    n_ctx = ctx.shape[1]
    x2 = x[0]

    c2t = jnp.stack([c[0], c_ctx], axis=1)
    mod = _modulation(c2t, w_mod[0], b_mod[0].reshape(1, -1))
    sh1, sc1, g1, sh2, sc2, g2 = [mod[0:1, j * d:(j + 1) * d] for j in range(6)]
    csh1, csc1 = mod[1:2, 0:d], mod[1:2, d:2 * d]

    w_in_b = w_in[0].astype(BF16)
    qg = q_norm[0].reshape(1, HEAD_DIM)
    kg = k_norm[0].reshape(1, HEAD_DIM)
    cos, sin = _rope_tables(s)
    q_l, k_l, vt_l, xr_l, yr_l = _in_projection(x2, sc1, sh1, w_in_b, qg, kg, cos, sin,
                                                tm=min(INPROJ_ROWS, s), kv_chunk=min(KV_CHUNK, s))
    _, k_c, vt_c, xr_c, _ = _in_projection(
        ctx[0], csc1, csh1, w_in_b, qg, kg,
        jnp.ones((n_ctx, HEAD_DIM), F32), jnp.zeros((n_ctx, HEAD_DIM), F32),
        tm=n_ctx, kv_chunk=n_ctx)

    attn = _attention(q_l, k_c, vt_c, k_l, vt_l, tq=min(ATTN_QUERIES, s))

    wg = jnp.concatenate([rg_wa[0], rg_wx[0]], axis=-1).astype(BF16)
    cb = conv_b[0].reshape(1, RG_W)
    zero_state = jnp.zeros((1, RG_W), F32)
    rg_args = []
    for dd in range(2):
        rg_args.append((conv_w[0], cb, wg[dd],
                        jnp.stack([rg_ba[0, dd], rg_bx[0, dd]], axis=0),
                        rg_lam[0, dd].reshape(1, RG_W)))
    t_rg = min(RG_ROWS, s)
    hc_f = _rg_scan(xr_c, *rg_args[0], zero_state, reverse=False, t=n_ctx)
    hc_b = _rg_scan(xr_c, *rg_args[1], zero_state, reverse=True, t=n_ctx)
    h_f = _rg_scan(xr_l, *rg_args[0], hc_f[n_ctx - 1:n_ctx], reverse=False, t=t_rg)
    rg = _rg_scan(xr_l, *rg_args[1], hc_b[0:1], reverse=True, t=t_rg, hf=h_f, yr=yr_l)

    h1, v = _out_projection(attn, rg, x2, w_out[0].astype(BF16), g1,
                            ln1_g[0].reshape(1, d), ln1_b[0].reshape(1, d), sc2, sh2,
                            tm=min(OUTPROJ_ROWS, s))

    wr = jnp.pad(w_router[0], ((0, 0), (0, LANES - N_EXPERTS)))
    wr_hi = wr.astype(BF16)
    wr_lo = (wr - wr_hi.astype(F32)).astype(BF16)
    idx, wts, rank, cnt = _router(v, wr_hi, wr_lo, e_bias[0].reshape(N_EXPERTS, 1),
                                  tm=min(ROUTER_ROWS, s))

    bm = min(EXPERT_BLOCK, s)
    n_blocks = s * TOP_K // bm + N_EXPERTS
    pad_start, last_row, has_blk, blk_e, n_used = _block_tables(cnt[:, 0].astype(I32), n_blocks, bm)
    dest, xs = _dispatch(last_row, has_blk, v, idx, rank,
                         pad_start.astype(F32).reshape(N_EXPERTS, 1), n_blocks * bm,
                         tm=min(DISPATCH_ROWS, s), bm=bm)
    ys = _experts(blk_e, n_used, xs, w_e1[0], w_e3[0], w_e2[0], bm)

    out = _combine(dest, wts, v, h1, g2, ln2_g[0].reshape(1, d), ln2_b[0].reshape(1, d),
                   w_s1[0].astype(BF16), w_s3[0].astype(BF16), w_s2[0].astype(BF16), ys,
                   tm=min(COMBINE_ROWS, s))
    return out[None]
```

```python
import functools

import jax
import jax.numpy as jnp
import numpy as np
from jax import lax
from jax.experimental import pallas as pl
from jax.experimental.pallas import tpu as pltpu

F32 = jnp.float32
BF16 = jnp.bfloat16
I32 = jnp.int32
U32 = jnp.uint32

GRID_W = 64
HEAD_DIM = 128
N_HEADS = 8
N_KV_HEADS = 2
GQA_GROUP = N_HEADS // N_KV_HEADS
ATTN_W = N_HEADS * HEAD_DIM
KV_W = N_KV_HEADS * HEAD_DIM
ROPE_THETA = 10000.0
RG_W = 1024
RG_HEADS = 8
RG_HD = RG_W // RG_HEADS
RG_C = 8.0
PROJ_W = ATTN_W + 2 * KV_W + 2 * RG_W
N_EXPERTS = 64
N_GROUPS = 8
GROUP_SIZE = N_EXPERTS // N_GROUPS
TOPK_GROUPS = 4
TOP_K = 8
ROUTED_SCALE = 2.5
NORM_EPS = 1e-6
DEPTH = 1
DEEPNORM_ALPHA = (2.0 * DEPTH) ** 0.25
LOG2E = 1.4426950408889634

LANES = 128
SUBLANES = 8
BF16_SUBLANES = 16
VMEM_LIMIT = 56 * 1024 * 1024

NEG_BIG = -1e30
ROPE_AXIS_DIM = HEAD_DIM // 2
VT_ROWS = HEAD_DIM + BF16_SUBLANES

MOD_COLS = 1024
INPROJ_ROWS = 512
ATTN_QUERIES = 256
KV_CHUNK = 512
ATTN_UNROLL = 32
RG_ROWS = 512
OUTPROJ_ROWS = 512
ROUTER_ROWS = 256
EXPERT_BLOCK = 512
DISPATCH_ROWS = 256
COMBINE_ROWS = 256


def _cparams(*sem):
    return pltpu.CompilerParams(dimension_semantics=sem, vmem_limit_bytes=VMEM_LIMIT)


def _const_spec(shape):
    nd = len(shape)
    return pl.BlockSpec(shape, lambda *_: (0,) * nd)


def _mod_kernel(ct_ref, w_ref, b_ref, o_ref, sb_ref, *, tn):
    d = w_ref.shape[0]

    @pl.when(pl.program_id(0) == 0)
    def _():
        ct = ct_ref[...]
        s = ct * jax.nn.sigmoid(ct)
        sb_ref[0] = jnp.broadcast_to(s[:, 0:1], (d, LANES))
        sb_ref[1] = jnp.broadcast_to(s[:, 1:2], (d, LANES))

    for c in range(tn // LANES):
        sl = slice(c * LANES, (c + 1) * LANES)
        wc = w_ref[:, sl]
        bc = b_ref[:, sl]
        o0 = jnp.sum(wc * sb_ref[0], axis=0, keepdims=True) + bc
        o1 = jnp.sum(wc * sb_ref[1], axis=0, keepdims=True) + bc
        o_ref[:, sl] = jnp.concatenate(
            [o0, o1, jnp.zeros((SUBLANES - 2, LANES), F32)], axis=0)


def _modulation(c2t, w_mod, b_mod):
    d, n = w_mod.shape
    tn = MOD_COLS
    return pl.pallas_call(
        functools.partial(_mod_kernel, tn=tn),
        grid=(n // tn,),
        in_specs=[_const_spec((d, 2)),
                  pl.BlockSpec((d, tn), lambda j: (0, j)),
                  pl.BlockSpec((1, tn), lambda j: (0, j))],
        out_specs=pl.BlockSpec((SUBLANES, tn), lambda j: (0, j)),
        out_shape=jax.ShapeDtypeStruct((SUBLANES, n), F32),
        scratch_shapes=[pltpu.VMEM((2, d, LANES), F32)],
        compiler_params=_cparams("arbitrary"),
        name="modulation",
    )(c2t, w_mod, b_mod)


def _swap_half(y):
    half = ROPE_AXIS_DIM // 2
    lane = lax.broadcasted_iota(I32, y.shape, 1)
    return jnp.where((lane % ROPE_AXIS_DIM) < half,
                     pltpu.roll(y, LANES - half, 1), pltpu.roll(y, half, 1))


def _norm_rope(ph, g, cos, sin_signed, scale):
    ms = jnp.mean(ph * ph, axis=-1, keepdims=True)
    y = (ph * lax.rsqrt(ms + NORM_EPS)) * g
    y = y * cos + _swap_half(y) * sin_signed
    if scale != 1.0:
        y = y * scale
    return y


def _inproj_kernel(x_ref, sc_ref, sh_ref, w_ref, qg_ref, kg_ref, cos_ref, sin_ref,
                   q_ref, k_ref, vt_ref, xr_ref, yr_ref, *, q_scale):
    tm = x_ref.shape[0]
    kc = vt_ref.shape[-1]
    u = (x_ref[...] * (1.0 + sc_ref[...]) + sh_ref[...]).astype(BF16)
    cos = cos_ref[...]
    sin = sin_ref[...]
    o = 0
    pq = jnp.dot(u, w_ref[:, o:o + ATTN_W], preferred_element_type=F32)
    for h in range(N_HEADS):
        sl = slice(h * HEAD_DIM, (h + 1) * HEAD_DIM)
        q_ref[:, sl] = _norm_rope(pq[:, sl], qg_ref[...], cos, sin, q_scale).astype(BF16)
    o += ATTN_W
    pk = jnp.dot(u, w_ref[:, o:o + KV_W], preferred_element_type=F32)
    for h in range(N_KV_HEADS):
        sl = slice(h * HEAD_DIM, (h + 1) * HEAD_DIM)
        k_ref[:, sl] = _norm_rope(pk[:, sl], kg_ref[...], cos, sin, 1.0).astype(BF16)
    o += KV_W
    pv = jnp.dot(u, w_ref[:, o:o + KV_W], preferred_element_type=F32)
    ones_rows = jnp.where(
        lax.broadcasted_iota(I32, (VT_ROWS - HEAD_DIM, kc), 0) == 0, 1.0, 0.0).astype(BF16)
    for h in range(N_KV_HEADS):
        for cc in range(tm // kc):
            blk = pv[cc * kc:(cc + 1) * kc, h * HEAD_DIM:(h + 1) * HEAD_DIM]
            vt_ref[h, cc, 0:HEAD_DIM, :] = blk.T.astype(BF16)
            vt_ref[h, cc, HEAD_DIM:VT_ROWS, :] = ones_rows
    o += KV_W
    xr_ref[...] = jnp.dot(u, w_ref[:, o:o + RG_W], preferred_element_type=F32)
    o += RG_W
    yr_ref[...] = jnp.dot(u, w_ref[:, o:o + RG_W], preferred_element_type=F32)


def _in_projection(x, sc, sh, w_in, qg, kg, cos, sin, tm, kv_chunk):
    s, d = x.shape
    row = lambda i: (i, 0)
    if kv_chunk >= tm:
        per = kv_chunk // tm
        vt_spec = pl.BlockSpec((N_KV_HEADS, 1, VT_ROWS, tm), lambda i: (0, i // per, 0, i % per))
    else:
        vt_spec = pl.BlockSpec((N_KV_HEADS, tm // kv_chunk, VT_ROWS, kv_chunk),
                               lambda i: (0, i, 0, 0))
    return pl.pallas_call(
        functools.partial(_inproj_kernel, q_scale=HEAD_DIM ** -0.5 * LOG2E),
        grid=(s // tm,),
        in_specs=[pl.BlockSpec((tm, d), row),
                  _const_spec((1, d)), _const_spec((1, d)),
                  pl.BlockSpec((d, PROJ_W), lambda i: (0, 0), pipeline_mode=pl.Buffered(1)),
                  _const_spec((1, HEAD_DIM)), _const_spec((1, HEAD_DIM)),
                  pl.BlockSpec((tm, HEAD_DIM), row), pl.BlockSpec((tm, HEAD_DIM), row)],
        out_specs=[pl.BlockSpec((tm, ATTN_W), row), pl.BlockSpec((tm, KV_W), row),
                   vt_spec,
                   pl.BlockSpec((tm, RG_W), row), pl.BlockSpec((tm, RG_W), row)],
        out_shape=[jax.ShapeDtypeStruct((s, ATTN_W), BF16),
                   jax.ShapeDtypeStruct((s, KV_W), BF16),
                   jax.ShapeDtypeStruct((N_KV_HEADS, s // kv_chunk, VT_ROWS, kv_chunk), BF16),
                   jax.ShapeDtypeStruct((s, RG_W), F32),
                   jax.ShapeDtypeStruct((s, RG_W), F32)],
        compiler_params=_cparams("arbitrary"),
        name="in_projection",
    )(x, sc, sh, w_in, qg, kg, cos, sin)


def _attn_kernel(q_ref, kc_ref, vtc_ref, kl_ref, vtl_ref, o_ref, sa_ref, sb_ref, pa_ref, pb_ref, sc_ref,
                 *, tq, tk, n_chunks, unroll):
    q = q_ref[...]
    qs = jnp.concatenate(
        [q[:, g * HEAD_DIM:(g + 1) * HEAD_DIM] for g in range(GQA_GROUP)], axis=0)
    cols = GQA_GROUP * tq

    def scores(k):
        return lax.dot_general(k, qs, (((1,), (1,)), ((), ())), preferred_element_type=F32)

    def softmax(s_ref, m):
        m_new = jnp.maximum(m, jnp.max(s_ref[...], axis=0, keepdims=True))
        return m_new, jnp.exp2(m - m_new), jnp.exp2(s_ref[...] - m_new).astype(BF16)

    def weighted_values(vt, p, alpha, acc):
        return alpha * acc + jnp.dot(vt, p, preferred_element_type=F32)

    m0 = jnp.full((1, cols), NEG_BIG, F32)
    a0 = jnp.zeros((VT_ROWS, cols), F32)
    sc_ref[...] = scores(kc_ref[...])
    m, alpha, p = softmax(sc_ref, m0)
    acc = weighted_values(vtc_ref[0], p, alpha, a0)

    def latent_scores(j):
        j = jnp.minimum(j, n_chunks - 1)
        return scores(kl_ref[pl.ds(pl.multiple_of(j * tk, tk), tk), :])

    s_refs = (sa_ref, sb_ref)
    p_refs = (pa_ref, pb_ref)
    sa_ref[...] = latent_scores(0)
    sb_ref[...] = latent_scores(1)
    m, alpha, pa_ref[...] = softmax(sa_ref, m)

    def body(i, carry):
        m, acc, alpha = carry
        for u in range(unroll):
            c = unroll * i + u
            cur, nxt = u % 2, (u + 1) % 2
            s_refs[cur][...] = latent_scores(c + 2)
            m, alpha_next, p_refs[nxt][...] = softmax(s_refs[nxt], m)
            acc = weighted_values(vtl_ref[c], p_refs[cur][...], alpha, acc)
            alpha = alpha_next
        return m, acc, alpha

    _, acc, _ = lax.fori_loop(0, n_chunks // unroll, body, (m, acc, alpha))
    out_t = acc[0:HEAD_DIM, :] / acc[HEAD_DIM:HEAD_DIM + 1, :]
    for g in range(GQA_GROUP):
        o_ref[:, g * HEAD_DIM:(g + 1) * HEAD_DIM] = out_t[:, g * tq:(g + 1) * tq].T.astype(BF16)


def _attention(q, k_c, vt_c, k_l, vt_l, tq):
    s = q.shape[0]
    n_ctx = k_c.shape[0]
    n_chunks, _, tk = vt_l.shape[1:]
    unroll = min(ATTN_UNROLL, n_chunks)
    assert n_chunks % unroll == 0 and unroll % 2 == 0
    gw = GQA_GROUP * HEAD_DIM
    return pl.pallas_call(
        functools.partial(_attn_kernel, tq=tq, tk=tk, n_chunks=n_chunks, unroll=unroll),
        grid=(N_KV_HEADS, s // tq),
        in_specs=[pl.BlockSpec((tq, gw), lambda h, i: (i, h)),
                  pl.BlockSpec((n_ctx, HEAD_DIM), lambda h, i: (0, h)),
                  pl.BlockSpec((None, 1, VT_ROWS, n_ctx), lambda h, i: (h, 0, 0, 0)),
                  pl.BlockSpec((s, HEAD_DIM), lambda h, i: (0, h)),
                  pl.BlockSpec((None, n_chunks, VT_ROWS, tk), lambda h, i: (h, 0, 0, 0))],
        out_specs=pl.BlockSpec((tq, gw), lambda h, i: (i, h)),
        out_shape=jax.ShapeDtypeStruct((s, ATTN_W), BF16),
        scratch_shapes=[pltpu.VMEM((tk, GQA_GROUP * tq), F32),
                        pltpu.VMEM((tk, GQA_GROUP * tq), F32),
                        pltpu.VMEM((tk, GQA_GROUP * tq), BF16),
                        pltpu.VMEM((tk, GQA_GROUP * tq), BF16),
                        pltpu.VMEM((n_ctx, GQA_GROUP * tq), F32)],
        compiler_params=_cparams("arbitrary", "arbitrary"),
        name="attention",
    )(q, k_c, vt_c, k_l, vt_l)


def _log_sigmoid(x):
    return jnp.minimum(x, 0.0) - jnp.log1p(jnp.exp(-jnp.abs(x)))


def _rg_kernel(x_ref, xp_ref, xn_ref, cw_ref, cb_ref, wg_ref, bg_ref, lam_ref, h0_ref,
               *rest, reverse, final, t, n_chunks):
    if final:
        hf_ref, yr_ref, o_ref, a_scr, b_scr, hc_scr, h_scr = rest
    else:
        o_ref, a_scr, b_scr, hc_scr = rest
        h_scr = o_ref
    i = pl.program_id(0)
    c = (n_chunks - 1 - i) if reverse else i
    w = x_ref.shape[1]

    @pl.when(i == 0)
    def _():
        hc_scr[...] = jnp.broadcast_to(h0_ref[...], (SUBLANES, w))

    x = x_ref[...]
    row = lax.broadcasted_iota(I32, (t, w), 0)
    pm = jnp.where(c == 0, 0.0, 1.0).astype(F32)
    nm = jnp.where(c == n_chunks - 1, 0.0, 1.0).astype(F32)
    p6 = xp_ref[SUBLANES - 2:SUBLANES - 1, :] * pm
    p7 = xp_ref[SUBLANES - 1:SUBLANES, :] * pm
    n0 = xn_ref[0:1, :] * nm
    x_m1 = jnp.where(row == 0, p7, pltpu.roll(x, 1, 0))
    x_m2 = jnp.where(row == 0, p6, jnp.where(row == 1, p7, pltpu.roll(x, 2, 0)))
    x_p1 = jnp.where(row == t - 1, n0, pltpu.roll(x, t - 1, 0))
    xc = cb_ref[...] + cw_ref[0:1, :] * x_m2
    xc = xc + cw_ref[1:2, :] * x_m1
    xc = xc + cw_ref[2:3, :] * x
    xc = xc + cw_ref[3:4, :] * x_p1

    xcb = xc.astype(BF16)
    clam = RG_C * _log_sigmoid(lam_ref[...])
    for h in range(RG_HEADS):
        sl = slice(h * RG_HD, (h + 1) * RG_HD)
        g = jnp.dot(xcb[:, sl], wg_ref[h], preferred_element_type=F32)
        r = jax.nn.sigmoid(g[:, :RG_HD] + bg_ref[0:1, sl])
        gi = jax.nn.sigmoid(g[:, RG_HD:] + bg_ref[1:2, sl])
        log_a = r * clam[:, sl]
        a = jnp.exp(log_a)
        a_scr[:, sl] = a
        b_scr[:, sl] = jnp.sqrt(-jnp.tanh(log_a) * (a * a + 1.0)) * (gi * xc[:, sl])

    srow = lax.broadcasted_iota(I32, (SUBLANES, w), 0)
    n_tiles = t // SUBLANES

    def tile_body(j, hprev):
        tile = (n_tiles - 1 - j) if reverse else j
        start = pl.multiple_of(tile * SUBLANES, SUBLANES)
        a = a_scr[pl.ds(start, SUBLANES), :]
        b = b_scr[pl.ds(start, SUBLANES), :]
        for k in (1, 2, 4):
            if reverse:
                keep = srow < SUBLANES - k
                shift = SUBLANES - k
            else:
                keep = srow >= k
                shift = k
            a_sh = jnp.where(keep, pltpu.roll(a, shift, 0), 1.0)
            b_sh = jnp.where(keep, pltpu.roll(b, shift, 0), 0.0)
            b = a * b_sh + b
            a = a * a_sh
        hh = a * hprev + b
        h_scr[pl.ds(start, SUBLANES), :] = hh
        last = hh[0:1, :] if reverse else hh[SUBLANES - 1:SUBLANES, :]
        return jnp.broadcast_to(last, (SUBLANES, w))

    hc_scr[...] = lax.fori_loop(0, n_tiles, tile_body, hc_scr[...], unroll=8)

    if final:
        gate = jax.nn.gelu(yr_ref[...], approximate=True)
        o_ref[...] = ((hf_ref[...] + h_scr[...]) * gate).astype(o_ref.dtype)


def _rg_scan(xr, conv_w, conv_b, wg, bg, lam, h0, *, reverse, t, hf=None, yr=None):
    s, w = xr.shape
    n_chunks = s // t
    final = hf is not None
    tb = t // SUBLANES
    last_blk = s // SUBLANES - 1
    if reverse:
        cidx = lambda i: n_chunks - 1 - i
    else:
        cidx = lambda i: i
    chunk_spec = pl.BlockSpec((t, w), lambda i: (cidx(i), 0))
    in_specs = [chunk_spec,
                pl.BlockSpec((SUBLANES, w), lambda i: (jnp.maximum(cidx(i) * tb - 1, 0), 0)),
                pl.BlockSpec((SUBLANES, w), lambda i: (jnp.minimum((cidx(i) + 1) * tb, last_blk), 0)),
                _const_spec((4, w)), _const_spec((1, w)),
                _const_spec((RG_HEADS, RG_HD, 2 * RG_HD)), _const_spec((2, w)),
                _const_spec((1, w)), _const_spec((1, w))]
    args = [xr, xr, xr, conv_w, conv_b, wg, bg, lam, h0]
    scratch = [pltpu.VMEM((t, w), F32), pltpu.VMEM((t, w), F32), pltpu.VMEM((SUBLANES, w), F32)]
    if final:
        in_specs += [chunk_spec, chunk_spec]
        args += [hf, yr]
        scratch.append(pltpu.VMEM((t, w), F32))
        out_dtype = BF16
    else:
        out_dtype = F32
    return pl.pallas_call(
        functools.partial(_rg_kernel, reverse=reverse, final=final, t=t, n_chunks=n_chunks),
        grid=(n_chunks,),
        in_specs=in_specs,
        out_specs=chunk_spec,
        out_shape=jax.ShapeDtypeStruct((s, w), out_dtype),
        scratch_shapes=scratch,
        compiler_params=_cparams("arbitrary"),
        name="rglru_bwd" if reverse else "rglru_fwd",
    )(*args)


def _layer_norm(y, g, b):
    mu = jnp.mean(y, axis=-1, keepdims=True)
    yc = y - mu
    var = jnp.mean(yc * yc, axis=-1, keepdims=True)
    return yc * lax.rsqrt(var + NORM_EPS) * g + b


def _outproj_kernel(attn_ref, rg_ref, x_ref, w_ref, g1_ref, lg_ref, lb_ref, sc2_ref, sh2_ref,
                    h1_ref, v_ref):
    mix = jnp.dot(attn_ref[...], w_ref[0:ATTN_W, :], preferred_element_type=F32)
    mix = mix + jnp.dot(rg_ref[...], w_ref[ATTN_W:, :], preferred_element_type=F32)
    h1 = _layer_norm(DEEPNORM_ALPHA * x_ref[...] + g1_ref[...] * mix, lg_ref[...], lb_ref[...])
    h1_ref[...] = h1
    v_ref[...] = h1 * (1.0 + sc2_ref[...]) + sh2_ref[...]


def _out_projection(attn, rg, x, w_out, g1, ln_g, ln_b, sc2, sh2, tm):
    s, d = x.shape
    row = lambda i: (i, 0)
    vec = _const_spec((1, d))
    return pl.pallas_call(
        _outproj_kernel,
        grid=(s // tm,),
        in_specs=[pl.BlockSpec((tm, ATTN_W), row), pl.BlockSpec((tm, RG_W), row),
                  pl.BlockSpec((tm, d), row),
                  pl.BlockSpec((ATTN_W + RG_W, d), lambda i: (0, 0), pipeline_mode=pl.Buffered(1)),
                  vec, vec, vec, vec, vec],
        out_specs=[pl.BlockSpec((tm, d), row), pl.BlockSpec((tm, d), row)],
        out_shape=[jax.ShapeDtypeStruct((s, d), F32), jax.ShapeDtypeStruct((s, d), F32)],
        compiler_params=_cparams("arbitrary"),
        name="out_projection",
    )(attn, rg, x, w_out, g1, ln_g, ln_b, sc2, sh2)


def _first_index_of_max(x, iota_f, axis):
    mx = jnp.max(x, axis=axis, keepdims=True)
    idx = jnp.min(jnp.where(x == mx, iota_f, float(N_EXPERTS)), axis=axis, keepdims=True)
    return mx, idx


def _router_kernel(v_ref, whi_ref, wlo_ref, eb_ref, tri_ref,
                   idx_ref, wts_ref, rank_ref, cnt_ref, base_scr):
    v = v_ref[...]
    tm = v.shape[0]

    @pl.when(pl.program_id(0) == 0)
    def _():
        base_scr[...] = jnp.zeros(base_scr.shape, F32)

    v_hi = v.astype(BF16)
    v_lo = (v - v_hi.astype(F32)).astype(BF16)
    logits = jnp.dot(v_hi, whi_ref[...], preferred_element_type=F32)
    logits = logits + jnp.dot(v_lo, whi_ref[...], preferred_element_type=F32)
    logits = logits + jnp.dot(v_hi, wlo_ref[...], preferred_element_type=F32)
    lt = logits.T[0:N_EXPERTS, :]
    scores = jax.nn.sigmoid(lt)
    biased = scores + eb_ref[...]
    neg_inf = float("-inf")

    ig = lax.broadcasted_iota(I32, (GROUP_SIZE, tm), 0).astype(F32)
    groups = [biased[g * GROUP_SIZE:(g + 1) * GROUP_SIZE, :] for g in range(N_GROUPS)]
    gscore = []
    for bg in groups:
        top1, i1 = _first_index_of_max(bg, ig, 0)
        top2 = jnp.max(jnp.where(ig == i1, neg_inf, bg), axis=0, keepdims=True)
        gscore.append(top1 + top2)

    masked = []
    for g in range(N_GROUPS):
        ahead = jnp.zeros((1, tm), F32)
        for o in range(N_GROUPS):
            if o == g:
                continue
            before = (gscore[o] >= gscore[g]) if o < g else (gscore[o] > gscore[g])
            ahead = ahead + jnp.where(before, 1.0, 0.0)
        keep = jnp.broadcast_to(ahead < TOPK_GROUPS, (GROUP_SIZE, tm))
        masked.append(jnp.where(keep, groups[g], neg_inf))
    masked = jnp.concatenate(masked, axis=0)

    ie = lax.broadcasted_iota(I32, masked.shape, 0).astype(F32)
    seen = base_scr[...]
    ws = []
    for k in range(TOP_K):
        _, ei = _first_index_of_max(masked, ie, 0)
        hit = ie == ei
        idx_ref[k:k + 1, :] = ei.astype(I32)
        ws.append(jnp.sum(jnp.where(hit, scores, 0.0), axis=0, keepdims=True))
        masked = jnp.where(hit, neg_inf, masked)
        onehot = jnp.where(hit, 1.0, 0.0)
        before = jnp.dot(onehot.astype(BF16), tri_ref[...], preferred_element_type=F32)
        rank = jnp.sum(jnp.where(hit, before + seen, 0.0), axis=0, keepdims=True)
        rank_ref[k:k + 1, :] = rank.astype(I32)
        seen = seen + jnp.sum(onehot, axis=1, keepdims=True)
    base_scr[...] = seen
    cnt_ref[...] = jnp.broadcast_to(seen, cnt_ref.shape)
    total = ws[0]
    for k in range(1, TOP_K):
        total = total + ws[k]
    for k in range(TOP_K):
        wts_ref[k:k + 1, :] = ws[k] / total * ROUTED_SCALE


def _router(v, w_hi, w_lo, e_bias_col, tm):
    s, d = v.shape
    tri = jnp.triu(jnp.ones((tm, tm), BF16), k=1)
    slot = pl.BlockSpec((TOP_K, tm), lambda i: (0, i))
    return pl.pallas_call(
        _router_kernel,
        grid=(s // tm,),
        in_specs=[pl.BlockSpec((tm, d), lambda i: (i, 0)),
                  _const_spec((d, LANES)), _const_spec((d, LANES)),
                  _const_spec((N_EXPERTS, 1)), _const_spec((tm, tm))],
        out_specs=[slot, slot, slot, _const_spec((N_EXPERTS, LANES))],
        out_shape=[jax.ShapeDtypeStruct((TOP_K, s), I32),
                   jax.ShapeDtypeStruct((TOP_K, s), F32),
                   jax.ShapeDtypeStruct((TOP_K, s), I32),
                   jax.ShapeDtypeStruct((N_EXPERTS, LANES), F32)],
        scratch_shapes=[pltpu.VMEM((N_EXPERTS, 1), F32)],
        compiler_params=_cparams("arbitrary"),
        name="router",
    )(v, w_hi, w_lo, e_bias_col, tri)


def _pack_pairs(x):
    h = x.shape[1] // 2
    lo = lax.bitcast_convert_type(x[:, :h].astype(BF16).astype(F32), U32)
    hi = lax.bitcast_convert_type(x[:, h:].astype(BF16).astype(F32), U32)
    return (lo >> 16) | (hi & jnp.uint32(0xFFFF0000))


def _unpack_pairs(w):
    lo = lax.bitcast_convert_type(w << 16, F32)
    hi = lax.bitcast_convert_type(w & jnp.uint32(0xFFFF0000), F32)
    return jnp.concatenate([lo, hi], axis=1)


def _dispatch_kernel(last_ref, has_ref, v_ref, idx_ref, rank_ref, pstart_ref,
                     dest_ref, xs_hbm, dest_v, dest_s, zeros_v, pk_a, pk_b, sem_z, sem_i, sem_r,
                     *, bm):
    tm = v_ref.shape[0]

    def zero_fill(e):
        row0 = pl.multiple_of(last_ref[e], bm)
        return pltpu.make_async_copy(zeros_v, xs_hbm.at[pl.ds(row0, bm), :], sem_z)

    @pl.when(pl.program_id(0) == 0)
    def _():
        zeros_v[...] = jnp.zeros(zeros_v.shape, zeros_v.dtype)

        def start(e, carry):
            @pl.when(has_ref[e] > 0)
            def _():
                zero_fill(e).start()
            return carry

        def wait(e, carry):
            @pl.when(has_ref[e] > 0)
            def _():
                zero_fill(e).wait()
            return carry

        lax.fori_loop(0, N_EXPERTS, start, 0)
        lax.fori_loop(0, N_EXPERTS, wait, 0)

    ie = lax.broadcasted_iota(I32, (N_EXPERTS, tm), 0)
    pstart = pstart_ref[...]
    for k in range(TOP_K):
        hit = ie == idx_ref[k:k + 1, :]
        seg = jnp.sum(jnp.where(hit, pstart, 0.0), axis=0, keepdims=True)
        dest_v[k:k + 1, :] = seg.astype(I32) + rank_ref[k:k + 1, :]
    dest_ref[...] = dest_v[...]
    cp = pltpu.make_async_copy(dest_v, dest_s, sem_i)
    cp.start()
    packed = _pack_pairs(v_ref[...])
    cp.wait()
    i = pl.program_id(0)
    pk_bufs = (pk_a, pk_b)

    def drain(slot):
        for k in range(TOP_K):
            pltpu.make_async_copy(pk_bufs[slot], xs_hbm.at[pl.ds(0, tm), :], sem_r.at[slot]).wait()

    def tile(slot):
        pk = pk_bufs[slot]
        pk[...] = packed
        for t in range(tm):
            for k in range(TOP_K):
                pltpu.make_async_copy(pk.at[pl.ds(t, 1), :], xs_hbm.at[pl.ds(dest_s[k, t], 1), :],
                                      sem_r.at[slot]).start(priority=k % 2)

        @pl.when(i > 0)
        def _():
            drain(1 - slot)

        @pl.when(i == pl.num_programs(0) - 1)
        def _():
            drain(slot)

    @pl.when(i % 2 == 0)
    def _():
        tile(0)

    @pl.when(i % 2 == 1)
    def _():
        tile(1)


def _dispatch(last_row, has_blk, v, idx, rank, pstart_col, n_rows, tm, bm):
    s, d = v.shape
    slot = pl.BlockSpec((TOP_K, tm), lambda i, *_: (0, i))
    grid_spec = pltpu.PrefetchScalarGridSpec(
        num_scalar_prefetch=2,
        grid=(s // tm,),
        in_specs=[pl.BlockSpec((tm, d), lambda i, *_: (i, 0)), slot, slot,
                  pl.BlockSpec((N_EXPERTS, 1), lambda i, *_: (0, 0))],
        out_specs=[slot, pl.BlockSpec(memory_space=pl.ANY)],
        scratch_shapes=[pltpu.VMEM((TOP_K, tm), I32), pltpu.SMEM((TOP_K, tm), I32),
                        pltpu.VMEM((bm, d // 2), U32),
                        pltpu.VMEM((tm, d // 2), U32), pltpu.VMEM((tm, d // 2), U32),
                        pltpu.SemaphoreType.DMA, pltpu.SemaphoreType.DMA,
                        pltpu.SemaphoreType.DMA((2,))])
    return pl.pallas_call(
        functools.partial(_dispatch_kernel, bm=bm),
        grid_spec=grid_spec,
        out_shape=[jax.ShapeDtypeStruct((TOP_K, s), I32),
                   jax.ShapeDtypeStruct((n_rows, d // 2), U32)],
        compiler_params=_cparams("arbitrary"),
        name="dispatch",
    )(last_row, has_blk, v, idx, rank, pstart_col)


def _expert_kernel(blk_e_ref, n_used_ref, grp_ref, nxt_ref, x_ref, w1_hbm, w3_hbm, w2_hbm, y_ref,
                   wf1, wf3, wf2, w1b, w3b, w2b, sem_w):
    b = pl.program_id(0)

    def weight_copies(e, slot):
        return (pltpu.make_async_copy(w1_hbm.at[e], wf1.at[slot], sem_w.at[slot, 0]),
                pltpu.make_async_copy(w3_hbm.at[e], wf3.at[slot], sem_w.at[slot, 1]),
                pltpu.make_async_copy(w2_hbm.at[e], wf2.at[slot], sem_w.at[slot, 2]))

    @pl.when(b < n_used_ref[0])
    def _():
        e = blk_e_ref[b]
        slot = grp_ref[b] % 2
        prev_e = blk_e_ref[jnp.maximum(b - 1, 0)]

        @pl.when(b == 0)
        def _():
            for cp in weight_copies(e, slot):
                cp.start()

        @pl.when(jnp.logical_or(b == 0, e != prev_e))
        def _():
            for cp in weight_copies(e, slot):
                cp.wait()
            w1b[...] = wf1[slot].astype(BF16)
            w3b[...] = wf3[slot].astype(BF16)
            w2b[...] = wf2[slot].astype(BF16)
            nxt = nxt_ref[b]

            @pl.when(nxt != e)
            def _():
                for cp in weight_copies(nxt, 1 - slot):
                    cp.start()

        xb = _unpack_pairs(x_ref[...]).astype(BF16)
        h1 = jnp.dot(xb, w1b[...], preferred_element_type=F32)
        h3 = jnp.dot(xb, w3b[...], preferred_element_type=F32)
        act = (h1 * jax.nn.sigmoid(h1) * h3).astype(BF16)
        y_ref[...] = _pack_pairs(jnp.dot(act, w2b[...], preferred_element_type=F32))


def _experts(blk_e, n_used, xs, w_e1, w_e3, w_e2, bm):
    n_rows = xs.shape[0]
    d, ff = w_e1.shape[-2:]
    rows = lambda b, be, nu, gr, nx: (jnp.minimum(b, nu[0] - 1), 0)
    grp = jnp.cumsum(jnp.concatenate([jnp.zeros((1,), I32),
                                      (blk_e[1:] != blk_e[:-1]).astype(I32)])).astype(I32)
    later = jnp.where(blk_e[None, :] > blk_e[:, None], blk_e[None, :], N_EXPERTS)
    nxt = jnp.min(later, axis=1)
    nxt = jnp.where(nxt == N_EXPERTS, blk_e, nxt).astype(I32)
    any_spec = pl.BlockSpec(memory_space=pl.ANY)
    grid_spec = pltpu.PrefetchScalarGridSpec(
        num_scalar_prefetch=4,
        grid=(n_rows // bm,),
        in_specs=[pl.BlockSpec((bm, d // 2), rows), any_spec, any_spec, any_spec],
        out_specs=pl.BlockSpec((bm, d // 2), rows),
        scratch_shapes=[pltpu.VMEM((2, d, ff), F32), pltpu.VMEM((2, d, ff), F32),
                        pltpu.VMEM((2, ff, d), F32),
                        pltpu.VMEM((d, ff), BF16), pltpu.VMEM((d, ff), BF16),
                        pltpu.VMEM((ff, d), BF16),
                        pltpu.SemaphoreType.DMA((2, 3))])
    return pl.pallas_call(
        _expert_kernel,
        grid_spec=grid_spec,
        out_shape=jax.ShapeDtypeStruct((n_rows, d // 2), U32),
        compiler_params=_cparams("arbitrary"),
        name="experts",
    )(blk_e, n_used, grp, nxt, xs, w_e1, w_e3, w_e2)


def _block_tables(counts, n_blocks, bm):
    padded = (counts + bm - 1) // bm * bm
    pad_end = jnp.cumsum(padded)
    pad_start = pad_end - padded
    blk_first = jnp.arange(n_blocks, dtype=I32) * bm
    blk_e = jnp.minimum(jnp.sum(pad_end[None, :] <= blk_first[:, None], axis=1), N_EXPERTS - 1)
    n_used = pad_end[-1] // bm
    blk_e = jnp.where(jnp.arange(n_blocks) < n_used, blk_e, blk_e[jnp.maximum(n_used - 1, 0)])
    last_row = jnp.maximum(pad_end - bm, 0)
    return (pad_start, last_row.astype(I32), (padded > 0).astype(I32),
            blk_e.astype(I32), n_used.astype(I32).reshape(1))


def _combine_kernel(dest0_ref, dnext_ref, wts_ref, v_ref, h1_ref, g2_ref, lg_ref, lb_ref,
                    ws1_ref, ws3_ref, ws2_ref, ys_hbm, o_ref, dest_s, ybuf_a, ybuf_b, sem_i, sem_r):
    tm = v_ref.shape[0]
    i = pl.program_id(0)
    ybufs = (ybuf_a, ybuf_b)

    def start_gathers(slot):
        for t in range(tm):
            for k in range(TOP_K):
                pltpu.make_async_copy(ys_hbm.at[pl.ds(dest_s[slot, k, t], 1), :],
                                      ybufs[slot].at[k, pl.ds(t, 1), :],
                                      sem_r.at[slot]).start(priority=k % 2)

    def wait_gathers(slot):
        for k in range(TOP_K):
            pltpu.make_async_copy(ys_hbm.at[pl.ds(0, tm), :], ybufs[slot].at[k], sem_r.at[slot]).wait()

    @pl.when(i == 0)
    def _():
        cp = pltpu.make_async_copy(dest0_ref, dest_s.at[0], sem_i)
        cp.start()
        cp.wait()

        def first_tile(t, carry):
            for k in range(TOP_K):
                pltpu.make_async_copy(ys_hbm.at[pl.ds(dest_s[0, k, t], 1), :],
                                      ybuf_a.at[k, pl.ds(t, 1), :], sem_r.at[0]).start()
            return carry

        lax.fori_loop(0, tm, first_tile, 0)

    def tile(slot):
        other = 1 - slot
        cp = pltpu.make_async_copy(dnext_ref, dest_s.at[other], sem_i)
        cp.start()
        cp.wait()
        wait_gathers(slot)
        start_gathers(other)
        vb = v_ref[...].astype(BF16)
        a1 = jnp.dot(vb, ws1_ref[...], preferred_element_type=F32)
        a3 = jnp.dot(vb, ws3_ref[...], preferred_element_type=F32)
        act = (a1 * jax.nn.sigmoid(a1) * a3).astype(BF16)
        ff = jnp.dot(act, ws2_ref[...], preferred_element_type=F32)
        wpad = jnp.concatenate([wts_ref[...], jnp.zeros((LANES - TOP_K, tm), F32)], axis=0)
        wt = wpad.T
        for k in range(TOP_K):
            ff = ff + _unpack_pairs(ybufs[slot][k]) * wt[:, k:k + 1]
        o_ref[...] = _layer_norm(DEEPNORM_ALPHA * h1_ref[...] + g2_ref[...] * ff,
                                 lg_ref[...], lb_ref[...])

        @pl.when(i == pl.num_programs(0) - 1)
        def _():
            wait_gathers(other)

    @pl.when(i % 2 == 0)
    def _():
        tile(0)

    @pl.when(i % 2 == 1)
    def _():
        tile(1)


def _combine(dest, wts, v, h1, g2, ln_g, ln_b, w_s1, w_s3, w_s2, ys, tm):
    s, d = v.shape
    n = s // tm
    row = lambda i: (i, 0)
    vec = _const_spec((1, d))
    slot = pl.BlockSpec((TOP_K, tm), lambda i: (0, i))
    single = lambda shape: pl.BlockSpec(shape, lambda i: (0, 0), pipeline_mode=pl.Buffered(1))
    return pl.pallas_call(
        _combine_kernel,
        grid=(n,),
        in_specs=[pl.BlockSpec((TOP_K, tm), lambda i: (0, 0)),
                  pl.BlockSpec((TOP_K, tm), lambda i: (0, jnp.minimum(i + 1, n - 1))),
                  slot, pl.BlockSpec((tm, d), row), pl.BlockSpec((tm, d), row),
                  vec, vec, vec,
                  single(w_s1.shape), single(w_s3.shape), single(w_s2.shape),
                  pl.BlockSpec(memory_space=pl.ANY)],
        out_specs=pl.BlockSpec((tm, d), row),
        out_shape=jax.ShapeDtypeStruct((s, d), F32),
        scratch_shapes=[pltpu.SMEM((2, TOP_K, tm), I32),
                        pltpu.VMEM((TOP_K, tm, d // 2), U32), pltpu.VMEM((TOP_K, tm, d // 2), U32),
                        pltpu.SemaphoreType.DMA, pltpu.SemaphoreType.DMA((2,))],
        compiler_params=_cparams("arbitrary"),
        name="combine",
    )(dest, dest, wts, v, h1, g2, ln_g, ln_b, w_s1, w_s3, w_s2, ys)


def _rope_tables(s):
    half = HEAD_DIM // 4
    inv_freq = ROPE_THETA ** (-np.arange(half, dtype=np.float32) / half)
    t = np.arange(s)
    ang_r = (t // GRID_W).astype(np.float32)[:, None] * inv_freq[None, :]
    ang_c = (t % GRID_W).astype(np.float32)[:, None] * inv_freq[None, :]
    cos = np.concatenate([np.cos(ang_r)] * 2 + [np.cos(ang_c)] * 2, axis=-1)
    sin = np.concatenate([-np.sin(ang_r), np.sin(ang_r), -np.sin(ang_c), np.sin(ang_c)], axis=-1)
    return jnp.asarray(cos, F32), jnp.asarray(sin, F32)


def kernel(x, c, ctx, c_ctx, w_mod, b_mod, w_in, q_norm, k_norm, conv_w, conv_b, rg_wa, rg_ba,
           rg_wx, rg_bx, rg_lam, w_out, ln1_g, ln1_b, w_router, e_bias, w_e1, w_e3, w_e2,
           w_s1, w_s3, w_s2, ln2_g, ln2_b):
    assert x.shape[0] == 1 and w_mod.shape[0] == DEPTH
    _, s, d = x.shape
    n_ctx = ctx.shape[1]
    x2 = x[0]

    c2t = jnp.stack([c[0], c_ctx], axis=1)
    mod = _modulation(c2t, w_mod[0], b_mod[0].reshape(1, -1))
    sh1, sc1, g1, sh2, sc2, g2 = [mod[0:1, j * d:(j + 1) * d] for j in range(6)]
    csh1, csc1 = mod[1:2, 0:d], mod[1:2, d:2 * d]

    w_in_b = w_in[0].astype(BF16)
    qg = q_norm[0].reshape(1, HEAD_DIM)
    kg = k_norm[0].reshape(1, HEAD_DIM)
    cos, sin = _rope_tables(s)
    q_l, k_l, vt_l, xr_l, yr_l = _in_projection(x2, sc1, sh1, w_in_b, qg, kg, cos, sin,
                                                tm=min(INPROJ_ROWS, s), kv_chunk=min(KV_CHUNK, s))
    _, k_c, vt_c, xr_c, _ = _in_projection(
        ctx[0], csc1, csh1, w_in_b, qg, kg,
        jnp.ones((n_ctx, HEAD_DIM), F32), jnp.zeros((n_ctx, HEAD_DIM), F32),
        tm=n_ctx, kv_chunk=n_ctx)

    attn = _attention(q_l, k_c, vt_c, k_l, vt_l, tq=min(ATTN_QUERIES, s))

    wg = jnp.concatenate([rg_wa[0], rg_wx[0]], axis=-1).astype(BF16)
    cb = conv_b[0].reshape(1, RG_W)
    zero_state = jnp.zeros((1, RG_W), F32)
    rg_args = []
    for dd in range(2):
        rg_args.append((conv_w[0], cb, wg[dd],
                        jnp.stack([rg_ba[0, dd], rg_bx[0, dd]], axis=0),
                        rg_lam[0, dd].reshape(1, RG_W)))
    t_rg = min(RG_ROWS, s)
    hc_f = _rg_scan(xr_c, *rg_args[0], zero_state, reverse=False, t=n_ctx)
    hc_b = _rg_scan(xr_c, *rg_args[1], zero_state, reverse=True, t=n_ctx)
    h_f = _rg_scan(xr_l, *rg_args[0], hc_f[n_ctx - 1:n_ctx], reverse=False, t=t_rg)
    rg = _rg_scan(xr_l, *rg_args[1], hc_b[0:1], reverse=True, t=t_rg, hf=h_f, yr=yr_l)

    h1, v = _out_projection(attn, rg, x2, w_out[0].astype(BF16), g1,
                            ln1_g[0].reshape(1, d), ln1_b[0].reshape(1, d), sc2, sh2,
                            tm=min(OUTPROJ_ROWS, s))

    wr = jnp.pad(w_router[0], ((0, 0), (0, LANES - N_EXPERTS)))
    wr_hi = wr.astype(BF16)
    wr_lo = (wr - wr_hi.astype(F32)).astype(BF16)
    idx, wts, rank, cnt = _router(v, wr_hi, wr_lo, e_bias[0].reshape(N_EXPERTS, 1),
                                  tm=min(ROUTER_ROWS, s))

    bm = min(EXPERT_BLOCK, s)
    n_blocks = s * TOP_K // bm + N_EXPERTS
    pad_start, last_row, has_blk, blk_e, n_used = _block_tables(cnt[:, 0].astype(I32), n_blocks, bm)
    dest, xs = _dispatch(last_row, has_blk, v, idx, rank,
                         pad_start.astype(F32).reshape(N_EXPERTS, 1), n_blocks * bm,
                         tm=min(DISPATCH_ROWS, s), bm=bm)
    ys = _experts(blk_e, n_used, xs, w_e1[0], w_e3[0], w_e2[0], bm)

    out = _combine(dest, wts, v, h1, g2, ln2_g[0].reshape(1, d), ln2_b[0].reshape(1, d),
                   w_s1[0].astype(BF16), w_s3[0].astype(BF16), w_s2[0].astype(BF16), ys,
                   tm=min(COMBINE_ROWS, s))
    return out[None]
```
